```python
import jax, jax.numpy as jnp
from jax import lax
import numpy as np

D_MODEL = 1024
BATCH = 2
SEQ = 8192
DEPTH = 1

D_MIX = D_MODEL
N_HEADS = 8
N_KV_HEADS = 2
HEAD_DIM = 64
D_ATTN = N_HEADS * HEAD_DIM
D_KV = N_KV_HEADS * HEAD_DIM
WINDOW = 128
BLOCK = 128
D_GMLP = D_MIX - D_ATTN
CHUNK = 128
N_GMLP_GROUPS = 8
GMLP_GROUP_DIM = D_GMLP // N_GMLP_GROUPS
D_IN = D_ATTN + 2 * D_KV + 2 * D_GMLP
N_EXPERTS = 16
CAPACITY_FACTOR = 2
D_FF_EXPERT = 2816
EPS = 1e-6
MASK_VALUE = -1e30

kernel_name = "hymba_ec_hybrid_encoder_block"


def rms_norm(x, g):
    xf = x.astype(jnp.float32)
    y = xf * lax.rsqrt(jnp.mean(xf * xf, axis=-1, keepdims=True) + EPS)
    return (y * g.astype(jnp.float32)).astype(x.dtype)


def layer_norm(x, g, b):
    xf = x.astype(jnp.float32)
    mu = jnp.mean(xf, axis=-1, keepdims=True)
    xc = xf - mu
    y = xc * lax.rsqrt(jnp.mean(xc * xc, axis=-1, keepdims=True) + EPS)
    return (y * g.astype(jnp.float32) + b.astype(jnp.float32)).astype(x.dtype)


def modulate(h, shift, scale):
    return h * (1 + scale[:, None, :]) + shift[:, None, :]


def alibi_slopes(n):
    return jnp.exp2(-8.0 * jnp.arange(1, n + 1, dtype=jnp.float32) / n)


def windowed_gqa(q, k, v, sink):
    b, s = q.shape[:2]
    nb = s // BLOCK
    g = N_HEADS // N_KV_HEADS
    qb = q.reshape(b, nb, BLOCK, N_KV_HEADS, g, HEAD_DIM)

    def band(t):
        tp = jnp.pad(t, ((0, 0), (BLOCK, BLOCK), (0, 0), (0, 0)))
        tp = tp.reshape(b, nb + 2, BLOCK, N_KV_HEADS, HEAD_DIM)
        return jnp.concatenate([tp[:, :-2], tp[:, 1:-1], tp[:, 2:]], axis=2)

    kb, vb = band(k), band(v)
    scores = jnp.einsum('bnqkgd,bnskd->bnkgqs', qb, kb).astype(jnp.float32) * (HEAD_DIM ** -0.5)
    blk = jnp.arange(nb)[:, None] * BLOCK
    q_pos = blk + jnp.arange(BLOCK)[None, :]
    k_pos = blk - BLOCK + jnp.arange(3 * BLOCK)[None, :]
    dist = jnp.abs(q_pos[:, :, None] - k_pos[:, None, :])
    valid = (dist <= WINDOW) & (k_pos[:, None, :] >= 0) & (k_pos[:, None, :] < s)
    slopes = alibi_slopes(N_HEADS).reshape(N_KV_HEADS, g)
    scores = scores - slopes[:, :, None, None] * dist[None, :, None, None].astype(jnp.float32)
    scores = jnp.where(valid[None, :, None, None], scores, MASK_VALUE)
    sink_l = sink.astype(jnp.float32).reshape(N_KV_HEADS, g)[:, :, None]
    m = jnp.maximum(scores.max(axis=-1), sink_l)
    p = jnp.exp(scores - m[..., None])
    denom = p.sum(axis=-1) + jnp.exp(sink_l - m)
    probs = (p / denom[..., None]).astype(v.dtype)
    out = jnp.einsum('bnkgqs,bnskd->bnqkgd', probs, vb)
    return out.reshape(b, s, D_ATTN)


def spatial_gating(u, v, ln_g, ln_b, w_s, b_s):
    b, s = v.shape[:2]
    nb = s // CHUNK
    vn = layer_norm(v, ln_g, ln_b).reshape(b, nb, CHUNK, N_GMLP_GROUPS, GMLP_GROUP_DIM)
    z = jnp.einsum('gts,bnsgc->bntgc', w_s, vn) + b_s.T[:, :, None]
    return u * z.reshape(b, s, D_GMLP)


def expert_choice_ffn(h, w_router, w_gate, w_up, w_down):
    b, s, d = h.shape
    cap = CAPACITY_FACTOR * s // N_EXPERTS
    aff = jax.nn.softmax((h @ w_router).astype(jnp.float32), axis=-1)
    gates, idx = lax.top_k(jnp.swapaxes(aff, 1, 2), cap)
    xs = jax.vmap(lambda hb, ib: hb[ib])(h, idx)
    hid = jax.nn.silu(jnp.einsum('becd,edf->becf', xs, w_gate)) * jnp.einsum('becd,edf->becf', xs, w_up)
    out = jnp.einsum('becf,efd->becd', hid, w_down) * gates[..., None].astype(h.dtype)
    return jax.vmap(lambda ob, ib: jnp.zeros((s, d), h.dtype).at[ib.reshape(-1)].add(ob.reshape(-1, d)))(out, idx)


def setup_inputs(seed: int = 0) -> dict:
    key = jax.random.key(seed)
    ks = jax.random.split(key, 24)
    nrm = jax.random.normal
    L, D = DEPTH, D_MODEL
    return {
        "x": nrm(ks[0], (BATCH, SEQ, D), jnp.float32),
        "c": nrm(ks[1], (BATCH, D), jnp.float32),
        "w_ada": nrm(ks[2], (L, D, 6 * D), jnp.float32) * (0.5 * D ** -0.5),
        "b_ada": 0.01 * nrm(ks[3], (L, 6 * D), jnp.float32),
        "norm_pre_mix": 1.0 + 0.1 * nrm(ks[4], (L, D), jnp.float32),
        "norm_post_mix": 1.0 + 0.1 * nrm(ks[5], (L, D), jnp.float32),
        "w_in": nrm(ks[6], (L, D, D_IN), jnp.float32) * D ** -0.5,
        "sink": 0.5 * nrm(ks[7], (L, N_HEADS), jnp.float32),
        "sgu_ln_g": 1.0 + 0.1 * nrm(ks[8], (L, D_GMLP), jnp.float32),
        "sgu_ln_b": 0.01 * nrm(ks[9], (L, D_GMLP), jnp.float32),
        "w_s": nrm(ks[10], (L, N_GMLP_GROUPS, CHUNK, CHUNK), jnp.float32) * CHUNK ** -0.5,
        "b_s": 1.0 + 0.1 * nrm(ks[11], (L, N_GMLP_GROUPS, CHUNK), jnp.float32),
        "norm_out_attn": 1.0 + 0.1 * nrm(ks[12], (L, D_ATTN), jnp.float32),
        "norm_out_gmlp": 1.0 + 0.1 * nrm(ks[13], (L, D_GMLP), jnp.float32),
        "w_out": nrm(ks[14], (L, D_MIX, D), jnp.float32) * D_MIX ** -0.5,
        "norm_pre_ffn": 1.0 + 0.1 * nrm(ks[15], (L, D), jnp.float32),
        "norm_post_ffn": 1.0 + 0.1 * nrm(ks[16], (L, D), jnp.float32),
        "w_router": nrm(ks[17], (L, D, N_EXPERTS), jnp.float32) * D ** -0.5,
        "w_gate": nrm(ks[18], (L, N_EXPERTS, D, D_FF_EXPERT), jnp.float32) * D ** -0.5,
        "w_up": nrm(ks[19], (L, N_EXPERTS, D, D_FF_EXPERT), jnp.float32) * D ** -0.5,
        "w_down": nrm(ks[20], (L, N_EXPERTS, D_FF_EXPERT, D), jnp.float32) * D_FF_EXPERT ** -0.5,
    }


def reference(x, c, w_ada, b_ada, norm_pre_mix, norm_post_mix, w_in, sink, sgu_ln_g, sgu_ln_b,
              w_s, b_s, norm_out_attn, norm_out_gmlp, w_out, norm_pre_ffn, norm_post_ffn,
              w_router, w_gate, w_up, w_down):
    b, s, _ = x.shape
    for l in range(DEPTH):
        mod = jax.nn.silu(c) @ w_ada[l] + b_ada[l]
        shift1, scale1, gate1, shift2, scale2, gate2 = jnp.split(mod, 6, axis=-1)

        h = modulate(rms_norm(x, norm_pre_mix[l]), shift1, scale1)
        proj = h @ w_in[l]
        q, k, v, zg = jnp.split(proj, [D_ATTN, D_ATTN + D_KV, D_ATTN + 2 * D_KV], axis=-1)
        attn = windowed_gqa(q.reshape(b, s, N_HEADS, HEAD_DIM),
                            k.reshape(b, s, N_KV_HEADS, HEAD_DIM),
                            v.reshape(b, s, N_KV_HEADS, HEAD_DIM), sink[l])
        u, vg = jnp.split(jax.nn.gelu(zg), 2, axis=-1)
        gm = spatial_gating(u, vg, sgu_ln_g[l], sgu_ln_b[l], w_s[l], b_s[l])
        mix = jnp.concatenate([rms_norm(attn, norm_out_attn[l]), rms_norm(gm, norm_out_gmlp[l])], axis=-1)
        x = x + gate1[:, None, :] * rms_norm(mix @ w_out[l], norm_post_mix[l])

        h2 = modulate(rms_norm(x, norm_pre_ffn[l]), shift2, scale2)
        y = expert_choice_ffn(h2, w_router[l], w_gate[l], w_up[l], w_down[l])
        x = x + gate2[:, None, :] * rms_norm(y, norm_post_ffn[l])
    return x
```

```python
import functools

import numpy as np
import jax
import jax.numpy as jnp
from jax import lax
from jax.experimental import pallas as pl
from jax.experimental.pallas import tpu as pltpu

N_HEADS = 8
N_KV_HEADS = 2
HEAD_DIM = 64
D_ATTN = N_HEADS * HEAD_DIM
D_KV = N_KV_HEADS * HEAD_DIM
WINDOW = 128
BLOCK = 128
N_GMLP_GROUPS = 8
N_EXPERTS = 16
CAPACITY_FACTOR = 2
EPS = 1e-6
MASK_VALUE = -1e30

LANES = 128
MXU_DIM = 256

TM = 512
FC = 256
QP = 2048
VMEM_LIMIT = 56 * 1024 * 1024

BF16 = jnp.bfloat16
F32 = jnp.float32


def _rms(x, g):
    return x * lax.rsqrt(jnp.mean(x * x, axis=-1, keepdims=True) + EPS) * g


def _dot(a, b):
    return jnp.dot(a, b, preferred_element_type=F32)


def _dot_nt(a, b):
    return lax.dot_general(a, b, (((1,), (1,)), ((), ())), preferred_element_type=F32)


def _dot_tn(a, b):
    return lax.dot_general(a, b, (((0,), (0,)), ((), ())), preferred_element_type=F32)


def _adaln_kernel(c_ref, w_ref, b_ref, o_ref):
    c = c_ref[...]
    a = (c * jax.nn.sigmoid(c)).astype(BF16)
    o_ref[...] = _dot(a, w_ref[...].astype(BF16)) + b_ref[...]


def _adaln(c, w_ada_all, b_ada, layer):
    b, d = c.shape
    n = w_ada_all.shape[1]
    return pl.pallas_call(
        _adaln_kernel,
        grid=(n // d,),
        in_specs=[
            pl.BlockSpec((b, d), lambda j: (0, 0)),
            pl.BlockSpec((d, d), lambda j: (layer, j)),
            pl.BlockSpec((1, d), lambda j: (0, j)),
        ],
        out_specs=pl.BlockSpec((b, d), lambda j: (0, j)),
        out_shape=jax.ShapeDtypeStruct((b, n), F32),
        compiler_params=pltpu.CompilerParams(dimension_semantics=("arbitrary",)),
        name="adaln",
    )(c, w_ada_all, b_ada.reshape(1, n))


def _gelu_tanh(x):
    c = np.float32(np.sqrt(2.0 / np.pi))
    return x * (0.5 * (1.0 + jnp.tanh(c * (x + 0.044715 * (x * x * x)))))


def _proj_kernel(x_ref, mod_ref, g_ref, w_ref, q_ref, kv_ref, u_ref, vg_ref):
    x = x_ref[...]
    shift = mod_ref[0, 0:1, :]
    scale = mod_ref[0, 1:2, :]
    h = (_rms(x, g_ref[...]) * (1.0 + scale) + shift).astype(BF16)
    p = _dot(h, w_ref[...])
    q_ref[...] = (p[:, :D_ATTN] * (HEAD_DIM ** -0.5)).astype(BF16)
    kv_ref[...] = p[:, D_ATTN:2 * D_ATTN].astype(BF16)
    dg = u_ref.shape[1]
    z = _gelu_tanh(p[:, 2 * D_ATTN:])
    u_ref[...] = z[:, :dg]
    vg_ref[...] = z[:, dg:]


def _proj(x2, mod, g_pre, w_ext, s):
    t, d = x2.shape
    nw = w_ext.shape[1]
    dg = (nw - 2 * D_ATTN) // 2
    nsb = s // TM
    return pl.pallas_call(
        _proj_kernel,
        grid=(t // TM,),
        in_specs=[
            pl.BlockSpec((TM, d), lambda i: (i, 0)),
            pl.BlockSpec((1, 6, d), lambda i: (i // nsb, 0, 0)),
            pl.BlockSpec((1, d), lambda i: (0, 0)),
            pl.BlockSpec((d, nw), lambda i: (0, 0)),
        ],
        out_specs=[
            pl.BlockSpec((TM, D_ATTN), lambda i: (i, 0)),
            pl.BlockSpec((TM, D_ATTN), lambda i: (i, 0)),
            pl.BlockSpec((TM, dg), lambda i: (i, 0)),
            pl.BlockSpec((TM, dg), lambda i: (i, 0)),
        ],
        out_shape=[
            jax.ShapeDtypeStruct((t, D_ATTN), BF16),
            jax.ShapeDtypeStruct((t, D_ATTN), BF16),
            jax.ShapeDtypeStruct((t, dg), F32),
            jax.ShapeDtypeStruct((t, dg), F32),
        ],
        compiler_params=pltpu.CompilerParams(
            dimension_semantics=("arbitrary",), vmem_limit_bytes=VMEM_LIMIT),
        name="proj",
    )(x2, mod, g_pre, w_ext)


def _attn_bias_table():
    qi = np.arange(BLOCK)[:, None]
    kj = np.arange(3 * BLOCK)[None, :] - BLOCK
    dist = np.abs(qi - kj).astype(np.float32)
    band = dist <= WINDOW
    slopes = np.exp2(-8.0 * np.arange(1, N_HEADS + 1, dtype=np.float32) / N_HEADS).astype(np.float32)
    key_ok = [kj >= 0, np.ones_like(kj, bool), kj < BLOCK]
    tab = np.empty((3, N_HEADS, BLOCK, 3 * BLOCK), np.float32)
    for v in range(3):
        ok = band & key_ok[v]
        for h in range(N_HEADS):
            tab[v, h] = np.where(ok, -slopes[h] * dist, np.float32(MASK_VALUE))
    return tab


def _mix_kernel(nblk_seq, sink_ref, x_ref, q_ref, kvp_ref, kvc_ref, kvn_ref, u_ref, vg_ref,
                mod_ref, bias_ref, ws_ref, bs_ref, lng_ref, lnb_ref, ga_ref, gg_ref,
                wout_ref, gpost_ref, gpre2_ref, wr_ref,
                x1_ref, h2_ref, aff_ref, kcat, attn_s, gm_s):
    i = pl.program_id(0)
    nb_tile = TM // BLOCK
    kcat[0:BLOCK, :] = kvp_ref[...]
    kcat[BLOCK:BLOCK + TM, :] = kvc_ref[...]
    kcat[BLOCK + TM:, :] = kvn_ref[...]
    lane = lax.broadcasted_iota(jnp.int32, (1, LANES), 1)
    lo = lane < HEAD_DIM
    zero = jnp.zeros((), BF16)

    def block_body(a, carry):
        r0 = pl.multiple_of(a * BLOCK, BLOCK)
        pos = (i * nb_tile + a) % nblk_seq
        var = jnp.where(pos == 0, 0, jnp.where(pos == nblk_seq - 1, 2, 1))
        kv = kcat[pl.ds(r0, 3 * BLOCK), :]
        k, v = kv[:, 0:LANES], kv[:, LANES:2 * LANES]
        ksw, vsw = kv[:, 2 * LANES:3 * LANES], kv[:, 3 * LANES:4 * LANES]
        kvar = ((jnp.where(lo, k, zero), jnp.where(lo, zero, ksw)),
                (jnp.where(lo, ksw, zero), jnp.where(lo, zero, k)))
        vvar = ((v, vsw), (vsw, v))
        for pi in range(N_HEADS // 2):
            qp = q_ref[pl.ds(r0, BLOCK), pi * LANES:(pi + 1) * LANES]
            halves = []
            for half in range(2):
                h = 2 * pi + half
                kvh = h // (N_HEADS // N_KV_HEADS)
                s = _dot_nt(qp, kvar[kvh][half]) + bias_ref[var, h]
                sink = sink_ref[h]
                m = jnp.maximum(jnp.max(s, axis=-1, keepdims=True), sink)
                p = jnp.exp(s - m)
                den = jnp.sum(p, axis=-1, keepdims=True) + jnp.exp(sink - m)
                o = _dot(p.astype(BF16), vvar[kvh][half])
                halves.append(o / den)
            attn_s[pl.ds(r0, BLOCK), pi * LANES:(pi + 1) * LANES] = jnp.where(lo, halves[0], halves[1])
        vg = vg_ref[pl.ds(r0, BLOCK), :]
        mu = jnp.mean(vg, axis=-1, keepdims=True)
        xc = vg - mu
        vn = (xc * lax.rsqrt(jnp.mean(xc * xc, axis=-1, keepdims=True) + EPS) * lng_ref[...]
              + lnb_ref[...]).astype(BF16)
        for pi in range(N_GMLP_GROUPS // 2):
            vnp = vn[:, pi * LANES:(pi + 1) * LANES]
            z = jnp.where(lo, _dot(ws_ref[2 * pi], vnp), _dot(ws_ref[2 * pi + 1], vnp))
            sl = slice(pi * LANES, (pi + 1) * LANES)
            gm_s[pl.ds(r0, BLOCK), sl] = u_ref[pl.ds(r0, BLOCK), sl] * (z + bs_ref[:, sl])
        return carry

    lax.fori_loop(0, nb_tile, block_body, 0)

    na = _rms(attn_s[...], ga_ref[...]).astype(BF16)
    ng = _rms(gm_s[...], gg_ref[...]).astype(BF16)
    mo = _dot(na, wout_ref[0:D_ATTN, :]) + _dot(ng, wout_ref[D_ATTN:, :])
    gate1 = mod_ref[0, 2:3, :]
    x1 = x_ref[...] + gate1 * _rms(mo, gpost_ref[...])
    x1_ref[...] = x1
    h2 = (_rms(x1, gpre2_ref[...]) * (1.0 + mod_ref[0, 4:5, :]) + mod_ref[0, 3:4, :]).astype(BF16)
    h2_ref[...] = h2
    lg = _dot_nt(wr_ref[...], h2)
    lg = lg - jnp.max(lg, axis=0, keepdims=True)
    ex = jnp.exp(lg)
    aff_ref[0] = ex / jnp.sum(ex, axis=0, keepdims=True)


def _mix(x2, q, kv, u, vg, mod, sink, bias, ws, bs_full, ln_g, ln_b, g_attn, g_gmlp,
         w_out, g_post, g_pre2, w_r_t, b, s):
    t, d = x2.shape
    dg = u.shape[1]
    e = w_r_t.shape[0]
    nsb = s // TM
    nb_tile = TM // BLOCK
    nblk = t // BLOCK
    row = lambda i: (i, 0)
    const2 = lambda i: (0, 0)
    in_specs = [
        pl.BlockSpec(memory_space=pltpu.SMEM),
        pl.BlockSpec((TM, d), row),
        pl.BlockSpec((TM, D_ATTN), row),
        pl.BlockSpec((BLOCK, 4 * LANES), lambda i: (jnp.maximum(i * nb_tile - 1, 0), 0)),
        pl.BlockSpec((TM, 4 * LANES), row),
        pl.BlockSpec((BLOCK, 4 * LANES), lambda i: (jnp.minimum((i + 1) * nb_tile, nblk - 1), 0)),
        pl.BlockSpec((TM, dg), row),
        pl.BlockSpec((TM, dg), row),
        pl.BlockSpec((1, 6, d), lambda i: (i // nsb, 0, 0)),
        pl.BlockSpec(bias.shape, lambda i: (0, 0, 0, 0)),
        pl.BlockSpec(ws.shape, lambda i: (0, 0, 0)),
        pl.BlockSpec((BLOCK, dg), const2),
        pl.BlockSpec((1, dg), const2),
        pl.BlockSpec((1, dg), const2),
        pl.BlockSpec((1, D_ATTN), const2),
        pl.BlockSpec((1, dg), const2),
        pl.BlockSpec(w_out.shape, const2),
        pl.BlockSpec((1, d), const2),
        pl.BlockSpec((1, d), const2),
        pl.BlockSpec(w_r_t.shape, const2),
    ]
    out_specs = [
        pl.BlockSpec((TM, d), row),
        pl.BlockSpec((TM, d), row),
        pl.BlockSpec((1, e, TM), lambda i: (i // nsb, 0, i % nsb)),
    ]
    out_shape = [
        jax.ShapeDtypeStruct((t, d), F32),
        jax.ShapeDtypeStruct((t, d), BF16),
        jax.ShapeDtypeStruct((b, e, s), F32),
    ]
    return pl.pallas_call(
        functools.partial(_mix_kernel, s // BLOCK),
        grid=(t // TM,),
        in_specs=in_specs,
        out_specs=out_specs,
        out_shape=out_shape,
        scratch_shapes=[
            pltpu.VMEM((TM + 2 * BLOCK, 4 * LANES), BF16),
            pltpu.VMEM((TM, D_ATTN), F32),
            pltpu.VMEM((TM, dg), F32),
        ],
        compiler_params=pltpu.CompilerParams(
            dimension_semantics=("arbitrary",), vmem_limit_bytes=VMEM_LIMIT),
        name="mix",
    )(sink, x2, q, kv, kv, kv, u, vg, mod, bias, ws, bs_full, ln_g, ln_b, g_attn, g_gmlp,
      w_out, g_post, g_pre2, w_r_t)


def _route_kernel(cap, aff_ref, slot_ref, cnt_ref):
    aff = aff_ref[0]
    e, s = aff.shape
    capf = jnp.float32(cap)

    def count_ge(t):
        return jnp.sum(jnp.where(aff >= t, 1.0, 0.0), axis=1, keepdims=True)

    def bit_body(it, cand):
        trial = cand | jnp.left_shift(jnp.int32(1), 30 - it)
        t = lax.bitcast_convert_type(trial, F32)
        return jnp.where(count_ge(t) >= capf, trial, cand)

    cand = lax.fori_loop(0, 31, bit_body, jnp.zeros((e, 1), jnp.int32))
    thr = lax.bitcast_convert_type(cand, F32)
    need = capf - jnp.sum(jnp.where(aff > thr, 1.0, 0.0), axis=1, keepdims=True)

    r = lax.broadcasted_iota(jnp.int32, (LANES, LANES), 0)
    c = lax.broadcasted_iota(jnp.int32, (LANES, LANES), 1)
    tri = jnp.where(r < c, 1.0, 0.0).astype(BF16)
    lane = lax.broadcasted_iota(jnp.int32, (e, LANES), 1)
    carry_eq = jnp.zeros((e, 1), F32)
    carry_sel = jnp.zeros((e, 1), F32)
    cnt = jnp.zeros((e, LANES), F32)
    per_blk = MXU_DIM // LANES
    for j in range(s // LANES):
        a = aff[:, j * LANES:(j + 1) * LANES]
        gt = a > thr
        eq = jnp.where(a == thr, 1.0, 0.0)
        tie_rank = _dot(eq.astype(BF16), tri) + carry_eq
        sel = jnp.where(gt | ((eq > 0.0) & (tie_rank < need)), 1.0, 0.0)
        slot = _dot(sel.astype(BF16), tri) + carry_sel
        slot_ref[0, :, j * LANES:(j + 1) * LANES] = jnp.where(sel > 0.0, slot, -1.0).astype(jnp.int32)
        if j % per_blk == 0:
            cnt = jnp.where(lane == j // per_blk, carry_sel, cnt)
        carry_eq = carry_eq + jnp.sum(eq, axis=1, keepdims=True)
        carry_sel = carry_sel + jnp.sum(sel, axis=1, keepdims=True)
    cnt = jnp.where(lane >= s // MXU_DIM, carry_sel, cnt)
    cnt_ref[0] = cnt.astype(jnp.int32)


def _route(aff_t, cap):
    b, e, s = aff_t.shape
    return pl.pallas_call(
        functools.partial(_route_kernel, cap),
        grid=(b,),
        in_specs=[pl.BlockSpec((1, e, s), lambda i: (i, 0, 0))],
        out_specs=[
            pl.BlockSpec((1, e, s), lambda i: (i, 0, 0)),
            pl.BlockSpec((1, e, LANES), lambda i: (i, 0, 0)),
        ],
        out_shape=[
            jax.ShapeDtypeStruct((b, e, s), jnp.int32),
            jax.ShapeDtypeStruct((b, e, LANES), jnp.int32),
        ],
        compiler_params=pltpu.CompilerParams(dimension_semantics=("arbitrary",)),
        name="route",
    )(aff_t)


def _slot_tiles(cnt_ref, row, pb, cap):
    c0 = jnp.minimum(cnt_ref[row, pb], cap)
    c1 = jnp.minimum(cnt_ref[row, pb + 1], cap)
    return c1 > c0, c0 // MXU_DIM, jnp.maximum(c1 - 1, 0) // MXU_DIM


def _ffn_kernel(n_exp, cnt_ref, h2_ref, slot_ref, aff_ref, wg_ref, wu_ref, wd_ref, out_ref,
                x_s, acc_s, gs_s):
    b, e, f = pl.program_id(0), pl.program_id(1), pl.program_id(2)
    cap = x_s.shape[0]
    row = b * n_exp + e
    nblk = slot_ref.shape[2]
    sub = lax.broadcasted_iota(jnp.int32, (MXU_DIM, MXU_DIM), 0)

    @pl.when(f == 0)
    def _gather():
        acc_s[...] = jnp.zeros_like(acc_s)
        gs_s[...] = jnp.zeros_like(gs_s)

        def pb_body(pb, carry):
            nonempty, t0, t1 = _slot_tiles(cnt_ref, row, pb, cap)

            def add_tile(t):
                base = pl.multiple_of(t * MXU_DIM, MXU_DIM)
                match = slot_ref[0, 0, pl.ds(pb, 1), :] == (sub + base)
                onehot = jnp.where(match, 1.0, 0.0).astype(BF16)
                blk = h2_ref[0, pl.ds(pl.multiple_of(pb * MXU_DIM, MXU_DIM), MXU_DIM), :]
                acc_s[pl.ds(base, MXU_DIM), :] += _dot(onehot, blk)
                gs_s[pl.ds(base, MXU_DIM), :] += jnp.sum(
                    jnp.where(match, aff_ref[0, 0, pl.ds(pb, 1), :], 0.0), axis=1, keepdims=True)

            @pl.when(nonempty)
            def _():
                add_tile(t0)

            @pl.when(nonempty & (t1 != t0))
            def _():
                add_tile(t1)

            return carry

        lax.fori_loop(0, nblk, pb_body, 0)
        x_s[...] = acc_s[...].astype(BF16)
        acc_s[...] = jnp.zeros_like(acc_s)

    x = x_s[...]
    g = _dot(x, wg_ref[0].astype(BF16))
    u = _dot(x, wu_ref[0].astype(BF16))
    hid = (g * jax.nn.sigmoid(g) * u).astype(BF16)
    acc_s[...] += _dot(hid, wd_ref[0].astype(BF16))

    @pl.when(f == pl.num_programs(2) - 1)
    def _emit():
        out_ref[0, 0] = (acc_s[...] * gs_s[...]).astype(BF16)


def _ffn(cnt2, h2, slots4, aff4, w_gate, w_up, w_down, cap, e, e0):
    b, s, d = h2.shape
    ff = w_gate.shape[2]
    nblk = s // MXU_DIM
    grid_spec = pltpu.PrefetchScalarGridSpec(
        num_scalar_prefetch=1,
        grid=(b, e, ff // FC),
        in_specs=[
            pl.BlockSpec((1, s, d), lambda bi, ei, fi, c: (bi, 0, 0), pipeline_mode=pl.Buffered(1)),
            pl.BlockSpec((1, 1, nblk, MXU_DIM), lambda bi, ei, fi, c: (bi, ei, 0, 0)),
            pl.BlockSpec((1, 1, nblk, MXU_DIM), lambda bi, ei, fi, c: (bi, ei, 0, 0)),
            pl.BlockSpec((1, d, FC), lambda bi, ei, fi, c: (e0 + ei, 0, fi)),
            pl.BlockSpec((1, d, FC), lambda bi, ei, fi, c: (e0 + ei, 0, fi)),
            pl.BlockSpec((1, FC, d), lambda bi, ei, fi, c: (e0 + ei, fi, 0)),
        ],
        out_specs=pl.BlockSpec((1, 1, cap, d), lambda bi, ei, fi, c: (bi, ei, 0, 0)),
        scratch_shapes=[
            pltpu.VMEM((cap, d), BF16),
            pltpu.VMEM((cap, d), F32),
            pltpu.VMEM((cap, 1), F32),
        ],
    )
    return pl.pallas_call(
        functools.partial(_ffn_kernel, e),
        grid_spec=grid_spec,
        out_shape=jax.ShapeDtypeStruct((b, e, cap, d), BF16),
        compiler_params=pltpu.CompilerParams(
            dimension_semantics=("arbitrary", "arbitrary", "arbitrary"),
            vmem_limit_bytes=VMEM_LIMIT),
        name="ffn",
    )(cnt2, h2, slots4, aff4, w_gate, w_up, w_down)


def _combine_kernel(n_exp, cnt_ref, eo_ref, slot_ref, x1_ref, mod_ref, g_ref, o_ref):
    b, q, e = pl.program_id(0), pl.program_id(1), pl.program_id(2)
    cap = eo_ref.shape[2]
    row = b * n_exp + e
    nblk_q = slot_ref.shape[2]
    sub = lax.broadcasted_iota(jnp.int32, (MXU_DIM, MXU_DIM), 0)

    @pl.when(e == 0)
    def _init():
        o_ref[...] = jnp.zeros_like(o_ref)

    for j in range(nblk_q):
        nonempty, t0, t1 = _slot_tiles(cnt_ref, row, q * nblk_q + j, cap)

        def add_tile(t, j=j):
            base = pl.multiple_of(t * MXU_DIM, MXU_DIM)
            match = slot_ref[0, 0, j:j + 1, :] == (sub + base)
            onehot = jnp.where(match, 1.0, 0.0).astype(BF16)
            o_ref[j * MXU_DIM:(j + 1) * MXU_DIM, :] += _dot_tn(onehot, eo_ref[0, 0, pl.ds(base, MXU_DIM), :])

        @pl.when(nonempty)
        def _():
            add_tile(t0)

        @pl.when(nonempty & (t1 != t0))
        def _():
            add_tile(t1)

    @pl.when(e == n_exp - 1)
    def _final():
        gate = mod_ref[0, 5:6, :]

        def rows(r, carry):
            sl = pl.ds(pl.multiple_of(r * MXU_DIM, MXU_DIM), MXU_DIM)
            o_ref[sl, :] = x1_ref[sl, :] + gate * _rms(o_ref[sl, :], g_ref[...])
            return carry

        lax.fori_loop(0, o_ref.shape[0] // MXU_DIM, rows, 0)


def _combine(cnt2, eo, slots4, x1, mod, g_post):
    b, e, cap, d = eo.shape
    t = x1.shape[0]
    s = t // b
    nq = s // QP
    nblk_q = QP // MXU_DIM
    grid_spec = pltpu.PrefetchScalarGridSpec(
        num_scalar_prefetch=1,
        grid=(b, nq, e),
        in_specs=[
            pl.BlockSpec((1, 1, cap, d), lambda bi, qi, ei, c: (bi, ei, 0, 0)),
            pl.BlockSpec((1, 1, nblk_q, MXU_DIM), lambda bi, qi, ei, c: (bi, ei, qi, 0)),
            pl.BlockSpec((QP, d), lambda bi, qi, ei, c: (bi * nq + qi, 0)),
            pl.BlockSpec((1, 6, d), lambda bi, qi, ei, c: (bi, 0, 0)),
            pl.BlockSpec((1, d), lambda bi, qi, ei, c: (0, 0)),
        ],
        out_specs=pl.BlockSpec((QP, d), lambda bi, qi, ei, c: (bi * nq + qi, 0)),
    )
    return pl.pallas_call(
        functools.partial(_combine_kernel, e),
        grid_spec=grid_spec,
        out_shape=jax.ShapeDtypeStruct((t, d), F32),
        compiler_params=pltpu.CompilerParams(
            dimension_semantics=("arbitrary", "arbitrary", "arbitrary"),
            vmem_limit_bytes=VMEM_LIMIT),
        name="combine",
    )(cnt2, eo, slots4, x1, mod, g_post)


def _layer(layer, x, c, w_ada_all, b_ada, norm_pre_mix, norm_post_mix, w_in, sink, sgu_ln_g,
           sgu_ln_b, w_s, b_s, norm_out_attn, norm_out_gmlp, w_out, norm_pre_ffn, norm_post_ffn,
           w_router, w_gate_all, w_up_all, w_down_all):
    b, s, d = x.shape
    t = b * s
    dg = sgu_ln_g.shape[0]
    e = w_router.shape[1]
    cap = CAPACITY_FACTOR * s // e
    assert s % QP == 0 and s % TM == 0 and cap % MXU_DIM == 0 and w_gate_all.shape[2] % FC == 0
    assert s // MXU_DIM < LANES and d == D_ATTN + dg

    mod = _adaln(c, w_ada_all, b_ada, layer).reshape(b, 6, d)

    kcol = w_in[:, D_ATTN:D_ATTN + D_KV]
    vcol = w_in[:, D_ATTN + D_KV:D_ATTN + 2 * D_KV]
    swap = lambda w: jnp.concatenate([w[:, HEAD_DIM:], w[:, :HEAD_DIM]], axis=1)
    w_ext = jnp.concatenate(
        [w_in[:, :D_ATTN], kcol, vcol, swap(kcol), swap(vcol), w_in[:, D_ATTN + 2 * D_KV:]],
        axis=1).astype(BF16)

    x2 = x.reshape(t, d)
    q, kv, u, vg = _proj(x2, mod, norm_pre_mix.reshape(1, d), w_ext, s)

    bias = jnp.asarray(_attn_bias_table())
    bs_full = jnp.repeat(b_s.T, dg // N_GMLP_GROUPS, axis=1)
    x1, h2, aff_t = _mix(
        x2, q, kv, u, vg, mod, sink, bias, w_s.astype(BF16), bs_full,
        sgu_ln_g.reshape(1, dg), sgu_ln_b.reshape(1, dg), norm_out_attn.reshape(1, D_ATTN),
        norm_out_gmlp.reshape(1, dg), w_out.astype(BF16), norm_post_mix.reshape(1, d),
        norm_pre_ffn.reshape(1, d), w_router.T.astype(BF16), b, s)

    slots, cnt = _route(aff_t, cap)
    nblk = s // MXU_DIM
    slots4 = slots.reshape(b, e, nblk, MXU_DIM)
    aff4 = aff_t.reshape(b, e, nblk, MXU_DIM)
    cnt2 = cnt.reshape(b * e, LANES)
    eo = _ffn(cnt2, h2.reshape(b, s, d), slots4, aff4, w_gate_all, w_up_all, w_down_all, cap,
              e, layer * e)
    out = _combine(cnt2, eo, slots4, x1, mod, norm_post_ffn.reshape(1, d))
    return out.reshape(b, s, d)


def kernel(x, c, w_ada, b_ada, norm_pre_mix, norm_post_mix, w_in, sink, sgu_ln_g, sgu_ln_b, w_s, b_s, norm_out_attn, norm_out_gmlp, w_out, norm_pre_ffn, norm_post_ffn, w_router, w_gate, w_up, w_down):
    depth, d = w_ada.shape[0], w_ada.shape[1]
    w_ada_all = w_ada.reshape(depth * d, w_ada.shape[2])
    stack = lambda w: w.reshape((depth * w.shape[1],) + w.shape[2:])
    w_gate_all, w_up_all, w_down_all = stack(w_gate), stack(w_up), stack(w_down)
    for l in range(depth):
        x = _layer(l, x, c, w_ada_all, b_ada[l], norm_pre_mix[l], norm_post_mix[l], w_in[l],
                   sink[l], sgu_ln_g[l], sgu_ln_b[l], w_s[l], b_s[l], norm_out_attn[l],
                   norm_out_gmlp[l], w_out[l], norm_pre_ffn[l], norm_post_ffn[l], w_router[l],
                   w_gate_all, w_up_all, w_down_all)
    return x
```

```python
import functools

import numpy as np
import jax
import jax.numpy as jnp
from jax import lax
from jax.experimental import pallas as pl
from jax.experimental.pallas import tpu as pltpu

N_HEADS = 8
N_KV_HEADS = 2
HEAD_DIM = 64
D_ATTN = N_HEADS * HEAD_DIM
D_KV = N_KV_HEADS * HEAD_DIM
WINDOW = 128
BLOCK = 128
N_GMLP_GROUPS = 8
N_EXPERTS = 16
CAPACITY_FACTOR = 2
EPS = 1e-6
MASK_VALUE = -1e30

LANES = 128
MXU_DIM = 256

TM = 512
FC = 256
QP = 1024
WIN = 64
SLOT_ALIGN = 16
VMEM_LIMIT = 56 * 1024 * 1024

BF16 = jnp.bfloat16
F32 = jnp.float32


def _rms(x, g):
    return x * lax.rsqrt(jnp.mean(x * x, axis=-1, keepdims=True) + EPS) * g


def _dot(a, b):
    return jnp.dot(a, b, preferred_element_type=F32)


def _dot_nt(a, b):
    return lax.dot_general(a, b, (((1,), (1,)), ((), ())), preferred_element_type=F32)


def _dot_tn(a, b):
    return lax.dot_general(a, b, (((0,), (0,)), ((), ())), preferred_element_type=F32)


def _adaln_kernel(c_ref, w_ref, b_ref, o_ref):
    c = c_ref[...]
    a = (c * jax.nn.sigmoid(c)).astype(BF16)
    o_ref[...] = _dot(a, w_ref[...].astype(BF16)) + b_ref[...]


def _adaln(c, w_ada_all, b_ada, layer):
    b, d = c.shape
    n = w_ada_all.shape[1]
    return pl.pallas_call(
        _adaln_kernel,
        grid=(n // d,),
        in_specs=[
            pl.BlockSpec((b, d), lambda j: (0, 0)),
            pl.BlockSpec((d, d), lambda j: (layer, j)),
            pl.BlockSpec((1, d), lambda j: (0, j)),
        ],
        out_specs=pl.BlockSpec((b, d), lambda j: (0, j)),
        out_shape=jax.ShapeDtypeStruct((b, n), F32),
        compiler_params=pltpu.CompilerParams(dimension_semantics=("arbitrary",)),
        name="adaln",
    )(c, w_ada_all, b_ada.reshape(1, n))


def _gelu_tanh(x):
    c = np.float32(np.sqrt(2.0 / np.pi))
    return x * (0.5 * (1.0 + jnp.tanh(c * (x + 0.044715 * (x * x * x)))))


def _proj_kernel(x_ref, mod_ref, g_ref, w_ref, q_ref, kv_ref, u_ref, vg_ref):
    x = x_ref[...]
    shift = mod_ref[0, 0:1, :]
    scale = mod_ref[0, 1:2, :]
    h = (_rms(x, g_ref[...]) * (1.0 + scale) + shift).astype(BF16)
    p = _dot(h, w_ref[...])
    q_ref[...] = (p[:, :D_ATTN] * (HEAD_DIM ** -0.5)).astype(BF16)
    kv_ref[...] = p[:, D_ATTN:2 * D_ATTN].astype(BF16)
    dg = u_ref.shape[1]
    z = _gelu_tanh(p[:, 2 * D_ATTN:])
    u_ref[...] = z[:, :dg]
    vg_ref[...] = z[:, dg:]


def _proj(x2, mod, g_pre, w_ext, s):
    t, d = x2.shape
    nw = w_ext.shape[1]
    dg = (nw - 2 * D_ATTN) // 2
    nsb = s // TM
    return pl.pallas_call(
        _proj_kernel,
        grid=(t // TM,),
        in_specs=[
            pl.BlockSpec((TM, d), lambda i: (i, 0)),
            pl.BlockSpec((1, 6, d), lambda i: (i // nsb, 0, 0)),
            pl.BlockSpec((1, d), lambda i: (0, 0)),
            pl.BlockSpec((d, nw), lambda i: (0, 0)),
        ],
        out_specs=[
            pl.BlockSpec((TM, D_ATTN), lambda i: (i, 0)),
            pl.BlockSpec((TM, D_ATTN), lambda i: (i, 0)),
            pl.BlockSpec((TM, dg), lambda i: (i, 0)),
            pl.BlockSpec((TM, dg), lambda i: (i, 0)),
        ],
        out_shape=[
            jax.ShapeDtypeStruct((t, D_ATTN), BF16),
            jax.ShapeDtypeStruct((t, D_ATTN), BF16),
            jax.ShapeDtypeStruct((t, dg), F32),
            jax.ShapeDtypeStruct((t, dg), F32),
        ],
        compiler_params=pltpu.CompilerParams(
            dimension_semantics=("arbitrary",), vmem_limit_bytes=VMEM_LIMIT),
        name="proj",
    )(x2, mod, g_pre, w_ext)


def _attn_bias_table():
    qi = np.arange(BLOCK)[:, None]
    kj = np.arange(3 * BLOCK)[None, :] - BLOCK
    dist = np.abs(qi - kj).astype(np.float32)
    band = dist <= WINDOW
    slopes = np.exp2(-8.0 * np.arange(1, N_HEADS + 1, dtype=np.float32) / N_HEADS).astype(np.float32)
    key_ok = [kj >= 0, np.ones_like(kj, bool), kj < BLOCK]
    tab = np.empty((3, N_HEADS, BLOCK, 3 * BLOCK), np.float32)
    for v in range(3):
        ok = band & key_ok[v]
        for h in range(N_HEADS):
            tab[v, h] = np.where(ok, -slopes[h] * dist, np.float32(MASK_VALUE))
    return tab


def _mix_kernel(nblk_seq, sink_ref, x_ref, q_ref, kvp_ref, kvc_ref, kvn_ref, u_ref, vg_ref,
                mod_ref, bias_ref, ws_ref, bs_ref, lng_ref, lnb_ref, ga_ref, gg_ref,
                wout_ref, gpost_ref, gpre2_ref, wr_ref,
                x1_ref, h2_ref, aff_ref, kcat, attn_s, gm_s):
    i = pl.program_id(0)
    nb_tile = TM // BLOCK
    kcat[0:BLOCK, :] = kvp_ref[...]
    kcat[BLOCK:BLOCK + TM, :] = kvc_ref[...]
    kcat[BLOCK + TM:, :] = kvn_ref[...]
    lane = lax.broadcasted_iota(jnp.int32, (1, LANES), 1)
    lo = lane < HEAD_DIM
    zero = jnp.zeros((), BF16)

    def block_body(a, carry):
        r0 = pl.multiple_of(a * BLOCK, BLOCK)
        pos = (i * nb_tile + a) % nblk_seq
        var = jnp.where(pos == 0, 0, jnp.where(pos == nblk_seq - 1, 2, 1))
        kv = kcat[pl.ds(r0, 3 * BLOCK), :]
        k, v = kv[:, 0:LANES], kv[:, LANES:2 * LANES]
        ksw, vsw = kv[:, 2 * LANES:3 * LANES], kv[:, 3 * LANES:4 * LANES]
        kvar = ((jnp.where(lo, k, zero), jnp.where(lo, zero, ksw)),
                (jnp.where(lo, ksw, zero), jnp.where(lo, zero, k)))
        vvar = ((v, vsw), (vsw, v))
        for pi in range(N_HEADS // 2):
            qp = q_ref[pl.ds(r0, BLOCK), pi * LANES:(pi + 1) * LANES]
            halves = []
            for half in range(2):
                h = 2 * pi + half
                kvh = h // (N_HEADS // N_KV_HEADS)
                s = _dot_nt(qp, kvar[kvh][half]) + bias_ref[var, h]
                sink = sink_ref[h]
                m = jnp.maximum(jnp.max(s, axis=-1, keepdims=True), sink)
                p = jnp.exp(s - m)
                den = jnp.sum(p, axis=-1, keepdims=True) + jnp.exp(sink - m)
                o = _dot(p.astype(BF16), vvar[kvh][half])
                halves.append(o / den)
            attn_s[pl.ds(r0, BLOCK), pi * LANES:(pi + 1) * LANES] = jnp.where(lo, halves[0], halves[1])
        vg = vg_ref[pl.ds(r0, BLOCK), :]
        mu = jnp.mean(vg, axis=-1, keepdims=True)
        xc = vg - mu
        vn = (xc * lax.rsqrt(jnp.mean(xc * xc, axis=-1, keepdims=True) + EPS) * lng_ref[...]
              + lnb_ref[...]).astype(BF16)
        for pi in range(N_GMLP_GROUPS // 2):
            vnp = vn[:, pi * LANES:(pi + 1) * LANES]
            z = jnp.where(lo, _dot(ws_ref[2 * pi], vnp), _dot(ws_ref[2 * pi + 1], vnp))
            sl = slice(pi * LANES, (pi + 1) * LANES)
            gm_s[pl.ds(r0, BLOCK), sl] = u_ref[pl.ds(r0, BLOCK), sl] * (z + bs_ref[:, sl])
        return carry

    lax.fori_loop(0, nb_tile, block_body, 0)

    na = _rms(attn_s[...], ga_ref[...]).astype(BF16)
    ng = _rms(gm_s[...], gg_ref[...]).astype(BF16)
    mo = _dot(na, wout_ref[0:D_ATTN, :]) + _dot(ng, wout_ref[D_ATTN:, :])
    gate1 = mod_ref[0, 2:3, :]
    x1 = x_ref[...] + gate1 * _rms(mo, gpost_ref[...])
    x1_ref[...] = x1
    h2 = (_rms(x1, gpre2_ref[...]) * (1.0 + mod_ref[0, 4:5, :]) + mod_ref[0, 3:4, :]).astype(BF16)
    h2_ref[...] = h2
    lg = _dot_nt(wr_ref[...], h2)
    lg = lg - jnp.max(lg, axis=0, keepdims=True)
    ex = jnp.exp(lg)
    aff_ref[0] = ex / jnp.sum(ex, axis=0, keepdims=True)


def _mix(x2, q, kv, u, vg, mod, sink, bias, ws, bs_full, ln_g, ln_b, g_attn, g_gmlp,
         w_out, g_post, g_pre2, w_r_t, b, s):
    t, d = x2.shape
    dg = u.shape[1]
    e = w_r_t.shape[0]
    nsb = s // TM
    nb_tile = TM // BLOCK
    nblk = t // BLOCK
    row = lambda i: (i, 0)
    const2 = lambda i: (0, 0)
    in_specs = [
        pl.BlockSpec(memory_space=pltpu.SMEM),
        pl.BlockSpec((TM, d), row),
        pl.BlockSpec((TM, D_ATTN), row),
        pl.BlockSpec((BLOCK, 4 * LANES), lambda i: (jnp.maximum(i * nb_tile - 1, 0), 0)),
        pl.BlockSpec((TM, 4 * LANES), row),
        pl.BlockSpec((BLOCK, 4 * LANES), lambda i: (jnp.minimum((i + 1) * nb_tile, nblk - 1), 0)),
        pl.BlockSpec((TM, dg), row),
        pl.BlockSpec((TM, dg), row),
        pl.BlockSpec((1, 6, d), lambda i: (i // nsb, 0, 0)),
        pl.BlockSpec(bias.shape, lambda i: (0, 0, 0, 0)),
        pl.BlockSpec(ws.shape, lambda i: (0, 0, 0)),
        pl.BlockSpec((BLOCK, dg), const2),
        pl.BlockSpec((1, dg), const2),
        pl.BlockSpec((1, dg), const2),
        pl.BlockSpec((1, D_ATTN), const2),
        pl.BlockSpec((1, dg), const2),
        pl.BlockSpec(w_out.shape, const2),
        pl.BlockSpec((1, d), const2),
        pl.BlockSpec((1, d), const2),
        pl.BlockSpec(w_r_t.shape, const2),
    ]
    out_specs = [
        pl.BlockSpec((TM, d), row),
        pl.BlockSpec((TM, d), row),
        pl.BlockSpec((1, e, TM), lambda i: (i // nsb, 0, i % nsb)),
    ]
    out_shape = [
        jax.ShapeDtypeStruct((t, d), F32),
        jax.ShapeDtypeStruct((t, d), BF16),
        jax.ShapeDtypeStruct((b, e, s), F32),
    ]
    return pl.pallas_call(
        functools.partial(_mix_kernel, s // BLOCK),
        grid=(t // TM,),
        in_specs=in_specs,
        out_specs=out_specs,
        out_shape=out_shape,
        scratch_shapes=[
            pltpu.VMEM((TM + 2 * BLOCK, 4 * LANES), BF16),
            pltpu.VMEM((TM, D_ATTN), F32),
            pltpu.VMEM((TM, dg), F32),
        ],
        compiler_params=pltpu.CompilerParams(
            dimension_semantics=("arbitrary",), vmem_limit_bytes=VMEM_LIMIT),
        name="mix",
    )(sink, x2, q, kv, kv, kv, u, vg, mod, bias, ws, bs_full, ln_g, ln_b, g_attn, g_gmlp,
      w_out, g_post, g_pre2, w_r_t)


def _route_kernel(cap, aff_ref, slot_ref, cnt_ref):
    aff = aff_ref[0]
    e, s = aff.shape
    capf = jnp.float32(cap)

    def count_ge(t):
        return jnp.sum(jnp.where(aff >= t, 1.0, 0.0), axis=1, keepdims=True)

    def bit_body(it, cand):
        trial = cand | jnp.left_shift(jnp.int32(1), 30 - it)
        t = lax.bitcast_convert_type(trial, F32)
        return jnp.where(count_ge(t) >= capf, trial, cand)

    cand = lax.fori_loop(0, 31, bit_body, jnp.zeros((e, 1), jnp.int32))
    thr = lax.bitcast_convert_type(cand, F32)
    need = capf - jnp.sum(jnp.where(aff > thr, 1.0, 0.0), axis=1, keepdims=True)

    r = lax.broadcasted_iota(jnp.int32, (LANES, LANES), 0)
    c = lax.broadcasted_iota(jnp.int32, (LANES, LANES), 1)
    tri = jnp.where(r < c, 1.0, 0.0).astype(BF16)
    lane = lax.broadcasted_iota(jnp.int32, (e, LANES), 1)
    carry_eq = jnp.zeros((e, 1), F32)
    carry_sel = jnp.zeros((e, 1), F32)
    cnt = jnp.zeros((e, LANES), F32)
    per_blk = MXU_DIM // LANES
    for j in range(s // LANES):
        a = aff[:, j * LANES:(j + 1) * LANES]
        gt = a > thr
        eq = jnp.where(a == thr, 1.0, 0.0)
        tie_rank = _dot(eq.astype(BF16), tri) + carry_eq
        sel = jnp.where(gt | ((eq > 0.0) & (tie_rank < need)), 1.0, 0.0)
        slot = _dot(sel.astype(BF16), tri) + carry_sel
        slot_ref[0, :, j * LANES:(j + 1) * LANES] = jnp.where(sel > 0.0, slot, -1.0).astype(jnp.int32)
        if j % per_blk == 0:
            cnt = jnp.where(lane == j // per_blk, carry_sel, cnt)
        carry_eq = carry_eq + jnp.sum(eq, axis=1, keepdims=True)
        carry_sel = carry_sel + jnp.sum(sel, axis=1, keepdims=True)
    cnt = jnp.where(lane >= s // MXU_DIM, carry_sel, cnt)
    cnt_ref[0] = cnt.astype(jnp.int32)


def _route(aff_t, cap):
    b, e, s = aff_t.shape
    return pl.pallas_call(
        functools.partial(_route_kernel, cap),
        grid=(b,),
        in_specs=[pl.BlockSpec((1, e, s), lambda i: (i, 0, 0))],
        out_specs=[
            pl.BlockSpec((1, e, s), lambda i: (i, 0, 0)),
            pl.BlockSpec((1, e, LANES), lambda i: (i, 0, 0)),
        ],
        out_shape=[
            jax.ShapeDtypeStruct((b, e, s), jnp.int32),
            jax.ShapeDtypeStruct((b, e, LANES), jnp.int32),
        ],
        compiler_params=pltpu.CompilerParams(dimension_semantics=("arbitrary",)),
        name="route",
    )(aff_t)


def _slot_tiles(cnt_ref, row, pb, cap):
    c0 = jnp.minimum(cnt_ref[row, pb], cap)
    c1 = jnp.minimum(cnt_ref[row, pb + 1], cap)
    return c1 > c0, c0 // MXU_DIM, jnp.maximum(c1 - 1, 0) // MXU_DIM


def _ffn_kernel(n_exp, cnt_ref, h2_ref, slot_ref, aff_ref, wg_ref, wu_ref, wd_ref, out_ref,
                x_s, acc_s, gs_s):
    b, e, f = pl.program_id(0), pl.program_id(1), pl.program_id(2)
    cap = x_s.shape[0]
    row = b * n_exp + e
    nblk = slot_ref.shape[2]
    sub = lax.broadcasted_iota(jnp.int32, (MXU_DIM, MXU_DIM), 0)

    @pl.when(f == 0)
    def _gather():
        acc_s[...] = jnp.zeros_like(acc_s)
        gs_s[...] = jnp.zeros_like(gs_s)

        def pb_body(pb, carry):
            nonempty, t0, t1 = _slot_tiles(cnt_ref, row, pb, cap)

            def add_tile(t):
                base = pl.multiple_of(t * MXU_DIM, MXU_DIM)
                match = slot_ref[0, 0, pl.ds(pb, 1), :] == (sub + base)
                onehot = jnp.where(match, 1.0, 0.0).astype(BF16)
                blk = h2_ref[0, pl.ds(pl.multiple_of(pb * MXU_DIM, MXU_DIM), MXU_DIM), :]
                acc_s[pl.ds(base, MXU_DIM), :] += _dot(onehot, blk)
                gs_s[pl.ds(base, MXU_DIM), :] += jnp.sum(
                    jnp.where(match, aff_ref[0, 0, pl.ds(pb, 1), :], 0.0), axis=1, keepdims=True)

            @pl.when(nonempty)
            def _():
                add_tile(t0)

            @pl.when(nonempty & (t1 != t0))
            def _():
                add_tile(t1)

            return carry

        lax.fori_loop(0, nblk, pb_body, 0)
        x_s[...] = acc_s[...].astype(BF16)
        acc_s[...] = jnp.zeros_like(acc_s)

    x = x_s[...]
    g = _dot(x, wg_ref[0].astype(BF16))
    u = _dot(x, wu_ref[0].astype(BF16))
    hid = (g * jax.nn.sigmoid(g) * u).astype(BF16)
    acc_s[...] += _dot(hid, wd_ref[0].astype(BF16))

    @pl.when(f == pl.num_programs(2) - 1)
    def _emit():
        out_ref[0, 0] = (acc_s[...] * gs_s[...]).astype(BF16)


def _ffn(cnt2, h2, slots4, aff4, w_gate, w_up, w_down, cap, e, e0):
    b, s, d = h2.shape
    ff = w_gate.shape[2]
    nblk = s // MXU_DIM
    grid_spec = pltpu.PrefetchScalarGridSpec(
        num_scalar_prefetch=1,
        grid=(b, e, ff // FC),
        in_specs=[
            pl.BlockSpec((1, s, d), lambda bi, ei, fi, c: (bi, 0, 0), pipeline_mode=pl.Buffered(1)),
            pl.BlockSpec((1, 1, nblk, MXU_DIM), lambda bi, ei, fi, c: (bi, ei, 0, 0)),
            pl.BlockSpec((1, 1, nblk, MXU_DIM), lambda bi, ei, fi, c: (bi, ei, 0, 0)),
            pl.BlockSpec((1, d, FC), lambda bi, ei, fi, c: (e0 + ei, 0, fi)),
            pl.BlockSpec((1, d, FC), lambda bi, ei, fi, c: (e0 + ei, 0, fi)),
            pl.BlockSpec((1, FC, d), lambda bi, ei, fi, c: (e0 + ei, fi, 0)),
        ],
        out_specs=pl.BlockSpec((1, 1, cap, d), lambda bi, ei, fi, c: (bi, ei, 0, 0)),
        scratch_shapes=[
            pltpu.VMEM((cap, d), BF16),
            pltpu.VMEM((cap, d), F32),
            pltpu.VMEM((cap, 1), F32),
        ],
    )
    return pl.pallas_call(
        functools.partial(_ffn_kernel, e),
        grid_spec=grid_spec,
        out_shape=jax.ShapeDtypeStruct((b, e, cap, d), BF16),
        compiler_params=pltpu.CompilerParams(
            dimension_semantics=("arbitrary", "arbitrary", "arbitrary"),
            vmem_limit_bytes=VMEM_LIMIT),
        name="ffn",
    )(cnt2, h2, slots4, aff4, w_gate, w_up, w_down)


def _segment(cnt_ref, row, pb, cap):
    c0 = jnp.minimum(cnt_ref[row, pb], cap)
    c1 = jnp.minimum(cnt_ref[row, pb + 1], cap)
    a0 = jnp.minimum(jnp.bitwise_and(c0, -SLOT_ALIGN), cap - WIN)
    return c0, c1, pl.multiple_of(a0, SLOT_ALIGN)


def _combine_kernel(n_exp, cnt_ref, eo_ref, slot_ref, x1_ref, mod_ref, g_ref, o_ref,
                    stage_s, oh_s, y_s):
    b, q = pl.program_id(0), pl.program_id(1)
    cap = eo_ref.shape[2]
    nblk_q = o_ref.shape[0] // MXU_DIM
    sub = lax.broadcasted_iota(jnp.int32, (WIN, MXU_DIM), 0)
    gate = mod_ref[0, 5:6, :]

    for j in range(nblk_q):
        pb = q * nblk_q + j
        overflow = jnp.int32(0)
        for e in range(n_exp):
            _, c1, a0 = _segment(cnt_ref, b * n_exp + e, pb, cap)
            stage_s[e * WIN:(e + 1) * WIN, :] = eo_ref[0, e, pl.ds(a0, WIN), :]
            match = slot_ref[0, e, pl.ds(pb, 1), :] == (sub + a0)
            oh_s[e * WIN:(e + 1) * WIN, :] = jnp.where(match, 1.0, 0.0).astype(BF16)
            overflow = overflow + jnp.maximum(c1 - (a0 + WIN), 0)
        y_s[...] = _dot_tn(oh_s[...], stage_s[...])

        @pl.when(overflow > 0)
        def _long_segments(j=j, pb=pb):
            def per_expert(e, carry):
                _, c1, a0 = _segment(cnt_ref, b * n_exp + e, pb, cap)
                n_more = (jnp.maximum(c1 - (a0 + WIN), 0) + WIN - 1) // WIN

                def per_window(k, carry2):
                    first = a0 + (k + 1) * WIN
                    aw = pl.multiple_of(jnp.minimum(first, cap - WIN), SLOT_ALIGN)
                    srow = slot_ref[0, e, pl.ds(pb, 1), :]
                    match = (srow == (sub + aw)) & (srow >= first)
                    onehot = jnp.where(match, 1.0, 0.0).astype(BF16)
                    y_s[...] += _dot_tn(onehot, eo_ref[0, e, pl.ds(aw, WIN), :])
                    return carry2

                return lax.fori_loop(0, n_more, per_window, carry)

            lax.fori_loop(0, n_exp, per_expert, 0)

        sl = slice(j * MXU_DIM, (j + 1) * MXU_DIM)
        o_ref[sl, :] = x1_ref[sl, :] + gate * _rms(y_s[...], g_ref[...])


def _combine(cnt2, eo, slots4, x1, mod, g_post):
    b, e, cap, d = eo.shape
    t = x1.shape[0]
    s = t // b
    nq = s // QP
    nblk_q = QP // MXU_DIM
    grid_spec = pltpu.PrefetchScalarGridSpec(
        num_scalar_prefetch=1,
        grid=(b, nq),
        in_specs=[
            pl.BlockSpec((1, e, cap, d), lambda bi, qi, c: (bi, 0, 0, 0), pipeline_mode=pl.Buffered(1)),
            pl.BlockSpec((1, e, s // MXU_DIM, MXU_DIM), lambda bi, qi, c: (bi, 0, 0, 0)),
            pl.BlockSpec((QP, d), lambda bi, qi, c: (bi * nq + qi, 0)),
            pl.BlockSpec((1, 6, d), lambda bi, qi, c: (bi, 0, 0)),
            pl.BlockSpec((1, d), lambda bi, qi, c: (0, 0)),
        ],
        out_specs=pl.BlockSpec((QP, d), lambda bi, qi, c: (bi * nq + qi, 0)),
        scratch_shapes=[
            pltpu.VMEM((e * WIN, d), BF16),
            pltpu.VMEM((e * WIN, MXU_DIM), BF16),
            pltpu.VMEM((MXU_DIM, d), F32),
        ],
    )
    return pl.pallas_call(
        functools.partial(_combine_kernel, e),
        grid_spec=grid_spec,
        out_shape=jax.ShapeDtypeStruct((t, d), F32),
        compiler_params=pltpu.CompilerParams(
            dimension_semantics=("arbitrary", "arbitrary"),
            vmem_limit_bytes=VMEM_LIMIT),
        name="combine",
    )(cnt2, eo, slots4, x1, mod, g_post)


def _layer(layer, x, c, w_ada_all, b_ada, norm_pre_mix, norm_post_mix, w_in, sink, sgu_ln_g,
           sgu_ln_b, w_s, b_s, norm_out_attn, norm_out_gmlp, w_out, norm_pre_ffn, norm_post_ffn,
           w_router, w_gate_all, w_up_all, w_down_all):
    b, s, d = x.shape
    t = b * s
    dg = sgu_ln_g.shape[0]
    e = w_router.shape[1]
    cap = CAPACITY_FACTOR * s // e
    assert s % QP == 0 and s % TM == 0 and cap % MXU_DIM == 0 and w_gate_all.shape[2] % FC == 0
    assert s // MXU_DIM < LANES and d == D_ATTN + dg

    mod = _adaln(c, w_ada_all, b_ada, layer).reshape(b, 6, d)

    kcol = w_in[:, D_ATTN:D_ATTN + D_KV]
    vcol = w_in[:, D_ATTN + D_KV:D_ATTN + 2 * D_KV]
    swap = lambda w: jnp.concatenate([w[:, HEAD_DIM:], w[:, :HEAD_DIM]], axis=1)
    w_ext = jnp.concatenate(
        [w_in[:, :D_ATTN], kcol, vcol, swap(kcol), swap(vcol), w_in[:, D_ATTN + 2 * D_KV:]],
        axis=1).astype(BF16)

    x2 = x.reshape(t, d)
    q, kv, u, vg = _proj(x2, mod, norm_pre_mix.reshape(1, d), w_ext, s)

    bias = jnp.asarray(_attn_bias_table())
    bs_full = jnp.repeat(b_s.T, dg // N_GMLP_GROUPS, axis=1)
    x1, h2, aff_t = _mix(
        x2, q, kv, u, vg, mod, sink, bias, w_s.astype(BF16), bs_full,
        sgu_ln_g.reshape(1, dg), sgu_ln_b.reshape(1, dg), norm_out_attn.reshape(1, D_ATTN),
        norm_out_gmlp.reshape(1, dg), w_out.astype(BF16), norm_post_mix.reshape(1, d),
        norm_pre_ffn.reshape(1, d), w_router.T.astype(BF16), b, s)

    slots, cnt = _route(aff_t, cap)
    nblk = s // MXU_DIM
    slots4 = slots.reshape(b, e, nblk, MXU_DIM)
    aff4 = aff_t.reshape(b, e, nblk, MXU_DIM)
    cnt2 = cnt.reshape(b * e, LANES)
    eo = _ffn(cnt2, h2.reshape(b, s, d), slots4, aff4, w_gate_all, w_up_all, w_down_all, cap,
              e, layer * e)
    out = _combine(cnt2, eo, slots4, x1, mod, norm_post_ffn.reshape(1, d))
    return out.reshape(b, s, d)


def kernel(x, c, w_ada, b_ada, norm_pre_mix, norm_post_mix, w_in, sink, sgu_ln_g, sgu_ln_b, w_s, b_s, norm_out_attn, norm_out_gmlp, w_out, norm_pre_ffn, norm_post_ffn, w_router, w_gate, w_up, w_down):
    depth, d = w_ada.shape[0], w_ada.shape[1]
    w_ada_all = w_ada.reshape(depth * d, w_ada.shape[2])
    stack = lambda w: w.reshape((depth * w.shape[1],) + w.shape[2:])
    w_gate_all, w_up_all, w_down_all = stack(w_gate), stack(w_up), stack(w_down)
    for l in range(depth):
        x = _layer(l, x, c, w_ada_all, b_ada[l], norm_pre_mix[l], norm_post_mix[l], w_in[l],
                   sink[l], sgu_ln_g[l], sgu_ln_b[l], w_s[l], b_s[l], norm_out_attn[l],
                   norm_out_gmlp[l], w_out[l], norm_pre_ffn[l], norm_post_ffn[l], w_router[l],
                   w_gate_all, w_up_all, w_down_all)
    return x
```

```python
import functools

import numpy as np
import jax
import jax.numpy as jnp
from jax import lax
from jax.experimental import pallas as pl
from jax.experimental.pallas import tpu as pltpu

N_HEADS = 8
N_KV_HEADS = 2
HEAD_DIM = 64
D_ATTN = N_HEADS * HEAD_DIM
D_KV = N_KV_HEADS * HEAD_DIM
WINDOW = 128
BLOCK = 128
N_GMLP_GROUPS = 8
N_EXPERTS = 16
CAPACITY_FACTOR = 2
EPS = 1e-6
MASK_VALUE = -1e30

LANES = 128
MXU_DIM = 256

TM = 512
FC = 256
EG = 4
QP = 1024
WIN = 64
SLOT_ALIGN = 16
VMEM_LIMIT = 56 * 1024 * 1024

BF16 = jnp.bfloat16
F32 = jnp.float32


def _rms(x, g):
    return x * lax.rsqrt(jnp.mean(x * x, axis=-1, keepdims=True) + EPS) * g


def _dot(a, b):
    return jnp.dot(a, b, preferred_element_type=F32)


def _dot_nt(a, b):
    return lax.dot_general(a, b, (((1,), (1,)), ((), ())), preferred_element_type=F32)


def _dot_tn(a, b):
    return lax.dot_general(a, b, (((0,), (0,)), ((), ())), preferred_element_type=F32)


def _adaln_kernel(c_ref, w_ref, b_ref, o_ref):
    c = c_ref[...]
    a = (c * jax.nn.sigmoid(c)).astype(BF16)
    o_ref[...] = _dot(a, w_ref[...].astype(BF16)) + b_ref[...]


def _adaln(c, w_ada_all, b_ada, layer):
    b, d = c.shape
    n = w_ada_all.shape[1]
    return pl.pallas_call(
        _adaln_kernel,
        grid=(n // d,),
        in_specs=[
            pl.BlockSpec((b, d), lambda j: (0, 0)),
            pl.BlockSpec((d, d), lambda j: (layer, j)),
            pl.BlockSpec((1, d), lambda j: (0, j)),
        ],
        out_specs=pl.BlockSpec((b, d), lambda j: (0, j)),
        out_shape=jax.ShapeDtypeStruct((b, n), F32),
        compiler_params=pltpu.CompilerParams(dimension_semantics=("arbitrary",)),
        name="adaln",
    )(c, w_ada_all, b_ada.reshape(1, n))


def _gelu_tanh(x):
    c = np.float32(np.sqrt(2.0 / np.pi))
    return x * (0.5 * (1.0 + jnp.tanh(c * (x + 0.044715 * (x * x * x)))))


def _proj_kernel(x_ref, mod_ref, g_ref, w_ref, q_ref, kv_ref, u_ref, vg_ref):
    x = x_ref[...]
    shift = mod_ref[0, 0:1, :]
    scale = mod_ref[0, 1:2, :]
    h = (_rms(x, g_ref[...]) * (1.0 + scale) + shift).astype(BF16)
    p = _dot(h, w_ref[...])
    q_ref[...] = (p[:, :D_ATTN] * (HEAD_DIM ** -0.5)).astype(BF16)
    kv_ref[...] = p[:, D_ATTN:2 * D_ATTN].astype(BF16)
    dg = u_ref.shape[1]
    z = _gelu_tanh(p[:, 2 * D_ATTN:])
    u_ref[...] = z[:, :dg]
    vg_ref[...] = z[:, dg:]


def _proj(x2, mod, g_pre, w_ext, s):
    t, d = x2.shape
    nw = w_ext.shape[1]
    dg = (nw - 2 * D_ATTN) // 2
    nsb = s // TM
    return pl.pallas_call(
        _proj_kernel,
        grid=(t // TM,),
        in_specs=[
            pl.BlockSpec((TM, d), lambda i: (i, 0)),
            pl.BlockSpec((1, 6, d), lambda i: (i // nsb, 0, 0)),
            pl.BlockSpec((1, d), lambda i: (0, 0)),
            pl.BlockSpec((d, nw), lambda i: (0, 0)),
        ],
        out_specs=[
            pl.BlockSpec((TM, D_ATTN), lambda i: (i, 0)),
            pl.BlockSpec((TM, D_ATTN), lambda i: (i, 0)),
            pl.BlockSpec((TM, dg), lambda i: (i, 0)),
            pl.BlockSpec((TM, dg), lambda i: (i, 0)),
        ],
        out_shape=[
            jax.ShapeDtypeStruct((t, D_ATTN), BF16),
            jax.ShapeDtypeStruct((t, D_ATTN), BF16),
            jax.ShapeDtypeStruct((t, dg), F32),
            jax.ShapeDtypeStruct((t, dg), F32),
        ],
        compiler_params=pltpu.CompilerParams(
            dimension_semantics=("arbitrary",), vmem_limit_bytes=VMEM_LIMIT),
        name="proj",
    )(x2, mod, g_pre, w_ext)


def _attn_bias_table():
    qi = np.arange(BLOCK)[:, None]
    kj = np.arange(3 * BLOCK)[None, :] - BLOCK
    dist = np.abs(qi - kj).astype(np.float32)
    band = dist <= WINDOW
    slopes = np.exp2(-8.0 * np.arange(1, N_HEADS + 1, dtype=np.float32) / N_HEADS).astype(np.float32)
    key_ok = [kj >= 0, np.ones_like(kj, bool), kj < BLOCK]
    tab = np.empty((3, N_HEADS, BLOCK, 3 * BLOCK), np.float32)
    for v in range(3):
        ok = band & key_ok[v]
        for h in range(N_HEADS):
            tab[v, h] = np.where(ok, -slopes[h] * dist, np.float32(MASK_VALUE))
    return tab


def _mix_kernel(nblk_seq, sink_ref, x_ref, q_ref, kvp_ref, kvc_ref, kvn_ref, u_ref, vg_ref,
                mod_ref, bias_ref, ws_ref, bs_ref, lng_ref, lnb_ref, ga_ref, gg_ref,
                wout_ref, gpost_ref, gpre2_ref, wr_ref,
                x1_ref, h2_ref, aff_ref, kcat, attn_s, gm_s):
    i = pl.program_id(0)
    nb_tile = TM // BLOCK
    kcat[0:BLOCK, :] = kvp_ref[...]
    kcat[BLOCK:BLOCK + TM, :] = kvc_ref[...]
    kcat[BLOCK + TM:, :] = kvn_ref[...]
    lane = lax.broadcasted_iota(jnp.int32, (1, LANES), 1)
    lo = lane < HEAD_DIM
    zero = jnp.zeros((), BF16)

    def block_body(a, carry):
        r0 = pl.multiple_of(a * BLOCK, BLOCK)
        pos = (i * nb_tile + a) % nblk_seq
        var = jnp.where(pos == 0, 0, jnp.where(pos == nblk_seq - 1, 2, 1))
        kv = kcat[pl.ds(r0, 3 * BLOCK), :]
        k, v = kv[:, 0:LANES], kv[:, LANES:2 * LANES]
        ksw, vsw = kv[:, 2 * LANES:3 * LANES], kv[:, 3 * LANES:4 * LANES]
        kvar = ((jnp.where(lo, k, zero), jnp.where(lo, zero, ksw)),
                (jnp.where(lo, ksw, zero), jnp.where(lo, zero, k)))
        vvar = ((v, vsw), (vsw, v))
        for pi in range(N_HEADS // 2):
            qp = q_ref[pl.ds(r0, BLOCK), pi * LANES:(pi + 1) * LANES]
            halves = []
            for half in range(2):
                h = 2 * pi + half
                kvh = h // (N_HEADS // N_KV_HEADS)
                s = _dot_nt(qp, kvar[kvh][half]) + bias_ref[var, h]
                sink = sink_ref[h]
                m = jnp.maximum(jnp.max(s, axis=-1, keepdims=True), sink)
                p = jnp.exp(s - m)
                den = jnp.sum(p, axis=-1, keepdims=True) + jnp.exp(sink - m)
                o = _dot(p.astype(BF16), vvar[kvh][half])
                halves.append(o / den)
            attn_s[pl.ds(r0, BLOCK), pi * LANES:(pi + 1) * LANES] = jnp.where(lo, halves[0], halves[1])
        vg = vg_ref[pl.ds(r0, BLOCK), :]
        mu = jnp.mean(vg, axis=-1, keepdims=True)
        xc = vg - mu
        vn = (xc * lax.rsqrt(jnp.mean(xc * xc, axis=-1, keepdims=True) + EPS) * lng_ref[...]
              + lnb_ref[...]).astype(BF16)
        for pi in range(N_GMLP_GROUPS // 2):
            vnp = vn[:, pi * LANES:(pi + 1) * LANES]
            z = jnp.where(lo, _dot(ws_ref[2 * pi], vnp), _dot(ws_ref[2 * pi + 1], vnp))
            sl = slice(pi * LANES, (pi + 1) * LANES)
            gm_s[pl.ds(r0, BLOCK), sl] = u_ref[pl.ds(r0, BLOCK), sl] * (z + bs_ref[:, sl])
        return carry

    lax.fori_loop(0, nb_tile, block_body, 0)

    na = _rms(attn_s[...], ga_ref[...]).astype(BF16)
    ng = _rms(gm_s[...], gg_ref[...]).astype(BF16)
    mo = _dot(na, wout_ref[0:D_ATTN, :]) + _dot(ng, wout_ref[D_ATTN:, :])
    gate1 = mod_ref[0, 2:3, :]
    x1 = x_ref[...] + gate1 * _rms(mo, gpost_ref[...])
    x1_ref[...] = x1
    h2 = (_rms(x1, gpre2_ref[...]) * (1.0 + mod_ref[0, 4:5, :]) + mod_ref[0, 3:4, :]).astype(BF16)
    h2_ref[...] = h2
    lg = _dot_nt(wr_ref[...], h2)
    lg = lg - jnp.max(lg, axis=0, keepdims=True)
    ex = jnp.exp(lg)
    aff_ref[0] = ex / jnp.sum(ex, axis=0, keepdims=True)


def _mix(x2, q, kv, u, vg, mod, sink, bias, ws, bs_full, ln_g, ln_b, g_attn, g_gmlp,
         w_out, g_post, g_pre2, w_r_t, b, s):
    t, d = x2.shape
    dg = u.shape[1]
    e = w_r_t.shape[0]
    nsb = s // TM
    nb_tile = TM // BLOCK
    nblk = t // BLOCK
    row = lambda i: (i, 0)
    const2 = lambda i: (0, 0)
    in_specs = [
        pl.BlockSpec(memory_space=pltpu.SMEM),
        pl.BlockSpec((TM, d), row),
        pl.BlockSpec((TM, D_ATTN), row),
        pl.BlockSpec((BLOCK, 4 * LANES), lambda i: (jnp.maximum(i * nb_tile - 1, 0), 0)),
        pl.BlockSpec((TM, 4 * LANES), row),
        pl.BlockSpec((BLOCK, 4 * LANES), lambda i: (jnp.minimum((i + 1) * nb_tile, nblk - 1), 0)),
        pl.BlockSpec((TM, dg), row),
        pl.BlockSpec((TM, dg), row),
        pl.BlockSpec((1, 6, d), lambda i: (i // nsb, 0, 0)),
        pl.BlockSpec(bias.shape, lambda i: (0, 0, 0, 0)),
        pl.BlockSpec(ws.shape, lambda i: (0, 0, 0)),
        pl.BlockSpec((BLOCK, dg), const2),
        pl.BlockSpec((1, dg), const2),
        pl.BlockSpec((1, dg), const2),
        pl.BlockSpec((1, D_ATTN), const2),
        pl.BlockSpec((1, dg), const2),
        pl.BlockSpec(w_out.shape, const2),
        pl.BlockSpec((1, d), const2),
        pl.BlockSpec((1, d), const2),
        pl.BlockSpec(w_r_t.shape, const2),
    ]
    out_specs = [
        pl.BlockSpec((TM, d), row),
        pl.BlockSpec((TM, d), row),
        pl.BlockSpec((1, e, TM), lambda i: (i // nsb, 0, i % nsb)),
    ]
    out_shape = [
        jax.ShapeDtypeStruct((t, d), F32),
        jax.ShapeDtypeStruct((t, d), BF16),
        jax.ShapeDtypeStruct((b, e, s), F32),
    ]
    return pl.pallas_call(
        functools.partial(_mix_kernel, s // BLOCK),
        grid=(t // TM,),
        in_specs=in_specs,
        out_specs=out_specs,
        out_shape=out_shape,
        scratch_shapes=[
            pltpu.VMEM((TM + 2 * BLOCK, 4 * LANES), BF16),
            pltpu.VMEM((TM, D_ATTN), F32),
            pltpu.VMEM((TM, dg), F32),
        ],
        compiler_params=pltpu.CompilerParams(
            dimension_semantics=("arbitrary",), vmem_limit_bytes=VMEM_LIMIT),
        name="mix",
    )(sink, x2, q, kv, kv, kv, u, vg, mod, bias, ws, bs_full, ln_g, ln_b, g_attn, g_gmlp,
      w_out, g_post, g_pre2, w_r_t)


def _route_kernel(cap, aff_ref, slot_ref, cnt_ref):
    aff = aff_ref[0]
    e, s = aff.shape
    capf = jnp.float32(cap)

    def count_ge(t):
        return jnp.sum(jnp.where(aff >= t, 1.0, 0.0), axis=1, keepdims=True)

    def bit_body(it, cand):
        trial = cand | jnp.left_shift(jnp.int32(1), 30 - it)
        t = lax.bitcast_convert_type(trial, F32)
        return jnp.where(count_ge(t) >= capf, trial, cand)

    cand = lax.fori_loop(0, 31, bit_body, jnp.zeros((e, 1), jnp.int32))
    thr = lax.bitcast_convert_type(cand, F32)
    need = capf - jnp.sum(jnp.where(aff > thr, 1.0, 0.0), axis=1, keepdims=True)

    r = lax.broadcasted_iota(jnp.int32, (LANES, LANES), 0)
    c = lax.broadcasted_iota(jnp.int32, (LANES, LANES), 1)
    tri = jnp.where(r < c, 1.0, 0.0).astype(BF16)
    lane = lax.broadcasted_iota(jnp.int32, (e, LANES), 1)
    carry_eq = jnp.zeros((e, 1), F32)
    carry_sel = jnp.zeros((e, 1), F32)
    cnt = jnp.zeros((e, LANES), F32)
    per_blk = MXU_DIM // LANES
    for j in range(s // LANES):
        a = aff[:, j * LANES:(j + 1) * LANES]
        gt = a > thr
        eq = jnp.where(a == thr, 1.0, 0.0)
        tie_rank = _dot(eq.astype(BF16), tri) + carry_eq
        sel = jnp.where(gt | ((eq > 0.0) & (tie_rank < need)), 1.0, 0.0)
        slot = _dot(sel.astype(BF16), tri) + carry_sel
        slot_ref[0, :, j * LANES:(j + 1) * LANES] = jnp.where(sel > 0.0, slot, -1.0).astype(jnp.int32)
        if j % per_blk == 0:
            cnt = jnp.where(lane == j // per_blk, carry_sel, cnt)
        carry_eq = carry_eq + jnp.sum(eq, axis=1, keepdims=True)
        carry_sel = carry_sel + jnp.sum(sel, axis=1, keepdims=True)
    cnt = jnp.where(lane >= s // MXU_DIM, carry_sel, cnt)
    cnt_ref[0] = cnt.astype(jnp.int32)


def _route(aff_t, cap):
    b, e, s = aff_t.shape
    return pl.pallas_call(
        functools.partial(_route_kernel, cap),
        grid=(b,),
        in_specs=[pl.BlockSpec((1, e, s), lambda i: (i, 0, 0))],
        out_specs=[
            pl.BlockSpec((1, e, s), lambda i: (i, 0, 0)),
            pl.BlockSpec((1, e, LANES), lambda i: (i, 0, 0)),
        ],
        out_shape=[
            jax.ShapeDtypeStruct((b, e, s), jnp.int32),
            jax.ShapeDtypeStruct((b, e, LANES), jnp.int32),
        ],
        compiler_params=pltpu.CompilerParams(dimension_semantics=("arbitrary",)),
        name="route",
    )(aff_t)


def _segment(cnt_ref, row, pb, cap):
    c0 = jnp.minimum(cnt_ref[row, pb], cap)
    c1 = jnp.minimum(cnt_ref[row, pb + 1], cap)
    a0 = jnp.minimum(jnp.bitwise_and(c0, -SLOT_ALIGN), cap - WIN)
    return c0, c1, pl.multiple_of(a0, SLOT_ALIGN)


def _extra_windows(c1, a0):
    return (jnp.maximum(c1 - (a0 + WIN), 0) + WIN - 1) // WIN


def _gather_kernel(n_exp, cnt_ref, h2_ref, slot_ref, aff_ref, x_ref, gs_ref, oh_s):
    b, eg = pl.program_id(0), pl.program_id(1)
    ng, cap = x_ref.shape[1], x_ref.shape[2]
    nblk = slot_ref.shape[2]
    sub = lax.broadcasted_iota(jnp.int32, (WIN, MXU_DIM), 0)
    sub_col = lax.broadcasted_iota(jnp.int32, (WIN, 1), 0)
    x_ref[...] = jnp.zeros_like(x_ref)
    gs_ref[...] = jnp.zeros_like(gs_ref)

    def merge(k, a0, first, last, rows, gate):
        slot_id = sub_col + a0
        keep = (slot_id >= first) & (slot_id < last)
        win = (0, k, pl.ds(a0, WIN), slice(None))
        x_ref[win] = jnp.where(keep, rows.astype(BF16), x_ref[win])
        gs_ref[win] = jnp.where(keep, gate, gs_ref[win])

    def pb_body(pb, carry):
        tok = pl.ds(pl.multiple_of(pb * MXU_DIM, MXU_DIM), MXU_DIM)
        segs, gates = [], []
        overflow = jnp.int32(0)
        for k in range(ng):
            c0, c1, a0 = _segment(cnt_ref, b * n_exp + eg * ng + k, pb, cap)
            match = slot_ref[0, k, pl.ds(pb, 1), :] == (sub + a0)
            oh_s[k * WIN:(k + 1) * WIN, :] = jnp.where(match, 1.0, 0.0).astype(BF16)
            gates.append(jnp.sum(jnp.where(match, aff_ref[0, k, pl.ds(pb, 1), :], 0.0),
                                 axis=1, keepdims=True))
            segs.append((c0, c1, a0))
            overflow = overflow + jnp.maximum(c1 - (a0 + WIN), 0)
        rows = _dot(oh_s[...], h2_ref[0, tok, :])
        for k in range(ng):
            c0, c1, a0 = segs[k]
            merge(k, a0, c0, c1, rows[k * WIN:(k + 1) * WIN, :], gates[k])

        @pl.when(overflow > 0)
        def _long_segments():
            def per_expert(k, carry2):
                _, c1, a0 = _segment(cnt_ref, b * n_exp + eg * ng + k, pb, cap)

                def per_window(w, carry3):
                    first = a0 + (w + 1) * WIN
                    aw = pl.multiple_of(jnp.minimum(first, cap - WIN), SLOT_ALIGN)
                    srow = slot_ref[0, k, pl.ds(pb, 1), :]
                    match = (srow == (sub + aw)) & (srow >= first)
                    onehot = jnp.where(match, 1.0, 0.0).astype(BF16)
                    gate = jnp.sum(jnp.where(match, aff_ref[0, k, pl.ds(pb, 1), :], 0.0),
                                   axis=1, keepdims=True)
                    merge(k, aw, first, c1, _dot(onehot, h2_ref[0, tok, :]), gate)
                    return carry3

                return lax.fori_loop(0, _extra_windows(c1, a0), per_window, carry2)

            lax.fori_loop(0, ng, per_expert, 0)

        return carry

    lax.fori_loop(0, nblk, pb_body, 0)


def _gather(cnt2, h2, slots4, aff4, cap):
    b, s, d = h2.shape
    e = slots4.shape[1]
    nblk = s // MXU_DIM
    grid_spec = pltpu.PrefetchScalarGridSpec(
        num_scalar_prefetch=1,
        grid=(b, e // EG),
        in_specs=[
            pl.BlockSpec((1, s, d), lambda bi, gi, c: (bi, 0, 0), pipeline_mode=pl.Buffered(1)),
            pl.BlockSpec((1, EG, nblk, MXU_DIM), lambda bi, gi, c: (bi, gi, 0, 0)),
            pl.BlockSpec((1, EG, nblk, MXU_DIM), lambda bi, gi, c: (bi, gi, 0, 0)),
        ],
        out_specs=[
            pl.BlockSpec((1, EG, cap, d), lambda bi, gi, c: (bi, gi, 0, 0)),
            pl.BlockSpec((1, EG, cap, LANES), lambda bi, gi, c: (bi, gi, 0, 0)),
        ],
        scratch_shapes=[pltpu.VMEM((EG * WIN, MXU_DIM), BF16)],
    )
    return pl.pallas_call(
        functools.partial(_gather_kernel, e),
        grid_spec=grid_spec,
        out_shape=[
            jax.ShapeDtypeStruct((b, e, cap, d), BF16),
            jax.ShapeDtypeStruct((b, e, cap, LANES), F32),
        ],
        compiler_params=pltpu.CompilerParams(
            dimension_semantics=("arbitrary", "arbitrary"), vmem_limit_bytes=VMEM_LIMIT),
        name="gather",
    )(cnt2, h2, slots4, aff4)


def _ffn_kernel(x_ref, gs_ref, wg_ref, wu_ref, wd_ref, out_ref, acc_s):
    f = pl.program_id(1)
    nb, _, cap, d = x_ref.shape

    @pl.when(f == 0)
    def _init():
        acc_s[...] = jnp.zeros_like(acc_s)

    x = x_ref[:, 0].reshape(nb * cap, d)
    g = _dot(x, wg_ref[0].astype(BF16))
    u = _dot(x, wu_ref[0].astype(BF16))
    hid = (g * jax.nn.sigmoid(g) * u).astype(BF16)
    acc_s[...] += _dot(hid, wd_ref[0].astype(BF16))

    @pl.when(f == pl.num_programs(1) - 1)
    def _emit():
        gate = gs_ref[:, 0, :, 0:1].reshape(nb * cap, 1)
        out_ref[:, 0] = (acc_s[...] * gate).reshape(nb, cap, d).astype(BF16)


def _ffn(xg, gs, w_gate, w_up, w_down, e0):
    b, e, cap, d = xg.shape
    ff = w_gate.shape[2]
    return pl.pallas_call(
        _ffn_kernel,
        grid=(e, ff // FC),
        in_specs=[
            pl.BlockSpec((b, 1, cap, d), lambda ei, fi: (0, ei, 0, 0)),
            pl.BlockSpec((b, 1, cap, LANES), lambda ei, fi: (0, ei, 0, 0)),
            pl.BlockSpec((1, d, FC), lambda ei, fi: (e0 + ei, 0, fi)),
            pl.BlockSpec((1, d, FC), lambda ei, fi: (e0 + ei, 0, fi)),
            pl.BlockSpec((1, FC, d), lambda ei, fi: (e0 + ei, fi, 0)),
        ],
        out_specs=pl.BlockSpec((b, 1, cap, d), lambda ei, fi: (0, ei, 0, 0)),
        out_shape=jax.ShapeDtypeStruct((b, e, cap, d), BF16),
        scratch_shapes=[pltpu.VMEM((b * cap, d), F32)],
        compiler_params=pltpu.CompilerParams(
            dimension_semantics=("arbitrary", "arbitrary"), vmem_limit_bytes=VMEM_LIMIT),
        name="ffn",
    )(xg, gs, w_gate, w_up, w_down)


def _combine_kernel(n_exp, cnt_ref, eo_ref, slot_ref, x1_ref, mod_ref, g_ref, o_ref,
                    stage_s, oh_s, y_s):
    b, q = pl.program_id(0), pl.program_id(1)
    cap = eo_ref.shape[2]
    nblk_q = o_ref.shape[0] // MXU_DIM
    sub = lax.broadcasted_iota(jnp.int32, (WIN, MXU_DIM), 0)
    gate = mod_ref[0, 5:6, :]

    for j in range(nblk_q):
        pb = q * nblk_q + j
        overflow = jnp.int32(0)
        for e in range(n_exp):
            _, c1, a0 = _segment(cnt_ref, b * n_exp + e, pb, cap)
            stage_s[e * WIN:(e + 1) * WIN, :] = eo_ref[0, e, pl.ds(a0, WIN), :]
            match = slot_ref[0, e, pl.ds(pb, 1), :] == (sub + a0)
            oh_s[e * WIN:(e + 1) * WIN, :] = jnp.where(match, 1.0, 0.0).astype(BF16)
            overflow = overflow + jnp.maximum(c1 - (a0 + WIN), 0)
        y_s[...] = _dot_tn(oh_s[...], stage_s[...])

        @pl.when(overflow > 0)
        def _long_segments(j=j, pb=pb):
            def per_expert(e, carry):
                _, c1, a0 = _segment(cnt_ref, b * n_exp + e, pb, cap)

                def per_window(k, carry2):
                    first = a0 + (k + 1) * WIN
                    aw = pl.multiple_of(jnp.minimum(first, cap - WIN), SLOT_ALIGN)
                    srow = slot_ref[0, e, pl.ds(pb, 1), :]
                    match = (srow == (sub + aw)) & (srow >= first)
                    onehot = jnp.where(match, 1.0, 0.0).astype(BF16)
                    y_s[...] += _dot_tn(onehot, eo_ref[0, e, pl.ds(aw, WIN), :])
                    return carry2

                return lax.fori_loop(0, _extra_windows(c1, a0), per_window, carry)

            lax.fori_loop(0, n_exp, per_expert, 0)

        sl = slice(j * MXU_DIM, (j + 1) * MXU_DIM)
        o_ref[sl, :] = x1_ref[sl, :] + gate * _rms(y_s[...], g_ref[...])


def _combine(cnt2, eo, slots4, x1, mod, g_post):
    b, e, cap, d = eo.shape
    t = x1.shape[0]
    s = t // b
    nq = s // QP
    nblk_q = QP // MXU_DIM
    grid_spec = pltpu.PrefetchScalarGridSpec(
        num_scalar_prefetch=1,
        grid=(b, nq),
        in_specs=[
            pl.BlockSpec((1, e, cap, d), lambda bi, qi, c: (bi, 0, 0, 0), pipeline_mode=pl.Buffered(1)),
            pl.BlockSpec((1, e, s // MXU_DIM, MXU_DIM), lambda bi, qi, c: (bi, 0, 0, 0)),
            pl.BlockSpec((QP, d), lambda bi, qi, c: (bi * nq + qi, 0)),
            pl.BlockSpec((1, 6, d), lambda bi, qi, c: (bi, 0, 0)),
            pl.BlockSpec((1, d), lambda bi, qi, c: (0, 0)),
        ],
        out_specs=pl.BlockSpec((QP, d), lambda bi, qi, c: (bi * nq + qi, 0)),
        scratch_shapes=[
            pltpu.VMEM((e * WIN, d), BF16),
            pltpu.VMEM((e * WIN, MXU_DIM), BF16),
            pltpu.VMEM((MXU_DIM, d), F32),
        ],
    )
    return pl.pallas_call(
        functools.partial(_combine_kernel, e),
        grid_spec=grid_spec,
        out_shape=jax.ShapeDtypeStruct((t, d), F32),
        compiler_params=pltpu.CompilerParams(
            dimension_semantics=("arbitrary", "arbitrary"),
            vmem_limit_bytes=VMEM_LIMIT),
        name="combine",
    )(cnt2, eo, slots4, x1, mod, g_post)


def _layer(layer, x, c, w_ada_all, b_ada, norm_pre_mix, norm_post_mix, w_in, sink, sgu_ln_g,
           sgu_ln_b, w_s, b_s, norm_out_attn, norm_out_gmlp, w_out, norm_pre_ffn, norm_post_ffn,
           w_router, w_gate_all, w_up_all, w_down_all):
    b, s, d = x.shape
    t = b * s
    dg = sgu_ln_g.shape[0]
    e = w_router.shape[1]
    cap = CAPACITY_FACTOR * s // e
    assert s % QP == 0 and s % TM == 0 and w_gate_all.shape[2] % FC == 0 and e % EG == 0
    assert cap % SLOT_ALIGN == 0 and cap >= WIN and s // MXU_DIM < LANES and d == D_ATTN + dg

    mod = _adaln(c, w_ada_all, b_ada, layer).reshape(b, 6, d)

    kcol = w_in[:, D_ATTN:D_ATTN + D_KV]
    vcol = w_in[:, D_ATTN + D_KV:D_ATTN + 2 * D_KV]
    swap = lambda w: jnp.concatenate([w[:, HEAD_DIM:], w[:, :HEAD_DIM]], axis=1)
    w_ext = jnp.concatenate(
        [w_in[:, :D_ATTN], kcol, vcol, swap(kcol), swap(vcol), w_in[:, D_ATTN + 2 * D_KV:]],
        axis=1).astype(BF16)

    x2 = x.reshape(t, d)
    q, kv, u, vg = _proj(x2, mod, norm_pre_mix.reshape(1, d), w_ext, s)

    bias = jnp.asarray(_attn_bias_table())
    bs_full = jnp.repeat(b_s.T, dg // N_GMLP_GROUPS, axis=1)
    x1, h2, aff_t = _mix(
        x2, q, kv, u, vg, mod, sink, bias, w_s.astype(BF16), bs_full,
        sgu_ln_g.reshape(1, dg), sgu_ln_b.reshape(1, dg), norm_out_attn.reshape(1, D_ATTN),
        norm_out_gmlp.reshape(1, dg), w_out.astype(BF16), norm_post_mix.reshape(1, d),
        norm_pre_ffn.reshape(1, d), w_router.T.astype(BF16), b, s)

    slots, cnt = _route(aff_t, cap)
    nblk = s // MXU_DIM
    slots4 = slots.reshape(b, e, nblk, MXU_DIM)
    aff4 = aff_t.reshape(b, e, nblk, MXU_DIM)
    cnt2 = cnt.reshape(b * e, LANES)
    xg, gs = _gather(cnt2, h2.reshape(b, s, d), slots4, aff4, cap)
    eo = _ffn(xg, gs, w_gate_all, w_up_all, w_down_all, layer * e)
    out = _combine(cnt2, eo, slots4, x1, mod, norm_post_ffn.reshape(1, d))
    return out.reshape(b, s, d)


def kernel(x, c, w_ada, b_ada, norm_pre_mix, norm_post_mix, w_in, sink, sgu_ln_g, sgu_ln_b, w_s, b_s, norm_out_attn, norm_out_gmlp, w_out, norm_pre_ffn, norm_post_ffn, w_router, w_gate, w_up, w_down):
    depth, d = w_ada.shape[0], w_ada.shape[1]
    w_ada_all = w_ada.reshape(depth * d, w_ada.shape[2])
    stack = lambda w: w.reshape((depth * w.shape[1],) + w.shape[2:])
    w_gate_all, w_up_all, w_down_all = stack(w_gate), stack(w_up), stack(w_down)
    for l in range(depth):
        x = _layer(l, x, c, w_ada_all, b_ada[l], norm_pre_mix[l], norm_post_mix[l], w_in[l],
                   sink[l], sgu_ln_g[l], sgu_ln_b[l], w_s[l], b_s[l], norm_out_attn[l],
                   norm_out_gmlp[l], w_out[l], norm_pre_ffn[l], norm_post_ffn[l], w_router[l],
                   w_gate_all, w_up_all, w_down_all)
    return x
```

```python
import functools

import numpy as np
import jax
import jax.numpy as jnp
from jax import lax
from jax.experimental import pallas as pl
from jax.experimental.pallas import tpu as pltpu

N_HEADS = 8
N_KV_HEADS = 2
HEAD_DIM = 64
D_ATTN = N_HEADS * HEAD_DIM
D_KV = N_KV_HEADS * HEAD_DIM
GROUP = N_HEADS // N_KV_HEADS
WINDOW = 128
BLOCK = 128
N_GMLP_GROUPS = 8
N_EXPERTS = 16
CAPACITY_FACTOR = 2
EPS = 1e-6
MASK_VALUE = -1e30

LANES = 128
MXU_DIM = 256

TM = 512
FC = 256
EG = 4
QP = 1024
WIN = 64
SLOT_ALIGN = 16
VMEM_LIMIT = 56 * 1024 * 1024

BF16 = jnp.bfloat16
F32 = jnp.float32


def _rms(x, g):
    return x * lax.rsqrt(jnp.mean(x * x, axis=-1, keepdims=True) + EPS) * g


def _dot(a, b):
    return jnp.dot(a, b, preferred_element_type=F32)


def _dot_nt(a, b):
    return lax.dot_general(a, b, (((1,), (1,)), ((), ())), preferred_element_type=F32)


def _dot_tn(a, b):
    return lax.dot_general(a, b, (((0,), (0,)), ((), ())), preferred_element_type=F32)


def _adaln_kernel(c_ref, w_ref, b_ref, o_ref):
    c = c_ref[...]
    a = (c * jax.nn.sigmoid(c)).astype(BF16)
    o_ref[...] = _dot(a, w_ref[...].astype(BF16)) + b_ref[...]


def _adaln(c, w_ada_all, b_ada, layer):
    b, d = c.shape
    n = w_ada_all.shape[1]
    return pl.pallas_call(
        _adaln_kernel,
        grid=(n // d,),
        in_specs=[
            pl.BlockSpec((b, d), lambda j: (0, 0)),
            pl.BlockSpec((d, d), lambda j: (layer, j)),
            pl.BlockSpec((1, d), lambda j: (0, j)),
        ],
        out_specs=pl.BlockSpec((b, d), lambda j: (0, j)),
        out_shape=jax.ShapeDtypeStruct((b, n), F32),
        compiler_params=pltpu.CompilerParams(dimension_semantics=("arbitrary",)),
        name="adaln",
    )(c, w_ada_all, b_ada.reshape(1, n))


def _gelu_tanh(x):
    c = np.float32(np.sqrt(2.0 / np.pi))
    return x * (0.5 * (1.0 + jnp.tanh(c * (x + 0.044715 * (x * x * x)))))


def _proj_kernel(x_ref, mod_ref, g_ref, w_ref, q_ref, kv_ref, u_ref, vg_ref):
    x = x_ref[...]
    shift = mod_ref[0, 0:1, :]
    scale = mod_ref[0, 1:2, :]
    h = (_rms(x, g_ref[...]) * (1.0 + scale) + shift).astype(BF16)
    p = _dot(h, w_ref[...])
    q_ref[...] = (p[:, :D_ATTN] * (HEAD_DIM ** -0.5)).astype(BF16)
    kv_ref[...] = p[:, D_ATTN:2 * D_ATTN].astype(BF16)
    dg = u_ref.shape[1]
    z = _gelu_tanh(p[:, 2 * D_ATTN:])
    u_ref[...] = z[:, :dg]
    vg_ref[...] = z[:, dg:]


def _proj(x2, mod, g_pre, w_ext, s):
    t, d = x2.shape
    nw = w_ext.shape[1]
    dg = (nw - 2 * D_ATTN) // 2
    nsb = s // TM
    return pl.pallas_call(
        _proj_kernel,
        grid=(t // TM,),
        in_specs=[
            pl.BlockSpec((TM, d), lambda i: (i, 0)),
            pl.BlockSpec((1, 6, d), lambda i: (i // nsb, 0, 0)),
            pl.BlockSpec((1, d), lambda i: (0, 0)),
            pl.BlockSpec((d, nw), lambda i: (0, 0)),
        ],
        out_specs=[
            pl.BlockSpec((TM, D_ATTN), lambda i: (i, 0)),
            pl.BlockSpec((TM, D_ATTN), lambda i: (i, 0)),
            pl.BlockSpec((TM, dg), lambda i: (i, 0)),
            pl.BlockSpec((TM, dg), lambda i: (i, 0)),
        ],
        out_shape=[
            jax.ShapeDtypeStruct((t, D_ATTN), BF16),
            jax.ShapeDtypeStruct((t, D_ATTN), BF16),
            jax.ShapeDtypeStruct((t, dg), F32),
            jax.ShapeDtypeStruct((t, dg), F32),
        ],
        compiler_params=pltpu.CompilerParams(
            dimension_semantics=("arbitrary",), vmem_limit_bytes=VMEM_LIMIT),
        name="proj",
    )(x2, mod, g_pre, w_ext)


def _attn_bias_table():
    qi = np.arange(BLOCK)[:, None]
    kj = np.arange(3 * BLOCK)[None, :] - BLOCK
    dist = np.abs(qi - kj).astype(np.float32)
    band = dist <= WINDOW
    slopes = np.exp2(-8.0 * np.arange(1, N_HEADS + 1, dtype=np.float32) / N_HEADS).astype(np.float32)
    key_ok = [kj >= 0, np.ones_like(kj, bool), kj < BLOCK]
    tab = np.empty((3, N_HEADS, BLOCK, 3 * BLOCK), np.float32)
    for v in range(3):
        ok = band & key_ok[v]
        for h in range(N_HEADS):
            tab[v, h] = np.where(ok, -slopes[h] * dist, np.float32(MASK_VALUE))
    return tab.reshape(3, N_KV_HEADS, GROUP * BLOCK, 3 * BLOCK)


def _mix_kernel(nblk_seq, sink_ref, x_ref, q_ref, kvp_ref, kvc_ref, kvn_ref, u_ref, vg_ref,
                mod_ref, bias_ref, ws_ref, bs_ref, lng_ref, lnb_ref, ga_ref, gg_ref,
                wout_ref, gpost_ref, gpre2_ref, wr_ref,
                x1_ref, h2_ref, aff_ref, kcat, attn_s, gm_s):
    i = pl.program_id(0)
    nb_tile = TM // BLOCK
    kcat[0:BLOCK, :] = kvp_ref[...]
    kcat[BLOCK:BLOCK + TM, :] = kvc_ref[...]
    kcat[BLOCK + TM:, :] = kvn_ref[...]
    lane = lax.broadcasted_iota(jnp.int32, (1, LANES), 1)
    lo = lane < HEAD_DIM
    zero = jnp.zeros((), BF16)
    head_of_row = lax.broadcasted_iota(jnp.int32, (GROUP * BLOCK, 1), 0) // BLOCK
    sinks = []
    for g in range(N_KV_HEADS):
        col = jnp.zeros((GROUP * BLOCK, 1), F32)
        for j in range(GROUP):
            col = jnp.where(head_of_row == j, sink_ref[g * GROUP + j], col)
        sinks.append(col)

    def block_body(a, carry):
        r0 = pl.multiple_of(a * BLOCK, BLOCK)
        pos = (i * nb_tile + a) % nblk_seq
        var = jnp.where(pos == 0, 0, jnp.where(pos == nblk_seq - 1, 2, 1))
        kv = kcat[pl.ds(r0, 3 * BLOCK), :]
        for g in range(N_KV_HEADS):
            k = kv[:, g * LANES:(g + 1) * LANES]
            v = kv[:, (N_KV_HEADS + g) * LANES:(N_KV_HEADS + g + 1) * LANES]
            pieces = []
            for pi in range(GROUP // 2):
                qp = q_ref[pl.ds(r0, BLOCK), (g * GROUP // 2 + pi) * LANES:(g * GROUP // 2 + pi + 1) * LANES]
                pieces += [jnp.where(lo, qp, zero), jnp.where(lo, zero, qp)]
            s = _dot_nt(jnp.concatenate(pieces, axis=0), k) + bias_ref[var, g]
            m = jnp.maximum(jnp.max(s, axis=-1, keepdims=True), sinks[g])
            p = jnp.exp(s - m)
            den = jnp.sum(p, axis=-1, keepdims=True) + jnp.exp(sinks[g] - m)
            o = _dot(p.astype(BF16), v) / den
            for pi in range(GROUP // 2):
                even = o[(2 * pi) * BLOCK:(2 * pi + 1) * BLOCK, :]
                odd = o[(2 * pi + 1) * BLOCK:(2 * pi + 2) * BLOCK, :]
                col0 = (g * GROUP // 2 + pi) * LANES
                attn_s[pl.ds(r0, BLOCK), col0:col0 + LANES] = jnp.where(lo, even, odd)
        vg = vg_ref[pl.ds(r0, BLOCK), :]
        mu = jnp.mean(vg, axis=-1, keepdims=True)
        xc = vg - mu
        vn = (xc * lax.rsqrt(jnp.mean(xc * xc, axis=-1, keepdims=True) + EPS) * lng_ref[...]
              + lnb_ref[...]).astype(BF16)
        for pi in range(N_GMLP_GROUPS // 2):
            vnp = vn[:, pi * LANES:(pi + 1) * LANES]
            z = jnp.where(lo, _dot(ws_ref[2 * pi], vnp), _dot(ws_ref[2 * pi + 1], vnp))
            sl = slice(pi * LANES, (pi + 1) * LANES)
            gm_s[pl.ds(r0, BLOCK), sl] = u_ref[pl.ds(r0, BLOCK), sl] * (z + bs_ref[:, sl])
        return carry

    lax.fori_loop(0, nb_tile, block_body, 0)

    na = _rms(attn_s[...], ga_ref[...]).astype(BF16)
    ng = _rms(gm_s[...], gg_ref[...]).astype(BF16)
    mo = _dot(na, wout_ref[0:D_ATTN, :]) + _dot(ng, wout_ref[D_ATTN:, :])
    gate1 = mod_ref[0, 2:3, :]
    x1 = x_ref[...] + gate1 * _rms(mo, gpost_ref[...])
    x1_ref[...] = x1
    h2 = (_rms(x1, gpre2_ref[...]) * (1.0 + mod_ref[0, 4:5, :]) + mod_ref[0, 3:4, :]).astype(BF16)
    h2_ref[...] = h2
    lg = _dot_nt(wr_ref[...], h2)
    lg = lg - jnp.max(lg, axis=0, keepdims=True)
    ex = jnp.exp(lg)
    aff_ref[0] = ex / jnp.sum(ex, axis=0, keepdims=True)


def _mix(x2, q, kv, u, vg, mod, sink, bias, ws, bs_full, ln_g, ln_b, g_attn, g_gmlp,
         w_out, g_post, g_pre2, w_r_t, b, s):
    t, d = x2.shape
    dg = u.shape[1]
    e = w_r_t.shape[0]
    nsb = s // TM
    nb_tile = TM // BLOCK
    nblk = t // BLOCK
    row = lambda i: (i, 0)
    const2 = lambda i: (0, 0)
    in_specs = [
        pl.BlockSpec(memory_space=pltpu.SMEM),
        pl.BlockSpec((TM, d), row),
        pl.BlockSpec((TM, D_ATTN), row),
        pl.BlockSpec((BLOCK, 4 * LANES), lambda i: (jnp.maximum(i * nb_tile - 1, 0), 0)),
        pl.BlockSpec((TM, 4 * LANES), row),
        pl.BlockSpec((BLOCK, 4 * LANES), lambda i: (jnp.minimum((i + 1) * nb_tile, nblk - 1), 0)),
        pl.BlockSpec((TM, dg), row),
        pl.BlockSpec((TM, dg), row),
        pl.BlockSpec((1, 6, d), lambda i: (i // nsb, 0, 0)),
        pl.BlockSpec(bias.shape, lambda i: (0, 0, 0, 0)),
        pl.BlockSpec(ws.shape, lambda i: (0, 0, 0)),
        pl.BlockSpec((BLOCK, dg), const2),
        pl.BlockSpec((1, dg), const2),
        pl.BlockSpec((1, dg), const2),
        pl.BlockSpec((1, D_ATTN), const2),
        pl.BlockSpec((1, dg), const2),
        pl.BlockSpec(w_out.shape, const2),
        pl.BlockSpec((1, d), const2),
        pl.BlockSpec((1, d), const2),
        pl.BlockSpec(w_r_t.shape, const2),
    ]
    out_specs = [
        pl.BlockSpec((TM, d), row),
        pl.BlockSpec((TM, d), row),
        pl.BlockSpec((1, e, TM), lambda i: (i // nsb, 0, i % nsb)),
    ]
    out_shape = [
        jax.ShapeDtypeStruct((t, d), F32),
        jax.ShapeDtypeStruct((t, d), BF16),
        jax.ShapeDtypeStruct((b, e, s), F32),
    ]
    return pl.pallas_call(
        functools.partial(_mix_kernel, s // BLOCK),
        grid=(t // TM,),
        in_specs=in_specs,
        out_specs=out_specs,
        out_shape=out_shape,
        scratch_shapes=[
            pltpu.VMEM((TM + 2 * BLOCK, 4 * LANES), BF16),
            pltpu.VMEM((TM, D_ATTN), F32),
            pltpu.VMEM((TM, dg), F32),
        ],
        compiler_params=pltpu.CompilerParams(
            dimension_semantics=("arbitrary",), vmem_limit_bytes=VMEM_LIMIT),
        name="mix",
    )(sink, x2, q, kv, kv, kv, u, vg, mod, bias, ws, bs_full, ln_g, ln_b, g_attn, g_gmlp,
      w_out, g_post, g_pre2, w_r_t)


def _route_kernel(cap, aff_ref, slot_ref, cnt_ref):
    aff = aff_ref[0]
    e, s = aff.shape
    capf = jnp.float32(cap)

    def count_ge(t):
        return jnp.sum(jnp.where(aff >= t, 1.0, 0.0), axis=1, keepdims=True)

    def bit_body(it, cand):
        trial = cand | jnp.left_shift(jnp.int32(1), 30 - it)
        t = lax.bitcast_convert_type(trial, F32)
        return jnp.where(count_ge(t) >= capf, trial, cand)

    cand = lax.fori_loop(0, 31, bit_body, jnp.zeros((e, 1), jnp.int32))
    thr = lax.bitcast_convert_type(cand, F32)
    need = capf - jnp.sum(jnp.where(aff > thr, 1.0, 0.0), axis=1, keepdims=True)

    r = lax.broadcasted_iota(jnp.int32, (LANES, LANES), 0)
    c = lax.broadcasted_iota(jnp.int32, (LANES, LANES), 1)
    tri = jnp.where(r < c, 1.0, 0.0).astype(BF16)
    lane = lax.broadcasted_iota(jnp.int32, (e, LANES), 1)
    carry_eq = jnp.zeros((e, 1), F32)
    carry_sel = jnp.zeros((e, 1), F32)
    cnt = jnp.zeros((e, LANES), F32)
    per_blk = MXU_DIM // LANES
    for j in range(s // LANES):
        a = aff[:, j * LANES:(j + 1) * LANES]
        gt = a > thr
        eq = jnp.where(a == thr, 1.0, 0.0)
        tie_rank = _dot(eq.astype(BF16), tri) + carry_eq
        sel = jnp.where(gt | ((eq > 0.0) & (tie_rank < need)), 1.0, 0.0)
        slot = _dot(sel.astype(BF16), tri) + carry_sel
        slot_ref[0, :, j * LANES:(j + 1) * LANES] = jnp.where(sel > 0.0, slot, -1.0).astype(jnp.int32)
        if j % per_blk == 0:
            cnt = jnp.where(lane == j // per_blk, carry_sel, cnt)
        carry_eq = carry_eq + jnp.sum(eq, axis=1, keepdims=True)
        carry_sel = carry_sel + jnp.sum(sel, axis=1, keepdims=True)
    cnt = jnp.where(lane >= s // MXU_DIM, carry_sel, cnt)
    cnt_ref[0] = cnt.astype(jnp.int32)


def _route(aff_t, cap):
    b, e, s = aff_t.shape
    return pl.pallas_call(
        functools.partial(_route_kernel, cap),
        grid=(b,),
        in_specs=[pl.BlockSpec((1, e, s), lambda i: (i, 0, 0))],
        out_specs=[
            pl.BlockSpec((1, e, s), lambda i: (i, 0, 0)),
            pl.BlockSpec((1, e, LANES), lambda i: (i, 0, 0)),
        ],
        out_shape=[
            jax.ShapeDtypeStruct((b, e, s), jnp.int32),
            jax.ShapeDtypeStruct((b, e, LANES), jnp.int32),
        ],
        compiler_params=pltpu.CompilerParams(dimension_semantics=("arbitrary",)),
        name="route",
    )(aff_t)


def _segment(cnt_ref, row, pb, cap):
    c0 = jnp.minimum(cnt_ref[row, pb], cap)
    c1 = jnp.minimum(cnt_ref[row, pb + 1], cap)
    a0 = jnp.minimum(jnp.bitwise_and(c0, -SLOT_ALIGN), cap - WIN)
    return c0, c1, pl.multiple_of(a0, SLOT_ALIGN)


def _extra_windows(c1, a0):
    return (jnp.maximum(c1 - (a0 + WIN), 0) + WIN - 1) // WIN


def _gather_kernel(n_exp, cnt_ref, h2_ref, slot_ref, aff_ref, x_ref, gs_ref, oh_s):
    b, eg = pl.program_id(0), pl.program_id(1)
    ng, cap = x_ref.shape[1], x_ref.shape[2]
    nblk = slot_ref.shape[2]
    sub = lax.broadcasted_iota(jnp.int32, (WIN, MXU_DIM), 0)
    sub_col = lax.broadcasted_iota(jnp.int32, (WIN, 1), 0)
    x_ref[...] = jnp.zeros_like(x_ref)
    gs_ref[...] = jnp.zeros_like(gs_ref)

    def merge(k, a0, first, last, rows, gate):
        slot_id = sub_col + a0
        keep = (slot_id >= first) & (slot_id < last)
        win = (0, k, pl.ds(a0, WIN), slice(None))
        x_ref[win] = jnp.where(keep, rows.astype(BF16), x_ref[win])
        gs_ref[win] = jnp.where(keep, gate, gs_ref[win])

    def pb_body(pb, carry):
        tok = pl.ds(pl.multiple_of(pb * MXU_DIM, MXU_DIM), MXU_DIM)
        segs, gates = [], []
        overflow = jnp.int32(0)
        for k in range(ng):
            c0, c1, a0 = _segment(cnt_ref, b * n_exp + eg * ng + k, pb, cap)
            match = slot_ref[0, k, pl.ds(pb, 1), :] == (sub + a0)
            oh_s[k * WIN:(k + 1) * WIN, :] = jnp.where(match, 1.0, 0.0).astype(BF16)
            gates.append(jnp.sum(jnp.where(match, aff_ref[0, k, pl.ds(pb, 1), :], 0.0),
                                 axis=1, keepdims=True))
            segs.append((c0, c1, a0))
            overflow = overflow + jnp.maximum(c1 - (a0 + WIN), 0)
        rows = _dot(oh_s[...], h2_ref[0, tok, :])
        for k in range(ng):
            c0, c1, a0 = segs[k]
            merge(k, a0, c0, c1, rows[k * WIN:(k + 1) * WIN, :], gates[k])

        @pl.when(overflow > 0)
        def _long_segments():
            def per_expert(k, carry2):
                _, c1, a0 = _segment(cnt_ref, b * n_exp + eg * ng + k, pb, cap)

                def per_window(w, carry3):
                    first = a0 + (w + 1) * WIN
                    aw = pl.multiple_of(jnp.minimum(first, cap - WIN), SLOT_ALIGN)
                    srow = slot_ref[0, k, pl.ds(pb, 1), :]
                    match = (srow == (sub + aw)) & (srow >= first)
                    onehot = jnp.where(match, 1.0, 0.0).astype(BF16)
                    gate = jnp.sum(jnp.where(match, aff_ref[0, k, pl.ds(pb, 1), :], 0.0),
                                   axis=1, keepdims=True)
                    merge(k, aw, first, c1, _dot(onehot, h2_ref[0, tok, :]), gate)
                    return carry3

                return lax.fori_loop(0, _extra_windows(c1, a0), per_window, carry2)

            lax.fori_loop(0, ng, per_expert, 0)

        return carry

    lax.fori_loop(0, nblk, pb_body, 0)


def _gather(cnt2, h2, slots4, aff4, cap):
    b, s, d = h2.shape
    e = slots4.shape[1]
    nblk = s // MXU_DIM
    grid_spec = pltpu.PrefetchScalarGridSpec(
        num_scalar_prefetch=1,
        grid=(b, e // EG),
        in_specs=[
            pl.BlockSpec((1, s, d), lambda bi, gi, c: (bi, 0, 0), pipeline_mode=pl.Buffered(1)),
            pl.BlockSpec((1, EG, nblk, MXU_DIM), lambda bi, gi, c: (bi, gi, 0, 0)),
            pl.BlockSpec((1, EG, nblk, MXU_DIM), lambda bi, gi, c: (bi, gi, 0, 0)),
        ],
        out_specs=[
            pl.BlockSpec((1, EG, cap, d), lambda bi, gi, c: (bi, gi, 0, 0)),
            pl.BlockSpec((1, EG, cap, LANES), lambda bi, gi, c: (bi, gi, 0, 0)),
        ],
        scratch_shapes=[pltpu.VMEM((EG * WIN, MXU_DIM), BF16)],
    )
    return pl.pallas_call(
        functools.partial(_gather_kernel, e),
        grid_spec=grid_spec,
        out_shape=[
            jax.ShapeDtypeStruct((b, e, cap, d), BF16),
            jax.ShapeDtypeStruct((b, e, cap, LANES), F32),
        ],
        compiler_params=pltpu.CompilerParams(
            dimension_semantics=("arbitrary", "arbitrary"), vmem_limit_bytes=VMEM_LIMIT),
        name="gather",
    )(cnt2, h2, slots4, aff4)


def _ffn_kernel(x_ref, gs_ref, wg_ref, wu_ref, wd_ref, out_ref, acc_s):
    f = pl.program_id(1)
    nb, _, cap, d = x_ref.shape

    @pl.when(f == 0)
    def _init():
        acc_s[...] = jnp.zeros_like(acc_s)

    x = x_ref[:, 0].reshape(nb * cap, d)
    g = _dot(x, wg_ref[0].astype(BF16))
    u = _dot(x, wu_ref[0].astype(BF16))
    hid = (g * jax.nn.sigmoid(g) * u).astype(BF16)
    acc_s[...] += _dot(hid, wd_ref[0].astype(BF16))

    @pl.when(f == pl.num_programs(1) - 1)
    def _emit():
        gate = gs_ref[:, 0, :, 0:1].reshape(nb * cap, 1)
        out_ref[:, 0] = (acc_s[...] * gate).reshape(nb, cap, d).astype(BF16)


def _ffn(xg, gs, w_gate, w_up, w_down, e0):
    b, e, cap, d = xg.shape
    ff = w_gate.shape[2]
    return pl.pallas_call(
        _ffn_kernel,
        grid=(e, ff // FC),
        in_specs=[
            pl.BlockSpec((b, 1, cap, d), lambda ei, fi: (0, ei, 0, 0)),
            pl.BlockSpec((b, 1, cap, LANES), lambda ei, fi: (0, ei, 0, 0)),
            pl.BlockSpec((1, d, FC), lambda ei, fi: (e0 + ei, 0, fi)),
            pl.BlockSpec((1, d, FC), lambda ei, fi: (e0 + ei, 0, fi)),
            pl.BlockSpec((1, FC, d), lambda ei, fi: (e0 + ei, fi, 0)),
        ],
        out_specs=pl.BlockSpec((b, 1, cap, d), lambda ei, fi: (0, ei, 0, 0)),
        out_shape=jax.ShapeDtypeStruct((b, e, cap, d), BF16),
        scratch_shapes=[pltpu.VMEM((b * cap, d), F32)],
        compiler_params=pltpu.CompilerParams(
            dimension_semantics=("arbitrary", "arbitrary"), vmem_limit_bytes=VMEM_LIMIT),
        name="ffn",
    )(xg, gs, w_gate, w_up, w_down)


def _combine_kernel(n_exp, cnt_ref, eo_ref, slot_ref, x1_ref, mod_ref, g_ref, o_ref,
                    stage_s, oh_s, y_s):
    b, q = pl.program_id(0), pl.program_id(1)
    cap = eo_ref.shape[2]
    nblk_q = o_ref.shape[0] // MXU_DIM
    sub = lax.broadcasted_iota(jnp.int32, (WIN, MXU_DIM), 0)
    gate = mod_ref[0, 5:6, :]

    for j in range(nblk_q):
        pb = q * nblk_q + j
        overflow = jnp.int32(0)
        for e in range(n_exp):
            _, c1, a0 = _segment(cnt_ref, b * n_exp + e, pb, cap)
            stage_s[e * WIN:(e + 1) * WIN, :] = eo_ref[0, e, pl.ds(a0, WIN), :]
            match = slot_ref[0, e, pl.ds(pb, 1), :] == (sub + a0)
            oh_s[e * WIN:(e + 1) * WIN, :] = jnp.where(match, 1.0, 0.0).astype(BF16)
            overflow = overflow + jnp.maximum(c1 - (a0 + WIN), 0)
        y_s[...] = _dot_tn(oh_s[...], stage_s[...])

        @pl.when(overflow > 0)
        def _long_segments(j=j, pb=pb):
            def per_expert(e, carry):
                _, c1, a0 = _segment(cnt_ref, b * n_exp + e, pb, cap)

                def per_window(k, carry2):
                    first = a0 + (k + 1) * WIN
                    aw = pl.multiple_of(jnp.minimum(first, cap - WIN), SLOT_ALIGN)
                    srow = slot_ref[0, e, pl.ds(pb, 1), :]
                    match = (srow == (sub + aw)) & (srow >= first)
                    onehot = jnp.where(match, 1.0, 0.0).astype(BF16)
                    y_s[...] += _dot_tn(onehot, eo_ref[0, e, pl.ds(aw, WIN), :])
                    return carry2

                return lax.fori_loop(0, _extra_windows(c1, a0), per_window, carry)

            lax.fori_loop(0, n_exp, per_expert, 0)

        sl = slice(j * MXU_DIM, (j + 1) * MXU_DIM)
        o_ref[sl, :] = x1_ref[sl, :] + gate * _rms(y_s[...], g_ref[...])


def _combine(cnt2, eo, slots4, x1, mod, g_post):
    b, e, cap, d = eo.shape
    t = x1.shape[0]
    s = t // b
    nq = s // QP
    nblk_q = QP // MXU_DIM
    grid_spec = pltpu.PrefetchScalarGridSpec(
        num_scalar_prefetch=1,
        grid=(b, nq),
        in_specs=[
            pl.BlockSpec((1, e, cap, d), lambda bi, qi, c: (bi, 0, 0, 0), pipeline_mode=pl.Buffered(1)),
            pl.BlockSpec((1, e, s // MXU_DIM, MXU_DIM), lambda bi, qi, c: (bi, 0, 0, 0)),
            pl.BlockSpec((QP, d), lambda bi, qi, c: (bi * nq + qi, 0)),
            pl.BlockSpec((1, 6, d), lambda bi, qi, c: (bi, 0, 0)),
            pl.BlockSpec((1, d), lambda bi, qi, c: (0, 0)),
        ],
        out_specs=pl.BlockSpec((QP, d), lambda bi, qi, c: (bi * nq + qi, 0)),
        scratch_shapes=[
            pltpu.VMEM((e * WIN, d), BF16),
            pltpu.VMEM((e * WIN, MXU_DIM), BF16),
            pltpu.VMEM((MXU_DIM, d), F32),
        ],
    )
    return pl.pallas_call(
        functools.partial(_combine_kernel, e),
        grid_spec=grid_spec,
        out_shape=jax.ShapeDtypeStruct((t, d), F32),
        compiler_params=pltpu.CompilerParams(
            dimension_semantics=("arbitrary", "arbitrary"),
            vmem_limit_bytes=VMEM_LIMIT),
        name="combine",
    )(cnt2, eo, slots4, x1, mod, g_post)


def _layer(layer, x, c, w_ada_all, b_ada, norm_pre_mix, norm_post_mix, w_in, sink, sgu_ln_g,
           sgu_ln_b, w_s, b_s, norm_out_attn, norm_out_gmlp, w_out, norm_pre_ffn, norm_post_ffn,
           w_router, w_gate_all, w_up_all, w_down_all):
    b, s, d = x.shape
    t = b * s
    dg = sgu_ln_g.shape[0]
    e = w_router.shape[1]
    cap = CAPACITY_FACTOR * s // e
    assert s % QP == 0 and s % TM == 0 and w_gate_all.shape[2] % FC == 0 and e % EG == 0
    assert cap % SLOT_ALIGN == 0 and cap >= WIN and s // MXU_DIM < LANES and d == D_ATTN + dg

    mod = _adaln(c, w_ada_all, b_ada, layer).reshape(b, 6, d)

    kcol = w_in[:, D_ATTN:D_ATTN + D_KV]
    vcol = w_in[:, D_ATTN + D_KV:D_ATTN + 2 * D_KV]
    twice = lambda w: [w[:, g * HEAD_DIM:(g + 1) * HEAD_DIM] for g in range(N_KV_HEADS) for _ in range(2)]
    w_ext = jnp.concatenate(
        [w_in[:, :D_ATTN]] + twice(kcol) + twice(vcol) + [w_in[:, D_ATTN + 2 * D_KV:]],
        axis=1).astype(BF16)

    x2 = x.reshape(t, d)
    q, kv, u, vg = _proj(x2, mod, norm_pre_mix.reshape(1, d), w_ext, s)

    bias = jnp.asarray(_attn_bias_table())
    bs_full = jnp.repeat(b_s.T, dg // N_GMLP_GROUPS, axis=1)
    x1, h2, aff_t = _mix(
        x2, q, kv, u, vg, mod, sink, bias, w_s.astype(BF16), bs_full,
        sgu_ln_g.reshape(1, dg), sgu_ln_b.reshape(1, dg), norm_out_attn.reshape(1, D_ATTN),
        norm_out_gmlp.reshape(1, dg), w_out.astype(BF16), norm_post_mix.reshape(1, d),
        norm_pre_ffn.reshape(1, d), w_router.T.astype(BF16), b, s)

    slots, cnt = _route(aff_t, cap)
    nblk = s // MXU_DIM
    slots4 = slots.reshape(b, e, nblk, MXU_DIM)
    aff4 = aff_t.reshape(b, e, nblk, MXU_DIM)
    cnt2 = cnt.reshape(b * e, LANES)
    xg, gs = _gather(cnt2, h2.reshape(b, s, d), slots4, aff4, cap)
    eo = _ffn(xg, gs, w_gate_all, w_up_all, w_down_all, layer * e)
    out = _combine(cnt2, eo, slots4, x1, mod, norm_post_ffn.reshape(1, d))
    return out.reshape(b, s, d)


def kernel(x, c, w_ada, b_ada, norm_pre_mix, norm_post_mix, w_in, sink, sgu_ln_g, sgu_ln_b, w_s, b_s, norm_out_attn, norm_out_gmlp, w_out, norm_pre_ffn, norm_post_ffn, w_router, w_gate, w_up, w_down):
    depth, d = w_ada.shape[0], w_ada.shape[1]
    w_ada_all = w_ada.reshape(depth * d, w_ada.shape[2])
    stack = lambda w: w.reshape((depth * w.shape[1],) + w.shape[2:])
    w_gate_all, w_up_all, w_down_all = stack(w_gate), stack(w_up), stack(w_down)
    for l in range(depth):
        x = _layer(l, x, c, w_ada_all, b_ada[l], norm_pre_mix[l], norm_post_mix[l], w_in[l],
                   sink[l], sgu_ln_g[l], sgu_ln_b[l], w_s[l], b_s[l], norm_out_attn[l],
                   norm_out_gmlp[l], w_out[l], norm_pre_ffn[l], norm_post_ffn[l], w_router[l],
                   w_gate_all, w_up_all, w_down_all)
    return x
```

```python
import functools

import numpy as np
import jax
import jax.numpy as jnp
from jax import lax
from jax.experimental import pallas as pl
from jax.experimental.pallas import tpu as pltpu

N_HEADS = 8
N_KV_HEADS = 2
HEAD_DIM = 64
D_ATTN = N_HEADS * HEAD_DIM
D_KV = N_KV_HEADS * HEAD_DIM
GROUP = N_HEADS // N_KV_HEADS
WINDOW = 128
BLOCK = 128
N_GMLP_GROUPS = 8
N_EXPERTS = 16
CAPACITY_FACTOR = 2
EPS = 1e-6
MASK_VALUE = -1e30

LANES = 128
MXU_DIM = 256

TM = 512
FC = 256
EG = 4
QP = 1024
WIN = 64
SLOT_ALIGN = 16
VMEM_LIMIT = 56 * 1024 * 1024

BF16 = jnp.bfloat16
F32 = jnp.float32


def _rms(x, g):
    return x * lax.rsqrt(jnp.mean(x * x, axis=-1, keepdims=True) + EPS) * g


def _dot(a, b):
    return jnp.dot(a, b, preferred_element_type=F32)


def _dot_nt(a, b):
    return lax.dot_general(a, b, (((1,), (1,)), ((), ())), preferred_element_type=F32)


def _dot_tn(a, b):
    return lax.dot_general(a, b, (((0,), (0,)), ((), ())), preferred_element_type=F32)


def _adaln_kernel(c_ref, w_ref, b_ref, o_ref):
    c = c_ref[...]
    a = (c * jax.nn.sigmoid(c)).astype(BF16)
    o_ref[...] = _dot(a, w_ref[...].astype(BF16)) + b_ref[...]


def _adaln(c, w_ada_all, b_ada, layer):
    b, d = c.shape
    n = w_ada_all.shape[1]
    return pl.pallas_call(
        _adaln_kernel,
        grid=(n // d,),
        in_specs=[
            pl.BlockSpec((b, d), lambda j: (0, 0)),
            pl.BlockSpec((d, d), lambda j: (layer, j)),
            pl.BlockSpec((1, d), lambda j: (0, j)),
        ],
        out_specs=pl.BlockSpec((b, d), lambda j: (0, j)),
        out_shape=jax.ShapeDtypeStruct((b, n), F32),
        compiler_params=pltpu.CompilerParams(dimension_semantics=("arbitrary",)),
        name="adaln",
    )(c, w_ada_all, b_ada.reshape(1, n))


def _gelu_tanh(x):
    c = np.float32(np.sqrt(2.0 / np.pi))
    return x * (0.5 * (1.0 + jnp.tanh(c * (x + 0.044715 * (x * x * x)))))


def _proj_kernel(x_ref, mod_ref, g_ref, w_ref, q_ref, kv_ref, u_ref, vg_ref):
    x = x_ref[...]
    shift = mod_ref[0, 0:1, :]
    scale = mod_ref[0, 1:2, :]
    h = (_rms(x, g_ref[...]) * (1.0 + scale) + shift).astype(BF16)
    p = _dot(h, w_ref[...])
    q_ref[...] = (p[:, :D_ATTN] * (HEAD_DIM ** -0.5)).astype(BF16)
    lo = lax.broadcasted_iota(jnp.int32, (1, LANES), 1) < HEAD_DIM
    for j in range(2):
        t = p[:, D_ATTN + j * D_KV:D_ATTN + (j + 1) * D_KV]
        t_sw = pltpu.roll(t, HEAD_DIM, axis=1)
        kv_ref[:, (2 * j) * LANES:(2 * j + 1) * LANES] = jnp.where(lo, t, t_sw).astype(BF16)
        kv_ref[:, (2 * j + 1) * LANES:(2 * j + 2) * LANES] = jnp.where(lo, t_sw, t).astype(BF16)
    dg = u_ref.shape[1]
    z = _gelu_tanh(p[:, D_ATTN + 2 * D_KV:])
    u_ref[...] = z[:, :dg]
    vg_ref[...] = z[:, dg:]


def _proj(x2, mod, g_pre, w_in, s):
    t, d = x2.shape
    nw = w_in.shape[1]
    dg = (nw - D_ATTN - 2 * D_KV) // 2
    nsb = s // TM
    return pl.pallas_call(
        _proj_kernel,
        grid=(t // TM,),
        in_specs=[
            pl.BlockSpec((TM, d), lambda i: (i, 0)),
            pl.BlockSpec((1, 6, d), lambda i: (i // nsb, 0, 0)),
            pl.BlockSpec((1, d), lambda i: (0, 0)),
            pl.BlockSpec((d, nw), lambda i: (0, 0)),
        ],
        out_specs=[
            pl.BlockSpec((TM, D_ATTN), lambda i: (i, 0)),
            pl.BlockSpec((TM, D_ATTN), lambda i: (i, 0)),
            pl.BlockSpec((TM, dg), lambda i: (i, 0)),
            pl.BlockSpec((TM, dg), lambda i: (i, 0)),
        ],
        out_shape=[
            jax.ShapeDtypeStruct((t, D_ATTN), BF16),
            jax.ShapeDtypeStruct((t, D_ATTN), BF16),
            jax.ShapeDtypeStruct((t, dg), F32),
            jax.ShapeDtypeStruct((t, dg), F32),
        ],
        compiler_params=pltpu.CompilerParams(
            dimension_semantics=("arbitrary",), vmem_limit_bytes=VMEM_LIMIT),
        name="proj",
    )(x2, mod, g_pre, w_in)


def _attn_bias_table():
    qi = np.arange(BLOCK)[:, None]
    kj = np.arange(3 * BLOCK)[None, :] - BLOCK
    dist = np.abs(qi - kj).astype(np.float32)
    band = dist <= WINDOW
    slopes = np.exp2(-8.0 * np.arange(1, N_HEADS + 1, dtype=np.float32) / N_HEADS).astype(np.float32)
    key_ok = [kj >= 0, np.ones_like(kj, bool), kj < BLOCK]
    tab = np.empty((3, N_HEADS, BLOCK, 3 * BLOCK), np.float32)
    for v in range(3):
        ok = band & key_ok[v]
        for h in range(N_HEADS):
            tab[v, h] = np.where(ok, -slopes[h] * dist, np.float32(MASK_VALUE))
    return tab.reshape(3, N_KV_HEADS, GROUP * BLOCK, 3 * BLOCK)


def _mix_kernel(nblk_seq, sink_ref, x_ref, q_ref, kvp_ref, kvc_ref, kvn_ref, u_ref, vg_ref,
                mod_ref, bias_ref, ws_ref, bs_ref, lng_ref, lnb_ref, ga_ref, gg_ref,
                wout_ref, gpost_ref, gpre2_ref, wr_ref,
                x1_ref, h2_ref, aff_ref, kcat, attn_s, gm_s):
    i = pl.program_id(0)
    nb_tile = TM // BLOCK
    kcat[0:BLOCK, :] = kvp_ref[...]
    kcat[BLOCK:BLOCK + TM, :] = kvc_ref[...]
    kcat[BLOCK + TM:, :] = kvn_ref[...]
    lane = lax.broadcasted_iota(jnp.int32, (1, LANES), 1)
    lo = lane < HEAD_DIM
    zero = jnp.zeros((), BF16)
    head_of_row = lax.broadcasted_iota(jnp.int32, (GROUP * BLOCK, 1), 0) // BLOCK
    sinks = []
    for g in range(N_KV_HEADS):
        col = jnp.zeros((GROUP * BLOCK, 1), F32)
        for j in range(GROUP):
            col = jnp.where(head_of_row == j, sink_ref[g * GROUP + j], col)
        sinks.append(col)

    def block_body(a, carry):
        r0 = pl.multiple_of(a * BLOCK, BLOCK)
        pos = (i * nb_tile + a) % nblk_seq
        var = jnp.where(pos == 0, 0, jnp.where(pos == nblk_seq - 1, 2, 1))
        kv = kcat[pl.ds(r0, 3 * BLOCK), :]
        for g in range(N_KV_HEADS):
            k = kv[:, g * LANES:(g + 1) * LANES]
            v = kv[:, (N_KV_HEADS + g) * LANES:(N_KV_HEADS + g + 1) * LANES]
            pieces = []
            for pi in range(GROUP // 2):
                qp = q_ref[pl.ds(r0, BLOCK), (g * GROUP // 2 + pi) * LANES:(g * GROUP // 2 + pi + 1) * LANES]
                pieces += [jnp.where(lo, qp, zero), jnp.where(lo, zero, qp)]
            s = _dot_nt(jnp.concatenate(pieces, axis=0), k) + bias_ref[var, g]
            m = jnp.maximum(jnp.max(s, axis=-1, keepdims=True), sinks[g])
            p = jnp.exp(s - m)
            den = jnp.sum(p, axis=-1, keepdims=True) + jnp.exp(sinks[g] - m)
            o = _dot(p.astype(BF16), v) / den
            for pi in range(GROUP // 2):
                even = o[(2 * pi) * BLOCK:(2 * pi + 1) * BLOCK, :]
                odd = o[(2 * pi + 1) * BLOCK:(2 * pi + 2) * BLOCK, :]
                col0 = (g * GROUP // 2 + pi) * LANES
                attn_s[pl.ds(r0, BLOCK), col0:col0 + LANES] = jnp.where(lo, even, odd)
        vg = vg_ref[pl.ds(r0, BLOCK), :]
        mu = jnp.mean(vg, axis=-1, keepdims=True)
        xc = vg - mu
        vn = (xc * lax.rsqrt(jnp.mean(xc * xc, axis=-1, keepdims=True) + EPS) * lng_ref[...]
              + lnb_ref[...]).astype(BF16)
        for pi in range(N_GMLP_GROUPS // 2):
            vnp = vn[:, pi * LANES:(pi + 1) * LANES]
            z = jnp.where(lo, _dot(ws_ref[2 * pi], vnp), _dot(ws_ref[2 * pi + 1], vnp))
            sl = slice(pi * LANES, (pi + 1) * LANES)
            gm_s[pl.ds(r0, BLOCK), sl] = u_ref[pl.ds(r0, BLOCK), sl] * (z + bs_ref[:, sl])
        return carry

    lax.fori_loop(0, nb_tile, block_body, 0, unroll=2)

    na = _rms(attn_s[...], ga_ref[...]).astype(BF16)
    ng = _rms(gm_s[...], gg_ref[...]).astype(BF16)
    mo = _dot(na, wout_ref[0:D_ATTN, :]) + _dot(ng, wout_ref[D_ATTN:, :])
    gate1 = mod_ref[0, 2:3, :]
    x1 = x_ref[...] + gate1 * _rms(mo, gpost_ref[...])
    x1_ref[...] = x1
    h2 = (_rms(x1, gpre2_ref[...]) * (1.0 + mod_ref[0, 4:5, :]) + mod_ref[0, 3:4, :]).astype(BF16)
    h2_ref[...] = h2
    lg = _dot_nt(wr_ref[...], h2)
    lg = lg - jnp.max(lg, axis=0, keepdims=True)
    ex = jnp.exp(lg)
    aff_ref[0] = ex / jnp.sum(ex, axis=0, keepdims=True)


def _mix(x2, q, kv, u, vg, mod, sink, bias, ws, bs_full, ln_g, ln_b, g_attn, g_gmlp,
         w_out, g_post, g_pre2, w_r_t, b, s):
    t, d = x2.shape
    dg = u.shape[1]
    e = w_r_t.shape[0]
    nsb = s // TM
    nb_tile = TM // BLOCK
    nblk = t // BLOCK
    row = lambda i: (i, 0)
    const2 = lambda i: (0, 0)
    in_specs = [
        pl.BlockSpec(memory_space=pltpu.SMEM),
        pl.BlockSpec((TM, d), row),
        pl.BlockSpec((TM, D_ATTN), row),
        pl.BlockSpec((BLOCK, 4 * LANES), lambda i: (jnp.maximum(i * nb_tile - 1, 0), 0)),
        pl.BlockSpec((TM, 4 * LANES), row),
        pl.BlockSpec((BLOCK, 4 * LANES), lambda i: (jnp.minimum((i + 1) * nb_tile, nblk - 1), 0)),
        pl.BlockSpec((TM, dg), row),
        pl.BlockSpec((TM, dg), row),
        pl.BlockSpec((1, 6, d), lambda i: (i // nsb, 0, 0)),
        pl.BlockSpec(bias.shape, lambda i: (0, 0, 0, 0)),
        pl.BlockSpec(ws.shape, lambda i: (0, 0, 0)),
        pl.BlockSpec((BLOCK, dg), const2),
        pl.BlockSpec((1, dg), const2),
        pl.BlockSpec((1, dg), const2),
        pl.BlockSpec((1, D_ATTN), const2),
        pl.BlockSpec((1, dg), const2),
        pl.BlockSpec(w_out.shape, const2),
        pl.BlockSpec((1, d), const2),
        pl.BlockSpec((1, d), const2),
        pl.BlockSpec(w_r_t.shape, const2),
    ]
    out_specs = [
        pl.BlockSpec((TM, d), row),
        pl.BlockSpec((TM, d), row),
        pl.BlockSpec((1, e, TM), lambda i: (i // nsb, 0, i % nsb)),
    ]
    out_shape = [
        jax.ShapeDtypeStruct((t, d), F32),
        jax.ShapeDtypeStruct((t, d), BF16),
        jax.ShapeDtypeStruct((b, e, s), F32),
    ]
    return pl.pallas_call(
        functools.partial(_mix_kernel, s // BLOCK),
        grid=(t // TM,),
        in_specs=in_specs,
        out_specs=out_specs,
        out_shape=out_shape,
        scratch_shapes=[
            pltpu.VMEM((TM + 2 * BLOCK, 4 * LANES), BF16),
            pltpu.VMEM((TM, D_ATTN), F32),
            pltpu.VMEM((TM, dg), F32),
        ],
        compiler_params=pltpu.CompilerParams(
            dimension_semantics=("arbitrary",), vmem_limit_bytes=VMEM_LIMIT),
        name="mix",
    )(sink, x2, q, kv, kv, kv, u, vg, mod, bias, ws, bs_full, ln_g, ln_b, g_attn, g_gmlp,
      w_out, g_post, g_pre2, w_r_t)


def _route_kernel(cap, aff_ref, slot_ref, affb_ref, cnt_ref):
    aff = aff_ref[...]
    e, s = aff.shape
    capf = jnp.float32(cap)

    def count_ge(t):
        return jnp.sum(jnp.where(aff >= t, 1.0, 0.0), axis=1, keepdims=True)

    def bit_body(it, cand):
        trial = cand | jnp.left_shift(jnp.int32(1), 30 - it)
        t = lax.bitcast_convert_type(trial, F32)
        return jnp.where(count_ge(t) >= capf, trial, cand)

    cand = lax.fori_loop(0, 31, bit_body, jnp.zeros((e, 1), jnp.int32))
    thr = lax.bitcast_convert_type(cand, F32)
    need = capf - jnp.sum(jnp.where(aff > thr, 1.0, 0.0), axis=1, keepdims=True)

    r = lax.broadcasted_iota(jnp.int32, (LANES, LANES), 0)
    c = lax.broadcasted_iota(jnp.int32, (LANES, LANES), 1)
    tri = jnp.where(r < c, 1.0, 0.0).astype(BF16)
    lane = lax.broadcasted_iota(jnp.int32, (e, LANES), 1)
    carry_eq = jnp.zeros((e, 1), F32)
    carry_sel = jnp.zeros((e, 1), F32)
    cnt = jnp.zeros((e, LANES), F32)
    per_blk = MXU_DIM // LANES
    for j in range(s // LANES):
        a = aff[:, j * LANES:(j + 1) * LANES]
        gt = a > thr
        eq = jnp.where(a == thr, 1.0, 0.0)
        tie_rank = _dot(eq.astype(BF16), tri) + carry_eq
        sel = jnp.where(gt | ((eq > 0.0) & (tie_rank < need)), 1.0, 0.0)
        slot = _dot(sel.astype(BF16), tri) + carry_sel
        blk = (slice(None), j // per_blk, slice((j % per_blk) * LANES, (j % per_blk + 1) * LANES))
        slot_ref[blk] = jnp.where(sel > 0.0, slot, -1.0).astype(jnp.int32)
        affb_ref[blk] = a
        if j % per_blk == 0:
            cnt = jnp.where(lane == j // per_blk, carry_sel, cnt)
        carry_eq = carry_eq + jnp.sum(eq, axis=1, keepdims=True)
        carry_sel = carry_sel + jnp.sum(sel, axis=1, keepdims=True)
    cnt = jnp.where(lane >= s // MXU_DIM, carry_sel, cnt)
    cnt_ref[...] = cnt.astype(jnp.int32)


def _route(aff2, cap):
    r, s = aff2.shape
    nblk = s // MXU_DIM
    return pl.pallas_call(
        functools.partial(_route_kernel, cap),
        grid=(1,),
        in_specs=[pl.BlockSpec((r, s), lambda i: (0, 0))],
        out_specs=[
            pl.BlockSpec((r, nblk, MXU_DIM), lambda i: (0, 0, 0)),
            pl.BlockSpec((r, nblk, MXU_DIM), lambda i: (0, 0, 0)),
            pl.BlockSpec((r, LANES), lambda i: (0, 0)),
        ],
        out_shape=[
            jax.ShapeDtypeStruct((r, nblk, MXU_DIM), jnp.int32),
            jax.ShapeDtypeStruct((r, nblk, MXU_DIM), F32),
            jax.ShapeDtypeStruct((r, LANES), jnp.int32),
        ],
        compiler_params=pltpu.CompilerParams(dimension_semantics=("arbitrary",)),
        name="route",
    )(aff2)


def _segment(cnt_ref, row, pb, cap):
    c0 = jnp.minimum(cnt_ref[row, pb], cap)
    c1 = jnp.minimum(cnt_ref[row, pb + 1], cap)
    a0 = jnp.minimum(jnp.bitwise_and(c0, -SLOT_ALIGN), cap - WIN)
    return c0, c1, pl.multiple_of(a0, SLOT_ALIGN)


def _extra_windows(c1, a0):
    return (jnp.maximum(c1 - (a0 + WIN), 0) + WIN - 1) // WIN


def _gather_kernel(n_exp, cnt_ref, h2_ref, slot_ref, aff_ref, x_ref, gs_ref):
    b, eg = pl.program_id(0), pl.program_id(1)
    ng, cap = x_ref.shape[1], x_ref.shape[2]
    nblk = slot_ref.shape[2]
    sub = lax.broadcasted_iota(jnp.int32, (WIN, MXU_DIM), 0)
    sub_col = lax.broadcasted_iota(jnp.int32, (WIN, 1), 0)
    x_ref[...] = jnp.zeros_like(x_ref)
    gs_ref[...] = jnp.zeros_like(gs_ref)

    def merge(k, a0, first, last, rows, gate):
        slot_id = sub_col + a0
        keep = (slot_id >= first) & (slot_id < last)
        win = (0, k, pl.ds(a0, WIN), slice(None))
        x_ref[win] = jnp.where(keep, rows.astype(BF16), x_ref[win])
        gs_ref[win] = jnp.where(keep, gate, gs_ref[win])

    def tokens(pb):
        return pl.ds(pl.multiple_of(pb * MXU_DIM, MXU_DIM), MXU_DIM)

    def first_windows(pb):
        segs, gates, onehots = [], [], []
        overflow = jnp.int32(0)
        for k in range(ng):
            c0, c1, a0 = _segment(cnt_ref, b * n_exp + eg * ng + k, pb, cap)
            match = slot_ref[0, k, pl.ds(pb, 1), :] == (sub + a0)
            onehots.append(jnp.where(match, 1.0, 0.0).astype(BF16))
            gates.append(jnp.sum(jnp.where(match, aff_ref[0, k, pl.ds(pb, 1), :], 0.0),
                                 axis=1, keepdims=True))
            segs.append((c0, c1, a0))
            overflow = overflow + jnp.maximum(c1 - (a0 + WIN), 0)
        rows = _dot(jnp.concatenate(onehots, axis=0), h2_ref[0, tokens(pb), :])
        for k in range(ng):
            c0, c1, a0 = segs[k]
            merge(k, a0, c0, c1, rows[k * WIN:(k + 1) * WIN, :], gates[k])
        return overflow

    def further_windows(pb):
        def per_expert(k, carry2):
            _, c1, a0 = _segment(cnt_ref, b * n_exp + eg * ng + k, pb, cap)

            def per_window(w, carry3):
                first = a0 + (w + 1) * WIN
                aw = pl.multiple_of(jnp.minimum(first, cap - WIN), SLOT_ALIGN)
                srow = slot_ref[0, k, pl.ds(pb, 1), :]
                match = (srow == (sub + aw)) & (srow >= first)
                onehot = jnp.where(match, 1.0, 0.0).astype(BF16)
                gate = jnp.sum(jnp.where(match, aff_ref[0, k, pl.ds(pb, 1), :], 0.0),
                               axis=1, keepdims=True)
                merge(k, aw, first, c1, _dot(onehot, h2_ref[0, tokens(pb), :]), gate)
                return carry3

            return lax.fori_loop(0, _extra_windows(c1, a0), per_window, carry2)

        lax.fori_loop(0, ng, per_expert, 0)

    def pair_body(i, carry):
        overflows = [first_windows(2 * i + j) for j in range(2)]
        for j in range(2):
            pl.when(overflows[j] > 0)(functools.partial(further_windows, 2 * i + j))
        return carry

    lax.fori_loop(0, nblk // 2, pair_body, 0)


def _gather(cnt2, h2, slots4, aff4, cap):
    b, s, d = h2.shape
    e = slots4.shape[1]
    nblk = s // MXU_DIM
    grid_spec = pltpu.PrefetchScalarGridSpec(
        num_scalar_prefetch=1,
        grid=(b, e // EG),
        in_specs=[
            pl.BlockSpec((1, s, d), lambda bi, gi, c: (bi, 0, 0), pipeline_mode=pl.Buffered(1)),
            pl.BlockSpec((1, EG, nblk, MXU_DIM), lambda bi, gi, c: (bi, gi, 0, 0)),
            pl.BlockSpec((1, EG, nblk, MXU_DIM), lambda bi, gi, c: (bi, gi, 0, 0)),
        ],
        out_specs=[
            pl.BlockSpec((1, EG, cap, d), lambda bi, gi, c: (bi, gi, 0, 0)),
            pl.BlockSpec((1, EG, cap, LANES), lambda bi, gi, c: (bi, gi, 0, 0)),
        ],
    )
    return pl.pallas_call(
        functools.partial(_gather_kernel, e),
        grid_spec=grid_spec,
        out_shape=[
            jax.ShapeDtypeStruct((b, e, cap, d), BF16),
            jax.ShapeDtypeStruct((b, e, cap, LANES), F32),
        ],
        compiler_params=pltpu.CompilerParams(
            dimension_semantics=("arbitrary", "arbitrary"), vmem_limit_bytes=VMEM_LIMIT),
        name="gather",
    )(cnt2, h2, slots4, aff4)


def _ffn_kernel(x_ref, gs_ref, wg_ref, wu_ref, wd_ref, out_ref, acc_s):
    f = pl.program_id(1)
    nb, _, cap, d = x_ref.shape

    @pl.when(f == 0)
    def _init():
        acc_s[...] = jnp.zeros_like(acc_s)

    x = x_ref[:, 0].reshape(nb * cap, d)
    g = _dot(x, wg_ref[0].astype(BF16))
    u = _dot(x, wu_ref[0].astype(BF16))
    hid = (g * jax.nn.sigmoid(g) * u).astype(BF16)
    acc_s[...] += _dot(hid, wd_ref[0].astype(BF16))

    @pl.when(f == pl.num_programs(1) - 1)
    def _emit():
        gate = gs_ref[:, 0, :, 0:1].reshape(nb * cap, 1)
        out_ref[:, 0] = (acc_s[...] * gate).reshape(nb, cap, d).astype(BF16)


def _ffn(xg, gs, w_gate, w_up, w_down, e0):
    b, e, cap, d = xg.shape
    ff = w_gate.shape[2]
    return pl.pallas_call(
        _ffn_kernel,
        grid=(e, ff // FC),
        in_specs=[
            pl.BlockSpec((b, 1, cap, d), lambda ei, fi: (0, ei, 0, 0)),
            pl.BlockSpec((b, 1, cap, LANES), lambda ei, fi: (0, ei, 0, 0)),
            pl.BlockSpec((1, d, FC), lambda ei, fi: (e0 + ei, 0, fi)),
            pl.BlockSpec((1, d, FC), lambda ei, fi: (e0 + ei, 0, fi)),
            pl.BlockSpec((1, FC, d), lambda ei, fi: (e0 + ei, fi, 0)),
        ],
        out_specs=pl.BlockSpec((b, 1, cap, d), lambda ei, fi: (0, ei, 0, 0)),
        out_shape=jax.ShapeDtypeStruct((b, e, cap, d), BF16),
        scratch_shapes=[pltpu.VMEM((b * cap, d), F32)],
        compiler_params=pltpu.CompilerParams(
            dimension_semantics=("arbitrary", "arbitrary"), vmem_limit_bytes=VMEM_LIMIT),
        name="ffn",
    )(xg, gs, w_gate, w_up, w_down)


def _combine_kernel(n_exp, cnt_ref, eo_ref, slot_ref, x1_ref, mod_ref, g_ref, o_ref,
                    stage_s, oh_s, y_s):
    b, q = pl.program_id(0), pl.program_id(1)
    cap = eo_ref.shape[2]
    nblk_q = o_ref.shape[0] // MXU_DIM
    sub = lax.broadcasted_iota(jnp.int32, (WIN, MXU_DIM), 0)
    gate = mod_ref[0, 5:6, :]

    for j in range(nblk_q):
        pb = q * nblk_q + j
        overflow = jnp.int32(0)
        for e in range(n_exp):
            _, c1, a0 = _segment(cnt_ref, b * n_exp + e, pb, cap)
            stage_s[e * WIN:(e + 1) * WIN, :] = eo_ref[0, e, pl.ds(a0, WIN), :]
            match = slot_ref[0, e, pl.ds(pb, 1), :] == (sub + a0)
            oh_s[e * WIN:(e + 1) * WIN, :] = jnp.where(match, 1.0, 0.0).astype(BF16)
            overflow = overflow + jnp.maximum(c1 - (a0 + WIN), 0)
        y_s[...] = _dot_tn(oh_s[...], stage_s[...])

        @pl.when(overflow > 0)
        def _long_segments(j=j, pb=pb):
            def per_expert(e, carry):
                _, c1, a0 = _segment(cnt_ref, b * n_exp + e, pb, cap)

                def per_window(k, carry2):
                    first = a0 + (k + 1) * WIN
                    aw = pl.multiple_of(jnp.minimum(first, cap - WIN), SLOT_ALIGN)
                    srow = slot_ref[0, e, pl.ds(pb, 1), :]
                    match = (srow == (sub + aw)) & (srow >= first)
                    onehot = jnp.where(match, 1.0, 0.0).astype(BF16)
                    y_s[...] += _dot_tn(onehot, eo_ref[0, e, pl.ds(aw, WIN), :])
                    return carry2

                return lax.fori_loop(0, _extra_windows(c1, a0), per_window, carry)

            lax.fori_loop(0, n_exp, per_expert, 0)

        sl = slice(j * MXU_DIM, (j + 1) * MXU_DIM)
        o_ref[sl, :] = x1_ref[sl, :] + gate * _rms(y_s[...], g_ref[...])


def _combine(cnt2, eo, slots4, x1, mod, g_post):
    b, e, cap, d = eo.shape
    t = x1.shape[0]
    s = t // b
    nq = s // QP
    nblk_q = QP // MXU_DIM
    grid_spec = pltpu.PrefetchScalarGridSpec(
        num_scalar_prefetch=1,
        grid=(b, nq),
        in_specs=[
            pl.BlockSpec((1, e, cap, d), lambda bi, qi, c: (bi, 0, 0, 0), pipeline_mode=pl.Buffered(1)),
            pl.BlockSpec((1, e, s // MXU_DIM, MXU_DIM), lambda bi, qi, c: (bi, 0, 0, 0)),
            pl.BlockSpec((QP, d), lambda bi, qi, c: (bi * nq + qi, 0)),
            pl.BlockSpec((1, 6, d), lambda bi, qi, c: (bi, 0, 0)),
            pl.BlockSpec((1, d), lambda bi, qi, c: (0, 0)),
        ],
        out_specs=pl.BlockSpec((QP, d), lambda bi, qi, c: (bi * nq + qi, 0)),
        scratch_shapes=[
            pltpu.VMEM((e * WIN, d), BF16),
            pltpu.VMEM((e * WIN, MXU_DIM), BF16),
            pltpu.VMEM((MXU_DIM, d), F32),
        ],
    )
    return pl.pallas_call(
        functools.partial(_combine_kernel, e),
        grid_spec=grid_spec,
        out_shape=jax.ShapeDtypeStruct((t, d), F32),
        compiler_params=pltpu.CompilerParams(
            dimension_semantics=("arbitrary", "arbitrary"),
            vmem_limit_bytes=VMEM_LIMIT),
        name="combine",
    )(cnt2, eo, slots4, x1, mod, g_post)


def _layer(layer, x, c, w_ada_all, b_ada, norm_pre_mix, norm_post_mix, w_in, sink, sgu_ln_g,
           sgu_ln_b, w_s, b_s, norm_out_attn, norm_out_gmlp, w_out, norm_pre_ffn, norm_post_ffn,
           w_router, w_gate_all, w_up_all, w_down_all):
    b, s, d = x.shape
    t = b * s
    dg = sgu_ln_g.shape[0]
    e = w_router.shape[1]
    cap = CAPACITY_FACTOR * s // e
    assert s % QP == 0 and s % TM == 0 and w_gate_all.shape[2] % FC == 0 and e % EG == 0
    assert cap % SLOT_ALIGN == 0 and cap >= WIN and s // MXU_DIM < LANES and d == D_ATTN + dg
    assert (s // MXU_DIM) % 2 == 0

    mod = _adaln(c, w_ada_all, b_ada, layer).reshape(b, 6, d)

    x2 = x.reshape(t, d)
    q, kv, u, vg = _proj(x2, mod, norm_pre_mix.reshape(1, d), w_in.astype(BF16), s)

    bias = jnp.asarray(_attn_bias_table())
    bs_full = jnp.repeat(b_s.T, dg // N_GMLP_GROUPS, axis=1)
    x1, h2, aff_t = _mix(
        x2, q, kv, u, vg, mod, sink, bias, w_s.astype(BF16), bs_full,
        sgu_ln_g.reshape(1, dg), sgu_ln_b.reshape(1, dg), norm_out_attn.reshape(1, D_ATTN),
        norm_out_gmlp.reshape(1, dg), w_out.astype(BF16), norm_post_mix.reshape(1, d),
        norm_pre_ffn.reshape(1, d), w_router.T.astype(BF16), b, s)

    slots3, aff3, cnt2 = _route(aff_t.reshape(b * e, s), cap)
    nblk = s // MXU_DIM
    slots4 = slots3.reshape(b, e, nblk, MXU_DIM)
    aff4 = aff3.reshape(b, e, nblk, MXU_DIM)
    xg, gs = _gather(cnt2, h2.reshape(b, s, d), slots4, aff4, cap)
    eo = _ffn(xg, gs, w_gate_all, w_up_all, w_down_all, layer * e)
    out = _combine(cnt2, eo, slots4, x1, mod, norm_post_ffn.reshape(1, d))
    return out.reshape(b, s, d)


def kernel(x, c, w_ada, b_ada, norm_pre_mix, norm_post_mix, w_in, sink, sgu_ln_g, sgu_ln_b, w_s, b_s, norm_out_attn, norm_out_gmlp, w_out, norm_pre_ffn, norm_post_ffn, w_router, w_gate, w_up, w_down):
    depth, d = w_ada.shape[0], w_ada.shape[1]
    w_ada_all = w_ada.reshape(depth * d, w_ada.shape[2])
    stack = lambda w: w.reshape((depth * w.shape[1],) + w.shape[2:])
    w_gate_all, w_up_all, w_down_all = stack(w_gate), stack(w_up), stack(w_down)
    for l in range(depth):
        x = _layer(l, x, c, w_ada_all, b_ada[l], norm_pre_mix[l], norm_post_mix[l], w_in[l],
                   sink[l], sgu_ln_g[l], sgu_ln_b[l], w_s[l], b_s[l], norm_out_attn[l],
                   norm_out_gmlp[l], w_out[l], norm_pre_ffn[l], norm_post_ffn[l], w_router[l],
                   w_gate_all, w_up_all, w_down_all)
    return x
```

```python
import functools

import numpy as np
import jax
import jax.numpy as jnp
from jax import lax
from jax.experimental import pallas as pl
from jax.experimental.pallas import tpu as pltpu

N_HEADS = 8
N_KV_HEADS = 2
HEAD_DIM = 64
D_ATTN = N_HEADS * HEAD_DIM
D_KV = N_KV_HEADS * HEAD_DIM
GROUP = N_HEADS // N_KV_HEADS
WINDOW = 128
BLOCK = 128
N_GMLP_GROUPS = 8
N_EXPERTS = 16
CAPACITY_FACTOR = 2
EPS = 1e-6
MASK_VALUE = -1e30

LANES = 128
MXU_DIM = 256

TM = 1024
FC = 256
EG = 4
QP = 1024
WIN = 64
SLOT_ALIGN = 16
VMEM_LIMIT = 56 * 1024 * 1024

BF16 = jnp.bfloat16
F32 = jnp.float32


def _rms(x, g):
    return x * lax.rsqrt(jnp.mean(x * x, axis=-1, keepdims=True) + EPS) * g


def _dot(a, b):
    return jnp.dot(a, b, preferred_element_type=F32)


def _dot_nt(a, b):
    return lax.dot_general(a, b, (((1,), (1,)), ((), ())), preferred_element_type=F32)


def _dot_tn(a, b):
    return lax.dot_general(a, b, (((0,), (0,)), ((), ())), preferred_element_type=F32)


def _adaln_kernel(c_ref, w_ref, b_ref, o_ref):
    c = c_ref[...]
    a = (c * jax.nn.sigmoid(c)).astype(BF16)
    o_ref[...] = _dot(a, w_ref[...].astype(BF16)) + b_ref[...]


def _adaln(c, w_ada_all, b_ada, layer):
    b, d = c.shape
    n = w_ada_all.shape[1]
    return pl.pallas_call(
        _adaln_kernel,
        grid=(n // d,),
        in_specs=[
            pl.BlockSpec((b, d), lambda j: (0, 0)),
            pl.BlockSpec((d, d), lambda j: (layer, j)),
            pl.BlockSpec((1, d), lambda j: (0, j)),
        ],
        out_specs=pl.BlockSpec((b, d), lambda j: (0, j)),
        out_shape=jax.ShapeDtypeStruct((b, n), F32),
        compiler_params=pltpu.CompilerParams(dimension_semantics=("arbitrary",)),
        name="adaln",
    )(c, w_ada_all, b_ada.reshape(1, n))


def _gelu_tanh(x):
    c = np.float32(np.sqrt(2.0 / np.pi))
    return x * (0.5 * (1.0 + jnp.tanh(c * (x + 0.044715 * (x * x * x)))))


def _proj_kernel(x_ref, mod_ref, g_ref, w_ref, q_ref, kv_ref, u_ref, vg_ref):
    x = x_ref[...]
    shift = mod_ref[0, 0:1, :]
    scale = mod_ref[0, 1:2, :]
    h = (_rms(x, g_ref[...]) * (1.0 + scale) + shift).astype(BF16)
    n_att = D_ATTN + 2 * D_KV
    dg = u_ref.shape[1]
    z = _gelu_tanh(_dot(h, w_ref[:, n_att:]))
    u_ref[...] = z[:, :dg]
    vg_ref[...] = z[:, dg:]
    p = _dot(h, w_ref[:, :n_att])
    q_ref[...] = (p[:, :D_ATTN] * (HEAD_DIM ** -0.5)).astype(BF16)
    lo = lax.broadcasted_iota(jnp.int32, (1, LANES), 1) < HEAD_DIM
    for j in range(2):
        t = p[:, D_ATTN + j * D_KV:D_ATTN + (j + 1) * D_KV]
        t_sw = pltpu.roll(t, HEAD_DIM, axis=1)
        kv_ref[:, (2 * j) * LANES:(2 * j + 1) * LANES] = jnp.where(lo, t, t_sw).astype(BF16)
        kv_ref[:, (2 * j + 1) * LANES:(2 * j + 2) * LANES] = jnp.where(lo, t_sw, t).astype(BF16)


def _proj(x2, mod, g_pre, w_in, s):
    t, d = x2.shape
    nw = w_in.shape[1]
    dg = (nw - D_ATTN - 2 * D_KV) // 2
    nsb = s // TM
    return pl.pallas_call(
        _proj_kernel,
        grid=(t // TM,),
        in_specs=[
            pl.BlockSpec((TM, d), lambda i: (i, 0)),
            pl.BlockSpec((1, 6, d), lambda i: (i // nsb, 0, 0)),
            pl.BlockSpec((1, d), lambda i: (0, 0)),
            pl.BlockSpec((d, nw), lambda i: (0, 0)),
        ],
        out_specs=[
            pl.BlockSpec((TM, D_ATTN), lambda i: (i, 0)),
            pl.BlockSpec((TM, D_ATTN), lambda i: (i, 0)),
            pl.BlockSpec((TM, dg), lambda i: (i, 0)),
            pl.BlockSpec((TM, dg), lambda i: (i, 0)),
        ],
        out_shape=[
            jax.ShapeDtypeStruct((t, D_ATTN), BF16),
            jax.ShapeDtypeStruct((t, D_ATTN), BF16),
            jax.ShapeDtypeStruct((t, dg), F32),
            jax.ShapeDtypeStruct((t, dg), F32),
        ],
        compiler_params=pltpu.CompilerParams(
            dimension_semantics=("arbitrary",), vmem_limit_bytes=VMEM_LIMIT),
        name="proj",
    )(x2, mod, g_pre, w_in)


def _attn_bias_table():
    qi = np.arange(BLOCK)[:, None]
    kj = np.arange(3 * BLOCK)[None, :] - BLOCK
    dist = np.abs(qi - kj).astype(np.float32)
    band = dist <= WINDOW
    slopes = np.exp2(-8.0 * np.arange(1, N_HEADS + 1, dtype=np.float32) / N_HEADS).astype(np.float32)
    key_ok = [kj >= 0, np.ones_like(kj, bool), kj < BLOCK]
    tab = np.empty((3, N_HEADS, BLOCK, 3 * BLOCK), np.float32)
    for v in range(3):
        ok = band & key_ok[v]
        for h in range(N_HEADS):
            tab[v, h] = np.where(ok, -slopes[h] * dist, np.float32(MASK_VALUE))
    return tab.reshape(3, N_KV_HEADS, GROUP * BLOCK, 3 * BLOCK)


def _mix_kernel(nblk_seq, sink_ref, x_ref, q_ref, kvp_ref, kvc_ref, kvn_ref, u_ref, vg_ref,
                mod_ref, bias_ref, ws_ref, bs_ref, lng_ref, lnb_ref, ga_ref, gg_ref,
                wout_ref, gpost_ref, gpre2_ref, wr_ref,
                x1_ref, h2_ref, aff_ref, kcat, attn_s, gm_s):
    i = pl.program_id(0)
    nb_tile = TM // BLOCK
    kcat[0:BLOCK, :] = kvp_ref[...]
    kcat[BLOCK:BLOCK + TM, :] = kvc_ref[...]
    kcat[BLOCK + TM:, :] = kvn_ref[...]
    lane = lax.broadcasted_iota(jnp.int32, (1, LANES), 1)
    lo = lane < HEAD_DIM
    zero = jnp.zeros((), BF16)
    head_of_row = lax.broadcasted_iota(jnp.int32, (GROUP * BLOCK, 1), 0) // BLOCK
    sinks = []
    for g in range(N_KV_HEADS):
        col = jnp.zeros((GROUP * BLOCK, 1), F32)
        for j in range(GROUP):
            col = jnp.where(head_of_row == j, sink_ref[g * GROUP + j], col)
        sinks.append(col)

    def block_body(a, carry):
        r0 = pl.multiple_of(a * BLOCK, BLOCK)
        pos = (i * nb_tile + a) % nblk_seq
        var = jnp.where(pos == 0, 0, jnp.where(pos == nblk_seq - 1, 2, 1))
        kv = kcat[pl.ds(r0, 3 * BLOCK), :]
        for g in range(N_KV_HEADS):
            k = kv[:, g * LANES:(g + 1) * LANES]
            v = kv[:, (N_KV_HEADS + g) * LANES:(N_KV_HEADS + g + 1) * LANES]
            pieces = []
            for pi in range(GROUP // 2):
                qp = q_ref[pl.ds(r0, BLOCK), (g * GROUP // 2 + pi) * LANES:(g * GROUP // 2 + pi + 1) * LANES]
                pieces += [jnp.where(lo, qp, zero), jnp.where(lo, zero, qp)]
            s = _dot_nt(jnp.concatenate(pieces, axis=0), k) + bias_ref[var, g]
            m = jnp.maximum(jnp.max(s, axis=-1, keepdims=True), sinks[g])
            p = jnp.exp(s - m)
            den = jnp.sum(p, axis=-1, keepdims=True) + jnp.exp(sinks[g] - m)
            o = _dot(p.astype(BF16), v) / den
            for pi in range(GROUP // 2):
                even = o[(2 * pi) * BLOCK:(2 * pi + 1) * BLOCK, :]
                odd = o[(2 * pi + 1) * BLOCK:(2 * pi + 2) * BLOCK, :]
                col0 = (g * GROUP // 2 + pi) * LANES
                attn_s[pl.ds(r0, BLOCK), col0:col0 + LANES] = jnp.where(lo, even, odd)
        vg = vg_ref[pl.ds(r0, BLOCK), :]
        mu = jnp.mean(vg, axis=-1, keepdims=True)
        xc = vg - mu
        vn = (xc * lax.rsqrt(jnp.mean(xc * xc, axis=-1, keepdims=True) + EPS) * lng_ref[...]
              + lnb_ref[...]).astype(BF16)
        for pi in range(N_GMLP_GROUPS // 2):
            vnp = vn[:, pi * LANES:(pi + 1) * LANES]
            z = jnp.where(lo, _dot(ws_ref[2 * pi], vnp), _dot(ws_ref[2 * pi + 1], vnp))
            sl = slice(pi * LANES, (pi + 1) * LANES)
            gm_s[pl.ds(r0, BLOCK), sl] = u_ref[pl.ds(r0, BLOCK), sl] * (z + bs_ref[:, sl])
        return carry

    lax.fori_loop(0, nb_tile, block_body, 0, unroll=2)

    na = _rms(attn_s[...], ga_ref[...]).astype(BF16)
    ng = _rms(gm_s[...], gg_ref[...]).astype(BF16)
    mo = _dot(na, wout_ref[0:D_ATTN, :]) + _dot(ng, wout_ref[D_ATTN:, :])
    gate1 = mod_ref[0, 2:3, :]
    x1 = x_ref[...] + gate1 * _rms(mo, gpost_ref[...])
    x1_ref[...] = x1
    h2 = (_rms(x1, gpre2_ref[...]) * (1.0 + mod_ref[0, 4:5, :]) + mod_ref[0, 3:4, :]).astype(BF16)
    h2_ref[...] = h2
    lg = _dot_nt(wr_ref[...], h2)
    lg = lg - jnp.max(lg, axis=0, keepdims=True)
    ex = jnp.exp(lg)
    aff_ref[0] = ex / jnp.sum(ex, axis=0, keepdims=True)


def _mix(x2, q, kv, u, vg, mod, sink, bias, ws, bs_full, ln_g, ln_b, g_attn, g_gmlp,
         w_out, g_post, g_pre2, w_r_t, b, s):
    t, d = x2.shape
    dg = u.shape[1]
    e = w_r_t.shape[0]
    nsb = s // TM
    nb_tile = TM // BLOCK
    nblk = t // BLOCK
    row = lambda i: (i, 0)
    const2 = lambda i: (0, 0)
    in_specs = [
        pl.BlockSpec(memory_space=pltpu.SMEM),
        pl.BlockSpec((TM, d), row),
        pl.BlockSpec((TM, D_ATTN), row),
        pl.BlockSpec((BLOCK, 4 * LANES), lambda i: (jnp.maximum(i * nb_tile - 1, 0), 0)),
        pl.BlockSpec((TM, 4 * LANES), row),
        pl.BlockSpec((BLOCK, 4 * LANES), lambda i: (jnp.minimum((i + 1) * nb_tile, nblk - 1), 0)),
        pl.BlockSpec((TM, dg), row),
        pl.BlockSpec((TM, dg), row),
        pl.BlockSpec((1, 6, d), lambda i: (i // nsb, 0, 0)),
        pl.BlockSpec(bias.shape, lambda i: (0, 0, 0, 0)),
        pl.BlockSpec(ws.shape, lambda i: (0, 0, 0)),
        pl.BlockSpec((BLOCK, dg), const2),
        pl.BlockSpec((1, dg), const2),
        pl.BlockSpec((1, dg), const2),
        pl.BlockSpec((1, D_ATTN), const2),
        pl.BlockSpec((1, dg), const2),
        pl.BlockSpec(w_out.shape, const2),
        pl.BlockSpec((1, d), const2),
        pl.BlockSpec((1, d), const2),
        pl.BlockSpec(w_r_t.shape, const2),
    ]
    out_specs = [
        pl.BlockSpec((TM, d), row),
        pl.BlockSpec((TM, d), row),
        pl.BlockSpec((1, e, TM), lambda i: (i // nsb, 0, i % nsb)),
    ]
    out_shape = [
        jax.ShapeDtypeStruct((t, d), F32),
        jax.ShapeDtypeStruct((t, d), BF16),
        jax.ShapeDtypeStruct((b, e, s), F32),
    ]
    return pl.pallas_call(
        functools.partial(_mix_kernel, s // BLOCK),
        grid=(t // TM,),
        in_specs=in_specs,
        out_specs=out_specs,
        out_shape=out_shape,
        scratch_shapes=[
            pltpu.VMEM((TM + 2 * BLOCK, 4 * LANES), BF16),
            pltpu.VMEM((TM, D_ATTN), F32),
            pltpu.VMEM((TM, dg), F32),
        ],
        compiler_params=pltpu.CompilerParams(
            dimension_semantics=("arbitrary",), vmem_limit_bytes=VMEM_LIMIT),
        name="mix",
    )(sink, x2, q, kv, kv, kv, u, vg, mod, bias, ws, bs_full, ln_g, ln_b, g_attn, g_gmlp,
      w_out, g_post, g_pre2, w_r_t)


def _route_kernel(cap, aff_ref, slot_ref, affb_ref, cnt_ref):
    aff = aff_ref[...]
    e, s = aff.shape
    capf = jnp.float32(cap)

    def count_ge(t):
        return jnp.sum(jnp.where(aff >= t, 1.0, 0.0), axis=1, keepdims=True)

    def bit_body(it, cand):
        trial = cand | jnp.left_shift(jnp.int32(1), 30 - it)
        t = lax.bitcast_convert_type(trial, F32)
        return jnp.where(count_ge(t) >= capf, trial, cand)

    cand = lax.fori_loop(0, 31, bit_body, jnp.zeros((e, 1), jnp.int32))
    thr = lax.bitcast_convert_type(cand, F32)
    need = capf - jnp.sum(jnp.where(aff > thr, 1.0, 0.0), axis=1, keepdims=True)

    r = lax.broadcasted_iota(jnp.int32, (LANES, LANES), 0)
    c = lax.broadcasted_iota(jnp.int32, (LANES, LANES), 1)
    tri = jnp.where(r < c, 1.0, 0.0).astype(BF16)
    lane = lax.broadcasted_iota(jnp.int32, (e, LANES), 1)
    carry_eq = jnp.zeros((e, 1), F32)
    carry_sel = jnp.zeros((e, 1), F32)
    cnt = jnp.zeros((e, LANES), F32)
    per_blk = MXU_DIM // LANES
    for j in range(s // LANES):
        a = aff[:, j * LANES:(j + 1) * LANES]
        gt = a > thr
        eq = jnp.where(a == thr, 1.0, 0.0)
        tie_rank = _dot(eq.astype(BF16), tri) + carry_eq
        sel = jnp.where(gt | ((eq > 0.0) & (tie_rank < need)), 1.0, 0.0)
        slot = _dot(sel.astype(BF16), tri) + carry_sel
        blk = (slice(None), j // per_blk, slice((j % per_blk) * LANES, (j % per_blk + 1) * LANES))
        slot_ref[blk] = jnp.where(sel > 0.0, slot, -1.0).astype(jnp.int32)
        affb_ref[blk] = a
        if j % per_blk == 0:
            cnt = jnp.where(lane == j // per_blk, carry_sel, cnt)
        carry_eq = carry_eq + jnp.sum(eq, axis=1, keepdims=True)
        carry_sel = carry_sel + jnp.sum(sel, axis=1, keepdims=True)
    cnt = jnp.where(lane >= s // MXU_DIM, carry_sel, cnt)
    cnt_ref[...] = cnt.astype(jnp.int32)


def _route(aff2, cap):
    r, s = aff2.shape
    nblk = s // MXU_DIM
    return pl.pallas_call(
        functools.partial(_route_kernel, cap),
        grid=(1,),
        in_specs=[pl.BlockSpec((r, s), lambda i: (0, 0))],
        out_specs=[
            pl.BlockSpec((r, nblk, MXU_DIM), lambda i: (0, 0, 0)),
            pl.BlockSpec((r, nblk, MXU_DIM), lambda i: (0, 0, 0)),
            pl.BlockSpec((r, LANES), lambda i: (0, 0)),
        ],
        out_shape=[
            jax.ShapeDtypeStruct((r, nblk, MXU_DIM), jnp.int32),
            jax.ShapeDtypeStruct((r, nblk, MXU_DIM), F32),
            jax.ShapeDtypeStruct((r, LANES), jnp.int32),
        ],
        compiler_params=pltpu.CompilerParams(dimension_semantics=("arbitrary",)),
        name="route",
    )(aff2)


def _segment(cnt_ref, row, pb, cap):
    c0 = jnp.minimum(cnt_ref[row, pb], cap)
    c1 = jnp.minimum(cnt_ref[row, pb + 1], cap)
    a0 = jnp.minimum(jnp.bitwise_and(c0, -SLOT_ALIGN), cap - WIN)
    return c0, c1, pl.multiple_of(a0, SLOT_ALIGN)


def _extra_windows(c1, a0):
    return (jnp.maximum(c1 - (a0 + WIN), 0) + WIN - 1) // WIN


def _gather_kernel(n_exp, cnt_ref, h2_ref, slot_ref, aff_ref, x_ref, gs_ref):
    b, eg = pl.program_id(0), pl.program_id(1)
    ng, cap = x_ref.shape[1], x_ref.shape[2]
    nblk = slot_ref.shape[2]
    sub = lax.broadcasted_iota(jnp.int32, (WIN, MXU_DIM), 0)
    sub_col = lax.broadcasted_iota(jnp.int32, (WIN, 1), 0)
    x_ref[...] = jnp.zeros_like(x_ref)
    gs_ref[...] = jnp.zeros_like(gs_ref)

    def merge(k, a0, first, last, rows, gate):
        slot_id = sub_col + a0
        keep = (slot_id >= first) & (slot_id < last)
        win = (0, k, pl.ds(a0, WIN), slice(None))
        x_ref[win] = jnp.where(keep, rows.astype(BF16), x_ref[win])
        gs_ref[win] = jnp.where(keep, gate, gs_ref[win])

    def tokens(pb):
        return pl.ds(pl.multiple_of(pb * MXU_DIM, MXU_DIM), MXU_DIM)

    def first_windows(pb):
        segs, gates, onehots = [], [], []
        overflow = jnp.int32(0)
        for k in range(ng):
            c0, c1, a0 = _segment(cnt_ref, b * n_exp + eg * ng + k, pb, cap)
            match = slot_ref[0, k, pl.ds(pb, 1), :] == (sub + a0)
            onehots.append(jnp.where(match, 1.0, 0.0).astype(BF16))
            gates.append(jnp.sum(jnp.where(match, aff_ref[0, k, pl.ds(pb, 1), :], 0.0),
                                 axis=1, keepdims=True))
            segs.append((c0, c1, a0))
            overflow = overflow + jnp.maximum(c1 - (a0 + WIN), 0)
        rows = _dot(jnp.concatenate(onehots, axis=0), h2_ref[0, tokens(pb), :])
        for k in range(ng):
            c0, c1, a0 = segs[k]
            merge(k, a0, c0, c1, rows[k * WIN:(k + 1) * WIN, :], gates[k])
        return overflow

    def further_windows(pb):
        def per_expert(k, carry2):
            _, c1, a0 = _segment(cnt_ref, b * n_exp + eg * ng + k, pb, cap)

            def per_window(w, carry3):
                first = a0 + (w + 1) * WIN
                aw = pl.multiple_of(jnp.minimum(first, cap - WIN), SLOT_ALIGN)
                srow = slot_ref[0, k, pl.ds(pb, 1), :]
                match = (srow == (sub + aw)) & (srow >= first)
                onehot = jnp.where(match, 1.0, 0.0).astype(BF16)
                gate = jnp.sum(jnp.where(match, aff_ref[0, k, pl.ds(pb, 1), :], 0.0),
                               axis=1, keepdims=True)
                merge(k, aw, first, c1, _dot(onehot, h2_ref[0, tokens(pb), :]), gate)
                return carry3

            return lax.fori_loop(0, _extra_windows(c1, a0), per_window, carry2)

        lax.fori_loop(0, ng, per_expert, 0)

    def pair_body(i, carry):
        overflows = [first_windows(2 * i + j) for j in range(2)]
        for j in range(2):
            pl.when(overflows[j] > 0)(functools.partial(further_windows, 2 * i + j))
        return carry

    lax.fori_loop(0, nblk // 2, pair_body, 0)


def _gather(cnt2, h2, slots4, aff4, cap):
    b, s, d = h2.shape
    e = slots4.shape[1]
    nblk = s // MXU_DIM
    grid_spec = pltpu.PrefetchScalarGridSpec(
        num_scalar_prefetch=1,
        grid=(b, e // EG),
        in_specs=[
            pl.BlockSpec((1, s, d), lambda bi, gi, c: (bi, 0, 0), pipeline_mode=pl.Buffered(1)),
            pl.BlockSpec((1, EG, nblk, MXU_DIM), lambda bi, gi, c: (bi, gi, 0, 0)),
            pl.BlockSpec((1, EG, nblk, MXU_DIM), lambda bi, gi, c: (bi, gi, 0, 0)),
        ],
        out_specs=[
            pl.BlockSpec((1, EG, cap, d), lambda bi, gi, c: (bi, gi, 0, 0)),
            pl.BlockSpec((1, EG, cap, LANES), lambda bi, gi, c: (bi, gi, 0, 0)),
        ],
    )
    return pl.pallas_call(
        functools.partial(_gather_kernel, e),
        grid_spec=grid_spec,
        out_shape=[
            jax.ShapeDtypeStruct((b, e, cap, d), BF16),
            jax.ShapeDtypeStruct((b, e, cap, LANES), F32),
        ],
        compiler_params=pltpu.CompilerParams(
            dimension_semantics=("arbitrary", "arbitrary"), vmem_limit_bytes=VMEM_LIMIT),
        name="gather",
    )(cnt2, h2, slots4, aff4)


def _ffn_kernel(x_ref, gs_ref, wg_ref, wu_ref, wd_ref, out_ref, acc_s):
    f = pl.program_id(1)
    nb, _, cap, d = x_ref.shape

    @pl.when(f == 0)
    def _init():
        acc_s[...] = jnp.zeros_like(acc_s)

    x = x_ref[:, 0].reshape(nb * cap, d)
    g = _dot(x, wg_ref[0].astype(BF16))
    u = _dot(x, wu_ref[0].astype(BF16))
    hid = (g * jax.nn.sigmoid(g) * u).astype(BF16)
    acc_s[...] += _dot(hid, wd_ref[0].astype(BF16))

    @pl.when(f == pl.num_programs(1) - 1)
    def _emit():
        gate = gs_ref[:, 0, :, 0:1].reshape(nb * cap, 1)
        out_ref[:, 0] = (acc_s[...] * gate).reshape(nb, cap, d).astype(BF16)


def _ffn(xg, gs, w_gate, w_up, w_down, e0):
    b, e, cap, d = xg.shape
    ff = w_gate.shape[2]
    return pl.pallas_call(
        _ffn_kernel,
        grid=(e, ff // FC),
        in_specs=[
            pl.BlockSpec((b, 1, cap, d), lambda ei, fi: (0, ei, 0, 0)),
            pl.BlockSpec((b, 1, cap, LANES), lambda ei, fi: (0, ei, 0, 0)),
            pl.BlockSpec((1, d, FC), lambda ei, fi: (e0 + ei, 0, fi)),
            pl.BlockSpec((1, d, FC), lambda ei, fi: (e0 + ei, 0, fi)),
            pl.BlockSpec((1, FC, d), lambda ei, fi: (e0 + ei, fi, 0)),
        ],
        out_specs=pl.BlockSpec((b, 1, cap, d), lambda ei, fi: (0, ei, 0, 0)),
        out_shape=jax.ShapeDtypeStruct((b, e, cap, d), BF16),
        scratch_shapes=[pltpu.VMEM((b * cap, d), F32)],
        compiler_params=pltpu.CompilerParams(
            dimension_semantics=("arbitrary", "arbitrary"), vmem_limit_bytes=VMEM_LIMIT),
        name="ffn",
    )(xg, gs, w_gate, w_up, w_down)


def _combine_kernel(n_exp, cnt_ref, eo_ref, slot_ref, x1_ref, mod_ref, g_ref, o_ref,
                    stage_s, oh_s, y_s):
    b, q = pl.program_id(0), pl.program_id(1)
    cap = eo_ref.shape[2]
    nblk_q = o_ref.shape[0] // MXU_DIM
    sub = lax.broadcasted_iota(jnp.int32, (WIN, MXU_DIM), 0)
    gate = mod_ref[0, 5:6, :]

    for j in range(nblk_q):
        pb = q * nblk_q + j
        overflow = jnp.int32(0)
        for e in range(n_exp):
            _, c1, a0 = _segment(cnt_ref, b * n_exp + e, pb, cap)
            stage_s[e * WIN:(e + 1) * WIN, :] = eo_ref[0, e, pl.ds(a0, WIN), :]
            match = slot_ref[0, e, pl.ds(pb, 1), :] == (sub + a0)
            oh_s[e * WIN:(e + 1) * WIN, :] = jnp.where(match, 1.0, 0.0).astype(BF16)
            overflow = overflow + jnp.maximum(c1 - (a0 + WIN), 0)
        y_s[...] = _dot_tn(oh_s[...], stage_s[...])

        @pl.when(overflow > 0)
        def _long_segments(j=j, pb=pb):
            def per_expert(e, carry):
                _, c1, a0 = _segment(cnt_ref, b * n_exp + e, pb, cap)

                def per_window(k, carry2):
                    first = a0 + (k + 1) * WIN
                    aw = pl.multiple_of(jnp.minimum(first, cap - WIN), SLOT_ALIGN)
                    srow = slot_ref[0, e, pl.ds(pb, 1), :]
                    match = (srow == (sub + aw)) & (srow >= first)
                    onehot = jnp.where(match, 1.0, 0.0).astype(BF16)
                    y_s[...] += _dot_tn(onehot, eo_ref[0, e, pl.ds(aw, WIN), :])
                    return carry2

                return lax.fori_loop(0, _extra_windows(c1, a0), per_window, carry)

            lax.fori_loop(0, n_exp, per_expert, 0)

        sl = slice(j * MXU_DIM, (j + 1) * MXU_DIM)
        o_ref[sl, :] = x1_ref[sl, :] + gate * _rms(y_s[...], g_ref[...])


def _combine(cnt2, eo, slots4, x1, mod, g_post):
    b, e, cap, d = eo.shape
    t = x1.shape[0]
    s = t // b
    nq = s // QP
    nblk_q = QP // MXU_DIM
    grid_spec = pltpu.PrefetchScalarGridSpec(
        num_scalar_prefetch=1,
        grid=(b, nq),
        in_specs=[
            pl.BlockSpec((1, e, cap, d), lambda bi, qi, c: (bi, 0, 0, 0), pipeline_mode=pl.Buffered(1)),
            pl.BlockSpec((1, e, s // MXU_DIM, MXU_DIM), lambda bi, qi, c: (bi, 0, 0, 0)),
            pl.BlockSpec((QP, d), lambda bi, qi, c: (bi * nq + qi, 0)),
            pl.BlockSpec((1, 6, d), lambda bi, qi, c: (bi, 0, 0)),
            pl.BlockSpec((1, d), lambda bi, qi, c: (0, 0)),
        ],
        out_specs=pl.BlockSpec((QP, d), lambda bi, qi, c: (bi * nq + qi, 0)),
        scratch_shapes=[
            pltpu.VMEM((e * WIN, d), BF16),
            pltpu.VMEM((e * WIN, MXU_DIM), BF16),
            pltpu.VMEM((MXU_DIM, d), F32),
        ],
    )
    return pl.pallas_call(
        functools.partial(_combine_kernel, e),
        grid_spec=grid_spec,
        out_shape=jax.ShapeDtypeStruct((t, d), F32),
        compiler_params=pltpu.CompilerParams(
            dimension_semantics=("arbitrary", "arbitrary"),
            vmem_limit_bytes=VMEM_LIMIT),
        name="combine",
    )(cnt2, eo, slots4, x1, mod, g_post)


def _layer(layer, x, c, w_ada_all, b_ada, norm_pre_mix, norm_post_mix, w_in, sink, sgu_ln_g,
           sgu_ln_b, w_s, b_s, norm_out_attn, norm_out_gmlp, w_out, norm_pre_ffn, norm_post_ffn,
           w_router, w_gate_all, w_up_all, w_down_all):
    b, s, d = x.shape
    t = b * s
    dg = sgu_ln_g.shape[0]
    e = w_router.shape[1]
    cap = CAPACITY_FACTOR * s // e
    assert s % QP == 0 and s % TM == 0 and w_gate_all.shape[2] % FC == 0 and e % EG == 0
    assert cap % SLOT_ALIGN == 0 and cap >= WIN and s // MXU_DIM < LANES and d == D_ATTN + dg
    assert (s // MXU_DIM) % 2 == 0

    mod = _adaln(c, w_ada_all, b_ada, layer).reshape(b, 6, d)

    x2 = x.reshape(t, d)
    q, kv, u, vg = _proj(x2, mod, norm_pre_mix.reshape(1, d), w_in.astype(BF16), s)

    bias = jnp.asarray(_attn_bias_table())
    bs_full = jnp.repeat(b_s.T, dg // N_GMLP_GROUPS, axis=1)
    x1, h2, aff_t = _mix(
        x2, q, kv, u, vg, mod, sink, bias, w_s.astype(BF16), bs_full,
        sgu_ln_g.reshape(1, dg), sgu_ln_b.reshape(1, dg), norm_out_attn.reshape(1, D_ATTN),
        norm_out_gmlp.reshape(1, dg), w_out.astype(BF16), norm_post_mix.reshape(1, d),
        norm_pre_ffn.reshape(1, d), w_router.T.astype(BF16), b, s)

    slots3, aff3, cnt2 = _route(aff_t.reshape(b * e, s), cap)
    nblk = s // MXU_DIM
    slots4 = slots3.reshape(b, e, nblk, MXU_DIM)
    aff4 = aff3.reshape(b, e, nblk, MXU_DIM)
    xg, gs = _gather(cnt2, h2.reshape(b, s, d), slots4, aff4, cap)
    eo = _ffn(xg, gs, w_gate_all, w_up_all, w_down_all, layer * e)
    out = _combine(cnt2, eo, slots4, x1, mod, norm_post_ffn.reshape(1, d))
    return out.reshape(b, s, d)


def kernel(x, c, w_ada, b_ada, norm_pre_mix, norm_post_mix, w_in, sink, sgu_ln_g, sgu_ln_b, w_s, b_s, norm_out_attn, norm_out_gmlp, w_out, norm_pre_ffn, norm_post_ffn, w_router, w_gate, w_up, w_down):
    depth, d = w_ada.shape[0], w_ada.shape[1]
    w_ada_all = w_ada.reshape(depth * d, w_ada.shape[2])
    stack = lambda w: w.reshape((depth * w.shape[1],) + w.shape[2:])
    w_gate_all, w_up_all, w_down_all = stack(w_gate), stack(w_up), stack(w_down)
    for l in range(depth):
        x = _layer(l, x, c, w_ada_all, b_ada[l], norm_pre_mix[l], norm_post_mix[l], w_in[l],
                   sink[l], sgu_ln_g[l], sgu_ln_b[l], w_s[l], b_s[l], norm_out_attn[l],
                   norm_out_gmlp[l], w_out[l], norm_pre_ffn[l], norm_post_ffn[l], w_router[l],
                   w_gate_all, w_up_all, w_down_all)
    return x
```

```python
import functools

import numpy as np
import jax
import jax.numpy as jnp
from jax import lax
from jax.experimental import pallas as pl
from jax.experimental.pallas import tpu as pltpu

N_HEADS = 8
N_KV_HEADS = 2
HEAD_DIM = 64
D_ATTN = N_HEADS * HEAD_DIM
D_KV = N_KV_HEADS * HEAD_DIM
GROUP = N_HEADS // N_KV_HEADS
WINDOW = 128
BLOCK = 128
N_GMLP_GROUPS = 8
N_EXPERTS = 16
CAPACITY_FACTOR = 2
EPS = 1e-6
MASK_VALUE = -1e30

LANES = 128
MXU_DIM = 256

TM = 1024
FC = 256
EG = 4
QP = 1024
WIN = 64
SLOT_ALIGN = 16
VMEM_LIMIT = 56 * 1024 * 1024

BF16 = jnp.bfloat16
F32 = jnp.float32


def _rms(x, g):
    return x * lax.rsqrt(jnp.mean(x * x, axis=-1, keepdims=True) + EPS) * g


def _dot(a, b):
    return jnp.dot(a, b, preferred_element_type=F32)


def _dot_nt(a, b):
    return lax.dot_general(a, b, (((1,), (1,)), ((), ())), preferred_element_type=F32)


def _dot_tn(a, b):
    return lax.dot_general(a, b, (((0,), (0,)), ((), ())), preferred_element_type=F32)


def _adaln_kernel(c_ref, w_ref, b_ref, o_ref):
    c = c_ref[...]
    a = (c * jax.nn.sigmoid(c)).astype(BF16)
    o_ref[...] = _dot(a, w_ref[...].astype(BF16)) + b_ref[...]


def _adaln(c, w_ada_all, b_ada, layer):
    b, d = c.shape
    n = w_ada_all.shape[1]
    return pl.pallas_call(
        _adaln_kernel,
        grid=(n // d,),
        in_specs=[
            pl.BlockSpec((b, d), lambda j: (0, 0)),
            pl.BlockSpec((d, d), lambda j: (layer, j)),
            pl.BlockSpec((1, d), lambda j: (0, j)),
        ],
        out_specs=pl.BlockSpec((b, d), lambda j: (0, j)),
        out_shape=jax.ShapeDtypeStruct((b, n), F32),
        compiler_params=pltpu.CompilerParams(dimension_semantics=("arbitrary",)),
        name="adaln",
    )(c, w_ada_all, b_ada.reshape(1, n))


def _gelu_tanh(x):
    c = np.float32(np.sqrt(2.0 / np.pi))
    return x * (0.5 * (1.0 + jnp.tanh(c * (x + 0.044715 * (x * x * x)))))


def _proj_kernel(x_ref, mod_ref, g_ref, w_ref, q_ref, kv_ref, u_ref, vg_ref):
    x = x_ref[...]
    shift = mod_ref[0, 0:1, :]
    scale = mod_ref[0, 1:2, :]
    h = (_rms(x, g_ref[...]) * (1.0 + scale) + shift).astype(BF16)
    n_att = D_ATTN + 2 * D_KV
    dg = u_ref.shape[1]
    z = _gelu_tanh(_dot(h, w_ref[:, n_att:]))
    u_ref[...] = z[:, :dg]
    vg_ref[...] = z[:, dg:]
    p = _dot(h, w_ref[:, :n_att])
    q_ref[...] = (p[:, :D_ATTN] * (HEAD_DIM ** -0.5)).astype(BF16)
    lo = lax.broadcasted_iota(jnp.int32, (1, LANES), 1) < HEAD_DIM
    for j in range(2):
        t = p[:, D_ATTN + j * D_KV:D_ATTN + (j + 1) * D_KV]
        t_sw = pltpu.roll(t, HEAD_DIM, axis=1)
        kv_ref[:, (2 * j) * LANES:(2 * j + 1) * LANES] = jnp.where(lo, t, t_sw).astype(BF16)
        kv_ref[:, (2 * j + 1) * LANES:(2 * j + 2) * LANES] = jnp.where(lo, t_sw, t).astype(BF16)


def _proj(x2, mod, g_pre, w_in, s):
    t, d = x2.shape
    nw = w_in.shape[1]
    dg = (nw - D_ATTN - 2 * D_KV) // 2
    nsb = s // TM
    return pl.pallas_call(
        _proj_kernel,
        grid=(t // TM,),
        in_specs=[
            pl.BlockSpec((TM, d), lambda i: (i, 0)),
            pl.BlockSpec((1, 6, d), lambda i: (i // nsb, 0, 0)),
            pl.BlockSpec((1, d), lambda i: (0, 0)),
            pl.BlockSpec((d, nw), lambda i: (0, 0)),
        ],
        out_specs=[
            pl.BlockSpec((TM, D_ATTN), lambda i: (i, 0)),
            pl.BlockSpec((TM, D_ATTN), lambda i: (i, 0)),
            pl.BlockSpec((TM, dg), lambda i: (i, 0)),
            pl.BlockSpec((TM, dg), lambda i: (i, 0)),
        ],
        out_shape=[
            jax.ShapeDtypeStruct((t, D_ATTN), BF16),
            jax.ShapeDtypeStruct((t, D_ATTN), BF16),
            jax.ShapeDtypeStruct((t, dg), F32),
            jax.ShapeDtypeStruct((t, dg), F32),
        ],
        compiler_params=pltpu.CompilerParams(
            dimension_semantics=("arbitrary",), vmem_limit_bytes=VMEM_LIMIT),
        name="proj",
    )(x2, mod, g_pre, w_in)


def _attn_bias_table():
    qi = np.arange(BLOCK)[:, None]
    kj = np.arange(3 * BLOCK)[None, :] - BLOCK
    dist = np.abs(qi - kj).astype(np.float32)
    band = dist <= WINDOW
    slopes = np.exp2(-8.0 * np.arange(1, N_HEADS + 1, dtype=np.float32) / N_HEADS).astype(np.float32)
    key_ok = [kj >= 0, np.ones_like(kj, bool), kj < BLOCK]
    tab = np.empty((3, N_HEADS, BLOCK, 3 * BLOCK), np.float32)
    for v in range(3):
        ok = band & key_ok[v]
        for h in range(N_HEADS):
            tab[v, h] = np.where(ok, -slopes[h] * dist, np.float32(MASK_VALUE))
    return tab.reshape(3, N_KV_HEADS, GROUP * BLOCK, 3 * BLOCK)


def _mix_kernel(nblk_seq, sink_ref, x_ref, q_ref, kvp_ref, kvc_ref, kvn_ref, u_ref, vg_ref,
                mod_ref, bias_ref, ws_ref, bs_ref, lng_ref, lnb_ref, ga_ref, gg_ref,
                wout_ref, gpost_ref, gpre2_ref, wr_ref,
                x1_ref, h2_ref, aff_ref, kcat, attn_s, gm_s):
    i = pl.program_id(0)
    nb_tile = TM // BLOCK
    kcat[0:BLOCK, :] = kvp_ref[...]
    kcat[BLOCK:BLOCK + TM, :] = kvc_ref[...]
    kcat[BLOCK + TM:, :] = kvn_ref[...]
    lane = lax.broadcasted_iota(jnp.int32, (1, LANES), 1)
    lo = lane < HEAD_DIM
    zero = jnp.zeros((), BF16)
    head_of_row = lax.broadcasted_iota(jnp.int32, (GROUP * BLOCK, 1), 0) // BLOCK
    sinks = []
    for g in range(N_KV_HEADS):
        col = jnp.zeros((GROUP * BLOCK, 1), F32)
        for j in range(GROUP):
            col = jnp.where(head_of_row == j, sink_ref[g * GROUP + j], col)
        sinks.append(col)

    def block_body(a, carry):
        r0 = pl.multiple_of(a * BLOCK, BLOCK)
        pos = (i * nb_tile + a) % nblk_seq
        var = jnp.where(pos == 0, 0, jnp.where(pos == nblk_seq - 1, 2, 1))
        kv = kcat[pl.ds(r0, 3 * BLOCK), :]
        for g in range(N_KV_HEADS):
            k = kv[:, g * LANES:(g + 1) * LANES]
            v = kv[:, (N_KV_HEADS + g) * LANES:(N_KV_HEADS + g + 1) * LANES]
            pieces = []
            for pi in range(GROUP // 2):
                qp = q_ref[pl.ds(r0, BLOCK), (g * GROUP // 2 + pi) * LANES:(g * GROUP // 2 + pi + 1) * LANES]
                pieces += [jnp.where(lo, qp, zero), jnp.where(lo, zero, qp)]
            s = _dot_nt(jnp.concatenate(pieces, axis=0), k) + bias_ref[var, g]
            m = jnp.maximum(jnp.max(s, axis=-1, keepdims=True), sinks[g])
            p = jnp.exp(s - m)
            den = jnp.sum(p, axis=-1, keepdims=True) + jnp.exp(sinks[g] - m)
            o = _dot(p.astype(BF16), v) / den
            for pi in range(GROUP // 2):
                even = o[(2 * pi) * BLOCK:(2 * pi + 1) * BLOCK, :]
                odd = o[(2 * pi + 1) * BLOCK:(2 * pi + 2) * BLOCK, :]
                col0 = (g * GROUP // 2 + pi) * LANES
                attn_s[pl.ds(r0, BLOCK), col0:col0 + LANES] = jnp.where(lo, even, odd)
        vg = vg_ref[pl.ds(r0, BLOCK), :]
        mu = jnp.mean(vg, axis=-1, keepdims=True)
        xc = vg - mu
        vn = (xc * lax.rsqrt(jnp.mean(xc * xc, axis=-1, keepdims=True) + EPS) * lng_ref[...]
              + lnb_ref[...]).astype(BF16)
        for pi in range(N_GMLP_GROUPS // 2):
            vnp = vn[:, pi * LANES:(pi + 1) * LANES]
            z = jnp.where(lo, _dot(ws_ref[2 * pi], vnp), _dot(ws_ref[2 * pi + 1], vnp))
            sl = slice(pi * LANES, (pi + 1) * LANES)
            gm_s[pl.ds(r0, BLOCK), sl] = u_ref[pl.ds(r0, BLOCK), sl] * (z + bs_ref[:, sl])
        return carry

    lax.fori_loop(0, nb_tile, block_body, 0, unroll=2)

    na = _rms(attn_s[...], ga_ref[...]).astype(BF16)
    ng = _rms(gm_s[...], gg_ref[...]).astype(BF16)
    mo = _dot(na, wout_ref[0:D_ATTN, :]) + _dot(ng, wout_ref[D_ATTN:, :])
    gate1 = mod_ref[0, 2:3, :]
    x1 = x_ref[...] + gate1 * _rms(mo, gpost_ref[...])
    x1_ref[...] = x1
    h2 = (_rms(x1, gpre2_ref[...]) * (1.0 + mod_ref[0, 4:5, :]) + mod_ref[0, 3:4, :]).astype(BF16)
    h2_ref[...] = h2
    lg = _dot_nt(wr_ref[...], h2)
    lg = lg - jnp.max(lg, axis=0, keepdims=True)
    ex = jnp.exp(lg)
    aff_ref[0] = ex / jnp.sum(ex, axis=0, keepdims=True)


def _mix(x2, q, kv, u, vg, mod, sink, bias, ws, bs_full, ln_g, ln_b, g_attn, g_gmlp,
         w_out, g_post, g_pre2, w_r_t, b, s):
    t, d = x2.shape
    dg = u.shape[1]
    e = w_r_t.shape[0]
    nsb = s // TM
    nb_tile = TM // BLOCK
    nblk = t // BLOCK
    row = lambda i: (i, 0)
    const2 = lambda i: (0, 0)
    in_specs = [
        pl.BlockSpec(memory_space=pltpu.SMEM),
        pl.BlockSpec((TM, d), row),
        pl.BlockSpec((TM, D_ATTN), row),
        pl.BlockSpec((BLOCK, 4 * LANES), lambda i: (jnp.maximum(i * nb_tile - 1, 0), 0)),
        pl.BlockSpec((TM, 4 * LANES), row),
        pl.BlockSpec((BLOCK, 4 * LANES), lambda i: (jnp.minimum((i + 1) * nb_tile, nblk - 1), 0)),
        pl.BlockSpec((TM, dg), row),
        pl.BlockSpec((TM, dg), row),
        pl.BlockSpec((1, 6, d), lambda i: (i // nsb, 0, 0)),
        pl.BlockSpec(bias.shape, lambda i: (0, 0, 0, 0)),
        pl.BlockSpec(ws.shape, lambda i: (0, 0, 0)),
        pl.BlockSpec((BLOCK, dg), const2),
        pl.BlockSpec((1, dg), const2),
        pl.BlockSpec((1, dg), const2),
        pl.BlockSpec((1, D_ATTN), const2),
        pl.BlockSpec((1, dg), const2),
        pl.BlockSpec(w_out.shape, const2),
        pl.BlockSpec((1, d), const2),
        pl.BlockSpec((1, d), const2),
        pl.BlockSpec(w_r_t.shape, const2),
    ]
    out_specs = [
        pl.BlockSpec((TM, d), row),
        pl.BlockSpec((TM, d), row),
        pl.BlockSpec((1, e, TM), lambda i: (i // nsb, 0, i % nsb)),
    ]
    out_shape = [
        jax.ShapeDtypeStruct((t, d), F32),
        jax.ShapeDtypeStruct((t, d), BF16),
        jax.ShapeDtypeStruct((b, e, s), F32),
    ]
    return pl.pallas_call(
        functools.partial(_mix_kernel, s // BLOCK),
        grid=(t // TM,),
        in_specs=in_specs,
        out_specs=out_specs,
        out_shape=out_shape,
        scratch_shapes=[
            pltpu.VMEM((TM + 2 * BLOCK, 4 * LANES), BF16),
            pltpu.VMEM((TM, D_ATTN), F32),
            pltpu.VMEM((TM, dg), F32),
        ],
        compiler_params=pltpu.CompilerParams(
            dimension_semantics=("arbitrary",), vmem_limit_bytes=VMEM_LIMIT),
        name="mix",
    )(sink, x2, q, kv, kv, kv, u, vg, mod, bias, ws, bs_full, ln_g, ln_b, g_attn, g_gmlp,
      w_out, g_post, g_pre2, w_r_t)


def _route_kernel(cap, aff_ref, slot_ref, affb_ref, cnt_ref):
    aff = aff_ref[...]
    e, s = aff.shape
    capf = jnp.float32(cap)

    def count_ge(t):
        return jnp.sum(jnp.where(aff >= t, 1.0, 0.0), axis=1, keepdims=True)

    def bit_body(it, cand):
        trial = cand | jnp.left_shift(jnp.int32(1), 30 - it)
        t = lax.bitcast_convert_type(trial, F32)
        return jnp.where(count_ge(t) >= capf, trial, cand)

    cand = lax.fori_loop(0, 31, bit_body, jnp.zeros((e, 1), jnp.int32))
    thr = lax.bitcast_convert_type(cand, F32)
    need = capf - jnp.sum(jnp.where(aff > thr, 1.0, 0.0), axis=1, keepdims=True)

    r = lax.broadcasted_iota(jnp.int32, (LANES, LANES), 0)
    c = lax.broadcasted_iota(jnp.int32, (LANES, LANES), 1)
    tri = jnp.where(r < c, 1.0, 0.0).astype(BF16)
    lane = lax.broadcasted_iota(jnp.int32, (e, LANES), 1)
    carry_eq = jnp.zeros((e, 1), F32)
    carry_sel = jnp.zeros((e, 1), F32)
    cnt = jnp.zeros((e, LANES), F32)
    per_blk = MXU_DIM // LANES
    for j in range(s // LANES):
        a = aff[:, j * LANES:(j + 1) * LANES]
        gt = a > thr
        eq = jnp.where(a == thr, 1.0, 0.0)
        tie_rank = _dot(eq.astype(BF16), tri) + carry_eq
        sel = jnp.where(gt | ((eq > 0.0) & (tie_rank < need)), 1.0, 0.0)
        slot = _dot(sel.astype(BF16), tri) + carry_sel
        blk = (slice(None), j // per_blk, slice((j % per_blk) * LANES, (j % per_blk + 1) * LANES))
        slot_ref[blk] = jnp.where(sel > 0.0, slot, -1.0).astype(jnp.int32)
        affb_ref[blk] = a
        if j % per_blk == 0:
            cnt = jnp.where(lane == j // per_blk, carry_sel, cnt)
        carry_eq = carry_eq + jnp.sum(eq, axis=1, keepdims=True)
        carry_sel = carry_sel + jnp.sum(sel, axis=1, keepdims=True)
    cnt = jnp.where(lane >= s // MXU_DIM, carry_sel, cnt)
    cnt_ref[...] = cnt.astype(jnp.int32)


def _route(aff2, cap):
    r, s = aff2.shape
    nblk = s // MXU_DIM
    return pl.pallas_call(
        functools.partial(_route_kernel, cap),
        grid=(1,),
        in_specs=[pl.BlockSpec((r, s), lambda i: (0, 0))],
        out_specs=[
            pl.BlockSpec((r, nblk, MXU_DIM), lambda i: (0, 0, 0)),
            pl.BlockSpec((r, nblk, MXU_DIM), lambda i: (0, 0, 0)),
            pl.BlockSpec((r, LANES), lambda i: (0, 0)),
        ],
        out_shape=[
            jax.ShapeDtypeStruct((r, nblk, MXU_DIM), jnp.int32),
            jax.ShapeDtypeStruct((r, nblk, MXU_DIM), F32),
            jax.ShapeDtypeStruct((r, LANES), jnp.int32),
        ],
        compiler_params=pltpu.CompilerParams(dimension_semantics=("arbitrary",)),
        name="route",
    )(aff2)


def _segment(cnt_ref, row, pb, cap):
    c0 = jnp.minimum(cnt_ref[row, pb], cap)
    c1 = jnp.minimum(cnt_ref[row, pb + 1], cap)
    a0 = jnp.minimum(jnp.bitwise_and(c0, -SLOT_ALIGN), cap - WIN)
    return c0, c1, pl.multiple_of(a0, SLOT_ALIGN)


def _extra_windows(c1, a0):
    return (jnp.maximum(c1 - (a0 + WIN), 0) + WIN - 1) // WIN


def _gather_kernel(n_exp, cnt_ref, h2_ref, slot_ref, aff_ref, x_ref, gs_ref):
    b, eg = pl.program_id(0), pl.program_id(1)
    ng, cap = x_ref.shape[1], x_ref.shape[2]
    nblk = slot_ref.shape[2]
    sub = lax.broadcasted_iota(jnp.int32, (WIN, MXU_DIM), 0)
    sub_col = lax.broadcasted_iota(jnp.int32, (WIN, 1), 0)
    x_ref[...] = jnp.zeros_like(x_ref)
    gs_ref[...] = jnp.zeros_like(gs_ref)

    def merge(k, a0, first, last, rows, gate):
        slot_id = sub_col + a0
        keep = (slot_id >= first) & (slot_id < last)
        win = (0, k, pl.ds(a0, WIN), slice(None))
        x_ref[win] = jnp.where(keep, rows.astype(BF16), x_ref[win])
        gs_ref[win] = jnp.where(keep, gate, gs_ref[win])

    def tokens(pb):
        return pl.ds(pl.multiple_of(pb * MXU_DIM, MXU_DIM), MXU_DIM)

    def first_windows(pb):
        segs, gates, onehots = [], [], []
        overflow = jnp.int32(0)
        for k in range(ng):
            c0, c1, a0 = _segment(cnt_ref, b * n_exp + eg * ng + k, pb, cap)
            match = slot_ref[0, k, pl.ds(pb, 1), :] == (sub + a0)
            onehots.append(jnp.where(match, 1.0, 0.0).astype(BF16))
            gates.append(jnp.sum(jnp.where(match, aff_ref[0, k, pl.ds(pb, 1), :], 0.0),
                                 axis=1, keepdims=True))
            segs.append((c0, c1, a0))
            overflow = overflow + jnp.maximum(c1 - (a0 + WIN), 0)
        rows = _dot(jnp.concatenate(onehots, axis=0), h2_ref[0, tokens(pb), :])
        for k in range(ng):
            c0, c1, a0 = segs[k]
            merge(k, a0, c0, c1, rows[k * WIN:(k + 1) * WIN, :], gates[k])
        return overflow

    def further_windows(pb):
        def per_expert(k, carry2):
            _, c1, a0 = _segment(cnt_ref, b * n_exp + eg * ng + k, pb, cap)

            def per_window(w, carry3):
                first = a0 + (w + 1) * WIN
                aw = pl.multiple_of(jnp.minimum(first, cap - WIN), SLOT_ALIGN)
                srow = slot_ref[0, k, pl.ds(pb, 1), :]
                match = (srow == (sub + aw)) & (srow >= first)
                onehot = jnp.where(match, 1.0, 0.0).astype(BF16)
                gate = jnp.sum(jnp.where(match, aff_ref[0, k, pl.ds(pb, 1), :], 0.0),
                               axis=1, keepdims=True)
                merge(k, aw, first, c1, _dot(onehot, h2_ref[0, tokens(pb), :]), gate)
                return carry3

            return lax.fori_loop(0, _extra_windows(c1, a0), per_window, carry2)

        lax.fori_loop(0, ng, per_expert, 0)

    def pair_body(i, carry):
        overflows = [first_windows(2 * i + j) for j in range(2)]
        for j in range(2):
            pl.when(overflows[j] > 0)(functools.partial(further_windows, 2 * i + j))
        return carry

    lax.fori_loop(0, nblk // 2, pair_body, 0)


def _gather(cnt2, h2, slots4, aff4, cap):
    b, s, d = h2.shape
    e = slots4.shape[1]
    nblk = s // MXU_DIM
    grid_spec = pltpu.PrefetchScalarGridSpec(
        num_scalar_prefetch=1,
        grid=(b, e // EG),
        in_specs=[
            pl.BlockSpec((1, s, d), lambda bi, gi, c: (bi, 0, 0), pipeline_mode=pl.Buffered(1)),
            pl.BlockSpec((1, EG, nblk, MXU_DIM), lambda bi, gi, c: (bi, gi, 0, 0)),
            pl.BlockSpec((1, EG, nblk, MXU_DIM), lambda bi, gi, c: (bi, gi, 0, 0)),
        ],
        out_specs=[
            pl.BlockSpec((1, EG, cap, d), lambda bi, gi, c: (bi, gi, 0, 0)),
            pl.BlockSpec((1, EG, cap, LANES), lambda bi, gi, c: (bi, gi, 0, 0)),
        ],
    )
    return pl.pallas_call(
        functools.partial(_gather_kernel, e),
        grid_spec=grid_spec,
        out_shape=[
            jax.ShapeDtypeStruct((b, e, cap, d), BF16),
            jax.ShapeDtypeStruct((b, e, cap, LANES), F32),
        ],
        compiler_params=pltpu.CompilerParams(
            dimension_semantics=("arbitrary", "arbitrary"), vmem_limit_bytes=VMEM_LIMIT),
        name="gather",
    )(cnt2, h2, slots4, aff4)


def _ffn_kernel(e0, nchunk, x_ref, gs_ref, wg_hbm, wu_hbm, wd_hbm, out_ref,
                acc_s, land_g, land_u, land_d, wb_g, wb_u, wb_d, sem):
    e = pl.program_id(0)
    total = pl.num_programs(0) * nchunk
    nb, _, cap, d = x_ref.shape

    def dma(t):
        tc = jnp.minimum(t, total - 1)
        ei = e0 + tc // nchunk
        cols = pl.ds(pl.multiple_of((tc % nchunk) * FC, FC), FC)
        slot = t % 2
        return (pltpu.make_async_copy(wg_hbm.at[ei, :, cols], land_g.at[slot], sem.at[slot]),
                pltpu.make_async_copy(wu_hbm.at[ei, :, cols], land_u.at[slot], sem.at[slot]),
                pltpu.make_async_copy(wd_hbm.at[ei, cols, :], land_d.at[slot], sem.at[slot]))

    def start(t):
        for c in dma(t):
            c.start()

    def wait(t):
        for c in dma(t):
            c.wait()

    def cast(slot):
        wb_g[slot] = land_g[slot].astype(BF16)
        wb_u[slot] = land_u[slot].astype(BF16)
        wb_d[slot] = land_d[slot].astype(BF16)

    @pl.when(e == 0)
    def _prime():
        start(0)
        wait(0)
        cast(0)
        start(1)

    acc_s[...] = jnp.zeros_like(acc_s)

    def chunk(f, carry):
        t = e * nchunk + f
        wait(t + 1)
        start(t + 2)
        cast((t + 1) % 2)
        slot = t % 2
        x = x_ref[:, 0].reshape(nb * cap, d)
        g = _dot(x, wb_g[slot])
        u = _dot(x, wb_u[slot])
        hid = (g * jax.nn.sigmoid(g) * u).astype(BF16)
        acc_s[...] += _dot(hid, wb_d[slot])
        return carry

    lax.fori_loop(0, nchunk, chunk, 0)

    @pl.when(e == pl.num_programs(0) - 1)
    def _drain():
        wait(total + 1)

    gate = gs_ref[:, 0, :, 0:1].reshape(nb * cap, 1)
    out_ref[:, 0] = (acc_s[...] * gate).reshape(nb, cap, d).astype(BF16)


def _ffn(xg, gs, w_gate, w_up, w_down, e0):
    b, e, cap, d = xg.shape
    ff = w_gate.shape[2]
    hbm = pl.BlockSpec(memory_space=pl.ANY)
    return pl.pallas_call(
        functools.partial(_ffn_kernel, e0, ff // FC),
        grid=(e,),
        in_specs=[
            pl.BlockSpec((b, 1, cap, d), lambda ei: (0, ei, 0, 0)),
            pl.BlockSpec((b, 1, cap, LANES), lambda ei: (0, ei, 0, 0)),
            hbm, hbm, hbm,
        ],
        out_specs=pl.BlockSpec((b, 1, cap, d), lambda ei: (0, ei, 0, 0)),
        out_shape=jax.ShapeDtypeStruct((b, e, cap, d), BF16),
        scratch_shapes=[
            pltpu.VMEM((b * cap, d), F32),
            pltpu.VMEM((2, d, FC), F32), pltpu.VMEM((2, d, FC), F32), pltpu.VMEM((2, FC, d), F32),
            pltpu.VMEM((2, d, FC), BF16), pltpu.VMEM((2, d, FC), BF16), pltpu.VMEM((2, FC, d), BF16),
            pltpu.SemaphoreType.DMA((2,)),
        ],
        compiler_params=pltpu.CompilerParams(
            dimension_semantics=("arbitrary",), vmem_limit_bytes=VMEM_LIMIT),
        name="ffn",
    )(xg, gs, w_gate, w_up, w_down)


def _combine_kernel(n_exp, cnt_ref, eo_ref, slot_ref, x1_ref, mod_ref, g_ref, o_ref,
                    stage_s, oh_s, y_s):
    b, q = pl.program_id(0), pl.program_id(1)
    cap = eo_ref.shape[2]
    nblk_q = o_ref.shape[0] // MXU_DIM
    sub = lax.broadcasted_iota(jnp.int32, (WIN, MXU_DIM), 0)
    gate = mod_ref[0, 5:6, :]

    for j in range(nblk_q):
        pb = q * nblk_q + j
        overflow = jnp.int32(0)
        for e in range(n_exp):
            _, c1, a0 = _segment(cnt_ref, b * n_exp + e, pb, cap)
            stage_s[e * WIN:(e + 1) * WIN, :] = eo_ref[0, e, pl.ds(a0, WIN), :]
            match = slot_ref[0, e, pl.ds(pb, 1), :] == (sub + a0)
            oh_s[e * WIN:(e + 1) * WIN, :] = jnp.where(match, 1.0, 0.0).astype(BF16)
            overflow = overflow + jnp.maximum(c1 - (a0 + WIN), 0)
        y_s[...] = _dot_tn(oh_s[...], stage_s[...])

        @pl.when(overflow > 0)
        def _long_segments(j=j, pb=pb):
            def per_expert(e, carry):
                _, c1, a0 = _segment(cnt_ref, b * n_exp + e, pb, cap)

                def per_window(k, carry2):
                    first = a0 + (k + 1) * WIN
                    aw = pl.multiple_of(jnp.minimum(first, cap - WIN), SLOT_ALIGN)
                    srow = slot_ref[0, e, pl.ds(pb, 1), :]
                    match = (srow == (sub + aw)) & (srow >= first)
                    onehot = jnp.where(match, 1.0, 0.0).astype(BF16)
                    y_s[...] += _dot_tn(onehot, eo_ref[0, e, pl.ds(aw, WIN), :])
                    return carry2

                return lax.fori_loop(0, _extra_windows(c1, a0), per_window, carry)

            lax.fori_loop(0, n_exp, per_expert, 0)

        sl = slice(j * MXU_DIM, (j + 1) * MXU_DIM)
        o_ref[sl, :] = x1_ref[sl, :] + gate * _rms(y_s[...], g_ref[...])


def _combine(cnt2, eo, slots4, x1, mod, g_post):
    b, e, cap, d = eo.shape
    t = x1.shape[0]
    s = t // b
    nq = s // QP
    nblk_q = QP // MXU_DIM
    grid_spec = pltpu.PrefetchScalarGridSpec(
        num_scalar_prefetch=1,
        grid=(b, nq),
        in_specs=[
            pl.BlockSpec((1, e, cap, d), lambda bi, qi, c: (bi, 0, 0, 0), pipeline_mode=pl.Buffered(1)),
            pl.BlockSpec((1, e, s // MXU_DIM, MXU_DIM), lambda bi, qi, c: (bi, 0, 0, 0)),
            pl.BlockSpec((QP, d), lambda bi, qi, c: (bi * nq + qi, 0)),
            pl.BlockSpec((1, 6, d), lambda bi, qi, c: (bi, 0, 0)),
            pl.BlockSpec((1, d), lambda bi, qi, c: (0, 0)),
        ],
        out_specs=pl.BlockSpec((QP, d), lambda bi, qi, c: (bi * nq + qi, 0)),
        scratch_shapes=[
            pltpu.VMEM((e * WIN, d), BF16),
            pltpu.VMEM((e * WIN, MXU_DIM), BF16),
            pltpu.VMEM((MXU_DIM, d), F32),
        ],
    )
    return pl.pallas_call(
        functools.partial(_combine_kernel, e),
        grid_spec=grid_spec,
        out_shape=jax.ShapeDtypeStruct((t, d), F32),
        compiler_params=pltpu.CompilerParams(
            dimension_semantics=("arbitrary", "arbitrary"),
            vmem_limit_bytes=VMEM_LIMIT),
        name="combine",
    )(cnt2, eo, slots4, x1, mod, g_post)


def _layer(layer, x, c, w_ada_all, b_ada, norm_pre_mix, norm_post_mix, w_in, sink, sgu_ln_g,
           sgu_ln_b, w_s, b_s, norm_out_attn, norm_out_gmlp, w_out, norm_pre_ffn, norm_post_ffn,
           w_router, w_gate_all, w_up_all, w_down_all):
    b, s, d = x.shape
    t = b * s
    dg = sgu_ln_g.shape[0]
    e = w_router.shape[1]
    cap = CAPACITY_FACTOR * s // e
    assert s % QP == 0 and s % TM == 0 and w_gate_all.shape[2] % FC == 0 and e % EG == 0
    assert cap % SLOT_ALIGN == 0 and cap >= WIN and s // MXU_DIM < LANES and d == D_ATTN + dg
    assert (s // MXU_DIM) % 2 == 0

    mod = _adaln(c, w_ada_all, b_ada, layer).reshape(b, 6, d)

    x2 = x.reshape(t, d)
    q, kv, u, vg = _proj(x2, mod, norm_pre_mix.reshape(1, d), w_in.astype(BF16), s)

    bias = jnp.asarray(_attn_bias_table())
    bs_full = jnp.repeat(b_s.T, dg // N_GMLP_GROUPS, axis=1)
    x1, h2, aff_t = _mix(
        x2, q, kv, u, vg, mod, sink, bias, w_s.astype(BF16), bs_full,
        sgu_ln_g.reshape(1, dg), sgu_ln_b.reshape(1, dg), norm_out_attn.reshape(1, D_ATTN),
        norm_out_gmlp.reshape(1, dg), w_out.astype(BF16), norm_post_mix.reshape(1, d),
        norm_pre_ffn.reshape(1, d), w_router.T.astype(BF16), b, s)

    slots3, aff3, cnt2 = _route(aff_t.reshape(b * e, s), cap)
    nblk = s // MXU_DIM
    slots4 = slots3.reshape(b, e, nblk, MXU_DIM)
    aff4 = aff3.reshape(b, e, nblk, MXU_DIM)
    xg, gs = _gather(cnt2, h2.reshape(b, s, d), slots4, aff4, cap)
    eo = _ffn(xg, gs, w_gate_all, w_up_all, w_down_all, layer * e)
    out = _combine(cnt2, eo, slots4, x1, mod, norm_post_ffn.reshape(1, d))
    return out.reshape(b, s, d)


def kernel(x, c, w_ada, b_ada, norm_pre_mix, norm_post_mix, w_in, sink, sgu_ln_g, sgu_ln_b, w_s, b_s, norm_out_attn, norm_out_gmlp, w_out, norm_pre_ffn, norm_post_ffn, w_router, w_gate, w_up, w_down):
    depth, d = w_ada.shape[0], w_ada.shape[1]
    w_ada_all = w_ada.reshape(depth * d, w_ada.shape[2])
    stack = lambda w: w.reshape((depth * w.shape[1],) + w.shape[2:])
    w_gate_all, w_up_all, w_down_all = stack(w_gate), stack(w_up), stack(w_down)
    for l in range(depth):
        x = _layer(l, x, c, w_ada_all, b_ada[l], norm_pre_mix[l], norm_post_mix[l], w_in[l],
                   sink[l], sgu_ln_g[l], sgu_ln_b[l], w_s[l], b_s[l], norm_out_attn[l],
                   norm_out_gmlp[l], w_out[l], norm_pre_ffn[l], norm_post_ffn[l], w_router[l],
                   w_gate_all, w_up_all, w_down_all)
    return x
```

```python
import functools

import numpy as np
import jax
import jax.numpy as jnp
from jax import lax
from jax.experimental import pallas as pl
from jax.experimental.pallas import tpu as pltpu

N_HEADS = 8
N_KV_HEADS = 2
HEAD_DIM = 64
D_ATTN = N_HEADS * HEAD_DIM
D_KV = N_KV_HEADS * HEAD_DIM
GROUP = N_HEADS // N_KV_HEADS
WINDOW = 128
BLOCK = 128
N_GMLP_GROUPS = 8
N_EXPERTS = 16
CAPACITY_FACTOR = 2
EPS = 1e-6
MASK_VALUE = -1e30

LANES = 128
MXU_DIM = 256

TM = 1024
FC = 256
EG = 4
QP = 1024
WIN = 64
SLOT_ALIGN = 16
VMEM_LIMIT = 56 * 1024 * 1024

BF16 = jnp.bfloat16
F32 = jnp.float32


def _rms(x, g):
    return x * lax.rsqrt(jnp.mean(x * x, axis=-1, keepdims=True) + EPS) * g


def _dot(a, b):
    return jnp.dot(a, b, preferred_element_type=F32)


def _dot_nt(a, b):
    return lax.dot_general(a, b, (((1,), (1,)), ((), ())), preferred_element_type=F32)


def _dot_tn(a, b):
    return lax.dot_general(a, b, (((0,), (0,)), ((), ())), preferred_element_type=F32)


def _adaln_kernel(c_ref, w_ref, b_ref, o_ref):
    c = c_ref[...]
    a = (c * jax.nn.sigmoid(c)).astype(BF16)
    o_ref[...] = _dot(a, w_ref[...].astype(BF16)) + b_ref[...]


def _adaln(c, w_ada_all, b_ada, layer):
    b, d = c.shape
    n = w_ada_all.shape[1]
    return pl.pallas_call(
        _adaln_kernel,
        grid=(n // d,),
        in_specs=[
            pl.BlockSpec((b, d), lambda j: (0, 0)),
            pl.BlockSpec((d, d), lambda j: (layer, j)),
            pl.BlockSpec((1, d), lambda j: (0, j)),
        ],
        out_specs=pl.BlockSpec((b, d), lambda j: (0, j)),
        out_shape=jax.ShapeDtypeStruct((b, n), F32),
        compiler_params=pltpu.CompilerParams(dimension_semantics=("arbitrary",)),
        name="adaln",
    )(c, w_ada_all, b_ada.reshape(1, n))


def _gelu_tanh(x):
    c = np.float32(np.sqrt(2.0 / np.pi))
    return x * (0.5 * (1.0 + jnp.tanh(c * (x + 0.044715 * (x * x * x)))))


def _proj_kernel(x_ref, mod_ref, g_ref, w_ref, q_ref, kv_ref, u_ref, vg_ref):
    x = x_ref[...]
    shift = mod_ref[0, 0:1, :]
    scale = mod_ref[0, 1:2, :]
    h = (_rms(x, g_ref[...]) * (1.0 + scale) + shift).astype(BF16)
    n_att = D_ATTN + 2 * D_KV
    dg = u_ref.shape[1]
    z = _gelu_tanh(_dot(h, w_ref[:, n_att:]))
    u_ref[...] = z[:, :dg]
    vg_ref[...] = z[:, dg:]
    p = _dot(h, w_ref[:, :n_att])
    q_ref[...] = (p[:, :D_ATTN] * (HEAD_DIM ** -0.5)).astype(BF16)
    lo = lax.broadcasted_iota(jnp.int32, (1, LANES), 1) < HEAD_DIM
    for j in range(2):
        t = p[:, D_ATTN + j * D_KV:D_ATTN + (j + 1) * D_KV]
        t_sw = pltpu.roll(t, HEAD_DIM, axis=1)
        kv_ref[:, (2 * j) * LANES:(2 * j + 1) * LANES] = jnp.where(lo, t, t_sw).astype(BF16)
        kv_ref[:, (2 * j + 1) * LANES:(2 * j + 2) * LANES] = jnp.where(lo, t_sw, t).astype(BF16)


def _proj(x2, mod, g_pre, w_in, s):
    t, d = x2.shape
    nw = w_in.shape[1]
    dg = (nw - D_ATTN - 2 * D_KV) // 2
    nsb = s // TM
    return pl.pallas_call(
        _proj_kernel,
        grid=(t // TM,),
        in_specs=[
            pl.BlockSpec((TM, d), lambda i: (i, 0)),
            pl.BlockSpec((1, 6, d), lambda i: (i // nsb, 0, 0)),
            pl.BlockSpec((1, d), lambda i: (0, 0)),
            pl.BlockSpec((d, nw), lambda i: (0, 0)),
        ],
        out_specs=[
            pl.BlockSpec((TM, D_ATTN), lambda i: (i, 0)),
            pl.BlockSpec((TM, D_ATTN), lambda i: (i, 0)),
            pl.BlockSpec((TM, dg), lambda i: (i, 0)),
            pl.BlockSpec((TM, dg), lambda i: (i, 0)),
        ],
        out_shape=[
            jax.ShapeDtypeStruct((t, D_ATTN), BF16),
            jax.ShapeDtypeStruct((t, D_ATTN), BF16),
            jax.ShapeDtypeStruct((t, dg), F32),
            jax.ShapeDtypeStruct((t, dg), F32),
        ],
        compiler_params=pltpu.CompilerParams(
            dimension_semantics=("arbitrary",), vmem_limit_bytes=VMEM_LIMIT),
        name="proj",
    )(x2, mod, g_pre, w_in)


def _attn_bias_table():
    qi = np.arange(BLOCK)[:, None]
    kj = np.arange(3 * BLOCK)[None, :] - BLOCK
    dist = np.abs(qi - kj).astype(np.float32)
    band = dist <= WINDOW
    slopes = np.exp2(-8.0 * np.arange(1, N_HEADS + 1, dtype=np.float32) / N_HEADS).astype(np.float32)
    key_ok = [kj >= 0, np.ones_like(kj, bool), kj < BLOCK]
    tab = np.empty((3, N_HEADS, BLOCK, 3 * BLOCK), np.float32)
    for v in range(3):
        ok = band & key_ok[v]
        for h in range(N_HEADS):
            tab[v, h] = np.where(ok, -slopes[h] * dist, np.float32(MASK_VALUE))
    return tab.reshape(3, N_KV_HEADS, GROUP * BLOCK, 3 * BLOCK)


def _mix_kernel(nblk_seq, sink_ref, x_ref, q_ref, kvp_ref, kvc_ref, kvn_ref, u_ref, vg_ref,
                mod_ref, bias_ref, ws_ref, bs_ref, lng_ref, lnb_ref, ga_ref, gg_ref,
                wout_ref, gpost_ref, gpre2_ref, wr_ref,
                x1_ref, h2_ref, aff_ref, kcat, attn_s, gm_s):
    i = pl.program_id(0)
    nb_tile = TM // BLOCK
    kcat[0:BLOCK, :] = kvp_ref[...]
    kcat[BLOCK:BLOCK + TM, :] = kvc_ref[...]
    kcat[BLOCK + TM:, :] = kvn_ref[...]
    lane = lax.broadcasted_iota(jnp.int32, (1, LANES), 1)
    lo = lane < HEAD_DIM
    zero = jnp.zeros((), BF16)
    head_of_row = lax.broadcasted_iota(jnp.int32, (GROUP * BLOCK, 1), 0) // BLOCK
    sinks = []
    for g in range(N_KV_HEADS):
        col = jnp.zeros((GROUP * BLOCK, 1), F32)
        for j in range(GROUP):
            col = jnp.where(head_of_row == j, sink_ref[g * GROUP + j], col)
        sinks.append(col)

    def block_body(a, carry):
        r0 = pl.multiple_of(a * BLOCK, BLOCK)
        pos = (i * nb_tile + a) % nblk_seq
        var = jnp.where(pos == 0, 0, jnp.where(pos == nblk_seq - 1, 2, 1))
        kv = kcat[pl.ds(r0, 3 * BLOCK), :]
        for g in range(N_KV_HEADS):
            k = kv[:, g * LANES:(g + 1) * LANES]
            v = kv[:, (N_KV_HEADS + g) * LANES:(N_KV_HEADS + g + 1) * LANES]
            pieces = []
            for pi in range(GROUP // 2):
                qp = q_ref[pl.ds(r0, BLOCK), (g * GROUP // 2 + pi) * LANES:(g * GROUP // 2 + pi + 1) * LANES]
                pieces += [jnp.where(lo, qp, zero), jnp.where(lo, zero, qp)]
            s = _dot_nt(jnp.concatenate(pieces, axis=0), k) + bias_ref[var, g]
            m = jnp.maximum(jnp.max(s, axis=-1, keepdims=True), sinks[g])
            p = jnp.exp(s - m)
            den = jnp.sum(p, axis=-1, keepdims=True) + jnp.exp(sinks[g] - m)
            o = _dot(p.astype(BF16), v) / den
            for pi in range(GROUP // 2):
                even = o[(2 * pi) * BLOCK:(2 * pi + 1) * BLOCK, :]
                odd = o[(2 * pi + 1) * BLOCK:(2 * pi + 2) * BLOCK, :]
                col0 = (g * GROUP // 2 + pi) * LANES
                attn_s[pl.ds(r0, BLOCK), col0:col0 + LANES] = jnp.where(lo, even, odd)
        vg = vg_ref[pl.ds(r0, BLOCK), :]
        mu = jnp.mean(vg, axis=-1, keepdims=True)
        xc = vg - mu
        vn = (xc * lax.rsqrt(jnp.mean(xc * xc, axis=-1, keepdims=True) + EPS) * lng_ref[...]
              + lnb_ref[...]).astype(BF16)
        for pi in range(N_GMLP_GROUPS // 2):
            vnp = vn[:, pi * LANES:(pi + 1) * LANES]
            z = jnp.where(lo, _dot(ws_ref[2 * pi], vnp), _dot(ws_ref[2 * pi + 1], vnp))
            sl = slice(pi * LANES, (pi + 1) * LANES)
            gm_s[pl.ds(r0, BLOCK), sl] = u_ref[pl.ds(r0, BLOCK), sl] * (z + bs_ref[:, sl])
        return carry

    lax.fori_loop(0, nb_tile, block_body, 0, unroll=2)

    na = _rms(attn_s[...], ga_ref[...]).astype(BF16)
    ng = _rms(gm_s[...], gg_ref[...]).astype(BF16)
    mo = _dot(na, wout_ref[0:D_ATTN, :]) + _dot(ng, wout_ref[D_ATTN:, :])
    gate1 = mod_ref[0, 2:3, :]
    x1 = x_ref[...] + gate1 * _rms(mo, gpost_ref[...])
    x1_ref[...] = x1
    h2 = (_rms(x1, gpre2_ref[...]) * (1.0 + mod_ref[0, 4:5, :]) + mod_ref[0, 3:4, :]).astype(BF16)
    h2_ref[...] = h2
    lg = _dot_nt(wr_ref[...], h2)
    lg = lg - jnp.max(lg, axis=0, keepdims=True)
    ex = jnp.exp(lg)
    aff_ref[0] = ex / jnp.sum(ex, axis=0, keepdims=True)


def _mix(x2, q, kv, u, vg, mod, sink, bias, ws, bs_full, ln_g, ln_b, g_attn, g_gmlp,
         w_out, g_post, g_pre2, w_r_t, b, s):
    t, d = x2.shape
    dg = u.shape[1]
    e = w_r_t.shape[0]
    nsb = s // TM
    nb_tile = TM // BLOCK
    nblk = t // BLOCK
    row = lambda i: (i, 0)
    const2 = lambda i: (0, 0)
    in_specs = [
        pl.BlockSpec(memory_space=pltpu.SMEM),
        pl.BlockSpec((TM, d), row),
        pl.BlockSpec((TM, D_ATTN), row),
        pl.BlockSpec((BLOCK, 4 * LANES), lambda i: (jnp.maximum(i * nb_tile - 1, 0), 0)),
        pl.BlockSpec((TM, 4 * LANES), row),
        pl.BlockSpec((BLOCK, 4 * LANES), lambda i: (jnp.minimum((i + 1) * nb_tile, nblk - 1), 0)),
        pl.BlockSpec((TM, dg), row),
        pl.BlockSpec((TM, dg), row),
        pl.BlockSpec((1, 6, d), lambda i: (i // nsb, 0, 0)),
        pl.BlockSpec(bias.shape, lambda i: (0, 0, 0, 0)),
        pl.BlockSpec(ws.shape, lambda i: (0, 0, 0)),
        pl.BlockSpec((BLOCK, dg), const2),
        pl.BlockSpec((1, dg), const2),
        pl.BlockSpec((1, dg), const2),
        pl.BlockSpec((1, D_ATTN), const2),
        pl.BlockSpec((1, dg), const2),
        pl.BlockSpec(w_out.shape, const2),
        pl.BlockSpec((1, d), const2),
        pl.BlockSpec((1, d), const2),
        pl.BlockSpec(w_r_t.shape, const2),
    ]
    out_specs = [
        pl.BlockSpec((TM, d), row),
        pl.BlockSpec((TM, d), row),
        pl.BlockSpec((1, e, TM), lambda i: (i // nsb, 0, i % nsb)),
    ]
    out_shape = [
        jax.ShapeDtypeStruct((t, d), F32),
        jax.ShapeDtypeStruct((t, d), BF16),
        jax.ShapeDtypeStruct((b, e, s), F32),
    ]
    return pl.pallas_call(
        functools.partial(_mix_kernel, s // BLOCK),
        grid=(t // TM,),
        in_specs=in_specs,
        out_specs=out_specs,
        out_shape=out_shape,
        scratch_shapes=[
            pltpu.VMEM((TM + 2 * BLOCK, 4 * LANES), BF16),
            pltpu.VMEM((TM, D_ATTN), F32),
            pltpu.VMEM((TM, dg), F32),
        ],
        compiler_params=pltpu.CompilerParams(
            dimension_semantics=("arbitrary",), vmem_limit_bytes=VMEM_LIMIT),
        name="mix",
    )(sink, x2, q, kv, kv, kv, u, vg, mod, bias, ws, bs_full, ln_g, ln_b, g_attn, g_gmlp,
      w_out, g_post, g_pre2, w_r_t)


def _route_kernel(cap, aff_ref, slot_ref, affb_ref, cnt_ref):
    aff = aff_ref[...]
    e, s = aff.shape
    capf = jnp.float32(cap)

    def count_ge(t):
        return jnp.sum(jnp.where(aff >= t, 1.0, 0.0), axis=1, keepdims=True)

    def bit_body(it, cand):
        trial = cand | jnp.left_shift(jnp.int32(1), 30 - it)
        t = lax.bitcast_convert_type(trial, F32)
        return jnp.where(count_ge(t) >= capf, trial, cand)

    cand = lax.fori_loop(0, 31, bit_body, jnp.zeros((e, 1), jnp.int32))
    thr = lax.bitcast_convert_type(cand, F32)
    need = capf - jnp.sum(jnp.where(aff > thr, 1.0, 0.0), axis=1, keepdims=True)

    r = lax.broadcasted_iota(jnp.int32, (LANES, LANES), 0)
    c = lax.broadcasted_iota(jnp.int32, (LANES, LANES), 1)
    tri = jnp.where(r < c, 1.0, 0.0).astype(BF16)
    lane = lax.broadcasted_iota(jnp.int32, (e, LANES), 1)
    carry_eq = jnp.zeros((e, 1), F32)
    carry_sel = jnp.zeros((e, 1), F32)
    cnt = jnp.zeros((e, LANES), F32)
    per_blk = MXU_DIM // LANES
    for j in range(s // LANES):
        a = aff[:, j * LANES:(j + 1) * LANES]
        gt = a > thr
        eq = jnp.where(a == thr, 1.0, 0.0)
        tie_rank = _dot(eq.astype(BF16), tri) + carry_eq
        sel = jnp.where(gt | ((eq > 0.0) & (tie_rank < need)), 1.0, 0.0)
        slot = _dot(sel.astype(BF16), tri) + carry_sel
        blk = (slice(None), j // per_blk, slice((j % per_blk) * LANES, (j % per_blk + 1) * LANES))
        slot_ref[blk] = jnp.where(sel > 0.0, slot, -1.0).astype(jnp.int32)
        affb_ref[blk] = a
        if j % per_blk == 0:
            cnt = jnp.where(lane == j // per_blk, carry_sel, cnt)
        carry_eq = carry_eq + jnp.sum(eq, axis=1, keepdims=True)
        carry_sel = carry_sel + jnp.sum(sel, axis=1, keepdims=True)
    cnt = jnp.where(lane >= s // MXU_DIM, carry_sel, cnt)
    cnt_ref[...] = cnt.astype(jnp.int32)


def _route(aff2, cap):
    r, s = aff2.shape
    nblk = s // MXU_DIM
    return pl.pallas_call(
        functools.partial(_route_kernel, cap),
        grid=(1,),
        in_specs=[pl.BlockSpec((r, s), lambda i: (0, 0))],
        out_specs=[
            pl.BlockSpec((r, nblk, MXU_DIM), lambda i: (0, 0, 0)),
            pl.BlockSpec((r, nblk, MXU_DIM), lambda i: (0, 0, 0)),
            pl.BlockSpec((r, LANES), lambda i: (0, 0)),
        ],
        out_shape=[
            jax.ShapeDtypeStruct((r, nblk, MXU_DIM), jnp.int32),
            jax.ShapeDtypeStruct((r, nblk, MXU_DIM), F32),
            jax.ShapeDtypeStruct((r, LANES), jnp.int32),
        ],
        compiler_params=pltpu.CompilerParams(dimension_semantics=("arbitrary",)),
        name="route",
    )(aff2)


def _segment(cnt_ref, row, pb, cap):
    c0 = jnp.minimum(cnt_ref[row, pb], cap)
    c1 = jnp.minimum(cnt_ref[row, pb + 1], cap)
    a0 = jnp.minimum(jnp.bitwise_and(c0, -SLOT_ALIGN), cap - WIN)
    return c0, c1, pl.multiple_of(a0, SLOT_ALIGN)


def _extra_windows(c1, a0):
    return (jnp.maximum(c1 - (a0 + WIN), 0) + WIN - 1) // WIN


def _gather_kernel(n_exp, cnt_ref, h2_ref, slot_ref, aff_ref, x_ref, gs_ref):
    b, eg = pl.program_id(0), pl.program_id(1)
    ng, cap = x_ref.shape[1], x_ref.shape[2]
    nblk = slot_ref.shape[2]
    sub = lax.broadcasted_iota(jnp.int32, (WIN, MXU_DIM), 0)
    sub_col = lax.broadcasted_iota(jnp.int32, (WIN, 1), 0)
    x_ref[...] = jnp.zeros_like(x_ref)
    gs_ref[...] = jnp.zeros_like(gs_ref)

    def merge(k, a0, first, last, rows, gate):
        slot_id = sub_col + a0
        keep = (slot_id >= first) & (slot_id < last)
        win = (0, k, pl.ds(a0, WIN), slice(None))
        x_ref[win] = jnp.where(keep, rows.astype(BF16), x_ref[win])
        gs_ref[win] = jnp.where(keep, gate, gs_ref[win])

    def tokens(pb):
        return pl.ds(pl.multiple_of(pb * MXU_DIM, MXU_DIM), MXU_DIM)

    def first_windows(pb):
        segs, gates, onehots = [], [], []
        overflow = jnp.int32(0)
        for k in range(ng):
            c0, c1, a0 = _segment(cnt_ref, b * n_exp + eg * ng + k, pb, cap)
            match = slot_ref[0, k, pl.ds(pb, 1), :] == (sub + a0)
            onehots.append(jnp.where(match, 1.0, 0.0).astype(BF16))
            gates.append(jnp.sum(jnp.where(match, aff_ref[0, k, pl.ds(pb, 1), :], 0.0),
                                 axis=1, keepdims=True))
            segs.append((c0, c1, a0))
            overflow = overflow + jnp.maximum(c1 - (a0 + WIN), 0)
        rows = _dot(jnp.concatenate(onehots, axis=0), h2_ref[0, tokens(pb), :])
        for k in range(ng):
            c0, c1, a0 = segs[k]
            merge(k, a0, c0, c1, rows[k * WIN:(k + 1) * WIN, :], gates[k])
        return overflow

    def further_windows(pb):
        def per_expert(k, carry2):
            _, c1, a0 = _segment(cnt_ref, b * n_exp + eg * ng + k, pb, cap)

            def per_window(w, carry3):
                first = a0 + (w + 1) * WIN
                aw = pl.multiple_of(jnp.minimum(first, cap - WIN), SLOT_ALIGN)
                srow = slot_ref[0, k, pl.ds(pb, 1), :]
                match = (srow == (sub + aw)) & (srow >= first)
                onehot = jnp.where(match, 1.0, 0.0).astype(BF16)
                gate = jnp.sum(jnp.where(match, aff_ref[0, k, pl.ds(pb, 1), :], 0.0),
                               axis=1, keepdims=True)
                merge(k, aw, first, c1, _dot(onehot, h2_ref[0, tokens(pb), :]), gate)
                return carry3

            return lax.fori_loop(0, _extra_windows(c1, a0), per_window, carry2)

        lax.fori_loop(0, ng, per_expert, 0)

    def pair_body(i, carry):
        overflows = [first_windows(2 * i + j) for j in range(2)]
        for j in range(2):
            pl.when(overflows[j] > 0)(functools.partial(further_windows, 2 * i + j))
        return carry

    lax.fori_loop(0, nblk // 2, pair_body, 0)


def _gather(cnt2, h2, slots4, aff4, cap):
    b, s, d = h2.shape
    e = slots4.shape[1]
    nblk = s // MXU_DIM
    grid_spec = pltpu.PrefetchScalarGridSpec(
        num_scalar_prefetch=1,
        grid=(b, e // EG),
        in_specs=[
            pl.BlockSpec((1, s, d), lambda bi, gi, c: (bi, 0, 0), pipeline_mode=pl.Buffered(1)),
            pl.BlockSpec((1, EG, nblk, MXU_DIM), lambda bi, gi, c: (bi, gi, 0, 0)),
            pl.BlockSpec((1, EG, nblk, MXU_DIM), lambda bi, gi, c: (bi, gi, 0, 0)),
        ],
        out_specs=[
            pl.BlockSpec((1, EG, cap, d), lambda bi, gi, c: (bi, gi, 0, 0)),
            pl.BlockSpec((1, EG, cap, LANES), lambda bi, gi, c: (bi, gi, 0, 0)),
        ],
    )
    return pl.pallas_call(
        functools.partial(_gather_kernel, e),
        grid_spec=grid_spec,
        out_shape=[
            jax.ShapeDtypeStruct((b, e, cap, d), BF16),
            jax.ShapeDtypeStruct((b, e, cap, LANES), F32),
        ],
        compiler_params=pltpu.CompilerParams(
            dimension_semantics=("arbitrary", "arbitrary"), vmem_limit_bytes=VMEM_LIMIT),
        name="gather",
    )(cnt2, h2, slots4, aff4)


def _ffn_kernel(x_ref, gs_ref, wg_ref, wu_ref, wd_ref, out_ref, acc_s):
    e, f = pl.program_id(0), pl.program_id(1)
    nb, _, cap, d = x_ref.shape

    @pl.when((e == 0) & (f == 0))
    def _init():
        acc_s[...] = jnp.zeros_like(acc_s)

    x = x_ref[:, 0].reshape(nb * cap, d)
    g = _dot(x, wg_ref[0].astype(BF16))
    u = _dot(x, wu_ref[0].astype(BF16))
    hid = (g * jax.nn.sigmoid(g) * u).astype(BF16)
    y = _dot(hid, wd_ref[0].astype(BF16))
    acc_s[...] = jnp.where(f == 0, y, acc_s[...] + y)

    @pl.when(f == pl.num_programs(1) - 1)
    def _emit():
        gate = gs_ref[:, 0, :, 0:1].reshape(nb * cap, 1)
        out_ref[:, 0] = (acc_s[...] * gate).reshape(nb, cap, d).astype(BF16)


def _ffn(xg, gs, w_gate, w_up, w_down, e0):
    b, e, cap, d = xg.shape
    ff = w_gate.shape[2]
    return pl.pallas_call(
        _ffn_kernel,
        grid=(e, ff // FC),
        in_specs=[
            pl.BlockSpec((b, 1, cap, d), lambda ei, fi: (0, ei, 0, 0)),
            pl.BlockSpec((b, 1, cap, LANES), lambda ei, fi: (0, ei, 0, 0)),
            pl.BlockSpec((1, d, FC), lambda ei, fi: (e0 + ei, 0, fi)),
            pl.BlockSpec((1, d, FC), lambda ei, fi: (e0 + ei, 0, fi)),
            pl.BlockSpec((1, FC, d), lambda ei, fi: (e0 + ei, fi, 0)),
        ],
        out_specs=pl.BlockSpec((b, 1, cap, d), lambda ei, fi: (0, ei, 0, 0)),
        out_shape=jax.ShapeDtypeStruct((b, e, cap, d), BF16),
        scratch_shapes=[pltpu.VMEM((b * cap, d), F32)],
        compiler_params=pltpu.CompilerParams(
            dimension_semantics=("arbitrary", "arbitrary"), vmem_limit_bytes=VMEM_LIMIT),
        name="ffn",
    )(xg, gs, w_gate, w_up, w_down)


def _combine_kernel(n_exp, cnt_ref, eo_ref, slot_ref, x1_ref, mod_ref, g_ref, o_ref,
                    stage_s, oh_s, y_s):
    b, q = pl.program_id(0), pl.program_id(1)
    cap = eo_ref.shape[2]
    nblk_q = o_ref.shape[0] // MXU_DIM
    sub = lax.broadcasted_iota(jnp.int32, (WIN, MXU_DIM), 0)
    gate = mod_ref[0, 5:6, :]

    for j in range(nblk_q):
        pb = q * nblk_q + j
        overflow = jnp.int32(0)
        for e in range(n_exp):
            _, c1, a0 = _segment(cnt_ref, b * n_exp + e, pb, cap)
            stage_s[e * WIN:(e + 1) * WIN, :] = eo_ref[0, e, pl.ds(a0, WIN), :]
            match = slot_ref[0, e, pl.ds(pb, 1), :] == (sub + a0)
            oh_s[e * WIN:(e + 1) * WIN, :] = jnp.where(match, 1.0, 0.0).astype(BF16)
            overflow = overflow + jnp.maximum(c1 - (a0 + WIN), 0)
        y_s[...] = _dot_tn(oh_s[...], stage_s[...])

        @pl.when(overflow > 0)
        def _long_segments(j=j, pb=pb):
            def per_expert(e, carry):
                _, c1, a0 = _segment(cnt_ref, b * n_exp + e, pb, cap)

                def per_window(k, carry2):
                    first = a0 + (k + 1) * WIN
                    aw = pl.multiple_of(jnp.minimum(first, cap - WIN), SLOT_ALIGN)
                    srow = slot_ref[0, e, pl.ds(pb, 1), :]
                    match = (srow == (sub + aw)) & (srow >= first)
                    onehot = jnp.where(match, 1.0, 0.0).astype(BF16)
                    y_s[...] += _dot_tn(onehot, eo_ref[0, e, pl.ds(aw, WIN), :])
                    return carry2

                return lax.fori_loop(0, _extra_windows(c1, a0), per_window, carry)

            lax.fori_loop(0, n_exp, per_expert, 0)

        sl = slice(j * MXU_DIM, (j + 1) * MXU_DIM)
        o_ref[sl, :] = x1_ref[sl, :] + gate * _rms(y_s[...], g_ref[...])


def _combine(cnt2, eo, slots4, x1, mod, g_post):
    b, e, cap, d = eo.shape
    t = x1.shape[0]
    s = t // b
    nq = s // QP
    nblk_q = QP // MXU_DIM
    grid_spec = pltpu.PrefetchScalarGridSpec(
        num_scalar_prefetch=1,
        grid=(b, nq),
        in_specs=[
            pl.BlockSpec((1, e, cap, d), lambda bi, qi, c: (bi, 0, 0, 0), pipeline_mode=pl.Buffered(1)),
            pl.BlockSpec((1, e, s // MXU_DIM, MXU_DIM), lambda bi, qi, c: (bi, 0, 0, 0)),
            pl.BlockSpec((QP, d), lambda bi, qi, c: (bi * nq + qi, 0)),
            pl.BlockSpec((1, 6, d), lambda bi, qi, c: (bi, 0, 0)),
            pl.BlockSpec((1, d), lambda bi, qi, c: (0, 0)),
        ],
        out_specs=pl.BlockSpec((QP, d), lambda bi, qi, c: (bi * nq + qi, 0)),
        scratch_shapes=[
            pltpu.VMEM((e * WIN, d), BF16),
            pltpu.VMEM((e * WIN, MXU_DIM), BF16),
            pltpu.VMEM((MXU_DIM, d), F32),
        ],
    )
    return pl.pallas_call(
        functools.partial(_combine_kernel, e),
        grid_spec=grid_spec,
        out_shape=jax.ShapeDtypeStruct((t, d), F32),
        compiler_params=pltpu.CompilerParams(
            dimension_semantics=("arbitrary", "arbitrary"),
            vmem_limit_bytes=VMEM_LIMIT),
        name="combine",
    )(cnt2, eo, slots4, x1, mod, g_post)


def _layer(layer, x, c, w_ada_all, b_ada, norm_pre_mix, norm_post_mix, w_in, sink, sgu_ln_g,
           sgu_ln_b, w_s, b_s, norm_out_attn, norm_out_gmlp, w_out, norm_pre_ffn, norm_post_ffn,
           w_router, w_gate_all, w_up_all, w_down_all):
    b, s, d = x.shape
    t = b * s
    dg = sgu_ln_g.shape[0]
    e = w_router.shape[1]
    cap = CAPACITY_FACTOR * s // e
    assert s % QP == 0 and s % TM == 0 and w_gate_all.shape[2] % FC == 0 and e % EG == 0
    assert cap % SLOT_ALIGN == 0 and cap >= WIN and s // MXU_DIM < LANES and d == D_ATTN + dg
    assert (s // MXU_DIM) % 2 == 0

    mod = _adaln(c, w_ada_all, b_ada, layer).reshape(b, 6, d)

    x2 = x.reshape(t, d)
    q, kv, u, vg = _proj(x2, mod, norm_pre_mix.reshape(1, d), w_in.astype(BF16), s)

    bias = jnp.asarray(_attn_bias_table())
    bs_full = jnp.repeat(b_s.T, dg // N_GMLP_GROUPS, axis=1)
    x1, h2, aff_t = _mix(
        x2, q, kv, u, vg, mod, sink, bias, w_s.astype(BF16), bs_full,
        sgu_ln_g.reshape(1, dg), sgu_ln_b.reshape(1, dg), norm_out_attn.reshape(1, D_ATTN),
        norm_out_gmlp.reshape(1, dg), w_out.astype(BF16), norm_post_mix.reshape(1, d),
        norm_pre_ffn.reshape(1, d), w_router.T.astype(BF16), b, s)

    slots3, aff3, cnt2 = _route(aff_t.reshape(b * e, s), cap)
    nblk = s // MXU_DIM
    slots4 = slots3.reshape(b, e, nblk, MXU_DIM)
    aff4 = aff3.reshape(b, e, nblk, MXU_DIM)
    xg, gs = _gather(cnt2, h2.reshape(b, s, d), slots4, aff4, cap)
    eo = _ffn(xg, gs, w_gate_all, w_up_all, w_down_all, layer * e)
    out = _combine(cnt2, eo, slots4, x1, mod, norm_post_ffn.reshape(1, d))
    return out.reshape(b, s, d)


def kernel(x, c, w_ada, b_ada, norm_pre_mix, norm_post_mix, w_in, sink, sgu_ln_g, sgu_ln_b, w_s, b_s, norm_out_attn, norm_out_gmlp, w_out, norm_pre_ffn, norm_post_ffn, w_router, w_gate, w_up, w_down):
    depth, d = w_ada.shape[0], w_ada.shape[1]
    w_ada_all = w_ada.reshape(depth * d, w_ada.shape[2])
    stack = lambda w: w.reshape((depth * w.shape[1],) + w.shape[2:])
    w_gate_all, w_up_all, w_down_all = stack(w_gate), stack(w_up), stack(w_down)
    for l in range(depth):
        x = _layer(l, x, c, w_ada_all, b_ada[l], norm_pre_mix[l], norm_post_mix[l], w_in[l],
                   sink[l], sgu_ln_g[l], sgu_ln_b[l], w_s[l], b_s[l], norm_out_attn[l],
                   norm_out_gmlp[l], w_out[l], norm_pre_ffn[l], norm_post_ffn[l], w_router[l],
                   w_gate_all, w_up_all, w_down_all)
    return x
```

```python
import functools

import numpy as np
import jax
import jax.numpy as jnp
from jax import lax
from jax.experimental import pallas as pl
from jax.experimental.pallas import tpu as pltpu

N_HEADS = 8
N_KV_HEADS = 2
HEAD_DIM = 64
D_ATTN = N_HEADS * HEAD_DIM
D_KV = N_KV_HEADS * HEAD_DIM
GROUP = N_HEADS // N_KV_HEADS
WINDOW = 128
BLOCK = 128
N_GMLP_GROUPS = 8
N_EXPERTS = 16
CAPACITY_FACTOR = 2
EPS = 1e-6
MASK_VALUE = -1e30

LANES = 128
MXU_DIM = 256

TM = 1024
FC = 256
EG = 4
QP = 1024
WIN = 64
SLOT_ALIGN = 16
VMEM_LIMIT = 56 * 1024 * 1024

BF16 = jnp.bfloat16
F32 = jnp.float32


def _rms(x, g):
    return x * lax.rsqrt(jnp.mean(x * x, axis=-1, keepdims=True) + EPS) * g


def _dot(a, b):
    return jnp.dot(a, b, preferred_element_type=F32)


def _dot_nt(a, b):
    return lax.dot_general(a, b, (((1,), (1,)), ((), ())), preferred_element_type=F32)


def _dot_tn(a, b):
    return lax.dot_general(a, b, (((0,), (0,)), ((), ())), preferred_element_type=F32)


def _adaln_kernel(c_ref, w_ref, b_ref, o_ref):
    c = c_ref[...]
    a = (c * jax.nn.sigmoid(c)).astype(BF16)
    o_ref[...] = _dot(a, w_ref[...].astype(BF16)) + b_ref[...]


def _adaln(c, w_ada_all, b_ada, layer):
    b, d = c.shape
    n = w_ada_all.shape[1]
    return pl.pallas_call(
        _adaln_kernel,
        grid=(n // d,),
        in_specs=[
            pl.BlockSpec((b, d), lambda j: (0, 0)),
            pl.BlockSpec((d, d), lambda j: (layer, j)),
            pl.BlockSpec((1, d), lambda j: (0, j)),
        ],
        out_specs=pl.BlockSpec((b, d), lambda j: (0, j)),
        out_shape=jax.ShapeDtypeStruct((b, n), F32),
        compiler_params=pltpu.CompilerParams(dimension_semantics=("arbitrary",)),
        name="adaln",
    )(c, w_ada_all, b_ada.reshape(1, n))


def _gelu_tanh(x):
    c = np.float32(np.sqrt(2.0 / np.pi))
    return x * (0.5 * (1.0 + jnp.tanh(c * (x + 0.044715 * (x * x * x)))))


def _proj_kernel(x_ref, mod_ref, g_ref, w_ref, q_ref, kv_ref, u_ref, vg_ref):
    x = x_ref[...]
    shift = mod_ref[0, 0:1, :]
    scale = mod_ref[0, 1:2, :]
    h = (_rms(x, g_ref[...]) * (1.0 + scale) + shift).astype(BF16)
    n_att = D_ATTN + 2 * D_KV
    dg = u_ref.shape[1]
    z = _gelu_tanh(_dot(h, w_ref[:, n_att:]))
    u_ref[...] = z[:, :dg]
    vg_ref[...] = z[:, dg:]
    p = _dot(h, w_ref[:, :n_att])
    q_ref[...] = (p[:, :D_ATTN] * (HEAD_DIM ** -0.5)).astype(BF16)
    lo = lax.broadcasted_iota(jnp.int32, (1, LANES), 1) < HEAD_DIM
    for j in range(2):
        t = p[:, D_ATTN + j * D_KV:D_ATTN + (j + 1) * D_KV]
        t_sw = pltpu.roll(t, HEAD_DIM, axis=1)
        kv_ref[:, (2 * j) * LANES:(2 * j + 1) * LANES] = jnp.where(lo, t, t_sw).astype(BF16)
        kv_ref[:, (2 * j + 1) * LANES:(2 * j + 2) * LANES] = jnp.where(lo, t_sw, t).astype(BF16)


def _proj(x2, mod, g_pre, w_in, s):
    t, d = x2.shape
    nw = w_in.shape[1]
    dg = (nw - D_ATTN - 2 * D_KV) // 2
    nsb = s // TM
    return pl.pallas_call(
        _proj_kernel,
        grid=(t // TM,),
        in_specs=[
            pl.BlockSpec((TM, d), lambda i: (i, 0)),
            pl.BlockSpec((1, 6, d), lambda i: (i // nsb, 0, 0)),
            pl.BlockSpec((1, d), lambda i: (0, 0)),
            pl.BlockSpec((d, nw), lambda i: (0, 0)),
        ],
        out_specs=[
            pl.BlockSpec((TM, D_ATTN), lambda i: (i, 0)),
            pl.BlockSpec((TM, D_ATTN), lambda i: (i, 0)),
            pl.BlockSpec((TM, dg), lambda i: (i, 0)),
            pl.BlockSpec((TM, dg), lambda i: (i, 0)),
        ],
        out_shape=[
            jax.ShapeDtypeStruct((t, D_ATTN), BF16),
            jax.ShapeDtypeStruct((t, D_ATTN), BF16),
            jax.ShapeDtypeStruct((t, dg), F32),
            jax.ShapeDtypeStruct((t, dg), F32),
        ],
        compiler_params=pltpu.CompilerParams(
            dimension_semantics=("arbitrary",), vmem_limit_bytes=VMEM_LIMIT),
        name="proj",
    )(x2, mod, g_pre, w_in)


def _attn_bias_table():
    qi = np.arange(BLOCK)[:, None]
    kj = np.arange(3 * BLOCK)[None, :] - BLOCK
    dist = np.abs(qi - kj).astype(np.float32)
    band = dist <= WINDOW
    slopes = np.exp2(-8.0 * np.arange(1, N_HEADS + 1, dtype=np.float32) / N_HEADS).astype(np.float32)
    key_ok = [kj >= 0, np.ones_like(kj, bool), kj < BLOCK]
    tab = np.empty((3, N_HEADS, BLOCK, 3 * BLOCK), np.float32)
    for v in range(3):
        ok = band & key_ok[v]
        for h in range(N_HEADS):
            tab[v, h] = np.where(ok, -slopes[h] * dist, np.float32(MASK_VALUE))
    return tab.reshape(3, N_KV_HEADS, GROUP * BLOCK, 3 * BLOCK)


def _mix_kernel(nblk_seq, sink_ref, x_ref, q_ref, kvp_ref, kvc_ref, kvn_ref, u_ref, vg_ref,
                mod_ref, bias_ref, ws_ref, bs_ref, lng_ref, lnb_ref, ga_ref, gg_ref,
                wout_ref, gpost_ref, gpre2_ref, wr_ref,
                x1_ref, h2_ref, aff_ref, kcat, attn_s, gm_s):
    i = pl.program_id(0)
    nb_tile = TM // BLOCK
    kcat[0:BLOCK, :] = kvp_ref[...]
    kcat[BLOCK:BLOCK + TM, :] = kvc_ref[...]
    kcat[BLOCK + TM:, :] = kvn_ref[...]
    lane = lax.broadcasted_iota(jnp.int32, (1, LANES), 1)
    lo = lane < HEAD_DIM
    zero = jnp.zeros((), BF16)
    head_of_row = lax.broadcasted_iota(jnp.int32, (GROUP * BLOCK, 1), 0) // BLOCK
    sinks = []
    for g in range(N_KV_HEADS):
        col = jnp.zeros((GROUP * BLOCK, 1), F32)
        for j in range(GROUP):
            col = jnp.where(head_of_row == j, sink_ref[g * GROUP + j], col)
        sinks.append(col)

    def block_body(a, carry):
        r0 = pl.multiple_of(a * BLOCK, BLOCK)
        pos = (i * nb_tile + a) % nblk_seq
        var = jnp.where(pos == 0, 0, jnp.where(pos == nblk_seq - 1, 2, 1))
        kv = kcat[pl.ds(r0, 3 * BLOCK), :]
        for g in range(N_KV_HEADS):
            k = kv[:, g * LANES:(g + 1) * LANES]
            v = kv[:, (N_KV_HEADS + g) * LANES:(N_KV_HEADS + g + 1) * LANES]
            pieces = []
            for pi in range(GROUP // 2):
                qp = q_ref[pl.ds(r0, BLOCK), (g * GROUP // 2 + pi) * LANES:(g * GROUP // 2 + pi + 1) * LANES]
                pieces += [jnp.where(lo, qp, zero), jnp.where(lo, zero, qp)]
            s = _dot_nt(jnp.concatenate(pieces, axis=0), k) + bias_ref[var, g]
            m = jnp.maximum(jnp.max(s, axis=-1, keepdims=True), sinks[g])
            p = jnp.exp(s - m)
            den = jnp.sum(p, axis=-1, keepdims=True) + jnp.exp(sinks[g] - m)
            o = _dot(p.astype(BF16), v) / den
            for pi in range(GROUP // 2):
                even = o[(2 * pi) * BLOCK:(2 * pi + 1) * BLOCK, :]
                odd = o[(2 * pi + 1) * BLOCK:(2 * pi + 2) * BLOCK, :]
                col0 = (g * GROUP // 2 + pi) * LANES
                attn_s[pl.ds(r0, BLOCK), col0:col0 + LANES] = jnp.where(lo, even, odd)
        vg = vg_ref[pl.ds(r0, BLOCK), :]
        mu = jnp.mean(vg, axis=-1, keepdims=True)
        xc = vg - mu
        vn = (xc * lax.rsqrt(jnp.mean(xc * xc, axis=-1, keepdims=True) + EPS) * lng_ref[...]
              + lnb_ref[...]).astype(BF16)
        for pi in range(N_GMLP_GROUPS // 2):
            vnp = vn[:, pi * LANES:(pi + 1) * LANES]
            z = jnp.where(lo, _dot(ws_ref[2 * pi], vnp), _dot(ws_ref[2 * pi + 1], vnp))
            sl = slice(pi * LANES, (pi + 1) * LANES)
            gm_s[pl.ds(r0, BLOCK), sl] = u_ref[pl.ds(r0, BLOCK), sl] * (z + bs_ref[:, sl])
        return carry

    lax.fori_loop(0, nb_tile, block_body, 0, unroll=2)

    na = _rms(attn_s[...], ga_ref[...]).astype(BF16)
    ng = _rms(gm_s[...], gg_ref[...]).astype(BF16)
    mo = _dot(na, wout_ref[0:D_ATTN, :]) + _dot(ng, wout_ref[D_ATTN:, :])
    gate1 = mod_ref[0, 2:3, :]
    x1 = x_ref[...] + gate1 * _rms(mo, gpost_ref[...])
    x1_ref[...] = x1
    h2 = (_rms(x1, gpre2_ref[...]) * (1.0 + mod_ref[0, 4:5, :]) + mod_ref[0, 3:4, :]).astype(BF16)
    h2_ref[...] = h2
    lg = _dot_nt(wr_ref[...], h2)
    lg = lg - jnp.max(lg, axis=0, keepdims=True)
    ex = jnp.exp(lg)
    aff_ref[0] = ex / jnp.sum(ex, axis=0, keepdims=True)


def _mix(x2, q, kv, u, vg, mod, sink, bias, ws, bs_full, ln_g, ln_b, g_attn, g_gmlp,
         w_out, g_post, g_pre2, w_r_t, b, s):
    t, d = x2.shape
    dg = u.shape[1]
    e = w_r_t.shape[0]
    nsb = s // TM
    nb_tile = TM // BLOCK
    nblk = t // BLOCK
    row = lambda i: (i, 0)
    const2 = lambda i: (0, 0)
    in_specs = [
        pl.BlockSpec(memory_space=pltpu.SMEM),
        pl.BlockSpec((TM, d), row),
        pl.BlockSpec((TM, D_ATTN), row),
        pl.BlockSpec((BLOCK, 4 * LANES), lambda i: (jnp.maximum(i * nb_tile - 1, 0), 0)),
        pl.BlockSpec((TM, 4 * LANES), row),
        pl.BlockSpec((BLOCK, 4 * LANES), lambda i: (jnp.minimum((i + 1) * nb_tile, nblk - 1), 0)),
        pl.BlockSpec((TM, dg), row),
        pl.BlockSpec((TM, dg), row),
        pl.BlockSpec((1, 6, d), lambda i: (i // nsb, 0, 0)),
        pl.BlockSpec(bias.shape, lambda i: (0, 0, 0, 0)),
        pl.BlockSpec(ws.shape, lambda i: (0, 0, 0)),
        pl.BlockSpec((BLOCK, dg), const2),
        pl.BlockSpec((1, dg), const2),
        pl.BlockSpec((1, dg), const2),
        pl.BlockSpec((1, D_ATTN), const2),
        pl.BlockSpec((1, dg), const2),
        pl.BlockSpec(w_out.shape, const2),
        pl.BlockSpec((1, d), const2),
        pl.BlockSpec((1, d), const2),
        pl.BlockSpec(w_r_t.shape, const2),
    ]
    out_specs = [
        pl.BlockSpec((TM, d), row),
        pl.BlockSpec((TM, d), row),
        pl.BlockSpec((1, e, TM), lambda i: (i // nsb, 0, i % nsb)),
    ]
    out_shape = [
        jax.ShapeDtypeStruct((t, d), F32),
        jax.ShapeDtypeStruct((t, d), BF16),
        jax.ShapeDtypeStruct((b, e, s), F32),
    ]
    return pl.pallas_call(
        functools.partial(_mix_kernel, s // BLOCK),
        grid=(t // TM,),
        in_specs=in_specs,
        out_specs=out_specs,
        out_shape=out_shape,
        scratch_shapes=[
            pltpu.VMEM((TM + 2 * BLOCK, 4 * LANES), BF16),
            pltpu.VMEM((TM, D_ATTN), F32),
            pltpu.VMEM((TM, dg), F32),
        ],
        compiler_params=pltpu.CompilerParams(
            dimension_semantics=("arbitrary",), vmem_limit_bytes=VMEM_LIMIT),
        name="mix",
    )(sink, x2, q, kv, kv, kv, u, vg, mod, bias, ws, bs_full, ln_g, ln_b, g_attn, g_gmlp,
      w_out, g_post, g_pre2, w_r_t)


def _route_kernel(cap, aff_ref, slot_ref, affb_ref, cnt_ref):
    aff = aff_ref[...]
    e, s = aff.shape
    capf = jnp.float32(cap)

    def count_ge(t):
        return jnp.sum(jnp.where(aff >= t, 1.0, 0.0), axis=1, keepdims=True)

    def bit_body(it, cand):
        trial = cand | jnp.left_shift(jnp.int32(1), 30 - it)
        t = lax.bitcast_convert_type(trial, F32)
        return jnp.where(count_ge(t) >= capf, trial, cand)

    cand = lax.fori_loop(0, 31, bit_body, jnp.zeros((e, 1), jnp.int32))
    thr = lax.bitcast_convert_type(cand, F32)
    need = capf - jnp.sum(jnp.where(aff > thr, 1.0, 0.0), axis=1, keepdims=True)

    r = lax.broadcasted_iota(jnp.int32, (LANES, LANES), 0)
    c = lax.broadcasted_iota(jnp.int32, (LANES, LANES), 1)
    tri = jnp.where(r < c, 1.0, 0.0).astype(BF16)
    lane = lax.broadcasted_iota(jnp.int32, (e, LANES), 1)
    carry_eq = jnp.zeros((e, 1), F32)
    carry_sel = jnp.zeros((e, 1), F32)
    cnt = jnp.zeros((e, LANES), F32)
    per_blk = MXU_DIM // LANES
    for j in range(s // LANES):
        a = aff[:, j * LANES:(j + 1) * LANES]
        gt = a > thr
        eq = jnp.where(a == thr, 1.0, 0.0)
        tie_rank = _dot(eq.astype(BF16), tri) + carry_eq
        sel = jnp.where(gt | ((eq > 0.0) & (tie_rank < need)), 1.0, 0.0)
        slot = _dot(sel.astype(BF16), tri) + carry_sel
        blk = (slice(None), j // per_blk, slice((j % per_blk) * LANES, (j % per_blk + 1) * LANES))
        slot_ref[blk] = jnp.where(sel > 0.0, slot, -1.0).astype(jnp.int32)
        affb_ref[blk] = a
        if j % per_blk == 0:
            cnt = jnp.where(lane == j // per_blk, carry_sel, cnt)
        carry_eq = carry_eq + jnp.sum(eq, axis=1, keepdims=True)
        carry_sel = carry_sel + jnp.sum(sel, axis=1, keepdims=True)
    cnt = jnp.where(lane >= s // MXU_DIM, carry_sel, cnt)
    cnt_ref[...] = cnt.astype(jnp.int32)


def _route(aff2, cap):
    r, s = aff2.shape
    nblk = s // MXU_DIM
    return pl.pallas_call(
        functools.partial(_route_kernel, cap),
        grid=(1,),
        in_specs=[pl.BlockSpec((r, s), lambda i: (0, 0))],
        out_specs=[
            pl.BlockSpec((r, nblk, MXU_DIM), lambda i: (0, 0, 0)),
            pl.BlockSpec((r, nblk, MXU_DIM), lambda i: (0, 0, 0)),
            pl.BlockSpec((r, LANES), lambda i: (0, 0)),
        ],
        out_shape=[
            jax.ShapeDtypeStruct((r, nblk, MXU_DIM), jnp.int32),
            jax.ShapeDtypeStruct((r, nblk, MXU_DIM), F32),
            jax.ShapeDtypeStruct((r, LANES), jnp.int32),
        ],
        compiler_params=pltpu.CompilerParams(dimension_semantics=("arbitrary",)),
        name="route",
    )(aff2)


def _segment(cnt_ref, row, pb, cap):
    c0 = jnp.minimum(cnt_ref[row, pb], cap)
    c1 = jnp.minimum(cnt_ref[row, pb + 1], cap)
    a0 = jnp.minimum(jnp.bitwise_and(c0, -SLOT_ALIGN), cap - WIN)
    return c0, c1, pl.multiple_of(a0, SLOT_ALIGN)


def _extra_windows(c1, a0):
    return (jnp.maximum(c1 - (a0 + WIN), 0) + WIN - 1) // WIN


def _gather_kernel(n_exp, cnt_ref, h2_ref, slot_ref, aff_ref, x_ref, gs_ref):
    b, eg = pl.program_id(0), pl.program_id(1)
    ng, cap = x_ref.shape[1], x_ref.shape[2]
    nblk = slot_ref.shape[2]
    sub = lax.broadcasted_iota(jnp.int32, (WIN, MXU_DIM), 0)
    sub_col = lax.broadcasted_iota(jnp.int32, (WIN, 1), 0)
    x_ref[...] = jnp.zeros_like(x_ref)
    gs_ref[...] = jnp.zeros_like(gs_ref)

    def merge(k, a0, first, last, rows, gate):
        slot_id = sub_col + a0
        keep = (slot_id >= first) & (slot_id < last)
        win = (0, k, pl.ds(a0, WIN), slice(None))
        x_ref[win] = jnp.where(keep, rows.astype(BF16), x_ref[win])
        gs_ref[win] = jnp.where(keep, gate, gs_ref[win])

    def tokens(pb):
        return pl.ds(pl.multiple_of(pb * MXU_DIM, MXU_DIM), MXU_DIM)

    def first_windows(pb):
        segs, gates, onehots = [], [], []
        overflow = jnp.int32(0)
        for k in range(ng):
            c0, c1, a0 = _segment(cnt_ref, b * n_exp + eg * ng + k, pb, cap)
            match = slot_ref[0, k, pl.ds(pb, 1), :] == (sub + a0)
            onehots.append(jnp.where(match, 1.0, 0.0).astype(BF16))
            gates.append(jnp.sum(jnp.where(match, aff_ref[0, k, pl.ds(pb, 1), :], 0.0),
                                 axis=1, keepdims=True))
            segs.append((c0, c1, a0))
            overflow = overflow + jnp.maximum(c1 - (a0 + WIN), 0)
        rows = _dot(jnp.concatenate(onehots, axis=0), h2_ref[0, tokens(pb), :])
        for k in range(ng):
            c0, c1, a0 = segs[k]
            merge(k, a0, c0, c1, rows[k * WIN:(k + 1) * WIN, :], gates[k])
        return overflow

    def further_windows(pb):
        def per_expert(k, carry2):
            _, c1, a0 = _segment(cnt_ref, b * n_exp + eg * ng + k, pb, cap)

            def per_window(w, carry3):
                first = a0 + (w + 1) * WIN
                aw = pl.multiple_of(jnp.minimum(first, cap - WIN), SLOT_ALIGN)
                srow = slot_ref[0, k, pl.ds(pb, 1), :]
                match = (srow == (sub + aw)) & (srow >= first)
                onehot = jnp.where(match, 1.0, 0.0).astype(BF16)
                gate = jnp.sum(jnp.where(match, aff_ref[0, k, pl.ds(pb, 1), :], 0.0),
                               axis=1, keepdims=True)
                merge(k, aw, first, c1, _dot(onehot, h2_ref[0, tokens(pb), :]), gate)
                return carry3

            return lax.fori_loop(0, _extra_windows(c1, a0), per_window, carry2)

        lax.fori_loop(0, ng, per_expert, 0)

    def pair_body(i, carry):
        overflows = [first_windows(2 * i + j) for j in range(2)]
        for j in range(2):
            pl.when(overflows[j] > 0)(functools.partial(further_windows, 2 * i + j))
        return carry

    lax.fori_loop(0, nblk // 2, pair_body, 0)


def _gather(cnt2, h2, slots4, aff4, cap):
    b, s, d = h2.shape
    e = slots4.shape[1]
    nblk = s // MXU_DIM
    grid_spec = pltpu.PrefetchScalarGridSpec(
        num_scalar_prefetch=1,
        grid=(b, e // EG),
        in_specs=[
            pl.BlockSpec((1, s, d), lambda bi, gi, c: (bi, 0, 0), pipeline_mode=pl.Buffered(1)),
            pl.BlockSpec((1, EG, nblk, MXU_DIM), lambda bi, gi, c: (bi, gi, 0, 0)),
            pl.BlockSpec((1, EG, nblk, MXU_DIM), lambda bi, gi, c: (bi, gi, 0, 0)),
        ],
        out_specs=[
            pl.BlockSpec((1, EG, cap, d), lambda bi, gi, c: (bi, gi, 0, 0)),
            pl.BlockSpec((1, EG, cap, LANES), lambda bi, gi, c: (bi, gi, 0, 0)),
        ],
    )
    return pl.pallas_call(
        functools.partial(_gather_kernel, e),
        grid_spec=grid_spec,
        out_shape=[
            jax.ShapeDtypeStruct((b, e, cap, d), BF16),
            jax.ShapeDtypeStruct((b, e, cap, LANES), F32),
        ],
        compiler_params=pltpu.CompilerParams(
            dimension_semantics=("arbitrary", "arbitrary"), vmem_limit_bytes=VMEM_LIMIT),
        name="gather",
    )(cnt2, h2, slots4, aff4)


def _ffn_kernel(x_ref, gs_ref, wg_ref, wu_ref, wd_ref, out_ref, acc_s):
    e, f = pl.program_id(0), pl.program_id(1)
    nb, _, cap, d = x_ref.shape

    @pl.when((e == 0) & (f == 0))
    def _init():
        acc_s[...] = jnp.zeros_like(acc_s)

    wg, wu, wd = wg_ref[0].astype(BF16), wu_ref[0].astype(BF16), wd_ref[0].astype(BF16)
    for bi in range(nb):
        x = x_ref[bi, 0]
        g = _dot(x, wg)
        u = _dot(x, wu)
        hid = (g * jax.nn.sigmoid(g) * u).astype(BF16)
        y = _dot(hid, wd)
        rows = slice(bi * cap, (bi + 1) * cap)
        acc_s[rows, :] = jnp.where(f == 0, y, acc_s[rows, :] + y)

    @pl.when(f == pl.num_programs(1) - 1)
    def _emit():
        gate = gs_ref[:, 0, :, 0:1].reshape(nb * cap, 1)
        out_ref[:, 0] = (acc_s[...] * gate).reshape(nb, cap, d).astype(BF16)


def _ffn(xg, gs, w_gate, w_up, w_down, e0):
    b, e, cap, d = xg.shape
    ff = w_gate.shape[2]
    return pl.pallas_call(
        _ffn_kernel,
        grid=(e, ff // FC),
        in_specs=[
            pl.BlockSpec((b, 1, cap, d), lambda ei, fi: (0, ei, 0, 0)),
            pl.BlockSpec((b, 1, cap, LANES), lambda ei, fi: (0, ei, 0, 0)),
            pl.BlockSpec((1, d, FC), lambda ei, fi: (e0 + ei, 0, fi)),
            pl.BlockSpec((1, d, FC), lambda ei, fi: (e0 + ei, 0, fi)),
            pl.BlockSpec((1, FC, d), lambda ei, fi: (e0 + ei, fi, 0)),
        ],
        out_specs=pl.BlockSpec((b, 1, cap, d), lambda ei, fi: (0, ei, 0, 0)),
        out_shape=jax.ShapeDtypeStruct((b, e, cap, d), BF16),
        scratch_shapes=[pltpu.VMEM((b * cap, d), F32)],
        compiler_params=pltpu.CompilerParams(
            dimension_semantics=("arbitrary", "arbitrary"), vmem_limit_bytes=VMEM_LIMIT),
        name="ffn",
    )(xg, gs, w_gate, w_up, w_down)


def _combine_kernel(n_exp, cnt_ref, eo_ref, slot_ref, x1_ref, mod_ref, g_ref, o_ref,
                    stage_s, oh_s, y_s):
    b, q = pl.program_id(0), pl.program_id(1)
    cap = eo_ref.shape[2]
    nblk_q = o_ref.shape[0] // MXU_DIM
    sub = lax.broadcasted_iota(jnp.int32, (WIN, MXU_DIM), 0)
    gate = mod_ref[0, 5:6, :]

    for j in range(nblk_q):
        pb = q * nblk_q + j
        overflow = jnp.int32(0)
        for e in range(n_exp):
            _, c1, a0 = _segment(cnt_ref, b * n_exp + e, pb, cap)
            stage_s[e * WIN:(e + 1) * WIN, :] = eo_ref[0, e, pl.ds(a0, WIN), :]
            match = slot_ref[0, e, pl.ds(pb, 1), :] == (sub + a0)
            oh_s[e * WIN:(e + 1) * WIN, :] = jnp.where(match, 1.0, 0.0).astype(BF16)
            overflow = overflow + jnp.maximum(c1 - (a0 + WIN), 0)
        y_s[...] = _dot_tn(oh_s[...], stage_s[...])

        @pl.when(overflow > 0)
        def _long_segments(j=j, pb=pb):
            def per_expert(e, carry):
                _, c1, a0 = _segment(cnt_ref, b * n_exp + e, pb, cap)

                def per_window(k, carry2):
                    first = a0 + (k + 1) * WIN
                    aw = pl.multiple_of(jnp.minimum(first, cap - WIN), SLOT_ALIGN)
                    srow = slot_ref[0, e, pl.ds(pb, 1), :]
                    match = (srow == (sub + aw)) & (srow >= first)
                    onehot = jnp.where(match, 1.0, 0.0).astype(BF16)
                    y_s[...] += _dot_tn(onehot, eo_ref[0, e, pl.ds(aw, WIN), :])
                    return carry2

                return lax.fori_loop(0, _extra_windows(c1, a0), per_window, carry)

            lax.fori_loop(0, n_exp, per_expert, 0)

        sl = slice(j * MXU_DIM, (j + 1) * MXU_DIM)
        o_ref[sl, :] = x1_ref[sl, :] + gate * _rms(y_s[...], g_ref[...])


def _combine(cnt2, eo, slots4, x1, mod, g_post):
    b, e, cap, d = eo.shape
    t = x1.shape[0]
    s = t // b
    nq = s // QP
    nblk_q = QP // MXU_DIM
    grid_spec = pltpu.PrefetchScalarGridSpec(
        num_scalar_prefetch=1,
        grid=(b, nq),
        in_specs=[
            pl.BlockSpec((1, e, cap, d), lambda bi, qi, c: (bi, 0, 0, 0), pipeline_mode=pl.Buffered(1)),
            pl.BlockSpec((1, e, s // MXU_DIM, MXU_DIM), lambda bi, qi, c: (bi, 0, 0, 0)),
            pl.BlockSpec((QP, d), lambda bi, qi, c: (bi * nq + qi, 0)),
            pl.BlockSpec((1, 6, d), lambda bi, qi, c: (bi, 0, 0)),
            pl.BlockSpec((1, d), lambda bi, qi, c: (0, 0)),
        ],
        out_specs=pl.BlockSpec((QP, d), lambda bi, qi, c: (bi * nq + qi, 0)),
        scratch_shapes=[
            pltpu.VMEM((e * WIN, d), BF16),
            pltpu.VMEM((e * WIN, MXU_DIM), BF16),
            pltpu.VMEM((MXU_DIM, d), F32),
        ],
    )
    return pl.pallas_call(
        functools.partial(_combine_kernel, e),
        grid_spec=grid_spec,
        out_shape=jax.ShapeDtypeStruct((t, d), F32),
        compiler_params=pltpu.CompilerParams(
            dimension_semantics=("arbitrary", "arbitrary"),
            vmem_limit_bytes=VMEM_LIMIT),
        name="combine",
    )(cnt2, eo, slots4, x1, mod, g_post)


def _layer(layer, x, c, w_ada_all, b_ada, norm_pre_mix, norm_post_mix, w_in, sink, sgu_ln_g,
           sgu_ln_b, w_s, b_s, norm_out_attn, norm_out_gmlp, w_out, norm_pre_ffn, norm_post_ffn,
           w_router, w_gate_all, w_up_all, w_down_all):
    b, s, d = x.shape
    t = b * s
    dg = sgu_ln_g.shape[0]
    e = w_router.shape[1]
    cap = CAPACITY_FACTOR * s // e
    assert s % QP == 0 and s % TM == 0 and w_gate_all.shape[2] % FC == 0 and e % EG == 0
    assert cap % SLOT_ALIGN == 0 and cap >= WIN and s // MXU_DIM < LANES and d == D_ATTN + dg
    assert (s // MXU_DIM) % 2 == 0

    mod = _adaln(c, w_ada_all, b_ada, layer).reshape(b, 6, d)

    x2 = x.reshape(t, d)
    q, kv, u, vg = _proj(x2, mod, norm_pre_mix.reshape(1, d), w_in.astype(BF16), s)

    bias = jnp.asarray(_attn_bias_table())
    bs_full = jnp.repeat(b_s.T, dg // N_GMLP_GROUPS, axis=1)
    x1, h2, aff_t = _mix(
        x2, q, kv, u, vg, mod, sink, bias, w_s.astype(BF16), bs_full,
        sgu_ln_g.reshape(1, dg), sgu_ln_b.reshape(1, dg), norm_out_attn.reshape(1, D_ATTN),
        norm_out_gmlp.reshape(1, dg), w_out.astype(BF16), norm_post_mix.reshape(1, d),
        norm_pre_ffn.reshape(1, d), w_router.T.astype(BF16), b, s)

    slots3, aff3, cnt2 = _route(aff_t.reshape(b * e, s), cap)
    nblk = s // MXU_DIM
    slots4 = slots3.reshape(b, e, nblk, MXU_DIM)
    aff4 = aff3.reshape(b, e, nblk, MXU_DIM)
    xg, gs = _gather(cnt2, h2.reshape(b, s, d), slots4, aff4, cap)
    eo = _ffn(xg, gs, w_gate_all, w_up_all, w_down_all, layer * e)
    out = _combine(cnt2, eo, slots4, x1, mod, norm_post_ffn.reshape(1, d))
    return out.reshape(b, s, d)


def kernel(x, c, w_ada, b_ada, norm_pre_mix, norm_post_mix, w_in, sink, sgu_ln_g, sgu_ln_b, w_s, b_s, norm_out_attn, norm_out_gmlp, w_out, norm_pre_ffn, norm_post_ffn, w_router, w_gate, w_up, w_down):
    depth, d = w_ada.shape[0], w_ada.shape[1]
    w_ada_all = w_ada.reshape(depth * d, w_ada.shape[2])
    stack = lambda w: w.reshape((depth * w.shape[1],) + w.shape[2:])
    w_gate_all, w_up_all, w_down_all = stack(w_gate), stack(w_up), stack(w_down)
    for l in range(depth):
        x = _layer(l, x, c, w_ada_all, b_ada[l], norm_pre_mix[l], norm_post_mix[l], w_in[l],
                   sink[l], sgu_ln_g[l], sgu_ln_b[l], w_s[l], b_s[l], norm_out_attn[l],
                   norm_out_gmlp[l], w_out[l], norm_pre_ffn[l], norm_post_ffn[l], w_router[l],
                   w_gate_all, w_up_all, w_down_all)
    return x
```

```python
import functools

import numpy as np
import jax
import jax.numpy as jnp
from jax import lax
from jax.experimental import pallas as pl
from jax.experimental.pallas import tpu as pltpu

N_HEADS = 8
N_KV_HEADS = 2
HEAD_DIM = 64
D_ATTN = N_HEADS * HEAD_DIM
D_KV = N_KV_HEADS * HEAD_DIM
GROUP = N_HEADS // N_KV_HEADS
WINDOW = 128
BLOCK = 128
N_GMLP_GROUPS = 8
N_EXPERTS = 16
CAPACITY_FACTOR = 2
EPS = 1e-6
MASK_VALUE = -1e30

LANES = 128
MXU_DIM = 256

TM = 1024
FC = 256
PROJ_ROWS = 256
OUT_ROWS = TM
EG = 4
QP = 1024
WIN = 64
SLOT_ALIGN = 16
VMEM_LIMIT = 56 * 1024 * 1024

BF16 = jnp.bfloat16
F32 = jnp.float32


def _rms(x, g):
    return x * lax.rsqrt(jnp.mean(x * x, axis=-1, keepdims=True) + EPS) * g


def _dot(a, b):
    return jnp.dot(a, b, preferred_element_type=F32)


def _dot_nt(a, b):
    return lax.dot_general(a, b, (((1,), (1,)), ((), ())), preferred_element_type=F32)


def _dot_tn(a, b):
    return lax.dot_general(a, b, (((0,), (0,)), ((), ())), preferred_element_type=F32)


def _adaln_kernel(c_ref, w_ref, b_ref, o_ref):
    c = c_ref[...]
    a = (c * jax.nn.sigmoid(c)).astype(BF16)
    o_ref[...] = _dot(a, w_ref[...].astype(BF16)) + b_ref[...]


def _adaln(c, w_ada_all, b_ada, layer):
    b, d = c.shape
    n = w_ada_all.shape[1]
    return pl.pallas_call(
        _adaln_kernel,
        grid=(n // d,),
        in_specs=[
            pl.BlockSpec((b, d), lambda j: (0, 0)),
            pl.BlockSpec((d, d), lambda j: (layer, j)),
            pl.BlockSpec((1, d), lambda j: (0, j)),
        ],
        out_specs=pl.BlockSpec((b, d), lambda j: (0, j)),
        out_shape=jax.ShapeDtypeStruct((b, n), F32),
        compiler_params=pltpu.CompilerParams(dimension_semantics=("arbitrary",)),
        name="adaln",
    )(c, w_ada_all, b_ada.reshape(1, n))


def _gelu_tanh(x):
    c = np.float32(np.sqrt(2.0 / np.pi))
    return x * (0.5 * (1.0 + jnp.tanh(c * (x + 0.044715 * (x * x * x)))))


def _proj_kernel(x_ref, mod_ref, g_ref, w_ref, q_ref, kv_ref, u_ref, vg_ref):
    shift = mod_ref[0, 0:1, :]
    scale = mod_ref[0, 1:2, :]
    n_att = D_ATTN + 2 * D_KV
    dg = u_ref.shape[1]
    lo = lax.broadcasted_iota(jnp.int32, (1, LANES), 1) < HEAD_DIM
    for r in range(0, TM, PROJ_ROWS):
        rows = slice(r, r + PROJ_ROWS)
        h = (_rms(x_ref[rows, :], g_ref[...]) * (1.0 + scale) + shift).astype(BF16)
        z = _gelu_tanh(_dot(h, w_ref[:, n_att:]))
        u_ref[rows, :] = z[:, :dg]
        vg_ref[rows, :] = z[:, dg:]
        p = _dot(h, w_ref[:, :n_att])
        q_ref[rows, :] = (p[:, :D_ATTN] * (HEAD_DIM ** -0.5)).astype(BF16)
        for j in range(2):
            t = p[:, D_ATTN + j * D_KV:D_ATTN + (j + 1) * D_KV]
            t_sw = pltpu.roll(t, HEAD_DIM, axis=1)
            kv_ref[rows, (2 * j) * LANES:(2 * j + 1) * LANES] = jnp.where(lo, t, t_sw).astype(BF16)
            kv_ref[rows, (2 * j + 1) * LANES:(2 * j + 2) * LANES] = jnp.where(lo, t_sw, t).astype(BF16)


def _proj(x2, mod, g_pre, w_in, s):
    t, d = x2.shape
    nw = w_in.shape[1]
    dg = (nw - D_ATTN - 2 * D_KV) // 2
    nsb = s // TM
    return pl.pallas_call(
        _proj_kernel,
        grid=(t // TM,),
        in_specs=[
            pl.BlockSpec((TM, d), lambda i: (i, 0)),
            pl.BlockSpec((1, 6, d), lambda i: (i // nsb, 0, 0)),
            pl.BlockSpec((1, d), lambda i: (0, 0)),
            pl.BlockSpec((d, nw), lambda i: (0, 0)),
        ],
        out_specs=[
            pl.BlockSpec((TM, D_ATTN), lambda i: (i, 0)),
            pl.BlockSpec((TM, D_ATTN), lambda i: (i, 0)),
            pl.BlockSpec((TM, dg), lambda i: (i, 0)),
            pl.BlockSpec((TM, dg), lambda i: (i, 0)),
        ],
        out_shape=[
            jax.ShapeDtypeStruct((t, D_ATTN), BF16),
            jax.ShapeDtypeStruct((t, D_ATTN), BF16),
            jax.ShapeDtypeStruct((t, dg), F32),
            jax.ShapeDtypeStruct((t, dg), F32),
        ],
        compiler_params=pltpu.CompilerParams(
            dimension_semantics=("arbitrary",), vmem_limit_bytes=VMEM_LIMIT),
        name="proj",
    )(x2, mod, g_pre, w_in)


def _attn_bias_table():
    qi = np.arange(BLOCK)[:, None]
    kj = np.arange(3 * BLOCK)[None, :] - BLOCK
    dist = np.abs(qi - kj).astype(np.float32)
    band = dist <= WINDOW
    slopes = np.exp2(-8.0 * np.arange(1, N_HEADS + 1, dtype=np.float32) / N_HEADS).astype(np.float32)
    key_ok = [kj >= 0, np.ones_like(kj, bool), kj < BLOCK]
    tab = np.empty((3, N_HEADS, BLOCK, 3 * BLOCK), np.float32)
    for v in range(3):
        ok = band & key_ok[v]
        for h in range(N_HEADS):
            tab[v, h] = np.where(ok, -slopes[h] * dist, np.float32(MASK_VALUE))
    return tab.reshape(3, N_KV_HEADS, GROUP * BLOCK, 3 * BLOCK)


def _mix_kernel(nblk_seq, sink_ref, x_ref, q_ref, kvp_ref, kvc_ref, kvn_ref, u_ref, vg_ref,
                mod_ref, bias_ref, ws_ref, bs_ref, lng_ref, lnb_ref, ga_ref, gg_ref,
                wout_ref, gpost_ref, gpre2_ref, wr_ref,
                x1_ref, h2_ref, aff_ref, kcat, attn_s, gm_s):
    i = pl.program_id(0)
    nb_tile = TM // BLOCK
    kcat[0:BLOCK, :] = kvp_ref[...]
    kcat[BLOCK:BLOCK + TM, :] = kvc_ref[...]
    kcat[BLOCK + TM:, :] = kvn_ref[...]
    lane = lax.broadcasted_iota(jnp.int32, (1, LANES), 1)
    lo = lane < HEAD_DIM
    zero = jnp.zeros((), BF16)
    head_of_row = lax.broadcasted_iota(jnp.int32, (GROUP * BLOCK, 1), 0) // BLOCK
    sinks = []
    for g in range(N_KV_HEADS):
        col = jnp.zeros((GROUP * BLOCK, 1), F32)
        for j in range(GROUP):
            col = jnp.where(head_of_row == j, sink_ref[g * GROUP + j], col)
        sinks.append(col)

    def block_body(a, carry):
        r0 = pl.multiple_of(a * BLOCK, BLOCK)
        pos = (i * nb_tile + a) % nblk_seq
        var = jnp.where(pos == 0, 0, jnp.where(pos == nblk_seq - 1, 2, 1))
        kv = kcat[pl.ds(r0, 3 * BLOCK), :]
        for g in range(N_KV_HEADS):
            k = kv[:, g * LANES:(g + 1) * LANES]
            v = kv[:, (N_KV_HEADS + g) * LANES:(N_KV_HEADS + g + 1) * LANES]
            pieces = []
            for pi in range(GROUP // 2):
                qp = q_ref[pl.ds(r0, BLOCK), (g * GROUP // 2 + pi) * LANES:(g * GROUP // 2 + pi + 1) * LANES]
                pieces += [jnp.where(lo, qp, zero), jnp.where(lo, zero, qp)]
            s = _dot_nt(jnp.concatenate(pieces, axis=0), k) + bias_ref[var, g]
            m = jnp.maximum(jnp.max(s, axis=-1, keepdims=True), sinks[g])
            p = jnp.exp(s - m)
            den = jnp.sum(p, axis=-1, keepdims=True) + jnp.exp(sinks[g] - m)
            o = _dot(p.astype(BF16), v) / den
            for pi in range(GROUP // 2):
                even = o[(2 * pi) * BLOCK:(2 * pi + 1) * BLOCK, :]
                odd = o[(2 * pi + 1) * BLOCK:(2 * pi + 2) * BLOCK, :]
                col0 = (g * GROUP // 2 + pi) * LANES
                attn_s[pl.ds(r0, BLOCK), col0:col0 + LANES] = jnp.where(lo, even, odd)
        vg = vg_ref[pl.ds(r0, BLOCK), :]
        mu = jnp.mean(vg, axis=-1, keepdims=True)
        xc = vg - mu
        vn = (xc * lax.rsqrt(jnp.mean(xc * xc, axis=-1, keepdims=True) + EPS) * lng_ref[...]
              + lnb_ref[...]).astype(BF16)
        for pi in range(N_GMLP_GROUPS // 2):
            vnp = vn[:, pi * LANES:(pi + 1) * LANES]
            z = jnp.where(lo, _dot(ws_ref[2 * pi], vnp), _dot(ws_ref[2 * pi + 1], vnp))
            sl = slice(pi * LANES, (pi + 1) * LANES)
            gm_s[pl.ds(r0, BLOCK), sl] = u_ref[pl.ds(r0, BLOCK), sl] * (z + bs_ref[:, sl])
        return carry

    lax.fori_loop(0, nb_tile, block_body, 0, unroll=2)

    gate1 = mod_ref[0, 2:3, :]
    for r in range(0, TM, OUT_ROWS):
        rows = slice(r, r + OUT_ROWS)
        na = _rms(attn_s[rows, :], ga_ref[...]).astype(BF16)
        ng = _rms(gm_s[rows, :], gg_ref[...]).astype(BF16)
        mo = _dot(na, wout_ref[0:D_ATTN, :]) + _dot(ng, wout_ref[D_ATTN:, :])
        x1 = x_ref[rows, :] + gate1 * _rms(mo, gpost_ref[...])
        x1_ref[rows, :] = x1
        h2 = (_rms(x1, gpre2_ref[...]) * (1.0 + mod_ref[0, 4:5, :]) + mod_ref[0, 3:4, :]).astype(BF16)
        h2_ref[rows, :] = h2
        lg = _dot_nt(wr_ref[...], h2)
        lg = lg - jnp.max(lg, axis=0, keepdims=True)
        ex = jnp.exp(lg)
        aff_ref[0, :, rows] = ex / jnp.sum(ex, axis=0, keepdims=True)


def _mix(x2, q, kv, u, vg, mod, sink, bias, ws, bs_full, ln_g, ln_b, g_attn, g_gmlp,
         w_out, g_post, g_pre2, w_r_t, b, s):
    t, d = x2.shape
    dg = u.shape[1]
    e = w_r_t.shape[0]
    nsb = s // TM
    nb_tile = TM // BLOCK
    nblk = t // BLOCK
    row = lambda i: (i, 0)
    const2 = lambda i: (0, 0)
    in_specs = [
        pl.BlockSpec(memory_space=pltpu.SMEM),
        pl.BlockSpec((TM, d), row),
        pl.BlockSpec((TM, D_ATTN), row),
        pl.BlockSpec((BLOCK, 4 * LANES), lambda i: (jnp.maximum(i * nb_tile - 1, 0), 0)),
        pl.BlockSpec((TM, 4 * LANES), row),
        pl.BlockSpec((BLOCK, 4 * LANES), lambda i: (jnp.minimum((i + 1) * nb_tile, nblk - 1), 0)),
        pl.BlockSpec((TM, dg), row),
        pl.BlockSpec((TM, dg), row),
        pl.BlockSpec((1, 6, d), lambda i: (i // nsb, 0, 0)),
        pl.BlockSpec(bias.shape, lambda i: (0, 0, 0, 0)),
        pl.BlockSpec(ws.shape, lambda i: (0, 0, 0)),
        pl.BlockSpec((BLOCK, dg), const2),
        pl.BlockSpec((1, dg), const2),
        pl.BlockSpec((1, dg), const2),
        pl.BlockSpec((1, D_ATTN), const2),
        pl.BlockSpec((1, dg), const2),
        pl.BlockSpec(w_out.shape, const2),
        pl.BlockSpec((1, d), const2),
        pl.BlockSpec((1, d), const2),
        pl.BlockSpec(w_r_t.shape, const2),
    ]
    out_specs = [
        pl.BlockSpec((TM, d), row),
        pl.BlockSpec((TM, d), row),
        pl.BlockSpec((1, e, TM), lambda i: (i // nsb, 0, i % nsb)),
    ]
    out_shape = [
        jax.ShapeDtypeStruct((t, d), F32),
        jax.ShapeDtypeStruct((t, d), BF16),
        jax.ShapeDtypeStruct((b, e, s), F32),
    ]
    return pl.pallas_call(
        functools.partial(_mix_kernel, s // BLOCK),
        grid=(t // TM,),
        in_specs=in_specs,
        out_specs=out_specs,
        out_shape=out_shape,
        scratch_shapes=[
            pltpu.VMEM((TM + 2 * BLOCK, 4 * LANES), BF16),
            pltpu.VMEM((TM, D_ATTN), F32),
            pltpu.VMEM((TM, dg), F32),
        ],
        compiler_params=pltpu.CompilerParams(
            dimension_semantics=("arbitrary",), vmem_limit_bytes=VMEM_LIMIT),
        name="mix",
    )(sink, x2, q, kv, kv, kv, u, vg, mod, bias, ws, bs_full, ln_g, ln_b, g_attn, g_gmlp,
      w_out, g_post, g_pre2, w_r_t)


def _route_kernel(cap, aff_ref, slot_ref, affb_ref, cnt_ref):
    aff = aff_ref[...]
    e, s = aff.shape
    capf = jnp.float32(cap)

    def count_ge(t):
        return jnp.sum(jnp.where(aff >= t, 1.0, 0.0), axis=1, keepdims=True)

    def bit_body(it, cand):
        trial = cand | jnp.left_shift(jnp.int32(1), 30 - it)
        t = lax.bitcast_convert_type(trial, F32)
        return jnp.where(count_ge(t) >= capf, trial, cand)

    cand = lax.fori_loop(0, 31, bit_body, jnp.zeros((e, 1), jnp.int32))
    thr = lax.bitcast_convert_type(cand, F32)
    need = capf - jnp.sum(jnp.where(aff > thr, 1.0, 0.0), axis=1, keepdims=True)

    r = lax.broadcasted_iota(jnp.int32, (LANES, LANES), 0)
    c = lax.broadcasted_iota(jnp.int32, (LANES, LANES), 1)
    tri = jnp.where(r < c, 1.0, 0.0).astype(BF16)
    lane = lax.broadcasted_iota(jnp.int32, (e, LANES), 1)
    carry_eq = jnp.zeros((e, 1), F32)
    carry_sel = jnp.zeros((e, 1), F32)
    cnt = jnp.zeros((e, LANES), F32)
    per_blk = MXU_DIM // LANES
    for j in range(s // LANES):
        a = aff[:, j * LANES:(j + 1) * LANES]
        gt = a > thr
        eq = jnp.where(a == thr, 1.0, 0.0)
        tie_rank = _dot(eq.astype(BF16), tri) + carry_eq
        sel = jnp.where(gt | ((eq > 0.0) & (tie_rank < need)), 1.0, 0.0)
        slot = _dot(sel.astype(BF16), tri) + carry_sel
        blk = (slice(None), j // per_blk, slice((j % per_blk) * LANES, (j % per_blk + 1) * LANES))
        slot_ref[blk] = jnp.where(sel > 0.0, slot, -1.0).astype(jnp.int32)
        affb_ref[blk] = a
        if j % per_blk == 0:
            cnt = jnp.where(lane == j // per_blk, carry_sel, cnt)
        carry_eq = carry_eq + jnp.sum(eq, axis=1, keepdims=True)
        carry_sel = carry_sel + jnp.sum(sel, axis=1, keepdims=True)
    cnt = jnp.where(lane >= s // MXU_DIM, carry_sel, cnt)
    cnt_ref[...] = cnt.astype(jnp.int32)


def _route(aff2, cap):
    r, s = aff2.shape
    nblk = s // MXU_DIM
    return pl.pallas_call(
        functools.partial(_route_kernel, cap),
        grid=(1,),
        in_specs=[pl.BlockSpec((r, s), lambda i: (0, 0))],
        out_specs=[
            pl.BlockSpec((r, nblk, MXU_DIM), lambda i: (0, 0, 0)),
            pl.BlockSpec((r, nblk, MXU_DIM), lambda i: (0, 0, 0)),
            pl.BlockSpec((r, LANES), lambda i: (0, 0)),
        ],
        out_shape=[
            jax.ShapeDtypeStruct((r, nblk, MXU_DIM), jnp.int32),
            jax.ShapeDtypeStruct((r, nblk, MXU_DIM), F32),
            jax.ShapeDtypeStruct((r, LANES), jnp.int32),
        ],
        compiler_params=pltpu.CompilerParams(dimension_semantics=("arbitrary",)),
        name="route",
    )(aff2)


def _segment(cnt_ref, row, pb, cap):
    c0 = jnp.minimum(cnt_ref[row, pb], cap)
    c1 = jnp.minimum(cnt_ref[row, pb + 1], cap)
    a0 = jnp.minimum(jnp.bitwise_and(c0, -SLOT_ALIGN), cap - WIN)
    return c0, c1, pl.multiple_of(a0, SLOT_ALIGN)


def _extra_windows(c1, a0):
    return (jnp.maximum(c1 - (a0 + WIN), 0) + WIN - 1) // WIN


def _gather_kernel(n_exp, cnt_ref, h2_ref, slot_ref, aff_ref, x_ref, gs_ref):
    b, eg = pl.program_id(0), pl.program_id(1)
    ng, cap = x_ref.shape[1], x_ref.shape[2]
    nblk = slot_ref.shape[2]
    sub = lax.broadcasted_iota(jnp.int32, (WIN, MXU_DIM), 0)
    sub_col = lax.broadcasted_iota(jnp.int32, (WIN, 1), 0)
    x_ref[...] = jnp.zeros_like(x_ref)
    gs_ref[...] = jnp.zeros_like(gs_ref)

    def merge(k, a0, first, last, rows, gate):
        slot_id = sub_col + a0
        keep = (slot_id >= first) & (slot_id < last)
        win = (0, k, pl.ds(a0, WIN), slice(None))
        x_ref[win] = jnp.where(keep, rows.astype(BF16), x_ref[win])
        gs_ref[win] = jnp.where(keep, gate, gs_ref[win])

    def tokens(pb):
        return pl.ds(pl.multiple_of(pb * MXU_DIM, MXU_DIM), MXU_DIM)

    def first_windows(pb):
        segs, gates, onehots = [], [], []
        overflow = jnp.int32(0)
        for k in range(ng):
            c0, c1, a0 = _segment(cnt_ref, b * n_exp + eg * ng + k, pb, cap)
            match = slot_ref[0, k, pl.ds(pb, 1), :] == (sub + a0)
            onehots.append(jnp.where(match, 1.0, 0.0).astype(BF16))
            gates.append(jnp.sum(jnp.where(match, aff_ref[0, k, pl.ds(pb, 1), :], 0.0),
                                 axis=1, keepdims=True))
            segs.append((c0, c1, a0))
            overflow = overflow + jnp.maximum(c1 - (a0 + WIN), 0)
        rows = _dot(jnp.concatenate(onehots, axis=0), h2_ref[0, tokens(pb), :])
        for k in range(ng):
            c0, c1, a0 = segs[k]
            merge(k, a0, c0, c1, rows[k * WIN:(k + 1) * WIN, :], gates[k])
        return overflow

    def further_windows(pb):
        def per_expert(k, carry2):
            _, c1, a0 = _segment(cnt_ref, b * n_exp + eg * ng + k, pb, cap)

            def per_window(w, carry3):
                first = a0 + (w + 1) * WIN
                aw = pl.multiple_of(jnp.minimum(first, cap - WIN), SLOT_ALIGN)
                srow = slot_ref[0, k, pl.ds(pb, 1), :]
                match = (srow == (sub + aw)) & (srow >= first)
                onehot = jnp.where(match, 1.0, 0.0).astype(BF16)
                gate = jnp.sum(jnp.where(match, aff_ref[0, k, pl.ds(pb, 1), :], 0.0),
                               axis=1, keepdims=True)
                merge(k, aw, first, c1, _dot(onehot, h2_ref[0, tokens(pb), :]), gate)
                return carry3

            return lax.fori_loop(0, _extra_windows(c1, a0), per_window, carry2)

        lax.fori_loop(0, ng, per_expert, 0)

    def pair_body(i, carry):
        overflows = [first_windows(2 * i + j) for j in range(2)]
        for j in range(2):
            pl.when(overflows[j] > 0)(functools.partial(further_windows, 2 * i + j))
        return carry

    lax.fori_loop(0, nblk // 2, pair_body, 0)


def _gather(cnt2, h2, slots4, aff4, cap):
    b, s, d = h2.shape
    e = slots4.shape[1]
    nblk = s // MXU_DIM
    grid_spec = pltpu.PrefetchScalarGridSpec(
        num_scalar_prefetch=1,
        grid=(b, e // EG),
        in_specs=[
            pl.BlockSpec((1, s, d), lambda bi, gi, c: (bi, 0, 0), pipeline_mode=pl.Buffered(1)),
            pl.BlockSpec((1, EG, nblk, MXU_DIM), lambda bi, gi, c: (bi, gi, 0, 0)),
            pl.BlockSpec((1, EG, nblk, MXU_DIM), lambda bi, gi, c: (bi, gi, 0, 0)),
        ],
        out_specs=[
            pl.BlockSpec((1, EG, cap, d), lambda bi, gi, c: (bi, gi, 0, 0)),
            pl.BlockSpec((1, EG, cap, LANES), lambda bi, gi, c: (bi, gi, 0, 0)),
        ],
    )
    return pl.pallas_call(
        functools.partial(_gather_kernel, e),
        grid_spec=grid_spec,
        out_shape=[
            jax.ShapeDtypeStruct((b, e, cap, d), BF16),
            jax.ShapeDtypeStruct((b, e, cap, LANES), F32),
        ],
        compiler_params=pltpu.CompilerParams(
            dimension_semantics=("arbitrary", "arbitrary"), vmem_limit_bytes=VMEM_LIMIT),
        name="gather",
    )(cnt2, h2, slots4, aff4)


def _ffn_kernel(x_ref, gs_ref, wg_ref, wu_ref, wd_ref, out_ref, acc_s):
    e, f = pl.program_id(0), pl.program_id(1)
    nb, _, cap, d = x_ref.shape

    @pl.when((e == 0) & (f == 0))
    def _init():
        acc_s[...] = jnp.zeros_like(acc_s)

    wg, wu, wd = wg_ref[0].astype(BF16), wu_ref[0].astype(BF16), wd_ref[0].astype(BF16)
    for bi in range(nb):
        x = x_ref[bi, 0]
        g = _dot(x, wg)
        u = _dot(x, wu)
        hid = (g * jax.nn.sigmoid(g) * u).astype(BF16)
        y = _dot(hid, wd)
        rows = slice(bi * cap, (bi + 1) * cap)
        acc_s[rows, :] = jnp.where(f == 0, y, acc_s[rows, :] + y)

    @pl.when(f == pl.num_programs(1) - 1)
    def _emit():
        gate = gs_ref[:, 0, :, 0:1].reshape(nb * cap, 1)
        out_ref[:, 0] = (acc_s[...] * gate).reshape(nb, cap, d).astype(BF16)


def _ffn(xg, gs, w_gate, w_up, w_down, e0):
    b, e, cap, d = xg.shape
    ff = w_gate.shape[2]
    return pl.pallas_call(
        _ffn_kernel,
        grid=(e, ff // FC),
        in_specs=[
            pl.BlockSpec((b, 1, cap, d), lambda ei, fi: (0, ei, 0, 0)),
            pl.BlockSpec((b, 1, cap, LANES), lambda ei, fi: (0, ei, 0, 0)),
            pl.BlockSpec((1, d, FC), lambda ei, fi: (e0 + ei, 0, fi)),
            pl.BlockSpec((1, d, FC), lambda ei, fi: (e0 + ei, 0, fi)),
            pl.BlockSpec((1, FC, d), lambda ei, fi: (e0 + ei, fi, 0)),
        ],
        out_specs=pl.BlockSpec((b, 1, cap, d), lambda ei, fi: (0, ei, 0, 0)),
        out_shape=jax.ShapeDtypeStruct((b, e, cap, d), BF16),
        scratch_shapes=[pltpu.VMEM((b * cap, d), F32)],
        compiler_params=pltpu.CompilerParams(
            dimension_semantics=("arbitrary", "arbitrary"), vmem_limit_bytes=VMEM_LIMIT),
        name="ffn",
    )(xg, gs, w_gate, w_up, w_down)


def _combine_kernel(n_exp, cnt_ref, eo_ref, slot_ref, x1_ref, mod_ref, g_ref, o_ref,
                    stage_s, oh_s, y_s):
    b, q = pl.program_id(0), pl.program_id(1)
    cap = eo_ref.shape[2]
    nblk_q = o_ref.shape[0] // MXU_DIM
    sub = lax.broadcasted_iota(jnp.int32, (WIN, MXU_DIM), 0)
    gate = mod_ref[0, 5:6, :]

    for j in range(nblk_q):
        pb = q * nblk_q + j
        overflow = jnp.int32(0)
        for e in range(n_exp):
            _, c1, a0 = _segment(cnt_ref, b * n_exp + e, pb, cap)
            stage_s[e * WIN:(e + 1) * WIN, :] = eo_ref[0, e, pl.ds(a0, WIN), :]
            match = slot_ref[0, e, pl.ds(pb, 1), :] == (sub + a0)
            oh_s[e * WIN:(e + 1) * WIN, :] = jnp.where(match, 1.0, 0.0).astype(BF16)
            overflow = overflow + jnp.maximum(c1 - (a0 + WIN), 0)
        y_s[...] = _dot_tn(oh_s[...], stage_s[...])

        @pl.when(overflow > 0)
        def _long_segments(j=j, pb=pb):
            def per_expert(e, carry):
                _, c1, a0 = _segment(cnt_ref, b * n_exp + e, pb, cap)

                def per_window(k, carry2):
                    first = a0 + (k + 1) * WIN
                    aw = pl.multiple_of(jnp.minimum(first, cap - WIN), SLOT_ALIGN)
                    srow = slot_ref[0, e, pl.ds(pb, 1), :]
                    match = (srow == (sub + aw)) & (srow >= first)
                    onehot = jnp.where(match, 1.0, 0.0).astype(BF16)
                    y_s[...] += _dot_tn(onehot, eo_ref[0, e, pl.ds(aw, WIN), :])
                    return carry2

                return lax.fori_loop(0, _extra_windows(c1, a0), per_window, carry)

            lax.fori_loop(0, n_exp, per_expert, 0)

        sl = slice(j * MXU_DIM, (j + 1) * MXU_DIM)
        o_ref[sl, :] = x1_ref[sl, :] + gate * _rms(y_s[...], g_ref[...])


def _combine(cnt2, eo, slots4, x1, mod, g_post):
    b, e, cap, d = eo.shape
    t = x1.shape[0]
    s = t // b
    nq = s // QP
    nblk_q = QP // MXU_DIM
    grid_spec = pltpu.PrefetchScalarGridSpec(
        num_scalar_prefetch=1,
        grid=(b, nq),
        in_specs=[
            pl.BlockSpec((1, e, cap, d), lambda bi, qi, c: (bi, 0, 0, 0), pipeline_mode=pl.Buffered(1)),
            pl.BlockSpec((1, e, s // MXU_DIM, MXU_DIM), lambda bi, qi, c: (bi, 0, 0, 0)),
            pl.BlockSpec((QP, d), lambda bi, qi, c: (bi * nq + qi, 0)),
            pl.BlockSpec((1, 6, d), lambda bi, qi, c: (bi, 0, 0)),
            pl.BlockSpec((1, d), lambda bi, qi, c: (0, 0)),
        ],
        out_specs=pl.BlockSpec((QP, d), lambda bi, qi, c: (bi * nq + qi, 0)),
        scratch_shapes=[
            pltpu.VMEM((e * WIN, d), BF16),
            pltpu.VMEM((e * WIN, MXU_DIM), BF16),
            pltpu.VMEM((MXU_DIM, d), F32),
        ],
    )
    return pl.pallas_call(
        functools.partial(_combine_kernel, e),
        grid_spec=grid_spec,
        out_shape=jax.ShapeDtypeStruct((t, d), F32),
        compiler_params=pltpu.CompilerParams(
            dimension_semantics=("arbitrary", "arbitrary"),
            vmem_limit_bytes=VMEM_LIMIT),
        name="combine",
    )(cnt2, eo, slots4, x1, mod, g_post)


def _layer(layer, x, c, w_ada_all, b_ada, norm_pre_mix, norm_post_mix, w_in, sink, sgu_ln_g,
           sgu_ln_b, w_s, b_s, norm_out_attn, norm_out_gmlp, w_out, norm_pre_ffn, norm_post_ffn,
           w_router, w_gate_all, w_up_all, w_down_all):
    b, s, d = x.shape
    t = b * s
    dg = sgu_ln_g.shape[0]
    e = w_router.shape[1]
    cap = CAPACITY_FACTOR * s // e
    assert s % QP == 0 and s % TM == 0 and w_gate_all.shape[2] % FC == 0 and e % EG == 0
    assert cap % SLOT_ALIGN == 0 and cap >= WIN and s // MXU_DIM < LANES and d == D_ATTN + dg
    assert (s // MXU_DIM) % 2 == 0

    mod = _adaln(c, w_ada_all, b_ada, layer).reshape(b, 6, d)

    x2 = x.reshape(t, d)
    q, kv, u, vg = _proj(x2, mod, norm_pre_mix.reshape(1, d), w_in.astype(BF16), s)

    bias = jnp.asarray(_attn_bias_table())
    bs_full = jnp.repeat(b_s.T, dg // N_GMLP_GROUPS, axis=1)
    x1, h2, aff_t = _mix(
        x2, q, kv, u, vg, mod, sink, bias, w_s.astype(BF16), bs_full,
        sgu_ln_g.reshape(1, dg), sgu_ln_b.reshape(1, dg), norm_out_attn.reshape(1, D_ATTN),
        norm_out_gmlp.reshape(1, dg), w_out.astype(BF16), norm_post_mix.reshape(1, d),
        norm_pre_ffn.reshape(1, d), w_router.T.astype(BF16), b, s)

    slots3, aff3, cnt2 = _route(aff_t.reshape(b * e, s), cap)
    nblk = s // MXU_DIM
    slots4 = slots3.reshape(b, e, nblk, MXU_DIM)
    aff4 = aff3.reshape(b, e, nblk, MXU_DIM)
    xg, gs = _gather(cnt2, h2.reshape(b, s, d), slots4, aff4, cap)
    eo = _ffn(xg, gs, w_gate_all, w_up_all, w_down_all, layer * e)
    out = _combine(cnt2, eo, slots4, x1, mod, norm_post_ffn.reshape(1, d))
    return out.reshape(b, s, d)


def kernel(x, c, w_ada, b_ada, norm_pre_mix, norm_post_mix, w_in, sink, sgu_ln_g, sgu_ln_b, w_s, b_s, norm_out_attn, norm_out_gmlp, w_out, norm_pre_ffn, norm_post_ffn, w_router, w_gate, w_up, w_down):
    depth, d = w_ada.shape[0], w_ada.shape[1]
    w_ada_all = w_ada.reshape(depth * d, w_ada.shape[2])
    stack = lambda w: w.reshape((depth * w.shape[1],) + w.shape[2:])
    w_gate_all, w_up_all, w_down_all = stack(w_gate), stack(w_up), stack(w_down)
    for l in range(depth):
        x = _layer(l, x, c, w_ada_all, b_ada[l], norm_pre_mix[l], norm_post_mix[l], w_in[l],
                   sink[l], sgu_ln_g[l], sgu_ln_b[l], w_s[l], b_s[l], norm_out_attn[l],
                   norm_out_gmlp[l], w_out[l], norm_pre_ffn[l], norm_post_ffn[l], w_router[l],
                   w_gate_all, w_up_all, w_down_all)
    return x
```

```python
import functools

import numpy as np
import jax
import jax.numpy as jnp
from jax import lax
from jax.experimental import pallas as pl
from jax.experimental.pallas import tpu as pltpu

N_HEADS = 8
N_KV_HEADS = 2
HEAD_DIM = 64
D_ATTN = N_HEADS * HEAD_DIM
D_KV = N_KV_HEADS * HEAD_DIM
GROUP = N_HEADS // N_KV_HEADS
WINDOW = 128
BLOCK = 128
N_GMLP_GROUPS = 8
N_EXPERTS = 16
CAPACITY_FACTOR = 2
EPS = 1e-6
MASK_VALUE = -1e30

LANES = 128
MXU_DIM = 256

TM = 1024
FC = 256
PROJ_ROWS = 256
OUT_ROWS = TM
EG = 4
QP = 1024
WIN = 64
SLOT_ALIGN = 16
VMEM_LIMIT = 56 * 1024 * 1024

BF16 = jnp.bfloat16
F32 = jnp.float32


def _rms(x, g):
    return x * lax.rsqrt(jnp.mean(x * x, axis=-1, keepdims=True) + EPS) * g


def _dot(a, b):
    return jnp.dot(a, b, preferred_element_type=F32)


def _dot_nt(a, b):
    return lax.dot_general(a, b, (((1,), (1,)), ((), ())), preferred_element_type=F32)


def _dot_tn(a, b):
    return lax.dot_general(a, b, (((0,), (0,)), ((), ())), preferred_element_type=F32)


def _adaln_kernel(c_ref, w_ref, b_ref, o_ref):
    c = c_ref[...]
    a = (c * jax.nn.sigmoid(c)).astype(BF16)
    o_ref[...] = _dot(a, w_ref[...].astype(BF16)) + b_ref[...]


def _adaln(c, w_ada_all, b_ada, layer):
    b, d = c.shape
    n = w_ada_all.shape[1]
    return pl.pallas_call(
        _adaln_kernel,
        grid=(n // d,),
        in_specs=[
            pl.BlockSpec((b, d), lambda j: (0, 0)),
            pl.BlockSpec((d, d), lambda j: (layer, j)),
            pl.BlockSpec((1, d), lambda j: (0, j)),
        ],
        out_specs=pl.BlockSpec((b, d), lambda j: (0, j)),
        out_shape=jax.ShapeDtypeStruct((b, n), F32),
        compiler_params=pltpu.CompilerParams(dimension_semantics=("arbitrary",)),
        name="adaln",
    )(c, w_ada_all, b_ada.reshape(1, n))


def _gelu_tanh(x):
    c = np.float32(np.sqrt(2.0 / np.pi))
    return x * (0.5 * (1.0 + jnp.tanh(c * (x + 0.044715 * (x * x * x)))))


def _proj_kernel(x_ref, mod_ref, g_ref, w_ref, q_ref, kv_ref, u_ref, vg_ref):
    shift = mod_ref[0, 0:1, :]
    scale = mod_ref[0, 1:2, :]
    n_att = D_ATTN + 2 * D_KV
    dg = u_ref.shape[1]
    lo = lax.broadcasted_iota(jnp.int32, (1, LANES), 1) < HEAD_DIM
    for r in range(0, TM, PROJ_ROWS):
        rows = slice(r, r + PROJ_ROWS)
        h = (_rms(x_ref[rows, :], g_ref[...]) * (1.0 + scale) + shift).astype(BF16)
        z = _gelu_tanh(_dot(h, w_ref[:, n_att:]))
        u_ref[rows, :] = z[:, :dg]
        vg_ref[rows, :] = z[:, dg:]
        p = _dot(h, w_ref[:, :n_att])
        q_ref[rows, :] = (p[:, :D_ATTN] * (HEAD_DIM ** -0.5)).astype(BF16)
        for j in range(2):
            t = p[:, D_ATTN + j * D_KV:D_ATTN + (j + 1) * D_KV]
            t_sw = pltpu.roll(t, HEAD_DIM, axis=1)
            kv_ref[rows, (2 * j) * LANES:(2 * j + 1) * LANES] = jnp.where(lo, t, t_sw).astype(BF16)
            kv_ref[rows, (2 * j + 1) * LANES:(2 * j + 2) * LANES] = jnp.where(lo, t_sw, t).astype(BF16)


def _proj(x2, mod, g_pre, w_in, s):
    t, d = x2.shape
    nw = w_in.shape[1]
    dg = (nw - D_ATTN - 2 * D_KV) // 2
    nsb = s // TM
    return pl.pallas_call(
        _proj_kernel,
        grid=(t // TM,),
        in_specs=[
            pl.BlockSpec((TM, d), lambda i: (i, 0)),
            pl.BlockSpec((1, 6, d), lambda i: (i // nsb, 0, 0)),
            pl.BlockSpec((1, d), lambda i: (0, 0)),
            pl.BlockSpec((d, nw), lambda i: (0, 0)),
        ],
        out_specs=[
            pl.BlockSpec((TM, D_ATTN), lambda i: (i, 0)),
            pl.BlockSpec((TM, D_ATTN), lambda i: (i, 0)),
            pl.BlockSpec((TM, dg), lambda i: (i, 0)),
            pl.BlockSpec((TM, dg), lambda i: (i, 0)),
        ],
        out_shape=[
            jax.ShapeDtypeStruct((t, D_ATTN), BF16),
            jax.ShapeDtypeStruct((t, D_ATTN), BF16),
            jax.ShapeDtypeStruct((t, dg), F32),
            jax.ShapeDtypeStruct((t, dg), F32),
        ],
        compiler_params=pltpu.CompilerParams(
            dimension_semantics=("arbitrary",), vmem_limit_bytes=VMEM_LIMIT),
        name="proj",
    )(x2, mod, g_pre, w_in)


def _attn_bias_table():
    qi = np.arange(BLOCK)[:, None]
    kj = np.arange(3 * BLOCK)[None, :] - BLOCK
    dist = np.abs(qi - kj).astype(np.float32)
    band = dist <= WINDOW
    slopes = np.exp2(-8.0 * np.arange(1, N_HEADS + 1, dtype=np.float32) / N_HEADS).astype(np.float32)
    key_ok = [kj >= 0, np.ones_like(kj, bool), kj < BLOCK]
    tab = np.empty((3, N_HEADS, BLOCK, 3 * BLOCK), np.float32)
    for v in range(3):
        ok = band & key_ok[v]
        for h in range(N_HEADS):
            tab[v, h] = np.where(ok, -slopes[h] * dist, np.float32(MASK_VALUE))
    return tab.reshape(3, N_KV_HEADS, GROUP * BLOCK, 3 * BLOCK)


def _mix_kernel(nblk_seq, sink_ref, x_ref, q_ref, kvp_ref, kvc_ref, kvn_ref, u_ref, vg_ref,
                mod_ref, bias_ref, ws_ref, bs_ref, lng_ref, lnb_ref, ga_ref, gg_ref,
                wout_ref, gpost_ref, gpre2_ref, wr_ref,
                x1_ref, h2_ref, aff_ref, kcat, attn_s, gm_s):
    i = pl.program_id(0)
    nb_tile = TM // BLOCK
    kcat[0:BLOCK, :] = kvp_ref[...]
    kcat[BLOCK:BLOCK + TM, :] = kvc_ref[...]
    kcat[BLOCK + TM:, :] = kvn_ref[...]
    lane = lax.broadcasted_iota(jnp.int32, (1, LANES), 1)
    lo = lane < HEAD_DIM
    zero = jnp.zeros((), BF16)
    head_of_row = lax.broadcasted_iota(jnp.int32, (GROUP * BLOCK, 1), 0) // BLOCK
    sinks = []
    for g in range(N_KV_HEADS):
        col = jnp.zeros((GROUP * BLOCK, 1), F32)
        for j in range(GROUP):
            col = jnp.where(head_of_row == j, sink_ref[g * GROUP + j], col)
        sinks.append(col)

    def block_body(a, carry):
        r0 = pl.multiple_of(a * BLOCK, BLOCK)
        pos = (i * nb_tile + a) % nblk_seq
        var = jnp.where(pos == 0, 0, jnp.where(pos == nblk_seq - 1, 2, 1))
        kv = kcat[pl.ds(r0, 3 * BLOCK), :]
        for g in range(N_KV_HEADS):
            k = kv[:, g * LANES:(g + 1) * LANES]
            v = kv[:, (N_KV_HEADS + g) * LANES:(N_KV_HEADS + g + 1) * LANES]
            pieces = []
            for pi in range(GROUP // 2):
                qp = q_ref[pl.ds(r0, BLOCK), (g * GROUP // 2 + pi) * LANES:(g * GROUP // 2 + pi + 1) * LANES]
                pieces += [jnp.where(lo, qp, zero), jnp.where(lo, zero, qp)]
            s = _dot_nt(jnp.concatenate(pieces, axis=0), k) + bias_ref[var, g]
            m = jnp.maximum(jnp.max(s, axis=-1, keepdims=True), sinks[g])
            p = jnp.exp(s - m)
            den = jnp.sum(p, axis=-1, keepdims=True) + jnp.exp(sinks[g] - m)
            o = _dot(p.astype(BF16), v) / den
            for pi in range(GROUP // 2):
                even = o[(2 * pi) * BLOCK:(2 * pi + 1) * BLOCK, :]
                odd = o[(2 * pi + 1) * BLOCK:(2 * pi + 2) * BLOCK, :]
                col0 = (g * GROUP // 2 + pi) * LANES
                attn_s[pl.ds(r0, BLOCK), col0:col0 + LANES] = jnp.where(lo, even, odd)
        vg = vg_ref[pl.ds(r0, BLOCK), :]
        mu = jnp.mean(vg, axis=-1, keepdims=True)
        xc = vg - mu
        vn = (xc * lax.rsqrt(jnp.mean(xc * xc, axis=-1, keepdims=True) + EPS) * lng_ref[...]
              + lnb_ref[...]).astype(BF16)
        for pi in range(N_GMLP_GROUPS // 2):
            vnp = vn[:, pi * LANES:(pi + 1) * LANES]
            z = jnp.where(lo, _dot(ws_ref[2 * pi], vnp), _dot(ws_ref[2 * pi + 1], vnp))
            sl = slice(pi * LANES, (pi + 1) * LANES)
            gm_s[pl.ds(r0, BLOCK), sl] = u_ref[pl.ds(r0, BLOCK), sl] * (z + bs_ref[:, sl])
        return carry

    lax.fori_loop(0, nb_tile, block_body, 0, unroll=2)

    gate1 = mod_ref[0, 2:3, :]
    for r in range(0, TM, OUT_ROWS):
        rows = slice(r, r + OUT_ROWS)
        na = _rms(attn_s[rows, :], ga_ref[...]).astype(BF16)
        ng = _rms(gm_s[rows, :], gg_ref[...]).astype(BF16)
        mo = _dot(na, wout_ref[0:D_ATTN, :]) + _dot(ng, wout_ref[D_ATTN:, :])
        x1 = x_ref[rows, :] + gate1 * _rms(mo, gpost_ref[...])
        x1_ref[rows, :] = x1
        h2 = (_rms(x1, gpre2_ref[...]) * (1.0 + mod_ref[0, 4:5, :]) + mod_ref[0, 3:4, :]).astype(BF16)
        h2_ref[rows, :] = h2
        lg = _dot_nt(wr_ref[...], h2)
        lg = lg - jnp.max(lg, axis=0, keepdims=True)
        ex = jnp.exp(lg)
        aff_ref[0, :, rows] = ex / jnp.sum(ex, axis=0, keepdims=True)


def _mix(x2, q, kv, u, vg, mod, sink, bias, ws, bs_full, ln_g, ln_b, g_attn, g_gmlp,
         w_out, g_post, g_pre2, w_r_t, b, s):
    t, d = x2.shape
    dg = u.shape[1]
    e = w_r_t.shape[0]
    nsb = s // TM
    nb_tile = TM // BLOCK
    nblk = t // BLOCK
    row = lambda i: (i, 0)
    const2 = lambda i: (0, 0)
    in_specs = [
        pl.BlockSpec(memory_space=pltpu.SMEM),
        pl.BlockSpec((TM, d), row),
        pl.BlockSpec((TM, D_ATTN), row),
        pl.BlockSpec((BLOCK, 4 * LANES), lambda i: (jnp.maximum(i * nb_tile - 1, 0), 0)),
        pl.BlockSpec((TM, 4 * LANES), row),
        pl.BlockSpec((BLOCK, 4 * LANES), lambda i: (jnp.minimum((i + 1) * nb_tile, nblk - 1), 0)),
        pl.BlockSpec((TM, dg), row),
        pl.BlockSpec((TM, dg), row),
        pl.BlockSpec((1, 6, d), lambda i: (i // nsb, 0, 0)),
        pl.BlockSpec(bias.shape, lambda i: (0, 0, 0, 0)),
        pl.BlockSpec(ws.shape, lambda i: (0, 0, 0)),
        pl.BlockSpec((BLOCK, dg), const2),
        pl.BlockSpec((1, dg), const2),
        pl.BlockSpec((1, dg), const2),
        pl.BlockSpec((1, D_ATTN), const2),
        pl.BlockSpec((1, dg), const2),
        pl.BlockSpec(w_out.shape, const2),
        pl.BlockSpec((1, d), const2),
        pl.BlockSpec((1, d), const2),
        pl.BlockSpec(w_r_t.shape, const2),
    ]
    out_specs = [
        pl.BlockSpec((TM, d), row),
        pl.BlockSpec((TM, d), row),
        pl.BlockSpec((1, e, TM), lambda i: (i // nsb, 0, i % nsb)),
    ]
    out_shape = [
        jax.ShapeDtypeStruct((t, d), F32),
        jax.ShapeDtypeStruct((t, d), BF16),
        jax.ShapeDtypeStruct((b, e, s), F32),
    ]
    return pl.pallas_call(
        functools.partial(_mix_kernel, s // BLOCK),
        grid=(t // TM,),
        in_specs=in_specs,
        out_specs=out_specs,
        out_shape=out_shape,
        scratch_shapes=[
            pltpu.VMEM((TM + 2 * BLOCK, 4 * LANES), BF16),
            pltpu.VMEM((TM, D_ATTN), F32),
            pltpu.VMEM((TM, dg), F32),
        ],
        compiler_params=pltpu.CompilerParams(
            dimension_semantics=("arbitrary",), vmem_limit_bytes=VMEM_LIMIT),
        name="mix",
    )(sink, x2, q, kv, kv, kv, u, vg, mod, bias, ws, bs_full, ln_g, ln_b, g_attn, g_gmlp,
      w_out, g_post, g_pre2, w_r_t)


def _route_kernel(cap, aff_ref, slot_ref, affb_ref, cnt_ref):
    aff = aff_ref[...]
    e, s = aff.shape
    capf = jnp.float32(cap)

    def count_ge(t):
        return jnp.sum(jnp.where(aff >= t, 1.0, 0.0), axis=1, keepdims=True)

    def bit_body(it, cand):
        trial = cand | jnp.left_shift(jnp.int32(1), 30 - it)
        t = lax.bitcast_convert_type(trial, F32)
        return jnp.where(count_ge(t) >= capf, trial, cand)

    cand = lax.fori_loop(0, 31, bit_body, jnp.zeros((e, 1), jnp.int32))
    thr = lax.bitcast_convert_type(cand, F32)
    need = capf - jnp.sum(jnp.where(aff > thr, 1.0, 0.0), axis=1, keepdims=True)

    r = lax.broadcasted_iota(jnp.int32, (LANES, LANES), 0)
    c = lax.broadcasted_iota(jnp.int32, (LANES, LANES), 1)
    tri = jnp.where(r < c, 1.0, 0.0).astype(BF16)
    lane = lax.broadcasted_iota(jnp.int32, (e, LANES), 1)
    carry_eq = jnp.zeros((e, 1), F32)
    carry_sel = jnp.zeros((e, 1), F32)
    cnt = jnp.zeros((e, LANES), F32)
    per_blk = MXU_DIM // LANES
    for j in range(s // LANES):
        a = aff[:, j * LANES:(j + 1) * LANES]
        gt = a > thr
        eq = jnp.where(a == thr, 1.0, 0.0)
        tie_rank = _dot(eq.astype(BF16), tri) + carry_eq
        sel = jnp.where(gt | ((eq > 0.0) & (tie_rank < need)), 1.0, 0.0)
        slot = _dot(sel.astype(BF16), tri) + carry_sel
        blk = (slice(None), j // per_blk, slice((j % per_blk) * LANES, (j % per_blk + 1) * LANES))
        slot_ref[blk] = jnp.where(sel > 0.0, slot, -1.0).astype(jnp.int32)
        affb_ref[blk] = a
        if j % per_blk == 0:
            cnt = jnp.where(lane == j // per_blk, carry_sel, cnt)
        carry_eq = carry_eq + jnp.sum(eq, axis=1, keepdims=True)
        carry_sel = carry_sel + jnp.sum(sel, axis=1, keepdims=True)
    cnt = jnp.where(lane >= s // MXU_DIM, carry_sel, cnt)
    cnt_ref[...] = cnt.astype(jnp.int32)


def _route(aff2, cap):
    r, s = aff2.shape
    nblk = s // MXU_DIM
    return pl.pallas_call(
        functools.partial(_route_kernel, cap),
        grid=(1,),
        in_specs=[pl.BlockSpec((r, s), lambda i: (0, 0))],
        out_specs=[
            pl.BlockSpec((r, nblk, MXU_DIM), lambda i: (0, 0, 0)),
            pl.BlockSpec((r, nblk, MXU_DIM), lambda i: (0, 0, 0)),
            pl.BlockSpec((r, LANES), lambda i: (0, 0)),
        ],
        out_shape=[
            jax.ShapeDtypeStruct((r, nblk, MXU_DIM), jnp.int32),
            jax.ShapeDtypeStruct((r, nblk, MXU_DIM), F32),
            jax.ShapeDtypeStruct((r, LANES), jnp.int32),
        ],
        compiler_params=pltpu.CompilerParams(dimension_semantics=("arbitrary",)),
        name="route",
    )(aff2)


def _segment(cnt_ref, row, pb, cap):
    c0 = jnp.minimum(cnt_ref[row, pb], cap)
    c1 = jnp.minimum(cnt_ref[row, pb + 1], cap)
    a0 = jnp.minimum(jnp.bitwise_and(c0, -SLOT_ALIGN), cap - WIN)
    return c0, c1, pl.multiple_of(a0, SLOT_ALIGN)


def _extra_windows(c1, a0):
    return (jnp.maximum(c1 - (a0 + WIN), 0) + WIN - 1) // WIN


def _gather_kernel(n_exp, cnt_ref, h2_ref, slot_ref, aff_ref, x_ref, gs_ref):
    b, eg = pl.program_id(0), pl.program_id(1)
    ng, cap = x_ref.shape[1], x_ref.shape[2]
    nblk = slot_ref.shape[2]
    sub = lax.broadcasted_iota(jnp.int32, (WIN, MXU_DIM), 0)
    x_ref[...] = jnp.zeros_like(x_ref)
    gs_ref[...] = jnp.zeros_like(gs_ref)

    def merge(k, a0, rows, gate):
        win = (0, k, pl.ds(a0, WIN), slice(None))
        x_ref[win] += rows.astype(BF16)
        gs_ref[win] += gate

    def tokens(pb):
        return pl.ds(pl.multiple_of(pb * MXU_DIM, MXU_DIM), MXU_DIM)

    def first_windows(pb):
        segs, gates, onehots = [], [], []
        overflow = jnp.int32(0)
        for k in range(ng):
            c0, c1, a0 = _segment(cnt_ref, b * n_exp + eg * ng + k, pb, cap)
            match = slot_ref[0, k, pl.ds(pb, 1), :] == (sub + a0)
            onehots.append(jnp.where(match, 1.0, 0.0).astype(BF16))
            gates.append(jnp.sum(jnp.where(match, aff_ref[0, k, pl.ds(pb, 1), :], 0.0),
                                 axis=1, keepdims=True))
            segs.append((c0, c1, a0))
            overflow = overflow + jnp.maximum(c1 - (a0 + WIN), 0)
        rows = _dot(jnp.concatenate(onehots, axis=0), h2_ref[0, tokens(pb), :])
        for k in range(ng):
            merge(k, segs[k][2], rows[k * WIN:(k + 1) * WIN, :], gates[k])
        return overflow

    def further_windows(pb):
        def per_expert(k, carry2):
            _, c1, a0 = _segment(cnt_ref, b * n_exp + eg * ng + k, pb, cap)

            def per_window(w, carry3):
                first = a0 + (w + 1) * WIN
                aw = pl.multiple_of(jnp.minimum(first, cap - WIN), SLOT_ALIGN)
                srow = slot_ref[0, k, pl.ds(pb, 1), :]
                match = (srow == (sub + aw)) & (srow >= first)
                onehot = jnp.where(match, 1.0, 0.0).astype(BF16)
                gate = jnp.sum(jnp.where(match, aff_ref[0, k, pl.ds(pb, 1), :], 0.0),
                               axis=1, keepdims=True)
                merge(k, aw, _dot(onehot, h2_ref[0, tokens(pb), :]), gate)
                return carry3

            return lax.fori_loop(0, _extra_windows(c1, a0), per_window, carry2)

        lax.fori_loop(0, ng, per_expert, 0)

    def pair_body(i, carry):
        overflows = [first_windows(2 * i + j) for j in range(2)]
        for j in range(2):
            pl.when(overflows[j] > 0)(functools.partial(further_windows, 2 * i + j))
        return carry

    lax.fori_loop(0, nblk // 2, pair_body, 0)


def _gather(cnt2, h2, slots4, aff4, cap):
    b, s, d = h2.shape
    e = slots4.shape[1]
    nblk = s // MXU_DIM
    grid_spec = pltpu.PrefetchScalarGridSpec(
        num_scalar_prefetch=1,
        grid=(b, e // EG),
        in_specs=[
            pl.BlockSpec((1, s, d), lambda bi, gi, c: (bi, 0, 0), pipeline_mode=pl.Buffered(1)),
            pl.BlockSpec((1, EG, nblk, MXU_DIM), lambda bi, gi, c: (bi, gi, 0, 0)),
            pl.BlockSpec((1, EG, nblk, MXU_DIM), lambda bi, gi, c: (bi, gi, 0, 0)),
        ],
        out_specs=[
            pl.BlockSpec((1, EG, cap, d), lambda bi, gi, c: (bi, gi, 0, 0)),
            pl.BlockSpec((1, EG, cap, LANES), lambda bi, gi, c: (bi, gi, 0, 0)),
        ],
    )
    return pl.pallas_call(
        functools.partial(_gather_kernel, e),
        grid_spec=grid_spec,
        out_shape=[
            jax.ShapeDtypeStruct((b, e, cap, d), BF16),
            jax.ShapeDtypeStruct((b, e, cap, LANES), F32),
        ],
        compiler_params=pltpu.CompilerParams(
            dimension_semantics=("arbitrary", "arbitrary"), vmem_limit_bytes=VMEM_LIMIT),
        name="gather",
    )(cnt2, h2, slots4, aff4)


def _ffn_kernel(x_ref, gs_ref, wg_ref, wu_ref, wd_ref, out_ref, acc_s):
    e, f = pl.program_id(0), pl.program_id(1)
    nb, _, cap, d = x_ref.shape

    @pl.when((e == 0) & (f == 0))
    def _init():
        acc_s[...] = jnp.zeros_like(acc_s)

    wg, wu, wd = wg_ref[0].astype(BF16), wu_ref[0].astype(BF16), wd_ref[0].astype(BF16)
    for bi in range(nb):
        x = x_ref[bi, 0]
        g = _dot(x, wg)
        u = _dot(x, wu)
        hid = (g * jax.nn.sigmoid(g) * u).astype(BF16)
        y = _dot(hid, wd)
        rows = slice(bi * cap, (bi + 1) * cap)
        acc_s[rows, :] = jnp.where(f == 0, y, acc_s[rows, :] + y)

    @pl.when(f == pl.num_programs(1) - 1)
    def _emit():
        gate = gs_ref[:, 0, :, 0:1].reshape(nb * cap, 1)
        out_ref[:, 0] = (acc_s[...] * gate).reshape(nb, cap, d).astype(BF16)


def _ffn(xg, gs, w_gate, w_up, w_down, e0):
    b, e, cap, d = xg.shape
    ff = w_gate.shape[2]
    return pl.pallas_call(
        _ffn_kernel,
        grid=(e, ff // FC),
        in_specs=[
            pl.BlockSpec((b, 1, cap, d), lambda ei, fi: (0, ei, 0, 0)),
            pl.BlockSpec((b, 1, cap, LANES), lambda ei, fi: (0, ei, 0, 0)),
            pl.BlockSpec((1, d, FC), lambda ei, fi: (e0 + ei, 0, fi)),
            pl.BlockSpec((1, d, FC), lambda ei, fi: (e0 + ei, 0, fi)),
            pl.BlockSpec((1, FC, d), lambda ei, fi: (e0 + ei, fi, 0)),
        ],
        out_specs=pl.BlockSpec((b, 1, cap, d), lambda ei, fi: (0, ei, 0, 0)),
        out_shape=jax.ShapeDtypeStruct((b, e, cap, d), BF16),
        scratch_shapes=[pltpu.VMEM((b * cap, d), F32)],
        compiler_params=pltpu.CompilerParams(
            dimension_semantics=("arbitrary", "arbitrary"), vmem_limit_bytes=VMEM_LIMIT),
        name="ffn",
    )(xg, gs, w_gate, w_up, w_down)


def _combine_kernel(n_exp, cnt_ref, eo_ref, slot_ref, x1_ref, mod_ref, g_ref, o_ref,
                    stage_s, oh_s, y_s):
    b, q = pl.program_id(0), pl.program_id(1)
    cap = eo_ref.shape[2]
    nblk_q = o_ref.shape[0] // MXU_DIM
    sub = lax.broadcasted_iota(jnp.int32, (WIN, MXU_DIM), 0)
    gate = mod_ref[0, 5:6, :]

    for j in range(nblk_q):
        pb = q * nblk_q + j
        overflow = jnp.int32(0)
        for e in range(n_exp):
            _, c1, a0 = _segment(cnt_ref, b * n_exp + e, pb, cap)
            stage_s[e * WIN:(e + 1) * WIN, :] = eo_ref[0, e, pl.ds(a0, WIN), :]
            match = slot_ref[0, e, pl.ds(pb, 1), :] == (sub + a0)
            oh_s[e * WIN:(e + 1) * WIN, :] = jnp.where(match, 1.0, 0.0).astype(BF16)
            overflow = overflow + jnp.maximum(c1 - (a0 + WIN), 0)
        y_s[...] = _dot_tn(oh_s[...], stage_s[...])

        @pl.when(overflow > 0)
        def _long_segments(j=j, pb=pb):
            def per_expert(e, carry):
                _, c1, a0 = _segment(cnt_ref, b * n_exp + e, pb, cap)

                def per_window(k, carry2):
                    first = a0 + (k + 1) * WIN
                    aw = pl.multiple_of(jnp.minimum(first, cap - WIN), SLOT_ALIGN)
                    srow = slot_ref[0, e, pl.ds(pb, 1), :]
                    match = (srow == (sub + aw)) & (srow >= first)
                    onehot = jnp.where(match, 1.0, 0.0).astype(BF16)
                    y_s[...] += _dot_tn(onehot, eo_ref[0, e, pl.ds(aw, WIN), :])
                    return carry2

                return lax.fori_loop(0, _extra_windows(c1, a0), per_window, carry)

            lax.fori_loop(0, n_exp, per_expert, 0)

        sl = slice(j * MXU_DIM, (j + 1) * MXU_DIM)
        o_ref[sl, :] = x1_ref[sl, :] + gate * _rms(y_s[...], g_ref[...])


def _combine(cnt2, eo, slots4, x1, mod, g_post):
    b, e, cap, d = eo.shape
    t = x1.shape[0]
    s = t // b
    nq = s // QP
    nblk_q = QP // MXU_DIM
    grid_spec = pltpu.PrefetchScalarGridSpec(
        num_scalar_prefetch=1,
        grid=(b, nq),
        in_specs=[
            pl.BlockSpec((1, e, cap, d), lambda bi, qi, c: (bi, 0, 0, 0), pipeline_mode=pl.Buffered(1)),
            pl.BlockSpec((1, e, s // MXU_DIM, MXU_DIM), lambda bi, qi, c: (bi, 0, 0, 0)),
            pl.BlockSpec((QP, d), lambda bi, qi, c: (bi * nq + qi, 0)),
            pl.BlockSpec((1, 6, d), lambda bi, qi, c: (bi, 0, 0)),
            pl.BlockSpec((1, d), lambda bi, qi, c: (0, 0)),
        ],
        out_specs=pl.BlockSpec((QP, d), lambda bi, qi, c: (bi * nq + qi, 0)),
        scratch_shapes=[
            pltpu.VMEM((e * WIN, d), BF16),
            pltpu.VMEM((e * WIN, MXU_DIM), BF16),
            pltpu.VMEM((MXU_DIM, d), F32),
        ],
    )
    return pl.pallas_call(
        functools.partial(_combine_kernel, e),
        grid_spec=grid_spec,
        out_shape=jax.ShapeDtypeStruct((t, d), F32),
        compiler_params=pltpu.CompilerParams(
            dimension_semantics=("arbitrary", "arbitrary"),
            vmem_limit_bytes=VMEM_LIMIT),
        name="combine",
    )(cnt2, eo, slots4, x1, mod, g_post)


def _layer(layer, x, c, w_ada_all, b_ada, norm_pre_mix, norm_post_mix, w_in, sink, sgu_ln_g,
           sgu_ln_b, w_s, b_s, norm_out_attn, norm_out_gmlp, w_out, norm_pre_ffn, norm_post_ffn,
           w_router, w_gate_all, w_up_all, w_down_all):
    b, s, d = x.shape
    t = b * s
    dg = sgu_ln_g.shape[0]
    e = w_router.shape[1]
    cap = CAPACITY_FACTOR * s // e
    assert s % QP == 0 and s % TM == 0 and w_gate_all.shape[2] % FC == 0 and e % EG == 0
    assert cap % SLOT_ALIGN == 0 and cap >= WIN and s // MXU_DIM < LANES and d == D_ATTN + dg
    assert (s // MXU_DIM) % 2 == 0

    mod = _adaln(c, w_ada_all, b_ada, layer).reshape(b, 6, d)

    x2 = x.reshape(t, d)
    q, kv, u, vg = _proj(x2, mod, norm_pre_mix.reshape(1, d), w_in.astype(BF16), s)

    bias = jnp.asarray(_attn_bias_table())
    bs_full = jnp.repeat(b_s.T, dg // N_GMLP_GROUPS, axis=1)
    x1, h2, aff_t = _mix(
        x2, q, kv, u, vg, mod, sink, bias, w_s.astype(BF16), bs_full,
        sgu_ln_g.reshape(1, dg), sgu_ln_b.reshape(1, dg), norm_out_attn.reshape(1, D_ATTN),
        norm_out_gmlp.reshape(1, dg), w_out.astype(BF16), norm_post_mix.reshape(1, d),
        norm_pre_ffn.reshape(1, d), w_router.T.astype(BF16), b, s)

    slots3, aff3, cnt2 = _route(aff_t.reshape(b * e, s), cap)
    nblk = s // MXU_DIM
    slots4 = slots3.reshape(b, e, nblk, MXU_DIM)
    aff4 = aff3.reshape(b, e, nblk, MXU_DIM)
    xg, gs = _gather(cnt2, h2.reshape(b, s, d), slots4, aff4, cap)
    eo = _ffn(xg, gs, w_gate_all, w_up_all, w_down_all, layer * e)
    out = _combine(cnt2, eo, slots4, x1, mod, norm_post_ffn.reshape(1, d))
    return out.reshape(b, s, d)


def kernel(x, c, w_ada, b_ada, norm_pre_mix, norm_post_mix, w_in, sink, sgu_ln_g, sgu_ln_b, w_s, b_s, norm_out_attn, norm_out_gmlp, w_out, norm_pre_ffn, norm_post_ffn, w_router, w_gate, w_up, w_down):
    depth, d = w_ada.shape[0], w_ada.shape[1]
    w_ada_all = w_ada.reshape(depth * d, w_ada.shape[2])
    stack = lambda w: w.reshape((depth * w.shape[1],) + w.shape[2:])
    w_gate_all, w_up_all, w_down_all = stack(w_gate), stack(w_up), stack(w_down)
    for l in range(depth):
        x = _layer(l, x, c, w_ada_all, b_ada[l], norm_pre_mix[l], norm_post_mix[l], w_in[l],
                   sink[l], sgu_ln_g[l], sgu_ln_b[l], w_s[l], b_s[l], norm_out_attn[l],
                   norm_out_gmlp[l], w_out[l], norm_pre_ffn[l], norm_post_ffn[l], w_router[l],
                   w_gate_all, w_up_all, w_down_all)
    return x
```

```python
import functools

import numpy as np
import jax
import jax.numpy as jnp
from jax import lax
from jax.experimental import pallas as pl
from jax.experimental.pallas import tpu as pltpu

N_HEADS = 8
N_KV_HEADS = 2
HEAD_DIM = 64
D_ATTN = N_HEADS * HEAD_DIM
D_KV = N_KV_HEADS * HEAD_DIM
GROUP = N_HEADS // N_KV_HEADS
WINDOW = 128
BLOCK = 128
N_GMLP_GROUPS = 8
N_EXPERTS = 16
CAPACITY_FACTOR = 2
EPS = 1e-6
MASK_VALUE = -1e30

LANES = 128
MXU_DIM = 256

TM = 1024
FC = 256
PROJ_ROWS = 256
EG = 4
QP = 1024
WIN = 64
SLOT_ALIGN = 16
VMEM_LIMIT = 56 * 1024 * 1024

BF16 = jnp.bfloat16
F32 = jnp.float32


def _rms(x, g):
    return x * lax.rsqrt(jnp.mean(x * x, axis=-1, keepdims=True) + EPS) * g


def _dot(a, b):
    return jnp.dot(a, b, preferred_element_type=F32)


def _dot_nt(a, b):
    return lax.dot_general(a, b, (((1,), (1,)), ((), ())), preferred_element_type=F32)


def _dot_tn(a, b):
    return lax.dot_general(a, b, (((0,), (0,)), ((), ())), preferred_element_type=F32)


def _adaln_kernel(c_ref, w_ref, b_ref, o_ref):
    c = c_ref[...]
    a = (c * jax.nn.sigmoid(c)).astype(BF16)
    o_ref[...] = _dot(a, w_ref[...].astype(BF16)) + b_ref[...]


def _adaln(c, w_ada_all, b_ada, layer):
    b, d = c.shape
    n = w_ada_all.shape[1]
    return pl.pallas_call(
        _adaln_kernel,
        grid=(n // d,),
        in_specs=[
            pl.BlockSpec((b, d), lambda j: (0, 0)),
            pl.BlockSpec((d, d), lambda j: (layer, j)),
            pl.BlockSpec((1, d), lambda j: (0, j)),
        ],
        out_specs=pl.BlockSpec((b, d), lambda j: (0, j)),
        out_shape=jax.ShapeDtypeStruct((b, n), F32),
        compiler_params=pltpu.CompilerParams(dimension_semantics=("arbitrary",)),
        name="adaln",
    )(c, w_ada_all, b_ada.reshape(1, n))


def _gelu_tanh(x):
    c = np.float32(np.sqrt(2.0 / np.pi))
    return x * (0.5 * (1.0 + jnp.tanh(c * (x + 0.044715 * (x * x * x)))))


def _proj_kernel(x_ref, mod_ref, g_ref, w_ref, q_ref, kv_ref, u_ref, vg_ref):
    shift = mod_ref[0, 0:1, :]
    gain = g_ref[...] * (1.0 + mod_ref[0, 1:2, :])
    n_att = D_ATTN + 2 * D_KV
    dg = u_ref.shape[1]
    lo = lax.broadcasted_iota(jnp.int32, (1, LANES), 1) < HEAD_DIM
    for r in range(0, TM, PROJ_ROWS):
        rows = slice(r, r + PROJ_ROWS)
        h = (_rms(x_ref[rows, :], gain) + shift).astype(BF16)
        z = _gelu_tanh(_dot(h, w_ref[:, n_att:]))
        u_ref[rows, :] = z[:, :dg]
        vg_ref[rows, :] = z[:, dg:]
        p = _dot(h, w_ref[:, :n_att])
        q_ref[rows, :] = (p[:, :D_ATTN] * (HEAD_DIM ** -0.5)).astype(BF16)
        for j in range(2):
            t = p[:, D_ATTN + j * D_KV:D_ATTN + (j + 1) * D_KV]
            t_sw = pltpu.roll(t, HEAD_DIM, axis=1)
            kv_ref[rows, (2 * j) * LANES:(2 * j + 1) * LANES] = jnp.where(lo, t, t_sw).astype(BF16)
            kv_ref[rows, (2 * j + 1) * LANES:(2 * j + 2) * LANES] = jnp.where(lo, t_sw, t).astype(BF16)


def _proj(x2, mod, g_pre, w_in, s):
    t, d = x2.shape
    nw = w_in.shape[1]
    dg = (nw - D_ATTN - 2 * D_KV) // 2
    nsb = s // TM
    return pl.pallas_call(
        _proj_kernel,
        grid=(t // TM,),
        in_specs=[
            pl.BlockSpec((TM, d), lambda i: (i, 0)),
            pl.BlockSpec((1, 6, d), lambda i: (i // nsb, 0, 0)),
            pl.BlockSpec((1, d), lambda i: (0, 0)),
            pl.BlockSpec((d, nw), lambda i: (0, 0)),
        ],
        out_specs=[
            pl.BlockSpec((TM, D_ATTN), lambda i: (i, 0)),
            pl.BlockSpec((TM, D_ATTN), lambda i: (i, 0)),
            pl.BlockSpec((TM, dg), lambda i: (i, 0)),
            pl.BlockSpec((TM, dg), lambda i: (i, 0)),
        ],
        out_shape=[
            jax.ShapeDtypeStruct((t, D_ATTN), BF16),
            jax.ShapeDtypeStruct((t, D_ATTN), BF16),
            jax.ShapeDtypeStruct((t, dg), F32),
            jax.ShapeDtypeStruct((t, dg), F32),
        ],
        compiler_params=pltpu.CompilerParams(
            dimension_semantics=("arbitrary",), vmem_limit_bytes=VMEM_LIMIT),
        name="proj",
    )(x2, mod, g_pre, w_in)


def _attn_bias_table():
    qi = np.arange(BLOCK)[:, None]
    kj = np.arange(3 * BLOCK)[None, :] - BLOCK
    dist = np.abs(qi - kj).astype(np.float32)
    band = dist <= WINDOW
    slopes = np.exp2(-8.0 * np.arange(1, N_HEADS + 1, dtype=np.float32) / N_HEADS).astype(np.float32)
    key_ok = [kj >= 0, np.ones_like(kj, bool), kj < BLOCK]
    tab = np.empty((3, N_HEADS, BLOCK, 3 * BLOCK), np.float32)
    for v in range(3):
        ok = band & key_ok[v]
        for h in range(N_HEADS):
            tab[v, h] = np.where(ok, -slopes[h] * dist, np.float32(MASK_VALUE))
    return tab.reshape(3, N_KV_HEADS, GROUP * BLOCK, 3 * BLOCK)


def _mix_kernel(nblk_seq, sink_ref, x_ref, q_ref, kvp_ref, kvc_ref, kvn_ref, u_ref, vg_ref,
                mod_ref, bias_ref, ws_ref, bs_ref, lng_ref, lnb_ref, ga_ref, gg_ref,
                wout_ref, gpost_ref, gpre2_ref, wr_ref,
                x1_ref, h2_ref, aff_ref, kcat, attn_s, gm_s):
    i = pl.program_id(0)
    nb_tile = TM // BLOCK
    kcat[0:BLOCK, :] = kvp_ref[...]
    kcat[BLOCK:BLOCK + TM, :] = kvc_ref[...]
    kcat[BLOCK + TM:, :] = kvn_ref[...]
    lane = lax.broadcasted_iota(jnp.int32, (1, LANES), 1)
    lo = lane < HEAD_DIM
    zero = jnp.zeros((), BF16)
    head_of_row = lax.broadcasted_iota(jnp.int32, (GROUP * BLOCK, 1), 0) // BLOCK
    sinks = []
    for g in range(N_KV_HEADS):
        col = jnp.zeros((GROUP * BLOCK, 1), F32)
        for j in range(GROUP):
            col = jnp.where(head_of_row == j, sink_ref[g * GROUP + j], col)
        sinks.append(col)

    def block_body(a, carry):
        r0 = pl.multiple_of(a * BLOCK, BLOCK)
        pos = (i * nb_tile + a) % nblk_seq
        var = jnp.where(pos == 0, 0, jnp.where(pos == nblk_seq - 1, 2, 1))
        kv = kcat[pl.ds(r0, 3 * BLOCK), :]
        for g in range(N_KV_HEADS):
            k = kv[:, g * LANES:(g + 1) * LANES]
            v = kv[:, (N_KV_HEADS + g) * LANES:(N_KV_HEADS + g + 1) * LANES]
            pieces = []
            for pi in range(GROUP // 2):
                qp = q_ref[pl.ds(r0, BLOCK), (g * GROUP // 2 + pi) * LANES:(g * GROUP // 2 + pi + 1) * LANES]
                pieces += [jnp.where(lo, qp, zero), jnp.where(lo, zero, qp)]
            s = _dot_nt(jnp.concatenate(pieces, axis=0), k) + bias_ref[var, g]
            m = jnp.maximum(jnp.max(s, axis=-1, keepdims=True), sinks[g])
            p = jnp.exp(s - m)
            den = jnp.sum(p, axis=-1, keepdims=True) + jnp.exp(sinks[g] - m)
            o = _dot(p.astype(BF16), v) / den
            for pi in range(GROUP // 2):
                even = o[(2 * pi) * BLOCK:(2 * pi + 1) * BLOCK, :]
                odd = o[(2 * pi + 1) * BLOCK:(2 * pi + 2) * BLOCK, :]
                col0 = (g * GROUP // 2 + pi) * LANES
                attn_s[pl.ds(r0, BLOCK), col0:col0 + LANES] = jnp.where(lo, even, odd)
        vg = vg_ref[pl.ds(r0, BLOCK), :]
        mu = jnp.mean(vg, axis=-1, keepdims=True)
        xc = vg - mu
        vn = (xc * lax.rsqrt(jnp.mean(xc * xc, axis=-1, keepdims=True) + EPS) * lng_ref[...]
              + lnb_ref[...]).astype(BF16)
        for pi in range(N_GMLP_GROUPS // 2):
            vnp = vn[:, pi * LANES:(pi + 1) * LANES]
            z = jnp.where(lo, _dot(ws_ref[2 * pi], vnp), _dot(ws_ref[2 * pi + 1], vnp))
            sl = slice(pi * LANES, (pi + 1) * LANES)
            gm_s[pl.ds(r0, BLOCK), sl] = u_ref[pl.ds(r0, BLOCK), sl] * (z + bs_ref[:, sl])
        return carry

    lax.fori_loop(0, nb_tile, block_body, 0, unroll=2)

    na = _rms(attn_s[...], ga_ref[...]).astype(BF16)
    ng = _rms(gm_s[...], gg_ref[...]).astype(BF16)
    mo = _dot(na, wout_ref[0:D_ATTN, :]) + _dot(ng, wout_ref[D_ATTN:, :])
    x1 = x_ref[...] + _rms(mo, gpost_ref[...] * mod_ref[0, 2:3, :])
    x1_ref[...] = x1
    h2 = (_rms(x1, gpre2_ref[...] * (1.0 + mod_ref[0, 4:5, :])) + mod_ref[0, 3:4, :]).astype(BF16)
    h2_ref[...] = h2
    lg = _dot_nt(wr_ref[...], h2)
    lg = lg - jnp.max(lg, axis=0, keepdims=True)
    ex = jnp.exp(lg)
    aff_ref[0] = ex / jnp.sum(ex, axis=0, keepdims=True)


def _mix(x2, q, kv, u, vg, mod, sink, bias, ws, bs_full, ln_g, ln_b, g_attn, g_gmlp,
         w_out, g_post, g_pre2, w_r_t, b, s):
    t, d = x2.shape
    dg = u.shape[1]
    e = w_r_t.shape[0]
    nsb = s // TM
    nb_tile = TM // BLOCK
    nblk = t // BLOCK
    row = lambda i: (i, 0)
    const2 = lambda i: (0, 0)
    in_specs = [
        pl.BlockSpec(memory_space=pltpu.SMEM),
        pl.BlockSpec((TM, d), row),
        pl.BlockSpec((TM, D_ATTN), row),
        pl.BlockSpec((BLOCK, 4 * LANES), lambda i: (jnp.maximum(i * nb_tile - 1, 0), 0)),
        pl.BlockSpec((TM, 4 * LANES), row),
        pl.BlockSpec((BLOCK, 4 * LANES), lambda i: (jnp.minimum((i + 1) * nb_tile, nblk - 1), 0)),
        pl.BlockSpec((TM, dg), row),
        pl.BlockSpec((TM, dg), row),
        pl.BlockSpec((1, 6, d), lambda i: (i // nsb, 0, 0)),
        pl.BlockSpec(bias.shape, lambda i: (0, 0, 0, 0)),
        pl.BlockSpec(ws.shape, lambda i: (0, 0, 0)),
        pl.BlockSpec((BLOCK, dg), const2),
        pl.BlockSpec((1, dg), const2),
        pl.BlockSpec((1, dg), const2),
        pl.BlockSpec((1, D_ATTN), const2),
        pl.BlockSpec((1, dg), const2),
        pl.BlockSpec(w_out.shape, const2),
        pl.BlockSpec((1, d), const2),
        pl.BlockSpec((1, d), const2),
        pl.BlockSpec(w_r_t.shape, const2),
    ]
    out_specs = [
        pl.BlockSpec((TM, d), row),
        pl.BlockSpec((TM, d), row),
        pl.BlockSpec((1, e, TM), lambda i: (i // nsb, 0, i % nsb)),
    ]
    out_shape = [
        jax.ShapeDtypeStruct((t, d), F32),
        jax.ShapeDtypeStruct((t, d), BF16),
        jax.ShapeDtypeStruct((b, e, s), F32),
    ]
    return pl.pallas_call(
        functools.partial(_mix_kernel, s // BLOCK),
        grid=(t // TM,),
        in_specs=in_specs,
        out_specs=out_specs,
        out_shape=out_shape,
        scratch_shapes=[
            pltpu.VMEM((TM + 2 * BLOCK, 4 * LANES), BF16),
            pltpu.VMEM((TM, D_ATTN), F32),
            pltpu.VMEM((TM, dg), F32),
        ],
        compiler_params=pltpu.CompilerParams(
            dimension_semantics=("arbitrary",), vmem_limit_bytes=VMEM_LIMIT),
        name="mix",
    )(sink, x2, q, kv, kv, kv, u, vg, mod, bias, ws, bs_full, ln_g, ln_b, g_attn, g_gmlp,
      w_out, g_post, g_pre2, w_r_t)


def _route_kernel(cap, aff_ref, slot_ref, affb_ref, cnt_ref):
    aff = aff_ref[...]
    e, s = aff.shape
    capf = jnp.float32(cap)

    def count_ge(t):
        return jnp.sum(jnp.where(aff >= t, 1.0, 0.0), axis=1, keepdims=True)

    def bit_body(it, cand):
        trial = cand | jnp.left_shift(jnp.int32(1), 30 - it)
        t = lax.bitcast_convert_type(trial, F32)
        return jnp.where(count_ge(t) >= capf, trial, cand)

    cand = lax.fori_loop(0, 31, bit_body, jnp.zeros((e, 1), jnp.int32))
    thr = lax.bitcast_convert_type(cand, F32)
    need = capf - jnp.sum(jnp.where(aff > thr, 1.0, 0.0), axis=1, keepdims=True)

    r = lax.broadcasted_iota(jnp.int32, (LANES, LANES), 0)
    c = lax.broadcasted_iota(jnp.int32, (LANES, LANES), 1)
    tri = jnp.where(r < c, 1.0, 0.0).astype(BF16)
    lane = lax.broadcasted_iota(jnp.int32, (e, LANES), 1)
    carry_eq = jnp.zeros((e, 1), F32)
    carry_sel = jnp.zeros((e, 1), F32)
    cnt = jnp.zeros((e, LANES), F32)
    per_blk = MXU_DIM // LANES
    for j in range(s // LANES):
        a = aff[:, j * LANES:(j + 1) * LANES]
        gt = a > thr
        eq = jnp.where(a == thr, 1.0, 0.0)
        tie_rank = _dot(eq.astype(BF16), tri) + carry_eq
        sel = jnp.where(gt | ((eq > 0.0) & (tie_rank < need)), 1.0, 0.0)
        slot = _dot(sel.astype(BF16), tri) + carry_sel
        blk = (slice(None), j // per_blk, slice((j % per_blk) * LANES, (j % per_blk + 1) * LANES))
        slot_ref[blk] = jnp.where(sel > 0.0, slot, -1.0).astype(jnp.int32)
        affb_ref[blk] = a
        if j % per_blk == 0:
            cnt = jnp.where(lane == j // per_blk, carry_sel, cnt)
        carry_eq = carry_eq + jnp.sum(eq, axis=1, keepdims=True)
        carry_sel = carry_sel + jnp.sum(sel, axis=1, keepdims=True)
    cnt = jnp.where(lane >= s // MXU_DIM, carry_sel, cnt)
    cnt_ref[...] = cnt.astype(jnp.int32)


def _route(aff2, cap):
    r, s = aff2.shape
    nblk = s // MXU_DIM
    return pl.pallas_call(
        functools.partial(_route_kernel, cap),
        grid=(1,),
        in_specs=[pl.BlockSpec((r, s), lambda i: (0, 0))],
        out_specs=[
            pl.BlockSpec((r, nblk, MXU_DIM), lambda i: (0, 0, 0)),
            pl.BlockSpec((r, nblk, MXU_DIM), lambda i: (0, 0, 0)),
            pl.BlockSpec((r, LANES), lambda i: (0, 0)),
        ],
        out_shape=[
            jax.ShapeDtypeStruct((r, nblk, MXU_DIM), jnp.int32),
            jax.ShapeDtypeStruct((r, nblk, MXU_DIM), F32),
            jax.ShapeDtypeStruct((r, LANES), jnp.int32),
        ],
        compiler_params=pltpu.CompilerParams(dimension_semantics=("arbitrary",)),
        name="route",
    )(aff2)


def _segment(cnt_ref, row, pb, cap):
    c0 = jnp.minimum(cnt_ref[row, pb], cap)
    c1 = jnp.minimum(cnt_ref[row, pb + 1], cap)
    a0 = jnp.minimum(jnp.bitwise_and(c0, -SLOT_ALIGN), cap - WIN)
    return c0, c1, pl.multiple_of(a0, SLOT_ALIGN)


def _extra_windows(c1, a0):
    return (jnp.maximum(c1 - (a0 + WIN), 0) + WIN - 1) // WIN


def _gather_kernel(n_exp, cnt_ref, h2_ref, slot_ref, aff_ref, x_ref, gs_ref):
    b, eg = pl.program_id(0), pl.program_id(1)
    ng, cap = x_ref.shape[1], x_ref.shape[2]
    nblk = slot_ref.shape[2]
    sub = lax.broadcasted_iota(jnp.int32, (WIN, MXU_DIM), 0)
    x_ref[...] = jnp.zeros_like(x_ref)
    gs_ref[...] = jnp.zeros_like(gs_ref)

    def merge(k, a0, rows, gate):
        win = (0, k, pl.ds(a0, WIN), slice(None))
        x_ref[win] += rows.astype(BF16)
        gs_ref[win] += gate

    def tokens(pb):
        return pl.ds(pl.multiple_of(pb * MXU_DIM, MXU_DIM), MXU_DIM)

    def first_windows(pb):
        segs, gates, onehots = [], [], []
        overflow = jnp.int32(0)
        for k in range(ng):
            c0, c1, a0 = _segment(cnt_ref, b * n_exp + eg * ng + k, pb, cap)
            match = slot_ref[0, k, pl.ds(pb, 1), :] == (sub + a0)
            onehots.append(jnp.where(match, 1.0, 0.0).astype(BF16))
            gates.append(jnp.sum(jnp.where(match, aff_ref[0, k, pl.ds(pb, 1), :], 0.0),
                                 axis=1, keepdims=True))
            segs.append((c0, c1, a0))
            overflow = overflow + jnp.maximum(c1 - (a0 + WIN), 0)
        rows = _dot(jnp.concatenate(onehots, axis=0), h2_ref[0, tokens(pb), :])
        for k in range(ng):
            merge(k, segs[k][2], rows[k * WIN:(k + 1) * WIN, :], gates[k])
        return overflow

    def further_windows(pb):
        def per_expert(k, carry2):
            _, c1, a0 = _segment(cnt_ref, b * n_exp + eg * ng + k, pb, cap)

            def per_window(w, carry3):
                first = a0 + (w + 1) * WIN
                aw = pl.multiple_of(jnp.minimum(first, cap - WIN), SLOT_ALIGN)
                srow = slot_ref[0, k, pl.ds(pb, 1), :]
                match = (srow == (sub + aw)) & (srow >= first)
                onehot = jnp.where(match, 1.0, 0.0).astype(BF16)
                gate = jnp.sum(jnp.where(match, aff_ref[0, k, pl.ds(pb, 1), :], 0.0),
                               axis=1, keepdims=True)
                merge(k, aw, _dot(onehot, h2_ref[0, tokens(pb), :]), gate)
                return carry3

            return lax.fori_loop(0, _extra_windows(c1, a0), per_window, carry2)

        lax.fori_loop(0, ng, per_expert, 0)

    def pair_body(i, carry):
        overflows = [first_windows(2 * i + j) for j in range(2)]
        for j in range(2):
            pl.when(overflows[j] > 0)(functools.partial(further_windows, 2 * i + j))
        return carry

    lax.fori_loop(0, nblk // 2, pair_body, 0)


def _gather(cnt2, h2, slots4, aff4, cap):
    b, s, d = h2.shape
    e = slots4.shape[1]
    nblk = s // MXU_DIM
    grid_spec = pltpu.PrefetchScalarGridSpec(
        num_scalar_prefetch=1,
        grid=(b, e // EG),
        in_specs=[
            pl.BlockSpec((1, s, d), lambda bi, gi, c: (bi, 0, 0)),
            pl.BlockSpec((1, EG, nblk, MXU_DIM), lambda bi, gi, c: (bi, gi, 0, 0)),
            pl.BlockSpec((1, EG, nblk, MXU_DIM), lambda bi, gi, c: (bi, gi, 0, 0)),
        ],
        out_specs=[
            pl.BlockSpec((1, EG, cap, d), lambda bi, gi, c: (bi, gi, 0, 0)),
            pl.BlockSpec((1, EG, cap, LANES), lambda bi, gi, c: (bi, gi, 0, 0)),
        ],
    )
    return pl.pallas_call(
        functools.partial(_gather_kernel, e),
        grid_spec=grid_spec,
        out_shape=[
            jax.ShapeDtypeStruct((b, e, cap, d), BF16),
            jax.ShapeDtypeStruct((b, e, cap, LANES), F32),
        ],
        compiler_params=pltpu.CompilerParams(
            dimension_semantics=("arbitrary", "arbitrary"), vmem_limit_bytes=VMEM_LIMIT),
        name="gather",
    )(cnt2, h2, slots4, aff4)


def _ffn_kernel(x_ref, gs_ref, wg_ref, wu_ref, wd_ref, out_ref, acc_s):
    e, f = pl.program_id(0), pl.program_id(1)
    nb, _, cap, d = x_ref.shape

    @pl.when((e == 0) & (f == 0))
    def _init():
        acc_s[...] = jnp.zeros_like(acc_s)

    wg, wu, wd = wg_ref[0].astype(BF16), wu_ref[0].astype(BF16), wd_ref[0].astype(BF16)
    for bi in range(nb):
        x = x_ref[bi, 0]
        g = _dot(x, wg)
        u = _dot(x, wu)
        hid = (g * jax.nn.sigmoid(g) * u).astype(BF16)
        y = _dot(hid, wd)
        rows = slice(bi * cap, (bi + 1) * cap)
        acc_s[rows, :] = jnp.where(f == 0, y, acc_s[rows, :] + y)

    @pl.when(f == pl.num_programs(1) - 1)
    def _emit():
        gate = gs_ref[:, 0, :, 0:1].reshape(nb * cap, 1)
        out_ref[:, 0] = (acc_s[...] * gate).reshape(nb, cap, d).astype(BF16)


def _ffn(xg, gs, w_gate, w_up, w_down, e0):
    b, e, cap, d = xg.shape
    ff = w_gate.shape[2]
    return pl.pallas_call(
        _ffn_kernel,
        grid=(e, ff // FC),
        in_specs=[
            pl.BlockSpec((b, 1, cap, d), lambda ei, fi: (0, ei, 0, 0)),
            pl.BlockSpec((b, 1, cap, LANES), lambda ei, fi: (0, ei, 0, 0)),
            pl.BlockSpec((1, d, FC), lambda ei, fi: (e0 + ei, 0, fi)),
            pl.BlockSpec((1, d, FC), lambda ei, fi: (e0 + ei, 0, fi)),
            pl.BlockSpec((1, FC, d), lambda ei, fi: (e0 + ei, fi, 0)),
        ],
        out_specs=pl.BlockSpec((b, 1, cap, d), lambda ei, fi: (0, ei, 0, 0)),
        out_shape=jax.ShapeDtypeStruct((b, e, cap, d), BF16),
        scratch_shapes=[pltpu.VMEM((b * cap, d), F32)],
        compiler_params=pltpu.CompilerParams(
            dimension_semantics=("arbitrary", "arbitrary"), vmem_limit_bytes=VMEM_LIMIT),
        name="ffn",
    )(xg, gs, w_gate, w_up, w_down)


def _combine_kernel(n_exp, cnt_ref, eo_ref, slot_ref, x1_ref, mod_ref, g_ref, o_ref,
                    stage_s, oh_s, y_s):
    b, q = pl.program_id(0), pl.program_id(1)
    cap = eo_ref.shape[2]
    nblk_q = o_ref.shape[0] // MXU_DIM
    sub = lax.broadcasted_iota(jnp.int32, (WIN, MXU_DIM), 0)
    gain = g_ref[...] * mod_ref[0, 5:6, :]

    for j in range(nblk_q):
        pb = q * nblk_q + j
        overflow = jnp.int32(0)
        for e in range(n_exp):
            _, c1, a0 = _segment(cnt_ref, b * n_exp + e, pb, cap)
            stage_s[e * WIN:(e + 1) * WIN, :] = eo_ref[0, e, pl.ds(a0, WIN), :]
            match = slot_ref[0, e, pl.ds(pb, 1), :] == (sub + a0)
            oh_s[e * WIN:(e + 1) * WIN, :] = jnp.where(match, 1.0, 0.0).astype(BF16)
            overflow = overflow + jnp.maximum(c1 - (a0 + WIN), 0)
        y_s[...] = _dot_tn(oh_s[...], stage_s[...])

        @pl.when(overflow > 0)
        def _long_segments(j=j, pb=pb):
            def per_expert(e, carry):
                _, c1, a0 = _segment(cnt_ref, b * n_exp + e, pb, cap)

                def per_window(k, carry2):
                    first = a0 + (k + 1) * WIN
                    aw = pl.multiple_of(jnp.minimum(first, cap - WIN), SLOT_ALIGN)
                    srow = slot_ref[0, e, pl.ds(pb, 1), :]
                    match = (srow == (sub + aw)) & (srow >= first)
                    onehot = jnp.where(match, 1.0, 0.0).astype(BF16)
                    y_s[...] += _dot_tn(onehot, eo_ref[0, e, pl.ds(aw, WIN), :])
                    return carry2

                return lax.fori_loop(0, _extra_windows(c1, a0), per_window, carry)

            lax.fori_loop(0, n_exp, per_expert, 0)

        sl = slice(j * MXU_DIM, (j + 1) * MXU_DIM)
        o_ref[sl, :] = x1_ref[sl, :] + _rms(y_s[...], gain)


def _combine(cnt2, eo, slots4, x1, mod, g_post):
    b, e, cap, d = eo.shape
    t = x1.shape[0]
    s = t // b
    nq = s // QP
    nblk_q = QP // MXU_DIM
    grid_spec = pltpu.PrefetchScalarGridSpec(
        num_scalar_prefetch=1,
        grid=(b, nq),
        in_specs=[
            pl.BlockSpec((1, e, cap, d), lambda bi, qi, c: (bi, 0, 0, 0), pipeline_mode=pl.Buffered(1)),
            pl.BlockSpec((1, e, s // MXU_DIM, MXU_DIM), lambda bi, qi, c: (bi, 0, 0, 0)),
            pl.BlockSpec((QP, d), lambda bi, qi, c: (bi * nq + qi, 0)),
            pl.BlockSpec((1, 6, d), lambda bi, qi, c: (bi, 0, 0)),
            pl.BlockSpec((1, d), lambda bi, qi, c: (0, 0)),
        ],
        out_specs=pl.BlockSpec((QP, d), lambda bi, qi, c: (bi * nq + qi, 0)),
        scratch_shapes=[
            pltpu.VMEM((e * WIN, d), BF16),
            pltpu.VMEM((e * WIN, MXU_DIM), BF16),
            pltpu.VMEM((MXU_DIM, d), F32),
        ],
    )
    return pl.pallas_call(
        functools.partial(_combine_kernel, e),
        grid_spec=grid_spec,
        out_shape=jax.ShapeDtypeStruct((t, d), F32),
        compiler_params=pltpu.CompilerParams(
            dimension_semantics=("arbitrary", "arbitrary"),
            vmem_limit_bytes=VMEM_LIMIT),
        name="combine",
    )(cnt2, eo, slots4, x1, mod, g_post)


def _layer(layer, x, c, w_ada_all, b_ada, norm_pre_mix, norm_post_mix, w_in, sink, sgu_ln_g,
           sgu_ln_b, w_s, b_s, norm_out_attn, norm_out_gmlp, w_out, norm_pre_ffn, norm_post_ffn,
           w_router, w_gate_all, w_up_all, w_down_all):
    b, s, d = x.shape
    t = b * s
    dg = sgu_ln_g.shape[0]
    e = w_router.shape[1]
    cap = CAPACITY_FACTOR * s // e
    assert s % QP == 0 and s % TM == 0 and w_gate_all.shape[2] % FC == 0 and e % EG == 0
    assert cap % SLOT_ALIGN == 0 and cap >= WIN and s // MXU_DIM < LANES and d == D_ATTN + dg
    assert (s // MXU_DIM) % 2 == 0

    mod = _adaln(c, w_ada_all, b_ada, layer).reshape(b, 6, d)

    x2 = x.reshape(t, d)
    q, kv, u, vg = _proj(x2, mod, norm_pre_mix.reshape(1, d), w_in.astype(BF16), s)

    bias = jnp.asarray(_attn_bias_table())
    bs_full = jnp.repeat(b_s.T, dg // N_GMLP_GROUPS, axis=1)
    x1, h2, aff_t = _mix(
        x2, q, kv, u, vg, mod, sink, bias, w_s.astype(BF16), bs_full,
        sgu_ln_g.reshape(1, dg), sgu_ln_b.reshape(1, dg), norm_out_attn.reshape(1, D_ATTN),
        norm_out_gmlp.reshape(1, dg), w_out.astype(BF16), norm_post_mix.reshape(1, d),
        norm_pre_ffn.reshape(1, d), w_router.T.astype(BF16), b, s)

    slots3, aff3, cnt2 = _route(aff_t.reshape(b * e, s), cap)
    nblk = s // MXU_DIM
    slots4 = slots3.reshape(b, e, nblk, MXU_DIM)
    aff4 = aff3.reshape(b, e, nblk, MXU_DIM)
    xg, gs = _gather(cnt2, h2.reshape(b, s, d), slots4, aff4, cap)
    eo = _ffn(xg, gs, w_gate_all, w_up_all, w_down_all, layer * e)
    out = _combine(cnt2, eo, slots4, x1, mod, norm_post_ffn.reshape(1, d))
    return out.reshape(b, s, d)


def kernel(x, c, w_ada, b_ada, norm_pre_mix, norm_post_mix, w_in, sink, sgu_ln_g, sgu_ln_b, w_s, b_s, norm_out_attn, norm_out_gmlp, w_out, norm_pre_ffn, norm_post_ffn, w_router, w_gate, w_up, w_down):
    depth, d = w_ada.shape[0], w_ada.shape[1]
    w_ada_all = w_ada.reshape(depth * d, w_ada.shape[2])
    stack = lambda w: w.reshape((depth * w.shape[1],) + w.shape[2:])
    w_gate_all, w_up_all, w_down_all = stack(w_gate), stack(w_up), stack(w_down)
    for l in range(depth):
        x = _layer(l, x, c, w_ada_all, b_ada[l], norm_pre_mix[l], norm_post_mix[l], w_in[l],
                   sink[l], sgu_ln_g[l], sgu_ln_b[l], w_s[l], b_s[l], norm_out_attn[l],
                   norm_out_gmlp[l], w_out[l], norm_pre_ffn[l], norm_post_ffn[l], w_router[l],
                   w_gate_all, w_up_all, w_down_all)
    return x
```

```python
import functools

import numpy as np
import jax
import jax.numpy as jnp
from jax import lax
from jax.experimental import pallas as pl
from jax.experimental.pallas import tpu as pltpu

N_HEADS = 8
N_KV_HEADS = 2
HEAD_DIM = 64
D_ATTN = N_HEADS * HEAD_DIM
D_KV = N_KV_HEADS * HEAD_DIM
GROUP = N_HEADS // N_KV_HEADS
WINDOW = 128
BLOCK = 128
N_GMLP_GROUPS = 8
N_EXPERTS = 16
CAPACITY_FACTOR = 2
EPS = 1e-6
MASK_VALUE = -1e30

LANES = 128
MXU_DIM = 256

TM = 1024
PROJ_TM = 2048
FC = 256
PROJ_ROWS = 256
EG = 4
QP = 1024
WIN = 64
SLOT_ALIGN = 16
VMEM_LIMIT = 56 * 1024 * 1024

BF16 = jnp.bfloat16
F32 = jnp.float32


def _rms(x, g):
    return x * lax.rsqrt(jnp.mean(x * x, axis=-1, keepdims=True) + EPS) * g


def _dot(a, b):
    return jnp.dot(a, b, preferred_element_type=F32)


def _dot_nt(a, b):
    return lax.dot_general(a, b, (((1,), (1,)), ((), ())), preferred_element_type=F32)


def _dot_tn(a, b):
    return lax.dot_general(a, b, (((0,), (0,)), ((), ())), preferred_element_type=F32)


def _adaln_kernel(c_ref, w_ref, b_ref, o_ref):
    c = c_ref[...]
    a = (c * jax.nn.sigmoid(c)).astype(BF16)
    o_ref[...] = _dot(a, w_ref[...].astype(BF16)) + b_ref[...]


def _adaln(c, w_ada_all, b_ada, layer):
    b, d = c.shape
    n = w_ada_all.shape[1]
    return pl.pallas_call(
        _adaln_kernel,
        grid=(n // d,),
        in_specs=[
            pl.BlockSpec((b, d), lambda j: (0, 0)),
            pl.BlockSpec((d, d), lambda j: (layer, j)),
            pl.BlockSpec((1, d), lambda j: (0, j)),
        ],
        out_specs=pl.BlockSpec((b, d), lambda j: (0, j)),
        out_shape=jax.ShapeDtypeStruct((b, n), F32),
        compiler_params=pltpu.CompilerParams(dimension_semantics=("arbitrary",)),
        name="adaln",
    )(c, w_ada_all, b_ada.reshape(1, n))


def _gelu_tanh(x):
    c = np.float32(np.sqrt(2.0 / np.pi))
    return x * (0.5 * (1.0 + jnp.tanh(c * (x + 0.044715 * (x * x * x)))))


def _proj_kernel(x_ref, mod_ref, g_ref, w_ref, q_ref, kv_ref, u_ref, vg_ref):
    shift = mod_ref[0, 0:1, :]
    gain = g_ref[...] * (1.0 + mod_ref[0, 1:2, :])
    n_att = D_ATTN + 2 * D_KV
    dg = u_ref.shape[1]
    lo = lax.broadcasted_iota(jnp.int32, (1, LANES), 1) < HEAD_DIM
    for r in range(0, PROJ_TM, PROJ_ROWS):
        rows = slice(r, r + PROJ_ROWS)
        h = (_rms(x_ref[rows, :], gain) + shift).astype(BF16)
        z = _gelu_tanh(_dot(h, w_ref[:, n_att:]))
        u_ref[rows, :] = z[:, :dg]
        vg_ref[rows, :] = z[:, dg:]
        p = _dot(h, w_ref[:, :n_att])
        q_ref[rows, :] = (p[:, :D_ATTN] * (HEAD_DIM ** -0.5)).astype(BF16)
        for j in range(2):
            t = p[:, D_ATTN + j * D_KV:D_ATTN + (j + 1) * D_KV]
            t_sw = pltpu.roll(t, HEAD_DIM, axis=1)
            kv_ref[rows, (2 * j) * LANES:(2 * j + 1) * LANES] = jnp.where(lo, t, t_sw).astype(BF16)
            kv_ref[rows, (2 * j + 1) * LANES:(2 * j + 2) * LANES] = jnp.where(lo, t_sw, t).astype(BF16)


def _proj(x2, mod, g_pre, w_in, s):
    t, d = x2.shape
    nw = w_in.shape[1]
    dg = (nw - D_ATTN - 2 * D_KV) // 2
    nsb = s // PROJ_TM
    return pl.pallas_call(
        _proj_kernel,
        grid=(t // PROJ_TM,),
        in_specs=[
            pl.BlockSpec((PROJ_TM, d), lambda i: (i, 0)),
            pl.BlockSpec((1, 6, d), lambda i: (i // nsb, 0, 0)),
            pl.BlockSpec((1, d), lambda i: (0, 0)),
            pl.BlockSpec((d, nw), lambda i: (0, 0)),
        ],
        out_specs=[
            pl.BlockSpec((PROJ_TM, D_ATTN), lambda i: (i, 0)),
            pl.BlockSpec((PROJ_TM, D_ATTN), lambda i: (i, 0)),
            pl.BlockSpec((PROJ_TM, dg), lambda i: (i, 0)),
            pl.BlockSpec((PROJ_TM, dg), lambda i: (i, 0)),
        ],
        out_shape=[
            jax.ShapeDtypeStruct((t, D_ATTN), BF16),
            jax.ShapeDtypeStruct((t, D_ATTN), BF16),
            jax.ShapeDtypeStruct((t, dg), F32),
            jax.ShapeDtypeStruct((t, dg), F32),
        ],
        compiler_params=pltpu.CompilerParams(
            dimension_semantics=("arbitrary",), vmem_limit_bytes=VMEM_LIMIT),
        name="proj",
    )(x2, mod, g_pre, w_in)


def _attn_bias_table():
    qi = np.arange(BLOCK)[:, None]
    kj = np.arange(3 * BLOCK)[None, :] - BLOCK
    dist = np.abs(qi - kj).astype(np.float32)
    band = dist <= WINDOW
    slopes = np.exp2(-8.0 * np.arange(1, N_HEADS + 1, dtype=np.float32) / N_HEADS).astype(np.float32)
    key_ok = [kj >= 0, np.ones_like(kj, bool), kj < BLOCK]
    tab = np.empty((3, N_HEADS, BLOCK, 3 * BLOCK), np.float32)
    for v in range(3):
        ok = band & key_ok[v]
        for h in range(N_HEADS):
            tab[v, h] = np.where(ok, -slopes[h] * dist, np.float32(MASK_VALUE))
    return tab.reshape(3, N_KV_HEADS, GROUP * BLOCK, 3 * BLOCK)


def _mix_kernel(nblk_seq, sink_ref, x_ref, q_ref, kvp_ref, kvc_ref, kvn_ref, u_ref, vg_ref,
                mod_ref, bias_ref, ws_ref, bs_ref, lng_ref, lnb_ref, ga_ref, gg_ref,
                wout_ref, gpost_ref, gpre2_ref, wr_ref,
                x1_ref, h2_ref, aff_ref, kcat, attn_s, gm_s):
    i = pl.program_id(0)
    nb_tile = TM // BLOCK
    kcat[0:BLOCK, :] = kvp_ref[...]
    kcat[BLOCK:BLOCK + TM, :] = kvc_ref[...]
    kcat[BLOCK + TM:, :] = kvn_ref[...]
    lane = lax.broadcasted_iota(jnp.int32, (1, LANES), 1)
    lo = lane < HEAD_DIM
    zero = jnp.zeros((), BF16)
    head_of_row = lax.broadcasted_iota(jnp.int32, (GROUP * BLOCK, 1), 0) // BLOCK
    sinks = []
    for g in range(N_KV_HEADS):
        col = jnp.zeros((GROUP * BLOCK, 1), F32)
        for j in range(GROUP):
            col = jnp.where(head_of_row == j, sink_ref[g * GROUP + j], col)
        sinks.append(col)

    def block_body(a, carry):
        r0 = pl.multiple_of(a * BLOCK, BLOCK)
        pos = (i * nb_tile + a) % nblk_seq
        var = jnp.where(pos == 0, 0, jnp.where(pos == nblk_seq - 1, 2, 1))
        kv = kcat[pl.ds(r0, 3 * BLOCK), :]
        for g in range(N_KV_HEADS):
            k = kv[:, g * LANES:(g + 1) * LANES]
            v = kv[:, (N_KV_HEADS + g) * LANES:(N_KV_HEADS + g + 1) * LANES]
            pieces = []
            for pi in range(GROUP // 2):
                qp = q_ref[pl.ds(r0, BLOCK), (g * GROUP // 2 + pi) * LANES:(g * GROUP // 2 + pi + 1) * LANES]
                pieces += [jnp.where(lo, qp, zero), jnp.where(lo, zero, qp)]
            s = _dot_nt(jnp.concatenate(pieces, axis=0), k) + bias_ref[var, g]
            m = jnp.maximum(jnp.max(s, axis=-1, keepdims=True), sinks[g])
            p = jnp.exp(s - m)
            den = jnp.sum(p, axis=-1, keepdims=True) + jnp.exp(sinks[g] - m)
            o = _dot(p.astype(BF16), v) / den
            for pi in range(GROUP // 2):
                even = o[(2 * pi) * BLOCK:(2 * pi + 1) * BLOCK, :]
                odd = o[(2 * pi + 1) * BLOCK:(2 * pi + 2) * BLOCK, :]
                col0 = (g * GROUP // 2 + pi) * LANES
                attn_s[pl.ds(r0, BLOCK), col0:col0 + LANES] = jnp.where(lo, even, odd)
        vg = vg_ref[pl.ds(r0, BLOCK), :]
        mu = jnp.mean(vg, axis=-1, keepdims=True)
        xc = vg - mu
        vn = (xc * lax.rsqrt(jnp.mean(xc * xc, axis=-1, keepdims=True) + EPS) * lng_ref[...]
              + lnb_ref[...]).astype(BF16)
        for pi in range(N_GMLP_GROUPS // 2):
            vnp = vn[:, pi * LANES:(pi + 1) * LANES]
            z = jnp.where(lo, _dot(ws_ref[2 * pi], vnp), _dot(ws_ref[2 * pi + 1], vnp))
            sl = slice(pi * LANES, (pi + 1) * LANES)
            gm_s[pl.ds(r0, BLOCK), sl] = u_ref[pl.ds(r0, BLOCK), sl] * (z + bs_ref[:, sl])
        return carry

    lax.fori_loop(0, nb_tile, block_body, 0, unroll=2)

    na = _rms(attn_s[...], ga_ref[...]).astype(BF16)
    ng = _rms(gm_s[...], gg_ref[...]).astype(BF16)
    mo = _dot(na, wout_ref[0:D_ATTN, :]) + _dot(ng, wout_ref[D_ATTN:, :])
    x1 = x_ref[...] + _rms(mo, gpost_ref[...] * mod_ref[0, 2:3, :])
    x1_ref[...] = x1
    h2 = (_rms(x1, gpre2_ref[...] * (1.0 + mod_ref[0, 4:5, :])) + mod_ref[0, 3:4, :]).astype(BF16)
    h2_ref[...] = h2
    lg = _dot_nt(wr_ref[...], h2)
    lg = lg - jnp.max(lg, axis=0, keepdims=True)
    ex = jnp.exp(lg)
    aff_ref[0] = ex / jnp.sum(ex, axis=0, keepdims=True)


def _mix(x2, q, kv, u, vg, mod, sink, bias, ws, bs_full, ln_g, ln_b, g_attn, g_gmlp,
         w_out, g_post, g_pre2, w_r_t, b, s):
    t, d = x2.shape
    dg = u.shape[1]
    e = w_r_t.shape[0]
    nsb = s // TM
    nb_tile = TM // BLOCK
    nblk = t // BLOCK
    row = lambda i: (i, 0)
    const2 = lambda i: (0, 0)
    in_specs = [
        pl.BlockSpec(memory_space=pltpu.SMEM),
        pl.BlockSpec((TM, d), row),
        pl.BlockSpec((TM, D_ATTN), row),
        pl.BlockSpec((BLOCK, 4 * LANES), lambda i: (jnp.maximum(i * nb_tile - 1, 0), 0)),
        pl.BlockSpec((TM, 4 * LANES), row),
        pl.BlockSpec((BLOCK, 4 * LANES), lambda i: (jnp.minimum((i + 1) * nb_tile, nblk - 1), 0)),
        pl.BlockSpec((TM, dg), row),
        pl.BlockSpec((TM, dg), row),
        pl.BlockSpec((1, 6, d), lambda i: (i // nsb, 0, 0)),
        pl.BlockSpec(bias.shape, lambda i: (0, 0, 0, 0)),
        pl.BlockSpec(ws.shape, lambda i: (0, 0, 0)),
        pl.BlockSpec((BLOCK, dg), const2),
        pl.BlockSpec((1, dg), const2),
        pl.BlockSpec((1, dg), const2),
        pl.BlockSpec((1, D_ATTN), const2),
        pl.BlockSpec((1, dg), const2),
        pl.BlockSpec(w_out.shape, const2),
        pl.BlockSpec((1, d), const2),
        pl.BlockSpec((1, d), const2),
        pl.BlockSpec(w_r_t.shape, const2),
    ]
    out_specs = [
        pl.BlockSpec((TM, d), row),
        pl.BlockSpec((TM, d), row),
        pl.BlockSpec((1, e, TM), lambda i: (i // nsb, 0, i % nsb)),
    ]
    out_shape = [
        jax.ShapeDtypeStruct((t, d), F32),
        jax.ShapeDtypeStruct((t, d), BF16),
        jax.ShapeDtypeStruct((b, e, s), F32),
    ]
    return pl.pallas_call(
        functools.partial(_mix_kernel, s // BLOCK),
        grid=(t // TM,),
        in_specs=in_specs,
        out_specs=out_specs,
        out_shape=out_shape,
        scratch_shapes=[
            pltpu.VMEM((TM + 2 * BLOCK, 4 * LANES), BF16),
            pltpu.VMEM((TM, D_ATTN), F32),
            pltpu.VMEM((TM, dg), F32),
        ],
        compiler_params=pltpu.CompilerParams(
            dimension_semantics=("arbitrary",), vmem_limit_bytes=VMEM_LIMIT),
        name="mix",
    )(sink, x2, q, kv, kv, kv, u, vg, mod, bias, ws, bs_full, ln_g, ln_b, g_attn, g_gmlp,
      w_out, g_post, g_pre2, w_r_t)


def _route_kernel(cap, aff_ref, slot_ref, affb_ref, cnt_ref):
    aff = aff_ref[...]
    e, s = aff.shape
    capf = jnp.float32(cap)

    def count_ge(t):
        return jnp.sum(jnp.where(aff >= t, 1.0, 0.0), axis=1, keepdims=True)

    def bit_body(it, cand):
        trial = cand | jnp.left_shift(jnp.int32(1), 30 - it)
        t = lax.bitcast_convert_type(trial, F32)
        return jnp.where(count_ge(t) >= capf, trial, cand)

    cand = lax.fori_loop(0, 31, bit_body, jnp.zeros((e, 1), jnp.int32))
    thr = lax.bitcast_convert_type(cand, F32)
    need = capf - jnp.sum(jnp.where(aff > thr, 1.0, 0.0), axis=1, keepdims=True)

    r = lax.broadcasted_iota(jnp.int32, (LANES, LANES), 0)
    c = lax.broadcasted_iota(jnp.int32, (LANES, LANES), 1)
    tri = jnp.where(r < c, 1.0, 0.0).astype(BF16)
    lane = lax.broadcasted_iota(jnp.int32, (e, LANES), 1)
    carry_eq = jnp.zeros((e, 1), F32)
    carry_sel = jnp.zeros((e, 1), F32)
    cnt = jnp.zeros((e, LANES), F32)
    per_blk = MXU_DIM // LANES
    for j in range(s // LANES):
        a = aff[:, j * LANES:(j + 1) * LANES]
        gt = a > thr
        eq = jnp.where(a == thr, 1.0, 0.0)
        tie_rank = _dot(eq.astype(BF16), tri) + carry_eq
        sel = jnp.where(gt | ((eq > 0.0) & (tie_rank < need)), 1.0, 0.0)
        slot = _dot(sel.astype(BF16), tri) + carry_sel
        blk = (slice(None), j // per_blk, slice((j % per_blk) * LANES, (j % per_blk + 1) * LANES))
        slot_ref[blk] = jnp.where(sel > 0.0, slot, -1.0).astype(jnp.int32)
        affb_ref[blk] = a
        if j % per_blk == 0:
            cnt = jnp.where(lane == j // per_blk, carry_sel, cnt)
        carry_eq = carry_eq + jnp.sum(eq, axis=1, keepdims=True)
        carry_sel = carry_sel + jnp.sum(sel, axis=1, keepdims=True)
    cnt = jnp.where(lane >= s // MXU_DIM, carry_sel, cnt)
    cnt_ref[...] = cnt.astype(jnp.int32)


def _route(aff2, cap):
    r, s = aff2.shape
    nblk = s // MXU_DIM
    return pl.pallas_call(
        functools.partial(_route_kernel, cap),
        grid=(1,),
        in_specs=[pl.BlockSpec((r, s), lambda i: (0, 0))],
        out_specs=[
            pl.BlockSpec((r, nblk, MXU_DIM), lambda i: (0, 0, 0)),
            pl.BlockSpec((r, nblk, MXU_DIM), lambda i: (0, 0, 0)),
            pl.BlockSpec((r, LANES), lambda i: (0, 0)),
        ],
        out_shape=[
            jax.ShapeDtypeStruct((r, nblk, MXU_DIM), jnp.int32),
            jax.ShapeDtypeStruct((r, nblk, MXU_DIM), F32),
            jax.ShapeDtypeStruct((r, LANES), jnp.int32),
        ],
        compiler_params=pltpu.CompilerParams(dimension_semantics=("arbitrary",)),
        name="route",
    )(aff2)


def _segment(cnt_ref, row, pb, cap):
    c0 = jnp.minimum(cnt_ref[row, pb], cap)
    c1 = jnp.minimum(cnt_ref[row, pb + 1], cap)
    a0 = jnp.minimum(jnp.bitwise_and(c0, -SLOT_ALIGN), cap - WIN)
    return c0, c1, pl.multiple_of(a0, SLOT_ALIGN)


def _extra_windows(c1, a0):
    return (jnp.maximum(c1 - (a0 + WIN), 0) + WIN - 1) // WIN


def _gather_kernel(n_exp, cnt_ref, h2_ref, slot_ref, aff_ref, x_ref, gs_ref):
    b, eg = pl.program_id(0), pl.program_id(1)
    ng, cap = x_ref.shape[1], x_ref.shape[2]
    nblk = slot_ref.shape[2]
    sub = lax.broadcasted_iota(jnp.int32, (WIN, MXU_DIM), 0)
    x_ref[...] = jnp.zeros_like(x_ref)
    gs_ref[...] = jnp.zeros_like(gs_ref)

    def merge(k, a0, rows, gate):
        win = (0, k, pl.ds(a0, WIN), slice(None))
        x_ref[win] += rows.astype(BF16)
        gs_ref[win] += gate

    def tokens(pb):
        return pl.ds(pl.multiple_of(pb * MXU_DIM, MXU_DIM), MXU_DIM)

    def first_windows(pb):
        segs, gates, onehots = [], [], []
        overflow = jnp.int32(0)
        for k in range(ng):
            c0, c1, a0 = _segment(cnt_ref, b * n_exp + eg * ng + k, pb, cap)
            match = slot_ref[0, k, pl.ds(pb, 1), :] == (sub + a0)
            onehots.append(jnp.where(match, 1.0, 0.0).astype(BF16))
            gates.append(jnp.sum(jnp.where(match, aff_ref[0, k, pl.ds(pb, 1), :], 0.0),
                                 axis=1, keepdims=True))
            segs.append((c0, c1, a0))
            overflow = overflow + jnp.maximum(c1 - (a0 + WIN), 0)
        rows = _dot(jnp.concatenate(onehots, axis=0), h2_ref[0, tokens(pb), :])
        for k in range(ng):
            merge(k, segs[k][2], rows[k * WIN:(k + 1) * WIN, :], gates[k])
        return overflow

    def further_windows(pb):
        def per_expert(k, carry2):
            _, c1, a0 = _segment(cnt_ref, b * n_exp + eg * ng + k, pb, cap)

            def per_window(w, carry3):
                first = a0 + (w + 1) * WIN
                aw = pl.multiple_of(jnp.minimum(first, cap - WIN), SLOT_ALIGN)
                srow = slot_ref[0, k, pl.ds(pb, 1), :]
                match = (srow == (sub + aw)) & (srow >= first)
                onehot = jnp.where(match, 1.0, 0.0).astype(BF16)
                gate = jnp.sum(jnp.where(match, aff_ref[0, k, pl.ds(pb, 1), :], 0.0),
                               axis=1, keepdims=True)
                merge(k, aw, _dot(onehot, h2_ref[0, tokens(pb), :]), gate)
                return carry3

            return lax.fori_loop(0, _extra_windows(c1, a0), per_window, carry2)

        lax.fori_loop(0, ng, per_expert, 0)

    def pair_body(i, carry):
        overflows = [first_windows(2 * i + j) for j in range(2)]
        for j in range(2):
            pl.when(overflows[j] > 0)(functools.partial(further_windows, 2 * i + j))
        return carry

    lax.fori_loop(0, nblk // 2, pair_body, 0)


def _gather(cnt2, h2, slots4, aff4, cap):
    b, s, d = h2.shape
    e = slots4.shape[1]
    nblk = s // MXU_DIM
    grid_spec = pltpu.PrefetchScalarGridSpec(
        num_scalar_prefetch=1,
        grid=(b, e // EG),
        in_specs=[
            pl.BlockSpec((1, s, d), lambda bi, gi, c: (bi, 0, 0)),
            pl.BlockSpec((1, EG, nblk, MXU_DIM), lambda bi, gi, c: (bi, gi, 0, 0)),
            pl.BlockSpec((1, EG, nblk, MXU_DIM), lambda bi, gi, c: (bi, gi, 0, 0)),
        ],
        out_specs=[
            pl.BlockSpec((1, EG, cap, d), lambda bi, gi, c: (bi, gi, 0, 0)),
            pl.BlockSpec((1, EG, cap, LANES), lambda bi, gi, c: (bi, gi, 0, 0)),
        ],
    )
    return pl.pallas_call(
        functools.partial(_gather_kernel, e),
        grid_spec=grid_spec,
        out_shape=[
            jax.ShapeDtypeStruct((b, e, cap, d), BF16),
            jax.ShapeDtypeStruct((b, e, cap, LANES), F32),
        ],
        compiler_params=pltpu.CompilerParams(
            dimension_semantics=("arbitrary", "arbitrary"), vmem_limit_bytes=VMEM_LIMIT),
        name="gather",
    )(cnt2, h2, slots4, aff4)


def _ffn_kernel(x_ref, gs_ref, wg_ref, wu_ref, wd_ref, out_ref, acc_s):
    e, f = pl.program_id(0), pl.program_id(1)
    nb, _, cap, d = x_ref.shape

    @pl.when((e == 0) & (f == 0))
    def _init():
        acc_s[...] = jnp.zeros_like(acc_s)

    wg, wu, wd = wg_ref[0].astype(BF16), wu_ref[0].astype(BF16), wd_ref[0].astype(BF16)
    for bi in range(nb):
        x = x_ref[bi, 0]
        g = _dot(x, wg)
        u = _dot(x, wu)
        hid = (g * jax.nn.sigmoid(g) * u).astype(BF16)
        y = _dot(hid, wd)
        rows = slice(bi * cap, (bi + 1) * cap)
        acc_s[rows, :] = jnp.where(f == 0, y, acc_s[rows, :] + y)

    @pl.when(f == pl.num_programs(1) - 1)
    def _emit():
        gate = gs_ref[:, 0, :, 0:1].reshape(nb * cap, 1)
        out_ref[:, 0] = (acc_s[...] * gate).reshape(nb, cap, d).astype(BF16)


def _ffn(xg, gs, w_gate, w_up, w_down, e0):
    b, e, cap, d = xg.shape
    ff = w_gate.shape[2]
    return pl.pallas_call(
        _ffn_kernel,
        grid=(e, ff // FC),
        in_specs=[
            pl.BlockSpec((b, 1, cap, d), lambda ei, fi: (0, ei, 0, 0)),
            pl.BlockSpec((b, 1, cap, LANES), lambda ei, fi: (0, ei, 0, 0)),
            pl.BlockSpec((1, d, FC), lambda ei, fi: (e0 + ei, 0, fi)),
            pl.BlockSpec((1, d, FC), lambda ei, fi: (e0 + ei, 0, fi)),
            pl.BlockSpec((1, FC, d), lambda ei, fi: (e0 + ei, fi, 0)),
        ],
        out_specs=pl.BlockSpec((b, 1, cap, d), lambda ei, fi: (0, ei, 0, 0)),
        out_shape=jax.ShapeDtypeStruct((b, e, cap, d), BF16),
        scratch_shapes=[pltpu.VMEM((b * cap, d), F32)],
        compiler_params=pltpu.CompilerParams(
            dimension_semantics=("arbitrary", "arbitrary"), vmem_limit_bytes=VMEM_LIMIT),
        name="ffn",
    )(xg, gs, w_gate, w_up, w_down)


def _combine_kernel(n_exp, cnt_ref, eo_ref, slot_ref, x1_ref, mod_ref, g_ref, o_ref,
                    stage_s, oh_s, y_s):
    b, q = pl.program_id(0), pl.program_id(1)
    cap = eo_ref.shape[2]
    nblk_q = o_ref.shape[0] // MXU_DIM
    sub = lax.broadcasted_iota(jnp.int32, (WIN, MXU_DIM), 0)
    gain = g_ref[...] * mod_ref[0, 5:6, :]

    for j in range(nblk_q):
        pb = q * nblk_q + j
        overflow = jnp.int32(0)
        for e in range(n_exp):
            _, c1, a0 = _segment(cnt_ref, b * n_exp + e, pb, cap)
            stage_s[e * WIN:(e + 1) * WIN, :] = eo_ref[0, e, pl.ds(a0, WIN), :]
            match = slot_ref[0, e, pl.ds(pb, 1), :] == (sub + a0)
            oh_s[e * WIN:(e + 1) * WIN, :] = jnp.where(match, 1.0, 0.0).astype(BF16)
            overflow = overflow + jnp.maximum(c1 - (a0 + WIN), 0)
        y_s[...] = _dot_tn(oh_s[...], stage_s[...])

        @pl.when(overflow > 0)
        def _long_segments(j=j, pb=pb):
            def per_expert(e, carry):
                _, c1, a0 = _segment(cnt_ref, b * n_exp + e, pb, cap)

                def per_window(k, carry2):
                    first = a0 + (k + 1) * WIN
                    aw = pl.multiple_of(jnp.minimum(first, cap - WIN), SLOT_ALIGN)
                    srow = slot_ref[0, e, pl.ds(pb, 1), :]
                    match = (srow == (sub + aw)) & (srow >= first)
                    onehot = jnp.where(match, 1.0, 0.0).astype(BF16)
                    y_s[...] += _dot_tn(onehot, eo_ref[0, e, pl.ds(aw, WIN), :])
                    return carry2

                return lax.fori_loop(0, _extra_windows(c1, a0), per_window, carry)

            lax.fori_loop(0, n_exp, per_expert, 0)

        sl = slice(j * MXU_DIM, (j + 1) * MXU_DIM)
        o_ref[sl, :] = x1_ref[sl, :] + _rms(y_s[...], gain)


def _combine(cnt2, eo, slots4, x1, mod, g_post):
    b, e, cap, d = eo.shape
    t = x1.shape[0]
    s = t // b
    nq = s // QP
    nblk_q = QP // MXU_DIM
    grid_spec = pltpu.PrefetchScalarGridSpec(
        num_scalar_prefetch=1,
        grid=(b, nq),
        in_specs=[
            pl.BlockSpec((1, e, cap, d), lambda bi, qi, c: (bi, 0, 0, 0), pipeline_mode=pl.Buffered(1)),
            pl.BlockSpec((1, e, s // MXU_DIM, MXU_DIM), lambda bi, qi, c: (bi, 0, 0, 0)),
            pl.BlockSpec((QP, d), lambda bi, qi, c: (bi * nq + qi, 0)),
            pl.BlockSpec((1, 6, d), lambda bi, qi, c: (bi, 0, 0)),
            pl.BlockSpec((1, d), lambda bi, qi, c: (0, 0)),
        ],
        out_specs=pl.BlockSpec((QP, d), lambda bi, qi, c: (bi * nq + qi, 0)),
        scratch_shapes=[
            pltpu.VMEM((e * WIN, d), BF16),
            pltpu.VMEM((e * WIN, MXU_DIM), BF16),
            pltpu.VMEM((MXU_DIM, d), F32),
        ],
    )
    return pl.pallas_call(
        functools.partial(_combine_kernel, e),
        grid_spec=grid_spec,
        out_shape=jax.ShapeDtypeStruct((t, d), F32),
        compiler_params=pltpu.CompilerParams(
            dimension_semantics=("arbitrary", "arbitrary"),
            vmem_limit_bytes=VMEM_LIMIT),
        name="combine",
    )(cnt2, eo, slots4, x1, mod, g_post)


def _layer(layer, x, c, w_ada_all, b_ada, norm_pre_mix, norm_post_mix, w_in, sink, sgu_ln_g,
           sgu_ln_b, w_s, b_s, norm_out_attn, norm_out_gmlp, w_out, norm_pre_ffn, norm_post_ffn,
           w_router, w_gate_all, w_up_all, w_down_all):
    b, s, d = x.shape
    t = b * s
    dg = sgu_ln_g.shape[0]
    e = w_router.shape[1]
    cap = CAPACITY_FACTOR * s // e
    assert s % QP == 0 and s % TM == 0 and s % PROJ_TM == 0
    assert w_gate_all.shape[2] % FC == 0 and e % EG == 0
    assert cap % SLOT_ALIGN == 0 and cap >= WIN and s // MXU_DIM < LANES and d == D_ATTN + dg
    assert (s // MXU_DIM) % 2 == 0

    mod = _adaln(c, w_ada_all, b_ada, layer).reshape(b, 6, d)

    x2 = x.reshape(t, d)
    q, kv, u, vg = _proj(x2, mod, norm_pre_mix.reshape(1, d), w_in.astype(BF16), s)

    bias = jnp.asarray(_attn_bias_table())
    bs_full = jnp.repeat(b_s.T, dg // N_GMLP_GROUPS, axis=1)
    x1, h2, aff_t = _mix(
        x2, q, kv, u, vg, mod, sink, bias, w_s.astype(BF16), bs_full,
        sgu_ln_g.reshape(1, dg), sgu_ln_b.reshape(1, dg), norm_out_attn.reshape(1, D_ATTN),
        norm_out_gmlp.reshape(1, dg), w_out.astype(BF16), norm_post_mix.reshape(1, d),
        norm_pre_ffn.reshape(1, d), w_router.T.astype(BF16), b, s)

    slots3, aff3, cnt2 = _route(aff_t.reshape(b * e, s), cap)
    nblk = s // MXU_DIM
    slots4 = slots3.reshape(b, e, nblk, MXU_DIM)
    aff4 = aff3.reshape(b, e, nblk, MXU_DIM)
    xg, gs = _gather(cnt2, h2.reshape(b, s, d), slots4, aff4, cap)
    eo = _ffn(xg, gs, w_gate_all, w_up_all, w_down_all, layer * e)
    out = _combine(cnt2, eo, slots4, x1, mod, norm_post_ffn.reshape(1, d))
    return out.reshape(b, s, d)


def kernel(x, c, w_ada, b_ada, norm_pre_mix, norm_post_mix, w_in, sink, sgu_ln_g, sgu_ln_b, w_s, b_s, norm_out_attn, norm_out_gmlp, w_out, norm_pre_ffn, norm_post_ffn, w_router, w_gate, w_up, w_down):
    depth, d = w_ada.shape[0], w_ada.shape[1]
    w_ada_all = w_ada.reshape(depth * d, w_ada.shape[2])
    stack = lambda w: w.reshape((depth * w.shape[1],) + w.shape[2:])
    w_gate_all, w_up_all, w_down_all = stack(w_gate), stack(w_up), stack(w_down)
    for l in range(depth):
        x = _layer(l, x, c, w_ada_all, b_ada[l], norm_pre_mix[l], norm_post_mix[l], w_in[l],
                   sink[l], sgu_ln_g[l], sgu_ln_b[l], w_s[l], b_s[l], norm_out_attn[l],
                   norm_out_gmlp[l], w_out[l], norm_pre_ffn[l], norm_post_ffn[l], w_router[l],
                   w_gate_all, w_up_all, w_down_all)
    return x
```

```python
import functools

import numpy as np
import jax
import jax.numpy as jnp
from jax import lax
from jax.experimental import pallas as pl
from jax.experimental.pallas import tpu as pltpu

N_HEADS = 8
N_KV_HEADS = 2
HEAD_DIM = 64
D_ATTN = N_HEADS * HEAD_DIM
D_KV = N_KV_HEADS * HEAD_DIM
GROUP = N_HEADS // N_KV_HEADS
WINDOW = 128
BLOCK = 128
N_GMLP_GROUPS = 8
N_EXPERTS = 16
CAPACITY_FACTOR = 2
EPS = 1e-6
MASK_VALUE = -1e30

LANES = 128
MXU_DIM = 256

TM = 1024
FC = 256
PROJ_ROWS = 256
EG = 4
QP = 1024
WIN = 64
SLOT_ALIGN = 16
VMEM_LIMIT = 56 * 1024 * 1024

BF16 = jnp.bfloat16
F32 = jnp.float32


def _rms(x, g):
    return x * lax.rsqrt(jnp.mean(x * x, axis=-1, keepdims=True) + EPS) * g


def _dot(a, b):
    return jnp.dot(a, b, preferred_element_type=F32)


def _dot_nt(a, b):
    return lax.dot_general(a, b, (((1,), (1,)), ((), ())), preferred_element_type=F32)


def _dot_tn(a, b):
    return lax.dot_general(a, b, (((0,), (0,)), ((), ())), preferred_element_type=F32)


def _adaln_kernel(c_ref, w_ref, b_ref, o_ref):
    c = c_ref[...]
    a = (c * jax.nn.sigmoid(c)).astype(BF16)
    o_ref[...] = _dot(a, w_ref[...].astype(BF16)) + b_ref[...]


def _adaln(c, w_ada_all, b_ada, layer):
    b, d = c.shape
    n = w_ada_all.shape[1]
    return pl.pallas_call(
        _adaln_kernel,
        grid=(n // d,),
        in_specs=[
            pl.BlockSpec((b, d), lambda j: (0, 0)),
            pl.BlockSpec((d, d), lambda j: (layer, j)),
            pl.BlockSpec((1, d), lambda j: (0, j)),
        ],
        out_specs=pl.BlockSpec((b, d), lambda j: (0, j)),
        out_shape=jax.ShapeDtypeStruct((b, n), F32),
        compiler_params=pltpu.CompilerParams(dimension_semantics=("arbitrary",)),
        name="adaln",
    )(c, w_ada_all, b_ada.reshape(1, n))


def _gelu_tanh(x):
    c = np.float32(np.sqrt(2.0 / np.pi))
    return x * (0.5 * (1.0 + jnp.tanh(c * (x + 0.044715 * (x * x * x)))))


def _proj_kernel(x_ref, mod_ref, g_ref, w_ref, q_ref, kv_ref, u_ref, vg_ref):
    shift = mod_ref[0, 0:1, :]
    gain = g_ref[...] * (1.0 + mod_ref[0, 1:2, :])
    n_att = D_ATTN + 2 * D_KV
    dg = u_ref.shape[1]
    lo = lax.broadcasted_iota(jnp.int32, (1, LANES), 1) < HEAD_DIM
    for r in range(0, TM, PROJ_ROWS):
        rows = slice(r, r + PROJ_ROWS)
        h = (_rms(x_ref[rows, :], gain) + shift).astype(BF16)
        z = _gelu_tanh(_dot(h, w_ref[:, n_att:]))
        u_ref[rows, :] = z[:, :dg]
        vg_ref[rows, :] = z[:, dg:]
        p = _dot(h, w_ref[:, :n_att])
        q_ref[rows, :] = (p[:, :D_ATTN] * (HEAD_DIM ** -0.5)).astype(BF16)
        for j in range(2):
            t = p[:, D_ATTN + j * D_KV:D_ATTN + (j + 1) * D_KV]
            t_sw = pltpu.roll(t, HEAD_DIM, axis=1)
            kv_ref[rows, (2 * j) * LANES:(2 * j + 1) * LANES] = jnp.where(lo, t, t_sw).astype(BF16)
            kv_ref[rows, (2 * j + 1) * LANES:(2 * j + 2) * LANES] = jnp.where(lo, t_sw, t).astype(BF16)


def _proj(x2, mod, g_pre, w_in, s):
    t, d = x2.shape
    nw = w_in.shape[1]
    dg = (nw - D_ATTN - 2 * D_KV) // 2
    nsb = s // TM
    return pl.pallas_call(
        _proj_kernel,
        grid=(t // TM,),
        in_specs=[
            pl.BlockSpec((TM, d), lambda i: (i, 0)),
            pl.BlockSpec((1, 6, d), lambda i: (i // nsb, 0, 0)),
            pl.BlockSpec((1, d), lambda i: (0, 0)),
            pl.BlockSpec((d, nw), lambda i: (0, 0)),
        ],
        out_specs=[
            pl.BlockSpec((TM, D_ATTN), lambda i: (i, 0)),
            pl.BlockSpec((TM, D_ATTN), lambda i: (i, 0)),
            pl.BlockSpec((TM, dg), lambda i: (i, 0)),
            pl.BlockSpec((TM, dg), lambda i: (i, 0)),
        ],
        out_shape=[
            jax.ShapeDtypeStruct((t, D_ATTN), BF16),
            jax.ShapeDtypeStruct((t, D_ATTN), BF16),
            jax.ShapeDtypeStruct((t, dg), F32),
            jax.ShapeDtypeStruct((t, dg), F32),
        ],
        compiler_params=pltpu.CompilerParams(
            dimension_semantics=("arbitrary",), vmem_limit_bytes=VMEM_LIMIT),
        name="proj",
    )(x2, mod, g_pre, w_in)


def _attn_bias_table():
    qi = np.arange(BLOCK)[:, None]
    kj = np.arange(3 * BLOCK)[None, :] - BLOCK
    dist = np.abs(qi - kj).astype(np.float32)
    band = dist <= WINDOW
    slopes = np.exp2(-8.0 * np.arange(1, N_HEADS + 1, dtype=np.float32) / N_HEADS).astype(np.float32)
    key_ok = [kj >= 0, np.ones_like(kj, bool), kj < BLOCK]
    tab = np.empty((3, N_HEADS, BLOCK, 3 * BLOCK), np.float32)
    for v in range(3):
        ok = band & key_ok[v]
        for h in range(N_HEADS):
            tab[v, h] = np.where(ok, -slopes[h] * dist, np.float32(MASK_VALUE))
    return tab.reshape(3, N_KV_HEADS, GROUP * BLOCK, 3 * BLOCK)


def _mix_kernel(nblk_seq, sink_ref, x_ref, q_ref, kvp_ref, kvc_ref, kvn_ref, u_ref, vg_ref,
                mod_ref, bias_ref, ws_ref, bs_ref, lng_ref, lnb_ref, ga_ref, gg_ref,
                wout_ref, gpost_ref, gpre2_ref, wr_ref,
                x1_ref, h2_ref, aff_ref, kcat, attn_s, gm_s):
    i = pl.program_id(0)
    nb_tile = TM // BLOCK
    kcat[0:BLOCK, :] = kvp_ref[...]
    kcat[BLOCK:BLOCK + TM, :] = kvc_ref[...]
    kcat[BLOCK + TM:, :] = kvn_ref[...]
    lane = lax.broadcasted_iota(jnp.int32, (1, LANES), 1)
    lo = lane < HEAD_DIM
    zero = jnp.zeros((), BF16)
    head_of_row = lax.broadcasted_iota(jnp.int32, (GROUP * BLOCK, 1), 0) // BLOCK
    sinks = []
    for g in range(N_KV_HEADS):
        col = jnp.zeros((GROUP * BLOCK, 1), F32)
        for j in range(GROUP):
            col = jnp.where(head_of_row == j, sink_ref[g * GROUP + j], col)
        sinks.append(col)

    def block_body(a, carry):
        r0 = pl.multiple_of(a * BLOCK, BLOCK)
        pos = (i * nb_tile + a) % nblk_seq
        var = jnp.where(pos == 0, 0, jnp.where(pos == nblk_seq - 1, 2, 1))
        kv = kcat[pl.ds(r0, 3 * BLOCK), :]
        for g in range(N_KV_HEADS):
            k = kv[:, g * LANES:(g + 1) * LANES]
            v = kv[:, (N_KV_HEADS + g) * LANES:(N_KV_HEADS + g + 1) * LANES]
            pieces = []
            for pi in range(GROUP // 2):
                qp = q_ref[pl.ds(r0, BLOCK), (g * GROUP // 2 + pi) * LANES:(g * GROUP // 2 + pi + 1) * LANES]
                pieces += [jnp.where(lo, qp, zero), jnp.where(lo, zero, qp)]
            s = _dot_nt(jnp.concatenate(pieces, axis=0), k) + bias_ref[var, g]
            m = jnp.maximum(jnp.max(s, axis=-1, keepdims=True), sinks[g])
            p = jnp.exp(s - m)
            den = jnp.sum(p, axis=-1, keepdims=True) + jnp.exp(sinks[g] - m)
            o = _dot(p.astype(BF16), v) / den
            for pi in range(GROUP // 2):
                even = o[(2 * pi) * BLOCK:(2 * pi + 1) * BLOCK, :]
                odd = o[(2 * pi + 1) * BLOCK:(2 * pi + 2) * BLOCK, :]
                col0 = (g * GROUP // 2 + pi) * LANES
                attn_s[pl.ds(r0, BLOCK), col0:col0 + LANES] = jnp.where(lo, even, odd)
        vg = vg_ref[pl.ds(r0, BLOCK), :]
        mu = jnp.mean(vg, axis=-1, keepdims=True)
        xc = vg - mu
        vn = (xc * lax.rsqrt(jnp.mean(xc * xc, axis=-1, keepdims=True) + EPS) * lng_ref[...]
              + lnb_ref[...]).astype(BF16)
        for pi in range(N_GMLP_GROUPS // 2):
            vnp = vn[:, pi * LANES:(pi + 1) * LANES]
            z = jnp.where(lo, _dot(ws_ref[2 * pi], vnp), _dot(ws_ref[2 * pi + 1], vnp))
            sl = slice(pi * LANES, (pi + 1) * LANES)
            gm_s[pl.ds(r0, BLOCK), sl] = u_ref[pl.ds(r0, BLOCK), sl] * (z + bs_ref[:, sl])
        return carry

    lax.fori_loop(0, nb_tile, block_body, 0, unroll=2)

    na = _rms(attn_s[...], ga_ref[...]).astype(BF16)
    ng = _rms(gm_s[...], gg_ref[...]).astype(BF16)
    mo = _dot(na, wout_ref[0:D_ATTN, :]) + _dot(ng, wout_ref[D_ATTN:, :])
    x1 = x_ref[...] + _rms(mo, gpost_ref[...] * mod_ref[0, 2:3, :])
    x1_ref[...] = x1
    h2 = (_rms(x1, gpre2_ref[...] * (1.0 + mod_ref[0, 4:5, :])) + mod_ref[0, 3:4, :]).astype(BF16)
    h2_ref[...] = h2
    lg = _dot_nt(wr_ref[...], h2)
    lg = lg - jnp.max(lg, axis=0, keepdims=True)
    ex = jnp.exp(lg)
    aff_ref[0] = ex / jnp.sum(ex, axis=0, keepdims=True)


def _mix(x2, q, kv, u, vg, mod, sink, bias, ws, bs_full, ln_g, ln_b, g_attn, g_gmlp,
         w_out, g_post, g_pre2, w_r_t, b, s):
    t, d = x2.shape
    dg = u.shape[1]
    e = w_r_t.shape[0]
    nsb = s // TM
    nb_tile = TM // BLOCK
    nblk = t // BLOCK
    row = lambda i: (i, 0)
    const2 = lambda i: (0, 0)
    in_specs = [
        pl.BlockSpec(memory_space=pltpu.SMEM),
        pl.BlockSpec((TM, d), row),
        pl.BlockSpec((TM, D_ATTN), row),
        pl.BlockSpec((BLOCK, 4 * LANES), lambda i: (jnp.maximum(i * nb_tile - 1, 0), 0)),
        pl.BlockSpec((TM, 4 * LANES), row),
        pl.BlockSpec((BLOCK, 4 * LANES), lambda i: (jnp.minimum((i + 1) * nb_tile, nblk - 1), 0)),
        pl.BlockSpec((TM, dg), row),
        pl.BlockSpec((TM, dg), row),
        pl.BlockSpec((1, 6, d), lambda i: (i // nsb, 0, 0)),
        pl.BlockSpec(bias.shape, lambda i: (0, 0, 0, 0)),
        pl.BlockSpec(ws.shape, lambda i: (0, 0, 0)),
        pl.BlockSpec((BLOCK, dg), const2),
        pl.BlockSpec((1, dg), const2),
        pl.BlockSpec((1, dg), const2),
        pl.BlockSpec((1, D_ATTN), const2),
        pl.BlockSpec((1, dg), const2),
        pl.BlockSpec(w_out.shape, const2),
        pl.BlockSpec((1, d), const2),
        pl.BlockSpec((1, d), const2),
        pl.BlockSpec(w_r_t.shape, const2),
    ]
    out_specs = [
        pl.BlockSpec((TM, d), row),
        pl.BlockSpec((TM, d), row),
        pl.BlockSpec((1, e, TM), lambda i: (i // nsb, 0, i % nsb)),
    ]
    out_shape = [
        jax.ShapeDtypeStruct((t, d), F32),
        jax.ShapeDtypeStruct((t, d), BF16),
        jax.ShapeDtypeStruct((b, e, s), F32),
    ]
    return pl.pallas_call(
        functools.partial(_mix_kernel, s // BLOCK),
        grid=(t // TM,),
        in_specs=in_specs,
        out_specs=out_specs,
        out_shape=out_shape,
        scratch_shapes=[
            pltpu.VMEM((TM + 2 * BLOCK, 4 * LANES), BF16),
            pltpu.VMEM((TM, D_ATTN), F32),
            pltpu.VMEM((TM, dg), F32),
        ],
        compiler_params=pltpu.CompilerParams(
            dimension_semantics=("arbitrary",), vmem_limit_bytes=VMEM_LIMIT),
        name="mix",
    )(sink, x2, q, kv, kv, kv, u, vg, mod, bias, ws, bs_full, ln_g, ln_b, g_attn, g_gmlp,
      w_out, g_post, g_pre2, w_r_t)


def _route_kernel(cap, aff_ref, slot_ref, affb_ref, cnt_ref):
    aff = aff_ref[...]
    e, s = aff.shape
    capf = jnp.float32(cap)

    def count_ge(t):
        return jnp.sum(jnp.where(aff >= t, 1.0, 0.0), axis=1, keepdims=True)

    def bit_body(it, cand):
        trial = cand | jnp.left_shift(jnp.int32(1), 30 - it)
        t = lax.bitcast_convert_type(trial, F32)
        return jnp.where(count_ge(t) >= capf, trial, cand)

    cand = lax.fori_loop(0, 31, bit_body, jnp.zeros((e, 1), jnp.int32))
    thr = lax.bitcast_convert_type(cand, F32)
    need = capf - jnp.sum(jnp.where(aff > thr, 1.0, 0.0), axis=1, keepdims=True)

    r = lax.broadcasted_iota(jnp.int32, (LANES, LANES), 0)
    c = lax.broadcasted_iota(jnp.int32, (LANES, LANES), 1)
    tri = jnp.where(r < c, 1.0, 0.0).astype(BF16)
    lane = lax.broadcasted_iota(jnp.int32, (e, LANES), 1)
    carry_eq = jnp.zeros((e, 1), F32)
    carry_sel = jnp.zeros((e, 1), F32)
    cnt = jnp.zeros((e, LANES), F32)
    per_blk = MXU_DIM // LANES
    for j in range(s // LANES):
        a = aff[:, j * LANES:(j + 1) * LANES]
        gt = a > thr
        eq = jnp.where(a == thr, 1.0, 0.0)
        tie_rank = _dot(eq.astype(BF16), tri) + carry_eq
        sel = jnp.where(gt | ((eq > 0.0) & (tie_rank < need)), 1.0, 0.0)
        slot = _dot(sel.astype(BF16), tri) + carry_sel
        blk = (slice(None), j // per_blk, slice((j % per_blk) * LANES, (j % per_blk + 1) * LANES))
        slot_ref[blk] = jnp.where(sel > 0.0, slot, -1.0).astype(jnp.int32)
        affb_ref[blk] = a
        if j % per_blk == 0:
            cnt = jnp.where(lane == j // per_blk, carry_sel, cnt)
        carry_eq = carry_eq + jnp.sum(eq, axis=1, keepdims=True)
        carry_sel = carry_sel + jnp.sum(sel, axis=1, keepdims=True)
    cnt = jnp.where(lane >= s // MXU_DIM, carry_sel, cnt)
    cnt_ref[...] = cnt.astype(jnp.int32)


def _route(aff2, cap):
    r, s = aff2.shape
    nblk = s // MXU_DIM
    return pl.pallas_call(
        functools.partial(_route_kernel, cap),
        grid=(1,),
        in_specs=[pl.BlockSpec((r, s), lambda i: (0, 0))],
        out_specs=[
            pl.BlockSpec((r, nblk, MXU_DIM), lambda i: (0, 0, 0)),
            pl.BlockSpec((r, nblk, MXU_DIM), lambda i: (0, 0, 0)),
            pl.BlockSpec((r, LANES), lambda i: (0, 0)),
        ],
        out_shape=[
            jax.ShapeDtypeStruct((r, nblk, MXU_DIM), jnp.int32),
            jax.ShapeDtypeStruct((r, nblk, MXU_DIM), F32),
            jax.ShapeDtypeStruct((r, LANES), jnp.int32),
        ],
        compiler_params=pltpu.CompilerParams(dimension_semantics=("arbitrary",)),
        name="route",
    )(aff2)


def _segment(cnt_ref, row, pb, cap):
    c0 = jnp.minimum(cnt_ref[row, pb], cap)
    c1 = jnp.minimum(cnt_ref[row, pb + 1], cap)
    a0 = jnp.minimum(jnp.bitwise_and(c0, -SLOT_ALIGN), cap - WIN)
    return c0, c1, pl.multiple_of(a0, SLOT_ALIGN)


def _extra_windows(c1, a0):
    return (jnp.maximum(c1 - (a0 + WIN), 0) + WIN - 1) // WIN


def _gather_kernel(n_exp, cnt_ref, h2_ref, slot_ref, aff_ref, x_ref, gs_ref):
    b, eg = pl.program_id(0), pl.program_id(1)
    ng, cap = x_ref.shape[1], x_ref.shape[2]
    nblk = slot_ref.shape[2]
    sub = lax.broadcasted_iota(jnp.int32, (WIN, MXU_DIM), 0)
    x_ref[...] = jnp.zeros_like(x_ref)
    gs_ref[...] = jnp.zeros_like(gs_ref)

    def merge(k, a0, rows, gate):
        win = (0, k, pl.ds(a0, WIN), slice(None))
        x_ref[win] += rows.astype(BF16)
        gs_ref[win] += gate

    def tokens(pb):
        return pl.ds(pl.multiple_of(pb * MXU_DIM, MXU_DIM), MXU_DIM)

    def first_windows(pb):
        segs, gates, onehots = [], [], []
        overflow = jnp.int32(0)
        for k in range(ng):
            c0, c1, a0 = _segment(cnt_ref, b * n_exp + eg * ng + k, pb, cap)
            match = slot_ref[0, k, pl.ds(pb, 1), :] == (sub + a0)
            onehots.append(jnp.where(match, 1.0, 0.0).astype(BF16))
            gates.append(jnp.sum(jnp.where(match, aff_ref[0, k, pl.ds(pb, 1), :], 0.0),
                                 axis=1, keepdims=True))
            segs.append((c0, c1, a0))
            overflow = overflow + jnp.maximum(c1 - (a0 + WIN), 0)
        rows = _dot(jnp.concatenate(onehots, axis=0), h2_ref[0, tokens(pb), :])
        for k in range(ng):
            merge(k, segs[k][2], rows[k * WIN:(k + 1) * WIN, :], gates[k])
        return overflow

    def further_windows(pb):
        def per_expert(k, carry2):
            _, c1, a0 = _segment(cnt_ref, b * n_exp + eg * ng + k, pb, cap)

            def per_window(w, carry3):
                first = a0 + (w + 1) * WIN
                aw = pl.multiple_of(jnp.minimum(first, cap - WIN), SLOT_ALIGN)
                srow = slot_ref[0, k, pl.ds(pb, 1), :]
                match = (srow == (sub + aw)) & (srow >= first)
                onehot = jnp.where(match, 1.0, 0.0).astype(BF16)
                gate = jnp.sum(jnp.where(match, aff_ref[0, k, pl.ds(pb, 1), :], 0.0),
                               axis=1, keepdims=True)
                merge(k, aw, _dot(onehot, h2_ref[0, tokens(pb), :]), gate)
                return carry3

            return lax.fori_loop(0, _extra_windows(c1, a0), per_window, carry2)

        lax.fori_loop(0, ng, per_expert, 0)

    def pair_body(i, carry):
        overflows = [first_windows(2 * i + j) for j in range(2)]
        for j in range(2):
            pl.when(overflows[j] > 0)(functools.partial(further_windows, 2 * i + j))
        return carry

    lax.fori_loop(0, nblk // 2, pair_body, 0)


def _gather(cnt2, h2, slots4, aff4, cap):
    b, s, d = h2.shape
    e = slots4.shape[1]
    nblk = s // MXU_DIM
    grid_spec = pltpu.PrefetchScalarGridSpec(
        num_scalar_prefetch=1,
        grid=(b, e // EG),
        in_specs=[
            pl.BlockSpec((1, s, d), lambda bi, gi, c: (bi, 0, 0)),
            pl.BlockSpec((1, EG, nblk, MXU_DIM), lambda bi, gi, c: (bi, gi, 0, 0)),
            pl.BlockSpec((1, EG, nblk, MXU_DIM), lambda bi, gi, c: (bi, gi, 0, 0)),
        ],
        out_specs=[
            pl.BlockSpec((1, EG, cap, d), lambda bi, gi, c: (bi, gi, 0, 0)),
            pl.BlockSpec((1, EG, cap, LANES), lambda bi, gi, c: (bi, gi, 0, 0)),
        ],
    )
    return pl.pallas_call(
        functools.partial(_gather_kernel, e),
        grid_spec=grid_spec,
        out_shape=[
            jax.ShapeDtypeStruct((b, e, cap, d), BF16),
            jax.ShapeDtypeStruct((b, e, cap, LANES), F32),
        ],
        compiler_params=pltpu.CompilerParams(
            dimension_semantics=("arbitrary", "arbitrary"), vmem_limit_bytes=VMEM_LIMIT),
        name="gather",
    )(cnt2, h2, slots4, aff4)


def _ffn_kernel(x_ref, gs_ref, wg_ref, wu_ref, wd_ref, out_ref, acc_s):
    e, f = pl.program_id(0), pl.program_id(1)
    nb, _, cap, d = x_ref.shape

    @pl.when((e == 0) & (f == 0))
    def _init():
        acc_s[...] = jnp.zeros_like(acc_s)

    wg, wu, wd = wg_ref[0].astype(BF16), wu_ref[0].astype(BF16), wd_ref[0].astype(BF16)
    for bi in range(nb):
        x = x_ref[bi, 0]
        g = _dot(x, wg)
        u = _dot(x, wu)
        hid = (g * jax.nn.sigmoid(g) * u).astype(BF16)
        y = _dot(hid, wd)
        rows = slice(bi * cap, (bi + 1) * cap)
        acc_s[rows, :] = jnp.where(f == 0, y, acc_s[rows, :] + y)

    @pl.when(f == pl.num_programs(1) - 1)
    def _emit():
        gate = gs_ref[:, 0, :, 0:1].reshape(nb * cap, 1)
        out_ref[:, 0] = (acc_s[...] * gate).reshape(nb, cap, d).astype(BF16)


def _ffn(xg, gs, w_gate, w_up, w_down, e0):
    b, e, cap, d = xg.shape
    ff = w_gate.shape[2]
    return pl.pallas_call(
        _ffn_kernel,
        grid=(e, ff // FC),
        in_specs=[
            pl.BlockSpec((b, 1, cap, d), lambda ei, fi: (0, ei, 0, 0)),
            pl.BlockSpec((b, 1, cap, LANES), lambda ei, fi: (0, ei, 0, 0)),
            pl.BlockSpec((1, d, FC), lambda ei, fi: (e0 + ei, 0, fi)),
            pl.BlockSpec((1, d, FC), lambda ei, fi: (e0 + ei, 0, fi)),
            pl.BlockSpec((1, FC, d), lambda ei, fi: (e0 + ei, fi, 0)),
        ],
        out_specs=pl.BlockSpec((b, 1, cap, d), lambda ei, fi: (0, ei, 0, 0)),
        out_shape=jax.ShapeDtypeStruct((b, e, cap, d), BF16),
        scratch_shapes=[pltpu.VMEM((b * cap, d), F32)],
        compiler_params=pltpu.CompilerParams(
            dimension_semantics=("arbitrary", "arbitrary"), vmem_limit_bytes=VMEM_LIMIT),
        name="ffn",
    )(xg, gs, w_gate, w_up, w_down)


def _combine_kernel(n_exp, cnt_ref, eo_hbm, slot_ref, x1_ref, mod_ref, g_ref, o_ref,
                    stage_s, extra_s, oh_s, y_s, sem, extra_sem):
    b, q = pl.program_id(0), pl.program_id(1)
    nq = pl.num_programs(1)
    t, total = b * nq + q, pl.num_programs(0) * nq
    cap = eo_hbm.shape[2]
    nblk_q = o_ref.shape[0] // MXU_DIM
    sub = lax.broadcasted_iota(jnp.int32, (WIN, MXU_DIM), 0)
    gain = g_ref[...] * mod_ref[0, 5:6, :]

    def window_copies(step):
        bs, qs = step // nq, step % nq
        copies = []
        for j in range(nblk_q):
            for e in range(n_exp):
                _, _, a0 = _segment(cnt_ref, bs * n_exp + e, qs * nblk_q + j, cap)
                copies.append(pltpu.make_async_copy(
                    eo_hbm.at[bs, e, pl.ds(a0, WIN), :],
                    stage_s.at[step % 2, j, pl.ds(e * WIN, WIN), :], sem.at[step % 2]))
        return copies

    @pl.when(t == 0)
    def _prime():
        for c in window_copies(t):
            c.start()

    @pl.when(t + 1 < total)
    def _prefetch():
        for c in window_copies(t + 1):
            c.start()

    for c in window_copies(t):
        c.wait()

    for j in range(nblk_q):
        pb = q * nblk_q + j
        overflow = jnp.int32(0)
        for e in range(n_exp):
            _, c1, a0 = _segment(cnt_ref, b * n_exp + e, pb, cap)
            match = slot_ref[0, e, pl.ds(pb, 1), :] == (sub + a0)
            oh_s[e * WIN:(e + 1) * WIN, :] = jnp.where(match, 1.0, 0.0).astype(BF16)
            overflow = overflow + jnp.maximum(c1 - (a0 + WIN), 0)
        y_s[...] = _dot_tn(oh_s[...], stage_s[t % 2, j])

        @pl.when(overflow > 0)
        def _long_segments(j=j, pb=pb):
            def per_expert(e, carry):
                _, c1, a0 = _segment(cnt_ref, b * n_exp + e, pb, cap)

                def per_window(k, carry2):
                    first = a0 + (k + 1) * WIN
                    aw = pl.multiple_of(jnp.minimum(first, cap - WIN), SLOT_ALIGN)
                    fetch = pltpu.make_async_copy(eo_hbm.at[b, e, pl.ds(aw, WIN), :], extra_s, extra_sem)
                    fetch.start()
                    srow = slot_ref[0, e, pl.ds(pb, 1), :]
                    match = (srow == (sub + aw)) & (srow >= first)
                    onehot = jnp.where(match, 1.0, 0.0).astype(BF16)
                    fetch.wait()
                    y_s[...] += _dot_tn(onehot, extra_s[...])
                    return carry2

                return lax.fori_loop(0, _extra_windows(c1, a0), per_window, carry)

            lax.fori_loop(0, n_exp, per_expert, 0)

        sl = slice(j * MXU_DIM, (j + 1) * MXU_DIM)
        o_ref[sl, :] = x1_ref[sl, :] + _rms(y_s[...], gain)


def _combine(cnt2, eo, slots4, x1, mod, g_post):
    b, e, cap, d = eo.shape
    t = x1.shape[0]
    s = t // b
    nq = s // QP
    nblk_q = QP // MXU_DIM
    grid_spec = pltpu.PrefetchScalarGridSpec(
        num_scalar_prefetch=1,
        grid=(b, nq),
        in_specs=[
            pl.BlockSpec(memory_space=pl.ANY),
            pl.BlockSpec((1, e, s // MXU_DIM, MXU_DIM), lambda bi, qi, c: (bi, 0, 0, 0)),
            pl.BlockSpec((QP, d), lambda bi, qi, c: (bi * nq + qi, 0)),
            pl.BlockSpec((1, 6, d), lambda bi, qi, c: (bi, 0, 0)),
            pl.BlockSpec((1, d), lambda bi, qi, c: (0, 0)),
        ],
        out_specs=pl.BlockSpec((QP, d), lambda bi, qi, c: (bi * nq + qi, 0)),
        scratch_shapes=[
            pltpu.VMEM((2, nblk_q, e * WIN, d), BF16),
            pltpu.VMEM((WIN, d), BF16),
            pltpu.VMEM((e * WIN, MXU_DIM), BF16),
            pltpu.VMEM((MXU_DIM, d), F32),
            pltpu.SemaphoreType.DMA((2,)),
            pltpu.SemaphoreType.DMA(()),
        ],
    )
    return pl.pallas_call(
        functools.partial(_combine_kernel, e),
        grid_spec=grid_spec,
        out_shape=jax.ShapeDtypeStruct((t, d), F32),
        compiler_params=pltpu.CompilerParams(
            dimension_semantics=("arbitrary", "arbitrary"),
            vmem_limit_bytes=VMEM_LIMIT),
        name="combine",
    )(cnt2, eo, slots4, x1, mod, g_post)


def _layer(layer, x, c, w_ada_all, b_ada, norm_pre_mix, norm_post_mix, w_in, sink, sgu_ln_g,
           sgu_ln_b, w_s, b_s, norm_out_attn, norm_out_gmlp, w_out, norm_pre_ffn, norm_post_ffn,
           w_router, w_gate_all, w_up_all, w_down_all):
    b, s, d = x.shape
    t = b * s
    dg = sgu_ln_g.shape[0]
    e = w_router.shape[1]
    cap = CAPACITY_FACTOR * s // e
    assert s % QP == 0 and s % TM == 0 and w_gate_all.shape[2] % FC == 0 and e % EG == 0
    assert cap % SLOT_ALIGN == 0 and cap >= WIN and s // MXU_DIM < LANES and d == D_ATTN + dg
    assert (s // MXU_DIM) % 2 == 0

    mod = _adaln(c, w_ada_all, b_ada, layer).reshape(b, 6, d)

    x2 = x.reshape(t, d)
    q, kv, u, vg = _proj(x2, mod, norm_pre_mix.reshape(1, d), w_in.astype(BF16), s)

    bias = jnp.asarray(_attn_bias_table())
    bs_full = jnp.repeat(b_s.T, dg // N_GMLP_GROUPS, axis=1)
    x1, h2, aff_t = _mix(
        x2, q, kv, u, vg, mod, sink, bias, w_s.astype(BF16), bs_full,
        sgu_ln_g.reshape(1, dg), sgu_ln_b.reshape(1, dg), norm_out_attn.reshape(1, D_ATTN),
        norm_out_gmlp.reshape(1, dg), w_out.astype(BF16), norm_post_mix.reshape(1, d),
        norm_pre_ffn.reshape(1, d), w_router.T.astype(BF16), b, s)

    slots3, aff3, cnt2 = _route(aff_t.reshape(b * e, s), cap)
    nblk = s // MXU_DIM
    slots4 = slots3.reshape(b, e, nblk, MXU_DIM)
    aff4 = aff3.reshape(b, e, nblk, MXU_DIM)
    xg, gs = _gather(cnt2, h2.reshape(b, s, d), slots4, aff4, cap)
    eo = _ffn(xg, gs, w_gate_all, w_up_all, w_down_all, layer * e)
    out = _combine(cnt2, eo, slots4, x1, mod, norm_post_ffn.reshape(1, d))
    return out.reshape(b, s, d)


def kernel(x, c, w_ada, b_ada, norm_pre_mix, norm_post_mix, w_in, sink, sgu_ln_g, sgu_ln_b, w_s, b_s, norm_out_attn, norm_out_gmlp, w_out, norm_pre_ffn, norm_post_ffn, w_router, w_gate, w_up, w_down):
    depth, d = w_ada.shape[0], w_ada.shape[1]
    w_ada_all = w_ada.reshape(depth * d, w_ada.shape[2])
    stack = lambda w: w.reshape((depth * w.shape[1],) + w.shape[2:])
    w_gate_all, w_up_all, w_down_all = stack(w_gate), stack(w_up), stack(w_down)
    for l in range(depth):
        x = _layer(l, x, c, w_ada_all, b_ada[l], norm_pre_mix[l], norm_post_mix[l], w_in[l],
                   sink[l], sgu_ln_g[l], sgu_ln_b[l], w_s[l], b_s[l], norm_out_attn[l],
                   norm_out_gmlp[l], w_out[l], norm_pre_ffn[l], norm_post_ffn[l], w_router[l],
                   w_gate_all, w_up_all, w_down_all)
    return x
```

```python
import functools

import numpy as np
import jax
import jax.numpy as jnp
from jax import lax
from jax.experimental import pallas as pl
from jax.experimental.pallas import tpu as pltpu

N_HEADS = 8
N_KV_HEADS = 2
HEAD_DIM = 64
D_ATTN = N_HEADS * HEAD_DIM
D_KV = N_KV_HEADS * HEAD_DIM
GROUP = N_HEADS // N_KV_HEADS
WINDOW = 128
BLOCK = 128
N_GMLP_GROUPS = 8
N_EXPERTS = 16
CAPACITY_FACTOR = 2
EPS = 1e-6
MASK_VALUE = -1e30

LANES = 128
MXU_DIM = 256

TM = 1024
FC = 256
PROJ_ROWS = 256
EG = 4
QP = 1024
WIN = 64
SLOT_ALIGN = 16
VMEM_LIMIT = 56 * 1024 * 1024

BF16 = jnp.bfloat16
F32 = jnp.float32


def _rms(x, g):
    return x * lax.rsqrt(jnp.mean(x * x, axis=-1, keepdims=True) + EPS) * g


def _dot(a, b):
    return jnp.dot(a, b, preferred_element_type=F32)


def _dot_nt(a, b):
    return lax.dot_general(a, b, (((1,), (1,)), ((), ())), preferred_element_type=F32)


def _dot_tn(a, b):
    return lax.dot_general(a, b, (((0,), (0,)), ((), ())), preferred_element_type=F32)


def _adaln_kernel(c_ref, w_ref, b_ref, o_ref):
    c = c_ref[...]
    a = (c * jax.nn.sigmoid(c)).astype(BF16)
    o_ref[...] = _dot(a, w_ref[...].astype(BF16)) + b_ref[...]


def _adaln(c, w_ada_all, b_ada, layer):
    b, d = c.shape
    n = w_ada_all.shape[1]
    return pl.pallas_call(
        _adaln_kernel,
        grid=(n // d,),
        in_specs=[
            pl.BlockSpec((b, d), lambda j: (0, 0)),
            pl.BlockSpec((d, d), lambda j: (layer, j)),
            pl.BlockSpec((1, d), lambda j: (0, j)),
        ],
        out_specs=pl.BlockSpec((b, d), lambda j: (0, j)),
        out_shape=jax.ShapeDtypeStruct((b, n), F32),
        compiler_params=pltpu.CompilerParams(dimension_semantics=("arbitrary",)),
        name="adaln",
    )(c, w_ada_all, b_ada.reshape(1, n))


def _gelu_tanh(x):
    c = np.float32(np.sqrt(2.0 / np.pi))
    return x * (0.5 * (1.0 + jnp.tanh(c * (x + 0.044715 * (x * x * x)))))


def _proj_kernel(x_ref, mod_ref, g_ref, w_ref, q_ref, kv_ref, u_ref, vg_ref):
    shift = mod_ref[0, 0:1, :]
    gain = g_ref[...] * (1.0 + mod_ref[0, 1:2, :])
    n_att = D_ATTN + 2 * D_KV
    dg = u_ref.shape[1]
    lo = lax.broadcasted_iota(jnp.int32, (1, LANES), 1) < HEAD_DIM
    for r in range(0, TM, PROJ_ROWS):
        rows = slice(r, r + PROJ_ROWS)
        h = (_rms(x_ref[rows, :], gain) + shift).astype(BF16)
        z = _gelu_tanh(_dot(h, w_ref[:, n_att:]))
        u_ref[rows, :] = z[:, :dg]
        vg_ref[rows, :] = z[:, dg:]
        p = _dot(h, w_ref[:, :n_att])
        q_ref[rows, :] = (p[:, :D_ATTN] * (HEAD_DIM ** -0.5)).astype(BF16)
        for j in range(2):
            t = p[:, D_ATTN + j * D_KV:D_ATTN + (j + 1) * D_KV]
            t_sw = pltpu.roll(t, HEAD_DIM, axis=1)
            kv_ref[rows, (2 * j) * LANES:(2 * j + 1) * LANES] = jnp.where(lo, t, t_sw).astype(BF16)
            kv_ref[rows, (2 * j + 1) * LANES:(2 * j + 2) * LANES] = jnp.where(lo, t_sw, t).astype(BF16)


def _proj(x2, mod, g_pre, w_in, s):
    t, d = x2.shape
    nw = w_in.shape[1]
    dg = (nw - D_ATTN - 2 * D_KV) // 2
    nsb = s // TM
    return pl.pallas_call(
        _proj_kernel,
        grid=(t // TM,),
        in_specs=[
            pl.BlockSpec((TM, d), lambda i: (i, 0)),
            pl.BlockSpec((1, 6, d), lambda i: (i // nsb, 0, 0)),
            pl.BlockSpec((1, d), lambda i: (0, 0)),
            pl.BlockSpec((d, nw), lambda i: (0, 0)),
        ],
        out_specs=[
            pl.BlockSpec((TM, D_ATTN), lambda i: (i, 0)),
            pl.BlockSpec((TM, D_ATTN), lambda i: (i, 0)),
            pl.BlockSpec((TM, dg), lambda i: (i, 0)),
            pl.BlockSpec((TM, dg), lambda i: (i, 0)),
        ],
        out_shape=[
            jax.ShapeDtypeStruct((t, D_ATTN), BF16),
            jax.ShapeDtypeStruct((t, D_ATTN), BF16),
            jax.ShapeDtypeStruct((t, dg), F32),
            jax.ShapeDtypeStruct((t, dg), F32),
        ],
        compiler_params=pltpu.CompilerParams(
            dimension_semantics=("arbitrary",), vmem_limit_bytes=VMEM_LIMIT),
        name="proj",
    )(x2, mod, g_pre, w_in)


def _attn_bias_table():
    qi = np.arange(BLOCK)[:, None]
    kj = np.arange(3 * BLOCK)[None, :] - BLOCK
    dist = np.abs(qi - kj).astype(np.float32)
    band = dist <= WINDOW
    slopes = np.exp2(-8.0 * np.arange(1, N_HEADS + 1, dtype=np.float32) / N_HEADS).astype(np.float32)
    key_ok = [kj >= 0, np.ones_like(kj, bool), kj < BLOCK]
    tab = np.empty((3, N_HEADS, BLOCK, 3 * BLOCK), np.float32)
    for v in range(3):
        ok = band & key_ok[v]
        for h in range(N_HEADS):
            tab[v, h] = np.where(ok, -slopes[h] * dist, np.float32(MASK_VALUE))
    return tab.reshape(3, N_KV_HEADS, GROUP * BLOCK, 3 * BLOCK)


def _mix_kernel(nblk_seq, sink_ref, x_ref, q_ref, kvp_ref, kvc_ref, kvn_ref, u_ref, vg_ref,
                mod_ref, bias_ref, ws_ref, bs_ref, lng_ref, lnb_ref, ga_ref, gg_ref,
                wout_ref, gpost_ref, gpre2_ref, wr_ref,
                x1_ref, h2_ref, aff_ref, kcat, attn_s, gm_s):
    i = pl.program_id(0)
    nb_tile = TM // BLOCK
    kcat[0:BLOCK, :] = kvp_ref[...]
    kcat[BLOCK:BLOCK + TM, :] = kvc_ref[...]
    kcat[BLOCK + TM:, :] = kvn_ref[...]
    lane = lax.broadcasted_iota(jnp.int32, (1, LANES), 1)
    lo = lane < HEAD_DIM
    zero = jnp.zeros((), BF16)
    head_of_row = lax.broadcasted_iota(jnp.int32, (GROUP * BLOCK, 1), 0) // BLOCK
    sinks = []
    for g in range(N_KV_HEADS):
        col = jnp.zeros((GROUP * BLOCK, 1), F32)
        for j in range(GROUP):
            col = jnp.where(head_of_row == j, sink_ref[g * GROUP + j], col)
        sinks.append(col)

    def block_body(a, carry):
        r0 = pl.multiple_of(a * BLOCK, BLOCK)
        pos = (i * nb_tile + a) % nblk_seq
        var = jnp.where(pos == 0, 0, jnp.where(pos == nblk_seq - 1, 2, 1))
        kv = kcat[pl.ds(r0, 3 * BLOCK), :]
        for g in range(N_KV_HEADS):
            k = kv[:, g * LANES:(g + 1) * LANES]
            v = kv[:, (N_KV_HEADS + g) * LANES:(N_KV_HEADS + g + 1) * LANES]
            pieces = []
            for pi in range(GROUP // 2):
                qp = q_ref[pl.ds(r0, BLOCK), (g * GROUP // 2 + pi) * LANES:(g * GROUP // 2 + pi + 1) * LANES]
                pieces += [jnp.where(lo, qp, zero), jnp.where(lo, zero, qp)]
            s = _dot_nt(jnp.concatenate(pieces, axis=0), k) + bias_ref[var, g]
            m = jnp.maximum(jnp.max(s, axis=-1, keepdims=True), sinks[g])
            p = jnp.exp(s - m)
            den = jnp.sum(p, axis=-1, keepdims=True) + jnp.exp(sinks[g] - m)
            o = _dot(p.astype(BF16), v) / den
            for pi in range(GROUP // 2):
                even = o[(2 * pi) * BLOCK:(2 * pi + 1) * BLOCK, :]
                odd = o[(2 * pi + 1) * BLOCK:(2 * pi + 2) * BLOCK, :]
                col0 = (g * GROUP // 2 + pi) * LANES
                attn_s[pl.ds(r0, BLOCK), col0:col0 + LANES] = jnp.where(lo, even, odd)
        vg = vg_ref[pl.ds(r0, BLOCK), :]
        mu = jnp.mean(vg, axis=-1, keepdims=True)
        xc = vg - mu
        vn = (xc * lax.rsqrt(jnp.mean(xc * xc, axis=-1, keepdims=True) + EPS) * lng_ref[...]
              + lnb_ref[...]).astype(BF16)
        for pi in range(N_GMLP_GROUPS // 2):
            vnp = vn[:, pi * LANES:(pi + 1) * LANES]
            z = jnp.where(lo, _dot(ws_ref[2 * pi], vnp), _dot(ws_ref[2 * pi + 1], vnp))
            sl = slice(pi * LANES, (pi + 1) * LANES)
            gm_s[pl.ds(r0, BLOCK), sl] = u_ref[pl.ds(r0, BLOCK), sl] * (z + bs_ref[:, sl])
        return carry

    lax.fori_loop(0, nb_tile, block_body, 0, unroll=2)

    na = _rms(attn_s[...], ga_ref[...]).astype(BF16)
    ng = _rms(gm_s[...], gg_ref[...]).astype(BF16)
    mo = _dot(na, wout_ref[0:D_ATTN, :]) + _dot(ng, wout_ref[D_ATTN:, :])
    x1 = x_ref[...] + _rms(mo, gpost_ref[...] * mod_ref[0, 2:3, :])
    x1_ref[...] = x1
    h2 = (_rms(x1, gpre2_ref[...] * (1.0 + mod_ref[0, 4:5, :])) + mod_ref[0, 3:4, :]).astype(BF16)
    h2_ref[...] = h2
    lg = _dot_nt(wr_ref[...], h2)
    lg = lg - jnp.max(lg, axis=0, keepdims=True)
    ex = jnp.exp(lg)
    aff_ref[0] = ex / jnp.sum(ex, axis=0, keepdims=True)


def _mix(x2, q, kv, u, vg, mod, sink, bias, ws, bs_full, ln_g, ln_b, g_attn, g_gmlp,
         w_out, g_post, g_pre2, w_r_t, b, s):
    t, d = x2.shape
    dg = u.shape[1]
    e = w_r_t.shape[0]
    nsb = s // TM
    nb_tile = TM // BLOCK
    nblk = t // BLOCK
    row = lambda i: (i, 0)
    const2 = lambda i: (0, 0)
    in_specs = [
        pl.BlockSpec(memory_space=pltpu.SMEM),
        pl.BlockSpec((TM, d), row),
        pl.BlockSpec((TM, D_ATTN), row),
        pl.BlockSpec((BLOCK, 4 * LANES), lambda i: (jnp.maximum(i * nb_tile - 1, 0), 0)),
        pl.BlockSpec((TM, 4 * LANES), row),
        pl.BlockSpec((BLOCK, 4 * LANES), lambda i: (jnp.minimum((i + 1) * nb_tile, nblk - 1), 0)),
        pl.BlockSpec((TM, dg), row),
        pl.BlockSpec((TM, dg), row),
        pl.BlockSpec((1, 6, d), lambda i: (i // nsb, 0, 0)),
        pl.BlockSpec(bias.shape, lambda i: (0, 0, 0, 0)),
        pl.BlockSpec(ws.shape, lambda i: (0, 0, 0)),
        pl.BlockSpec((BLOCK, dg), const2),
        pl.BlockSpec((1, dg), const2),
        pl.BlockSpec((1, dg), const2),
        pl.BlockSpec((1, D_ATTN), const2),
        pl.BlockSpec((1, dg), const2),
        pl.BlockSpec(w_out.shape, const2),
        pl.BlockSpec((1, d), const2),
        pl.BlockSpec((1, d), const2),
        pl.BlockSpec(w_r_t.shape, const2),
    ]
    out_specs = [
        pl.BlockSpec((TM, d), row),
        pl.BlockSpec((TM, d), row),
        pl.BlockSpec((1, e, TM), lambda i: (i // nsb, 0, i % nsb)),
    ]
    out_shape = [
        jax.ShapeDtypeStruct((t, d), F32),
        jax.ShapeDtypeStruct((t, d), BF16),
        jax.ShapeDtypeStruct((b, e, s), F32),
    ]
    return pl.pallas_call(
        functools.partial(_mix_kernel, s // BLOCK),
        grid=(t // TM,),
        in_specs=in_specs,
        out_specs=out_specs,
        out_shape=out_shape,
        scratch_shapes=[
            pltpu.VMEM((TM + 2 * BLOCK, 4 * LANES), BF16),
            pltpu.VMEM((TM, D_ATTN), F32),
            pltpu.VMEM((TM, dg), F32),
        ],
        compiler_params=pltpu.CompilerParams(
            dimension_semantics=("arbitrary",), vmem_limit_bytes=VMEM_LIMIT),
        name="mix",
    )(sink, x2, q, kv, kv, kv, u, vg, mod, bias, ws, bs_full, ln_g, ln_b, g_attn, g_gmlp,
      w_out, g_post, g_pre2, w_r_t)


def _route_kernel(cap, aff_ref, slot_ref, affb_ref, cnt_ref):
    aff = aff_ref[...]
    e, s = aff.shape
    capf = jnp.float32(cap)

    def count_ge(t):
        return jnp.sum(jnp.where(aff >= t, 1.0, 0.0), axis=1, keepdims=True)

    def bit_body(it, cand):
        trial = cand | jnp.left_shift(jnp.int32(1), 30 - it)
        t = lax.bitcast_convert_type(trial, F32)
        return jnp.where(count_ge(t) >= capf, trial, cand)

    cand = lax.fori_loop(0, 31, bit_body, jnp.zeros((e, 1), jnp.int32))
    thr = lax.bitcast_convert_type(cand, F32)
    need = capf - jnp.sum(jnp.where(aff > thr, 1.0, 0.0), axis=1, keepdims=True)

    r = lax.broadcasted_iota(jnp.int32, (LANES, LANES), 0)
    c = lax.broadcasted_iota(jnp.int32, (LANES, LANES), 1)
    tri = jnp.where(r < c, 1.0, 0.0).astype(BF16)
    lane = lax.broadcasted_iota(jnp.int32, (e, LANES), 1)
    carry_eq = jnp.zeros((e, 1), F32)
    carry_sel = jnp.zeros((e, 1), F32)
    cnt = jnp.zeros((e, LANES), F32)
    per_blk = MXU_DIM // LANES
    for j in range(s // LANES):
        a = aff[:, j * LANES:(j + 1) * LANES]
        gt = a > thr
        eq = jnp.where(a == thr, 1.0, 0.0)
        tie_rank = _dot(eq.astype(BF16), tri) + carry_eq
        sel = jnp.where(gt | ((eq > 0.0) & (tie_rank < need)), 1.0, 0.0)
        slot = _dot(sel.astype(BF16), tri) + carry_sel
        blk = (slice(None), j // per_blk, slice((j % per_blk) * LANES, (j % per_blk + 1) * LANES))
        slot_ref[blk] = jnp.where(sel > 0.0, slot, -1.0).astype(jnp.int32)
        affb_ref[blk] = a
        if j % per_blk == 0:
            cnt = jnp.where(lane == j // per_blk, carry_sel, cnt)
        carry_eq = carry_eq + jnp.sum(eq, axis=1, keepdims=True)
        carry_sel = carry_sel + jnp.sum(sel, axis=1, keepdims=True)
    cnt = jnp.where(lane >= s // MXU_DIM, carry_sel, cnt)
    cnt_ref[...] = cnt.astype(jnp.int32)


def _route(aff2, cap):
    r, s = aff2.shape
    nblk = s // MXU_DIM
    return pl.pallas_call(
        functools.partial(_route_kernel, cap),
        grid=(1,),
        in_specs=[pl.BlockSpec((r, s), lambda i: (0, 0))],
        out_specs=[
            pl.BlockSpec((r, nblk, MXU_DIM), lambda i: (0, 0, 0)),
            pl.BlockSpec((r, nblk, MXU_DIM), lambda i: (0, 0, 0)),
            pl.BlockSpec((r, LANES), lambda i: (0, 0)),
        ],
        out_shape=[
            jax.ShapeDtypeStruct((r, nblk, MXU_DIM), jnp.int32),
            jax.ShapeDtypeStruct((r, nblk, MXU_DIM), F32),
            jax.ShapeDtypeStruct((r, LANES), jnp.int32),
        ],
        compiler_params=pltpu.CompilerParams(dimension_semantics=("arbitrary",)),
        name="route",
    )(aff2)


def _segment(cnt_ref, row, pb, cap):
    c0 = jnp.minimum(cnt_ref[row, pb], cap)
    c1 = jnp.minimum(cnt_ref[row, pb + 1], cap)
    a0 = jnp.minimum(jnp.bitwise_and(c0, -SLOT_ALIGN), cap - WIN)
    return c0, c1, pl.multiple_of(a0, SLOT_ALIGN)


def _extra_windows(c1, a0):
    return (jnp.maximum(c1 - (a0 + WIN), 0) + WIN - 1) // WIN


def _gather_kernel(n_exp, cnt_ref, h2_ref, slot_ref, aff_ref, x_ref, gs_ref):
    b, eg = pl.program_id(0), pl.program_id(1)
    ng, cap = x_ref.shape[1], x_ref.shape[2]
    nblk = slot_ref.shape[2]
    sub = lax.broadcasted_iota(jnp.int32, (WIN, MXU_DIM), 0)
    x_ref[...] = jnp.zeros_like(x_ref)
    gs_ref[...] = jnp.zeros_like(gs_ref)

    def merge(k, a0, rows, gate):
        win = (0, k, pl.ds(a0, WIN), slice(None))
        x_ref[win] += rows.astype(BF16)
        gs_ref[win] += gate

    def tokens(pb):
        return pl.ds(pl.multiple_of(pb * MXU_DIM, MXU_DIM), MXU_DIM)

    def first_windows(pb):
        segs, gates, onehots = [], [], []
        overflow = jnp.int32(0)
        for k in range(ng):
            c0, c1, a0 = _segment(cnt_ref, b * n_exp + eg * ng + k, pb, cap)
            match = slot_ref[0, k, pl.ds(pb, 1), :] == (sub + a0)
            onehots.append(jnp.where(match, 1.0, 0.0).astype(BF16))
            gates.append(jnp.sum(jnp.where(match, aff_ref[0, k, pl.ds(pb, 1), :], 0.0),
                                 axis=1, keepdims=True))
            segs.append((c0, c1, a0))
            overflow = overflow + jnp.maximum(c1 - (a0 + WIN), 0)
        rows = _dot(jnp.concatenate(onehots, axis=0), h2_ref[0, tokens(pb), :])
        for k in range(ng):
            merge(k, segs[k][2], rows[k * WIN:(k + 1) * WIN, :], gates[k])
        return overflow

    def further_windows(pb):
        def per_expert(k, carry2):
            _, c1, a0 = _segment(cnt_ref, b * n_exp + eg * ng + k, pb, cap)

            def per_window(w, carry3):
                first = a0 + (w + 1) * WIN
                aw = pl.multiple_of(jnp.minimum(first, cap - WIN), SLOT_ALIGN)
                srow = slot_ref[0, k, pl.ds(pb, 1), :]
                match = (srow == (sub + aw)) & (srow >= first)
                onehot = jnp.where(match, 1.0, 0.0).astype(BF16)
                gate = jnp.sum(jnp.where(match, aff_ref[0, k, pl.ds(pb, 1), :], 0.0),
                               axis=1, keepdims=True)
                merge(k, aw, _dot(onehot, h2_ref[0, tokens(pb), :]), gate)
                return carry3

            return lax.fori_loop(0, _extra_windows(c1, a0), per_window, carry2)

        lax.fori_loop(0, ng, per_expert, 0)

    def pair_body(i, carry):
        overflows = [first_windows(2 * i + j) for j in range(2)]
        for j in range(2):
            pl.when(overflows[j] > 0)(functools.partial(further_windows, 2 * i + j))
        return carry

    lax.fori_loop(0, nblk // 2, pair_body, 0)


def _gather(cnt2, h2, slots4, aff4, cap):
    b, s, d = h2.shape
    e = slots4.shape[1]
    nblk = s // MXU_DIM
    grid_spec = pltpu.PrefetchScalarGridSpec(
        num_scalar_prefetch=1,
        grid=(b, e // EG),
        in_specs=[
            pl.BlockSpec((1, s, d), lambda bi, gi, c: (bi, 0, 0)),
            pl.BlockSpec((1, EG, nblk, MXU_DIM), lambda bi, gi, c: (bi, gi, 0, 0)),
            pl.BlockSpec((1, EG, nblk, MXU_DIM), lambda bi, gi, c: (bi, gi, 0, 0)),
        ],
        out_specs=[
            pl.BlockSpec((1, EG, cap, d), lambda bi, gi, c: (bi, gi, 0, 0)),
            pl.BlockSpec((1, EG, cap, LANES), lambda bi, gi, c: (bi, gi, 0, 0)),
        ],
    )
    return pl.pallas_call(
        functools.partial(_gather_kernel, e),
        grid_spec=grid_spec,
        out_shape=[
            jax.ShapeDtypeStruct((b, e, cap, d), BF16),
            jax.ShapeDtypeStruct((b, e, cap, LANES), F32),
        ],
        compiler_params=pltpu.CompilerParams(
            dimension_semantics=("arbitrary", "arbitrary"), vmem_limit_bytes=VMEM_LIMIT),
        name="gather",
    )(cnt2, h2, slots4, aff4)


def _ffn_kernel(nchunk, x_ref, gs_ref, wg0_ref, wu0_ref, wd0_ref, wg1_ref, wu1_ref, wd1_ref,
                out_ref, acc_s):
    e, f = pl.program_id(0), pl.program_id(1)
    nb, _, cap, d = x_ref.shape
    last = pl.num_programs(1) - 1

    @pl.when((e == 0) & (f == 0))
    def _init():
        acc_s[...] = jnp.zeros_like(acc_s)

    def accumulate(chunks):
        wg = jnp.concatenate([w[0][0].astype(BF16) for w in chunks], axis=1)
        wu = jnp.concatenate([w[1][0].astype(BF16) for w in chunks], axis=1)
        wd = jnp.concatenate([w[2][0].astype(BF16) for w in chunks], axis=0)
        for bi in range(nb):
            x = x_ref[bi, 0]
            g = _dot(x, wg)
            u = _dot(x, wu)
            hid = (g * jax.nn.sigmoid(g) * u).astype(BF16)
            y = _dot(hid, wd)
            rows = slice(bi * cap, (bi + 1) * cap)
            acc_s[rows, :] = jnp.where(f == 0, y, acc_s[rows, :] + y)

    first, second = (wg0_ref, wu0_ref, wd0_ref), (wg1_ref, wu1_ref, wd1_ref)
    if nchunk % 2 == 0:
        accumulate([first, second])
    else:
        pl.when(f < last)(lambda: accumulate([first, second]))
        pl.when(f == last)(lambda: accumulate([first]))

    @pl.when(f == last)
    def _emit():
        gate = gs_ref[:, 0, :, 0:1].reshape(nb * cap, 1)
        out_ref[:, 0] = (acc_s[...] * gate).reshape(nb, cap, d).astype(BF16)


def _ffn(xg, gs, w_gate, w_up, w_down, e0):
    b, e, cap, d = xg.shape
    nchunk = w_gate.shape[2] // FC
    second = lambda fi: jnp.minimum(2 * fi + 1, nchunk - 1)
    return pl.pallas_call(
        functools.partial(_ffn_kernel, nchunk),
        grid=(e, (nchunk + 1) // 2),
        in_specs=[
            pl.BlockSpec((b, 1, cap, d), lambda ei, fi: (0, ei, 0, 0)),
            pl.BlockSpec((b, 1, cap, LANES), lambda ei, fi: (0, ei, 0, 0)),
            pl.BlockSpec((1, d, FC), lambda ei, fi: (e0 + ei, 0, 2 * fi)),
            pl.BlockSpec((1, d, FC), lambda ei, fi: (e0 + ei, 0, 2 * fi)),
            pl.BlockSpec((1, FC, d), lambda ei, fi: (e0 + ei, 2 * fi, 0)),
            pl.BlockSpec((1, d, FC), lambda ei, fi: (e0 + ei, 0, second(fi))),
            pl.BlockSpec((1, d, FC), lambda ei, fi: (e0 + ei, 0, second(fi))),
            pl.BlockSpec((1, FC, d), lambda ei, fi: (e0 + ei, second(fi), 0)),
        ],
        out_specs=pl.BlockSpec((b, 1, cap, d), lambda ei, fi: (0, ei, 0, 0)),
        out_shape=jax.ShapeDtypeStruct((b, e, cap, d), BF16),
        scratch_shapes=[pltpu.VMEM((b * cap, d), F32)],
        compiler_params=pltpu.CompilerParams(
            dimension_semantics=("arbitrary", "arbitrary"), vmem_limit_bytes=VMEM_LIMIT),
        name="ffn",
    )(xg, gs, w_gate, w_up, w_down, w_gate, w_up, w_down)


def _combine_kernel(n_exp, cnt_ref, eo_ref, slot_ref, x1_ref, mod_ref, g_ref, o_ref,
                    stage_s, oh_s, y_s):
    b, q = pl.program_id(0), pl.program_id(1)
    cap = eo_ref.shape[2]
    nblk_q = o_ref.shape[0] // MXU_DIM
    sub = lax.broadcasted_iota(jnp.int32, (WIN, MXU_DIM), 0)
    gain = g_ref[...] * mod_ref[0, 5:6, :]

    for j in range(nblk_q):
        pb = q * nblk_q + j
        overflow = jnp.int32(0)
        for e in range(n_exp):
            _, c1, a0 = _segment(cnt_ref, b * n_exp + e, pb, cap)
            stage_s[e * WIN:(e + 1) * WIN, :] = eo_ref[0, e, pl.ds(a0, WIN), :]
            match = slot_ref[0, e, pl.ds(pb, 1), :] == (sub + a0)
            oh_s[e * WIN:(e + 1) * WIN, :] = jnp.where(match, 1.0, 0.0).astype(BF16)
            overflow = overflow + jnp.maximum(c1 - (a0 + WIN), 0)
        y_s[...] = _dot_tn(oh_s[...], stage_s[...])

        @pl.when(overflow > 0)
        def _long_segments(j=j, pb=pb):
            def per_expert(e, carry):
                _, c1, a0 = _segment(cnt_ref, b * n_exp + e, pb, cap)

                def per_window(k, carry2):
                    first = a0 + (k + 1) * WIN
                    aw = pl.multiple_of(jnp.minimum(first, cap - WIN), SLOT_ALIGN)
                    srow = slot_ref[0, e, pl.ds(pb, 1), :]
                    match = (srow == (sub + aw)) & (srow >= first)
                    onehot = jnp.where(match, 1.0, 0.0).astype(BF16)
                    y_s[...] += _dot_tn(onehot, eo_ref[0, e, pl.ds(aw, WIN), :])
                    return carry2

                return lax.fori_loop(0, _extra_windows(c1, a0), per_window, carry)

            lax.fori_loop(0, n_exp, per_expert, 0)

        sl = slice(j * MXU_DIM, (j + 1) * MXU_DIM)
        o_ref[sl, :] = x1_ref[sl, :] + _rms(y_s[...], gain)


def _combine(cnt2, eo, slots4, x1, mod, g_post):
    b, e, cap, d = eo.shape
    t = x1.shape[0]
    s = t // b
    nq = s // QP
    nblk_q = QP // MXU_DIM
    grid_spec = pltpu.PrefetchScalarGridSpec(
        num_scalar_prefetch=1,
        grid=(b, nq),
        in_specs=[
            pl.BlockSpec((1, e, cap, d), lambda bi, qi, c: (bi, 0, 0, 0), pipeline_mode=pl.Buffered(1)),
            pl.BlockSpec((1, e, s // MXU_DIM, MXU_DIM), lambda bi, qi, c: (bi, 0, 0, 0)),
            pl.BlockSpec((QP, d), lambda bi, qi, c: (bi * nq + qi, 0)),
            pl.BlockSpec((1, 6, d), lambda bi, qi, c: (bi, 0, 0)),
            pl.BlockSpec((1, d), lambda bi, qi, c: (0, 0)),
        ],
        out_specs=pl.BlockSpec((QP, d), lambda bi, qi, c: (bi * nq + qi, 0)),
        scratch_shapes=[
            pltpu.VMEM((e * WIN, d), BF16),
            pltpu.VMEM((e * WIN, MXU_DIM), BF16),
            pltpu.VMEM((MXU_DIM, d), F32),
        ],
    )
    return pl.pallas_call(
        functools.partial(_combine_kernel, e),
        grid_spec=grid_spec,
        out_shape=jax.ShapeDtypeStruct((t, d), F32),
        compiler_params=pltpu.CompilerParams(
            dimension_semantics=("arbitrary", "arbitrary"),
            vmem_limit_bytes=VMEM_LIMIT),
        name="combine",
    )(cnt2, eo, slots4, x1, mod, g_post)


def _layer(layer, x, c, w_ada_all, b_ada, norm_pre_mix, norm_post_mix, w_in, sink, sgu_ln_g,
           sgu_ln_b, w_s, b_s, norm_out_attn, norm_out_gmlp, w_out, norm_pre_ffn, norm_post_ffn,
           w_router, w_gate_all, w_up_all, w_down_all):
    b, s, d = x.shape
    t = b * s
    dg = sgu_ln_g.shape[0]
    e = w_router.shape[1]
    cap = CAPACITY_FACTOR * s // e
    assert s % QP == 0 and s % TM == 0 and w_gate_all.shape[2] % FC == 0 and e % EG == 0
    assert cap % SLOT_ALIGN == 0 and cap >= WIN and s // MXU_DIM < LANES and d == D_ATTN + dg
    assert (s // MXU_DIM) % 2 == 0

    mod = _adaln(c, w_ada_all, b_ada, layer).reshape(b, 6, d)

    x2 = x.reshape(t, d)
    q, kv, u, vg = _proj(x2, mod, norm_pre_mix.reshape(1, d), w_in.astype(BF16), s)

    bias = jnp.asarray(_attn_bias_table())
    bs_full = jnp.repeat(b_s.T, dg // N_GMLP_GROUPS, axis=1)
    x1, h2, aff_t = _mix(
        x2, q, kv, u, vg, mod, sink, bias, w_s.astype(BF16), bs_full,
        sgu_ln_g.reshape(1, dg), sgu_ln_b.reshape(1, dg), norm_out_attn.reshape(1, D_ATTN),
        norm_out_gmlp.reshape(1, dg), w_out.astype(BF16), norm_post_mix.reshape(1, d),
        norm_pre_ffn.reshape(1, d), w_router.T.astype(BF16), b, s)

    slots3, aff3, cnt2 = _route(aff_t.reshape(b * e, s), cap)
    nblk = s // MXU_DIM
    slots4 = slots3.reshape(b, e, nblk, MXU_DIM)
    aff4 = aff3.reshape(b, e, nblk, MXU_DIM)
    xg, gs = _gather(cnt2, h2.reshape(b, s, d), slots4, aff4, cap)
    eo = _ffn(xg, gs, w_gate_all, w_up_all, w_down_all, layer * e)
    out = _combine(cnt2, eo, slots4, x1, mod, norm_post_ffn.reshape(1, d))
    return out.reshape(b, s, d)


def kernel(x, c, w_ada, b_ada, norm_pre_mix, norm_post_mix, w_in, sink, sgu_ln_g, sgu_ln_b, w_s, b_s, norm_out_attn, norm_out_gmlp, w_out, norm_pre_ffn, norm_post_ffn, w_router, w_gate, w_up, w_down):
    depth, d = w_ada.shape[0], w_ada.shape[1]
    w_ada_all = w_ada.reshape(depth * d, w_ada.shape[2])
    stack = lambda w: w.reshape((depth * w.shape[1],) + w.shape[2:])
    w_gate_all, w_up_all, w_down_all = stack(w_gate), stack(w_up), stack(w_down)
    for l in range(depth):
        x = _layer(l, x, c, w_ada_all, b_ada[l], norm_pre_mix[l], norm_post_mix[l], w_in[l],
                   sink[l], sgu_ln_g[l], sgu_ln_b[l], w_s[l], b_s[l], norm_out_attn[l],
                   norm_out_gmlp[l], w_out[l], norm_pre_ffn[l], norm_post_ffn[l], w_router[l],
                   w_gate_all, w_up_all, w_down_all)
    return x
```

```python
import functools

import numpy as np
import jax
import jax.numpy as jnp
from jax import lax
from jax.experimental import pallas as pl
from jax.experimental.pallas import tpu as pltpu

N_HEADS = 8
N_KV_HEADS = 2
HEAD_DIM = 64
D_ATTN = N_HEADS * HEAD_DIM
D_KV = N_KV_HEADS * HEAD_DIM
GROUP = N_HEADS // N_KV_HEADS
WINDOW = 128
BLOCK = 128
N_GMLP_GROUPS = 8
N_EXPERTS = 16
CAPACITY_FACTOR = 2
EPS = 1e-6
MASK_VALUE = -1e30

LANES = 128
MXU_DIM = 256

TM = 1024
FC = 256
FFN_CHUNKS = 3
PROJ_ROWS = 256
EG = 4
QP = 1024
WIN = 64
SLOT_ALIGN = 16
VMEM_LIMIT = 56 * 1024 * 1024

BF16 = jnp.bfloat16
F32 = jnp.float32


def _rms(x, g):
    return x * lax.rsqrt(jnp.mean(x * x, axis=-1, keepdims=True) + EPS) * g


def _dot(a, b):
    return jnp.dot(a, b, preferred_element_type=F32)


def _dot_nt(a, b):
    return lax.dot_general(a, b, (((1,), (1,)), ((), ())), preferred_element_type=F32)


def _dot_tn(a, b):
    return lax.dot_general(a, b, (((0,), (0,)), ((), ())), preferred_element_type=F32)


def _adaln_kernel(c_ref, w_ref, b_ref, o_ref):
    c = c_ref[...]
    a = (c * jax.nn.sigmoid(c)).astype(BF16)
    o_ref[...] = _dot(a, w_ref[...].astype(BF16)) + b_ref[...]


def _adaln(c, w_ada_all, b_ada, layer):
    b, d = c.shape
    n = w_ada_all.shape[1]
    return pl.pallas_call(
        _adaln_kernel,
        grid=(n // d,),
        in_specs=[
            pl.BlockSpec((b, d), lambda j: (0, 0)),
            pl.BlockSpec((d, d), lambda j: (layer, j)),
            pl.BlockSpec((1, d), lambda j: (0, j)),
        ],
        out_specs=pl.BlockSpec((b, d), lambda j: (0, j)),
        out_shape=jax.ShapeDtypeStruct((b, n), F32),
        compiler_params=pltpu.CompilerParams(dimension_semantics=("arbitrary",)),
        name="adaln",
    )(c, w_ada_all, b_ada.reshape(1, n))


def _gelu_tanh(x):
    c = np.float32(np.sqrt(2.0 / np.pi))
    return x * (0.5 * (1.0 + jnp.tanh(c * (x + 0.044715 * (x * x * x)))))


def _proj_kernel(x_ref, mod_ref, g_ref, w_ref, q_ref, kv_ref, u_ref, vg_ref):
    shift = mod_ref[0, 0:1, :]
    gain = g_ref[...] * (1.0 + mod_ref[0, 1:2, :])
    n_att = D_ATTN + 2 * D_KV
    dg = u_ref.shape[1]
    lo = lax.broadcasted_iota(jnp.int32, (1, LANES), 1) < HEAD_DIM
    for r in range(0, TM, PROJ_ROWS):
        rows = slice(r, r + PROJ_ROWS)
        h = (_rms(x_ref[rows, :], gain) + shift).astype(BF16)
        z = _gelu_tanh(_dot(h, w_ref[:, n_att:]))
        u_ref[rows, :] = z[:, :dg]
        vg_ref[rows, :] = z[:, dg:]
        p = _dot(h, w_ref[:, :n_att])
        q_ref[rows, :] = (p[:, :D_ATTN] * (HEAD_DIM ** -0.5)).astype(BF16)
        for j in range(2):
            t = p[:, D_ATTN + j * D_KV:D_ATTN + (j + 1) * D_KV]
            t_sw = pltpu.roll(t, HEAD_DIM, axis=1)
            kv_ref[rows, (2 * j) * LANES:(2 * j + 1) * LANES] = jnp.where(lo, t, t_sw).astype(BF16)
            kv_ref[rows, (2 * j + 1) * LANES:(2 * j + 2) * LANES] = jnp.where(lo, t_sw, t).astype(BF16)


def _proj(x2, mod, g_pre, w_in, s):
    t, d = x2.shape
    nw = w_in.shape[1]
    dg = (nw - D_ATTN - 2 * D_KV) // 2
    nsb = s // TM
    return pl.pallas_call(
        _proj_kernel,
        grid=(t // TM,),
        in_specs=[
            pl.BlockSpec((TM, d), lambda i: (i, 0)),
            pl.BlockSpec((1, 6, d), lambda i: (i // nsb, 0, 0)),
            pl.BlockSpec((1, d), lambda i: (0, 0)),
            pl.BlockSpec((d, nw), lambda i: (0, 0)),
        ],
        out_specs=[
            pl.BlockSpec((TM, D_ATTN), lambda i: (i, 0)),
            pl.BlockSpec((TM, D_ATTN), lambda i: (i, 0)),
            pl.BlockSpec((TM, dg), lambda i: (i, 0)),
            pl.BlockSpec((TM, dg), lambda i: (i, 0)),
        ],
        out_shape=[
            jax.ShapeDtypeStruct((t, D_ATTN), BF16),
            jax.ShapeDtypeStruct((t, D_ATTN), BF16),
            jax.ShapeDtypeStruct((t, dg), F32),
            jax.ShapeDtypeStruct((t, dg), F32),
        ],
        compiler_params=pltpu.CompilerParams(
            dimension_semantics=("arbitrary",), vmem_limit_bytes=VMEM_LIMIT),
        name="proj",
    )(x2, mod, g_pre, w_in)


def _attn_bias_table():
    qi = np.arange(BLOCK)[:, None]
    kj = np.arange(3 * BLOCK)[None, :] - BLOCK
    dist = np.abs(qi - kj).astype(np.float32)
    band = dist <= WINDOW
    slopes = np.exp2(-8.0 * np.arange(1, N_HEADS + 1, dtype=np.float32) / N_HEADS).astype(np.float32)
    key_ok = [kj >= 0, np.ones_like(kj, bool), kj < BLOCK]
    tab = np.empty((3, N_HEADS, BLOCK, 3 * BLOCK), np.float32)
    for v in range(3):
        ok = band & key_ok[v]
        for h in range(N_HEADS):
            tab[v, h] = np.where(ok, -slopes[h] * dist, np.float32(MASK_VALUE))
    return tab.reshape(3, N_KV_HEADS, GROUP * BLOCK, 3 * BLOCK)


def _mix_kernel(nblk_seq, sink_ref, x_ref, q_ref, kvp_ref, kvc_ref, kvn_ref, u_ref, vg_ref,
                mod_ref, bias_ref, ws_ref, bs_ref, lng_ref, lnb_ref, ga_ref, gg_ref,
                wout_ref, gpost_ref, gpre2_ref, wr_ref,
                x1_ref, h2_ref, aff_ref, kcat, attn_s, gm_s):
    i = pl.program_id(0)
    nb_tile = TM // BLOCK
    kcat[0:BLOCK, :] = kvp_ref[...]
    kcat[BLOCK:BLOCK + TM, :] = kvc_ref[...]
    kcat[BLOCK + TM:, :] = kvn_ref[...]
    lane = lax.broadcasted_iota(jnp.int32, (1, LANES), 1)
    lo = lane < HEAD_DIM
    zero = jnp.zeros((), BF16)
    head_of_row = lax.broadcasted_iota(jnp.int32, (GROUP * BLOCK, 1), 0) // BLOCK
    sinks = []
    for g in range(N_KV_HEADS):
        col = jnp.zeros((GROUP * BLOCK, 1), F32)
        for j in range(GROUP):
            col = jnp.where(head_of_row == j, sink_ref[g * GROUP + j], col)
        sinks.append(col)

    def block_body(a, carry):
        r0 = pl.multiple_of(a * BLOCK, BLOCK)
        pos = (i * nb_tile + a) % nblk_seq
        var = jnp.where(pos == 0, 0, jnp.where(pos == nblk_seq - 1, 2, 1))
        kv = kcat[pl.ds(r0, 3 * BLOCK), :]
        for g in range(N_KV_HEADS):
            k = kv[:, g * LANES:(g + 1) * LANES]
            v = kv[:, (N_KV_HEADS + g) * LANES:(N_KV_HEADS + g + 1) * LANES]
            pieces = []
            for pi in range(GROUP // 2):
                qp = q_ref[pl.ds(r0, BLOCK), (g * GROUP // 2 + pi) * LANES:(g * GROUP // 2 + pi + 1) * LANES]
                pieces += [jnp.where(lo, qp, zero), jnp.where(lo, zero, qp)]
            s = _dot_nt(jnp.concatenate(pieces, axis=0), k) + bias_ref[var, g]
            m = jnp.maximum(jnp.max(s, axis=-1, keepdims=True), sinks[g])
            p = jnp.exp(s - m)
            den = jnp.sum(p, axis=-1, keepdims=True) + jnp.exp(sinks[g] - m)
            o = _dot(p.astype(BF16), v) / den
            for pi in range(GROUP // 2):
                even = o[(2 * pi) * BLOCK:(2 * pi + 1) * BLOCK, :]
                odd = o[(2 * pi + 1) * BLOCK:(2 * pi + 2) * BLOCK, :]
                col0 = (g * GROUP // 2 + pi) * LANES
                attn_s[pl.ds(r0, BLOCK), col0:col0 + LANES] = jnp.where(lo, even, odd)
        vg = vg_ref[pl.ds(r0, BLOCK), :]
        mu = jnp.mean(vg, axis=-1, keepdims=True)
        xc = vg - mu
        vn = (xc * lax.rsqrt(jnp.mean(xc * xc, axis=-1, keepdims=True) + EPS) * lng_ref[...]
              + lnb_ref[...]).astype(BF16)
        for pi in range(N_GMLP_GROUPS // 2):
            vnp = vn[:, pi * LANES:(pi + 1) * LANES]
            z = jnp.where(lo, _dot(ws_ref[2 * pi], vnp), _dot(ws_ref[2 * pi + 1], vnp))
            sl = slice(pi * LANES, (pi + 1) * LANES)
            gm_s[pl.ds(r0, BLOCK), sl] = u_ref[pl.ds(r0, BLOCK), sl] * (z + bs_ref[:, sl])
        return carry

    lax.fori_loop(0, nb_tile, block_body, 0, unroll=2)

    na = _rms(attn_s[...], ga_ref[...]).astype(BF16)
    ng = _rms(gm_s[...], gg_ref[...]).astype(BF16)
    mo = _dot(na, wout_ref[0:D_ATTN, :]) + _dot(ng, wout_ref[D_ATTN:, :])
    x1 = x_ref[...] + _rms(mo, gpost_ref[...] * mod_ref[0, 2:3, :])
    x1_ref[...] = x1
    h2 = (_rms(x1, gpre2_ref[...] * (1.0 + mod_ref[0, 4:5, :])) + mod_ref[0, 3:4, :]).astype(BF16)
    h2_ref[...] = h2
    lg = _dot_nt(wr_ref[...], h2)
    lg = lg - jnp.max(lg, axis=0, keepdims=True)
    ex = jnp.exp(lg)
    aff_ref[0] = ex / jnp.sum(ex, axis=0, keepdims=True)


def _mix(x2, q, kv, u, vg, mod, sink, bias, ws, bs_full, ln_g, ln_b, g_attn, g_gmlp,
         w_out, g_post, g_pre2, w_r_t, b, s):
    t, d = x2.shape
    dg = u.shape[1]
    e = w_r_t.shape[0]
    nsb = s // TM
    nb_tile = TM // BLOCK
    nblk = t // BLOCK
    row = lambda i: (i, 0)
    const2 = lambda i: (0, 0)
    in_specs = [
        pl.BlockSpec(memory_space=pltpu.SMEM),
        pl.BlockSpec((TM, d), row),
        pl.BlockSpec((TM, D_ATTN), row),
        pl.BlockSpec((BLOCK, 4 * LANES), lambda i: (jnp.maximum(i * nb_tile - 1, 0), 0)),
        pl.BlockSpec((TM, 4 * LANES), row),
        pl.BlockSpec((BLOCK, 4 * LANES), lambda i: (jnp.minimum((i + 1) * nb_tile, nblk - 1), 0)),
        pl.BlockSpec((TM, dg), row),
        pl.BlockSpec((TM, dg), row),
        pl.BlockSpec((1, 6, d), lambda i: (i // nsb, 0, 0)),
        pl.BlockSpec(bias.shape, lambda i: (0, 0, 0, 0)),
        pl.BlockSpec(ws.shape, lambda i: (0, 0, 0)),
        pl.BlockSpec((BLOCK, dg), const2),
        pl.BlockSpec((1, dg), const2),
        pl.BlockSpec((1, dg), const2),
        pl.BlockSpec((1, D_ATTN), const2),
        pl.BlockSpec((1, dg), const2),
        pl.BlockSpec(w_out.shape, const2),
        pl.BlockSpec((1, d), const2),
        pl.BlockSpec((1, d), const2),
        pl.BlockSpec(w_r_t.shape, const2),
    ]
    out_specs = [
        pl.BlockSpec((TM, d), row),
        pl.BlockSpec((TM, d), row),
        pl.BlockSpec((1, e, TM), lambda i: (i // nsb, 0, i % nsb)),
    ]
    out_shape = [
        jax.ShapeDtypeStruct((t, d), F32),
        jax.ShapeDtypeStruct((t, d), BF16),
        jax.ShapeDtypeStruct((b, e, s), F32),
    ]
    return pl.pallas_call(
        functools.partial(_mix_kernel, s // BLOCK),
        grid=(t // TM,),
        in_specs=in_specs,
        out_specs=out_specs,
        out_shape=out_shape,
        scratch_shapes=[
            pltpu.VMEM((TM + 2 * BLOCK, 4 * LANES), BF16),
            pltpu.VMEM((TM, D_ATTN), F32),
            pltpu.VMEM((TM, dg), F32),
        ],
        compiler_params=pltpu.CompilerParams(
            dimension_semantics=("arbitrary",), vmem_limit_bytes=VMEM_LIMIT),
        name="mix",
    )(sink, x2, q, kv, kv, kv, u, vg, mod, bias, ws, bs_full, ln_g, ln_b, g_attn, g_gmlp,
      w_out, g_post, g_pre2, w_r_t)


def _route_kernel(cap, aff_ref, slot_ref, affb_ref, cnt_ref):
    aff = aff_ref[...]
    e, s = aff.shape
    capf = jnp.float32(cap)

    def count_ge(t):
        return jnp.sum(jnp.where(aff >= t, 1.0, 0.0), axis=1, keepdims=True)

    def bit_body(it, cand):
        trial = cand | jnp.left_shift(jnp.int32(1), 30 - it)
        t = lax.bitcast_convert_type(trial, F32)
        return jnp.where(count_ge(t) >= capf, trial, cand)

    cand = lax.fori_loop(0, 31, bit_body, jnp.zeros((e, 1), jnp.int32))
    thr = lax.bitcast_convert_type(cand, F32)
    need = capf - jnp.sum(jnp.where(aff > thr, 1.0, 0.0), axis=1, keepdims=True)

    r = lax.broadcasted_iota(jnp.int32, (LANES, LANES), 0)
    c = lax.broadcasted_iota(jnp.int32, (LANES, LANES), 1)
    tri = jnp.where(r < c, 1.0, 0.0).astype(BF16)
    lane = lax.broadcasted_iota(jnp.int32, (e, LANES), 1)
    carry_eq = jnp.zeros((e, 1), F32)
    carry_sel = jnp.zeros((e, 1), F32)
    cnt = jnp.zeros((e, LANES), F32)
    per_blk = MXU_DIM // LANES
    for j in range(s // LANES):
        a = aff[:, j * LANES:(j + 1) * LANES]
        gt = a > thr
        eq = jnp.where(a == thr, 1.0, 0.0)
        tie_rank = _dot(eq.astype(BF16), tri) + carry_eq
        sel = jnp.where(gt | ((eq > 0.0) & (tie_rank < need)), 1.0, 0.0)
        slot = _dot(sel.astype(BF16), tri) + carry_sel
        blk = (slice(None), j // per_blk, slice((j % per_blk) * LANES, (j % per_blk + 1) * LANES))
        slot_ref[blk] = jnp.where(sel > 0.0, slot, -1.0).astype(jnp.int32)
        affb_ref[blk] = a
        if j % per_blk == 0:
            cnt = jnp.where(lane == j // per_blk, carry_sel, cnt)
        carry_eq = carry_eq + jnp.sum(eq, axis=1, keepdims=True)
        carry_sel = carry_sel + jnp.sum(sel, axis=1, keepdims=True)
    cnt = jnp.where(lane >= s // MXU_DIM, carry_sel, cnt)
    cnt_ref[...] = cnt.astype(jnp.int32)


def _route(aff2, cap):
    r, s = aff2.shape
    nblk = s // MXU_DIM
    return pl.pallas_call(
        functools.partial(_route_kernel, cap),
        grid=(1,),
        in_specs=[pl.BlockSpec((r, s), lambda i: (0, 0))],
        out_specs=[
            pl.BlockSpec((r, nblk, MXU_DIM), lambda i: (0, 0, 0)),
            pl.BlockSpec((r, nblk, MXU_DIM), lambda i: (0, 0, 0)),
            pl.BlockSpec((r, LANES), lambda i: (0, 0)),
        ],
        out_shape=[
            jax.ShapeDtypeStruct((r, nblk, MXU_DIM), jnp.int32),
            jax.ShapeDtypeStruct((r, nblk, MXU_DIM), F32),
            jax.ShapeDtypeStruct((r, LANES), jnp.int32),
        ],
        compiler_params=pltpu.CompilerParams(dimension_semantics=("arbitrary",)),
        name="route",
    )(aff2)


def _segment(cnt_ref, row, pb, cap):
    c0 = jnp.minimum(cnt_ref[row, pb], cap)
    c1 = jnp.minimum(cnt_ref[row, pb + 1], cap)
    a0 = jnp.minimum(jnp.bitwise_and(c0, -SLOT_ALIGN), cap - WIN)
    return c0, c1, pl.multiple_of(a0, SLOT_ALIGN)


def _extra_windows(c1, a0):
    return (jnp.maximum(c1 - (a0 + WIN), 0) + WIN - 1) // WIN


def _gather_kernel(n_exp, cnt_ref, h2_ref, slot_ref, aff_ref, x_ref, gs_ref):
    b, eg = pl.program_id(0), pl.program_id(1)
    ng, cap = x_ref.shape[1], x_ref.shape[2]
    nblk = slot_ref.shape[2]
    sub = lax.broadcasted_iota(jnp.int32, (WIN, MXU_DIM), 0)
    x_ref[...] = jnp.zeros_like(x_ref)
    gs_ref[...] = jnp.zeros_like(gs_ref)

    def merge(k, a0, rows, gate):
        win = (0, k, pl.ds(a0, WIN), slice(None))
        x_ref[win] += rows.astype(BF16)
        gs_ref[win] += gate

    def tokens(pb):
        return pl.ds(pl.multiple_of(pb * MXU_DIM, MXU_DIM), MXU_DIM)

    def first_windows(pb):
        segs, gates, onehots = [], [], []
        overflow = jnp.int32(0)
        for k in range(ng):
            c0, c1, a0 = _segment(cnt_ref, b * n_exp + eg * ng + k, pb, cap)
            match = slot_ref[0, k, pl.ds(pb, 1), :] == (sub + a0)
            onehots.append(jnp.where(match, 1.0, 0.0).astype(BF16))
            gates.append(jnp.sum(jnp.where(match, aff_ref[0, k, pl.ds(pb, 1), :], 0.0),
                                 axis=1, keepdims=True))
            segs.append((c0, c1, a0))
            overflow = overflow + jnp.maximum(c1 - (a0 + WIN), 0)
        rows = _dot(jnp.concatenate(onehots, axis=0), h2_ref[0, tokens(pb), :])
        for k in range(ng):
            merge(k, segs[k][2], rows[k * WIN:(k + 1) * WIN, :], gates[k])
        return overflow

    def further_windows(pb):
        def per_expert(k, carry2):
            _, c1, a0 = _segment(cnt_ref, b * n_exp + eg * ng + k, pb, cap)

            def per_window(w, carry3):
                first = a0 + (w + 1) * WIN
                aw = pl.multiple_of(jnp.minimum(first, cap - WIN), SLOT_ALIGN)
                srow = slot_ref[0, k, pl.ds(pb, 1), :]
                match = (srow == (sub + aw)) & (srow >= first)
                onehot = jnp.where(match, 1.0, 0.0).astype(BF16)
                gate = jnp.sum(jnp.where(match, aff_ref[0, k, pl.ds(pb, 1), :], 0.0),
                               axis=1, keepdims=True)
                merge(k, aw, _dot(onehot, h2_ref[0, tokens(pb), :]), gate)
                return carry3

            return lax.fori_loop(0, _extra_windows(c1, a0), per_window, carry2)

        lax.fori_loop(0, ng, per_expert, 0)

    def pair_body(i, carry):
        overflows = [first_windows(2 * i + j) for j in range(2)]
        for j in range(2):
            pl.when(overflows[j] > 0)(functools.partial(further_windows, 2 * i + j))
        return carry

    lax.fori_loop(0, nblk // 2, pair_body, 0)


def _gather(cnt2, h2, slots4, aff4, cap):
    b, s, d = h2.shape
    e = slots4.shape[1]
    nblk = s // MXU_DIM
    grid_spec = pltpu.PrefetchScalarGridSpec(
        num_scalar_prefetch=1,
        grid=(b, e // EG),
        in_specs=[
            pl.BlockSpec((1, s, d), lambda bi, gi, c: (bi, 0, 0)),
            pl.BlockSpec((1, EG, nblk, MXU_DIM), lambda bi, gi, c: (bi, gi, 0, 0)),
            pl.BlockSpec((1, EG, nblk, MXU_DIM), lambda bi, gi, c: (bi, gi, 0, 0)),
        ],
        out_specs=[
            pl.BlockSpec((1, EG, cap, d), lambda bi, gi, c: (bi, gi, 0, 0)),
            pl.BlockSpec((1, EG, cap, LANES), lambda bi, gi, c: (bi, gi, 0, 0)),
        ],
    )
    return pl.pallas_call(
        functools.partial(_gather_kernel, e),
        grid_spec=grid_spec,
        out_shape=[
            jax.ShapeDtypeStruct((b, e, cap, d), BF16),
            jax.ShapeDtypeStruct((b, e, cap, LANES), F32),
        ],
        compiler_params=pltpu.CompilerParams(
            dimension_semantics=("arbitrary", "arbitrary"), vmem_limit_bytes=VMEM_LIMIT),
        name="gather",
    )(cnt2, h2, slots4, aff4)


def _ffn_kernel(nchunk, x_ref, gs_ref, *refs):
    w_refs, out_ref, acc_s = refs[:-2], refs[-2], refs[-1]
    groups = [w_refs[3 * k:3 * k + 3] for k in range(FFN_CHUNKS)]
    e, f = pl.program_id(0), pl.program_id(1)
    nb, _, cap, d = x_ref.shape
    last = pl.num_programs(1) - 1

    @pl.when((e == 0) & (f == 0))
    def _init():
        acc_s[...] = jnp.zeros_like(acc_s)

    def accumulate(chunks):
        wg = jnp.concatenate([w[0][0].astype(BF16) for w in chunks], axis=1)
        wu = jnp.concatenate([w[1][0].astype(BF16) for w in chunks], axis=1)
        wd = jnp.concatenate([w[2][0].astype(BF16) for w in chunks], axis=0)
        for bi in range(nb):
            x = x_ref[bi, 0]
            g = _dot(x, wg)
            u = _dot(x, wu)
            hid = (g * jax.nn.sigmoid(g) * u).astype(BF16)
            y = _dot(hid, wd)
            rows = slice(bi * cap, (bi + 1) * cap)
            acc_s[rows, :] = jnp.where(f == 0, y, acc_s[rows, :] + y)

    tail = nchunk % FFN_CHUNKS
    if tail == 0:
        accumulate(groups)
    else:
        pl.when(f < last)(lambda: accumulate(groups))
        pl.when(f == last)(lambda: accumulate(groups[:tail]))

    @pl.when(f == last)
    def _emit():
        gate = gs_ref[:, 0, :, 0:1].reshape(nb * cap, 1)
        out_ref[:, 0] = (acc_s[...] * gate).reshape(nb, cap, d).astype(BF16)


def _ffn(xg, gs, w_gate, w_up, w_down, e0):
    b, e, cap, d = xg.shape
    nchunk = w_gate.shape[2] // FC
    weight_specs = []
    for k in range(FFN_CHUNKS):
        chunk = lambda fi, k=k: jnp.minimum(FFN_CHUNKS * fi + k, nchunk - 1)
        weight_specs += [
            pl.BlockSpec((1, d, FC), lambda ei, fi, chunk=chunk: (e0 + ei, 0, chunk(fi))),
            pl.BlockSpec((1, d, FC), lambda ei, fi, chunk=chunk: (e0 + ei, 0, chunk(fi))),
            pl.BlockSpec((1, FC, d), lambda ei, fi, chunk=chunk: (e0 + ei, chunk(fi), 0)),
        ]
    return pl.pallas_call(
        functools.partial(_ffn_kernel, nchunk),
        grid=(e, pl.cdiv(nchunk, FFN_CHUNKS)),
        in_specs=[
            pl.BlockSpec((b, 1, cap, d), lambda ei, fi: (0, ei, 0, 0)),
            pl.BlockSpec((b, 1, cap, LANES), lambda ei, fi: (0, ei, 0, 0)),
        ] + weight_specs,
        out_specs=pl.BlockSpec((b, 1, cap, d), lambda ei, fi: (0, ei, 0, 0)),
        out_shape=jax.ShapeDtypeStruct((b, e, cap, d), BF16),
        scratch_shapes=[pltpu.VMEM((b * cap, d), F32)],
        compiler_params=pltpu.CompilerParams(
            dimension_semantics=("arbitrary", "arbitrary"), vmem_limit_bytes=VMEM_LIMIT),
        name="ffn",
    )(xg, gs, *([w_gate, w_up, w_down] * FFN_CHUNKS))


def _combine_kernel(n_exp, cnt_ref, eo_ref, slot_ref, x1_ref, mod_ref, g_ref, o_ref,
                    stage_s, oh_s, y_s):
    b, q = pl.program_id(0), pl.program_id(1)
    cap = eo_ref.shape[2]
    nblk_q = o_ref.shape[0] // MXU_DIM
    sub = lax.broadcasted_iota(jnp.int32, (WIN, MXU_DIM), 0)
    gain = g_ref[...] * mod_ref[0, 5:6, :]

    for j in range(nblk_q):
        pb = q * nblk_q + j
        overflow = jnp.int32(0)
        for e in range(n_exp):
            _, c1, a0 = _segment(cnt_ref, b * n_exp + e, pb, cap)
            stage_s[e * WIN:(e + 1) * WIN, :] = eo_ref[0, e, pl.ds(a0, WIN), :]
            match = slot_ref[0, e, pl.ds(pb, 1), :] == (sub + a0)
            oh_s[e * WIN:(e + 1) * WIN, :] = jnp.where(match, 1.0, 0.0).astype(BF16)
            overflow = overflow + jnp.maximum(c1 - (a0 + WIN), 0)
        y_s[...] = _dot_tn(oh_s[...], stage_s[...])

        @pl.when(overflow > 0)
        def _long_segments(j=j, pb=pb):
            def per_expert(e, carry):
                _, c1, a0 = _segment(cnt_ref, b * n_exp + e, pb, cap)

                def per_window(k, carry2):
                    first = a0 + (k + 1) * WIN
                    aw = pl.multiple_of(jnp.minimum(first, cap - WIN), SLOT_ALIGN)
                    srow = slot_ref[0, e, pl.ds(pb, 1), :]
                    match = (srow == (sub + aw)) & (srow >= first)
                    onehot = jnp.where(match, 1.0, 0.0).astype(BF16)
                    y_s[...] += _dot_tn(onehot, eo_ref[0, e, pl.ds(aw, WIN), :])
                    return carry2

                return lax.fori_loop(0, _extra_windows(c1, a0), per_window, carry)

            lax.fori_loop(0, n_exp, per_expert, 0)

        sl = slice(j * MXU_DIM, (j + 1) * MXU_DIM)
        o_ref[sl, :] = x1_ref[sl, :] + _rms(y_s[...], gain)


def _combine(cnt2, eo, slots4, x1, mod, g_post):
    b, e, cap, d = eo.shape
    t = x1.shape[0]
    s = t // b
    nq = s // QP
    nblk_q = QP // MXU_DIM
    grid_spec = pltpu.PrefetchScalarGridSpec(
        num_scalar_prefetch=1,
        grid=(b, nq),
        in_specs=[
            pl.BlockSpec((1, e, cap, d), lambda bi, qi, c: (bi, 0, 0, 0), pipeline_mode=pl.Buffered(1)),
            pl.BlockSpec((1, e, s // MXU_DIM, MXU_DIM), lambda bi, qi, c: (bi, 0, 0, 0)),
            pl.BlockSpec((QP, d), lambda bi, qi, c: (bi * nq + qi, 0)),
            pl.BlockSpec((1, 6, d), lambda bi, qi, c: (bi, 0, 0)),
            pl.BlockSpec((1, d), lambda bi, qi, c: (0, 0)),
        ],
        out_specs=pl.BlockSpec((QP, d), lambda bi, qi, c: (bi * nq + qi, 0)),
        scratch_shapes=[
            pltpu.VMEM((e * WIN, d), BF16),
            pltpu.VMEM((e * WIN, MXU_DIM), BF16),
            pltpu.VMEM((MXU_DIM, d), F32),
        ],
    )
    return pl.pallas_call(
        functools.partial(_combine_kernel, e),
        grid_spec=grid_spec,
        out_shape=jax.ShapeDtypeStruct((t, d), F32),
        compiler_params=pltpu.CompilerParams(
            dimension_semantics=("arbitrary", "arbitrary"),
            vmem_limit_bytes=VMEM_LIMIT),
        name="combine",
    )(cnt2, eo, slots4, x1, mod, g_post)


def _layer(layer, x, c, w_ada_all, b_ada, norm_pre_mix, norm_post_mix, w_in, sink, sgu_ln_g,
           sgu_ln_b, w_s, b_s, norm_out_attn, norm_out_gmlp, w_out, norm_pre_ffn, norm_post_ffn,
           w_router, w_gate_all, w_up_all, w_down_all):
    b, s, d = x.shape
    t = b * s
    dg = sgu_ln_g.shape[0]
    e = w_router.shape[1]
    cap = CAPACITY_FACTOR * s // e
    assert s % QP == 0 and s % TM == 0 and w_gate_all.shape[2] % FC == 0 and e % EG == 0
    assert cap % SLOT_ALIGN == 0 and cap >= WIN and s // MXU_DIM < LANES and d == D_ATTN + dg
    assert (s // MXU_DIM) % 2 == 0

    mod = _adaln(c, w_ada_all, b_ada, layer).reshape(b, 6, d)

    x2 = x.reshape(t, d)
    q, kv, u, vg = _proj(x2, mod, norm_pre_mix.reshape(1, d), w_in.astype(BF16), s)

    bias = jnp.asarray(_attn_bias_table())
    bs_full = jnp.repeat(b_s.T, dg // N_GMLP_GROUPS, axis=1)
    x1, h2, aff_t = _mix(
        x2, q, kv, u, vg, mod, sink, bias, w_s.astype(BF16), bs_full,
        sgu_ln_g.reshape(1, dg), sgu_ln_b.reshape(1, dg), norm_out_attn.reshape(1, D_ATTN),
        norm_out_gmlp.reshape(1, dg), w_out.astype(BF16), norm_post_mix.reshape(1, d),
        norm_pre_ffn.reshape(1, d), w_router.T.astype(BF16), b, s)

    slots3, aff3, cnt2 = _route(aff_t.reshape(b * e, s), cap)
    nblk = s // MXU_DIM
    slots4 = slots3.reshape(b, e, nblk, MXU_DIM)
    aff4 = aff3.reshape(b, e, nblk, MXU_DIM)
    xg, gs = _gather(cnt2, h2.reshape(b, s, d), slots4, aff4, cap)
    eo = _ffn(xg, gs, w_gate_all, w_up_all, w_down_all, layer * e)
    out = _combine(cnt2, eo, slots4, x1, mod, norm_post_ffn.reshape(1, d))
    return out.reshape(b, s, d)


def kernel(x, c, w_ada, b_ada, norm_pre_mix, norm_post_mix, w_in, sink, sgu_ln_g, sgu_ln_b, w_s, b_s, norm_out_attn, norm_out_gmlp, w_out, norm_pre_ffn, norm_post_ffn, w_router, w_gate, w_up, w_down):
    depth, d = w_ada.shape[0], w_ada.shape[1]
    w_ada_all = w_ada.reshape(depth * d, w_ada.shape[2])
    stack = lambda w: w.reshape((depth * w.shape[1],) + w.shape[2:])
    w_gate_all, w_up_all, w_down_all = stack(w_gate), stack(w_up), stack(w_down)
    for l in range(depth):
        x = _layer(l, x, c, w_ada_all, b_ada[l], norm_pre_mix[l], norm_post_mix[l], w_in[l],
                   sink[l], sgu_ln_g[l], sgu_ln_b[l], w_s[l], b_s[l], norm_out_attn[l],
                   norm_out_gmlp[l], w_out[l], norm_pre_ffn[l], norm_post_ffn[l], w_router[l],
                   w_gate_all, w_up_all, w_down_all)
    return x
```

```python
import functools

import numpy as np
import jax
import jax.numpy as jnp
from jax import lax
from jax.experimental import pallas as pl
from jax.experimental.pallas import tpu as pltpu

N_HEADS = 8
N_KV_HEADS = 2
HEAD_DIM = 64
D_ATTN = N_HEADS * HEAD_DIM
D_KV = N_KV_HEADS * HEAD_DIM
GROUP = N_HEADS // N_KV_HEADS
WINDOW = 128
BLOCK = 128
N_GMLP_GROUPS = 8
N_EXPERTS = 16
CAPACITY_FACTOR = 2
EPS = 1e-6
MASK_VALUE = -1e30

LANES = 128
MXU_DIM = 256

TM = 1024
FC = 256
FFN_CHUNKS = 3
PROJ_ROWS = 256
EG = 4
QP = 1024
WIN = 64
SLOT_ALIGN = 16
VMEM_LIMIT = 56 * 1024 * 1024

BF16 = jnp.bfloat16
F32 = jnp.float32


def _rms(x, g):
    return x * lax.rsqrt(jnp.mean(x * x, axis=-1, keepdims=True) + EPS) * g


def _dot(a, b):
    return jnp.dot(a, b, preferred_element_type=F32)


def _dot_nt(a, b):
    return lax.dot_general(a, b, (((1,), (1,)), ((), ())), preferred_element_type=F32)


def _dot_tn(a, b):
    return lax.dot_general(a, b, (((0,), (0,)), ((), ())), preferred_element_type=F32)


def _adaln_kernel(c_ref, w_ref, b_ref, o_ref):
    c = c_ref[...]
    a = (c * jax.nn.sigmoid(c)).astype(BF16)
    o_ref[...] = _dot(a, w_ref[...].astype(BF16)) + b_ref[...]


def _adaln(c, w_ada_all, b_ada, layer):
    b, d = c.shape
    n = w_ada_all.shape[1]
    return pl.pallas_call(
        _adaln_kernel,
        grid=(n // d,),
        in_specs=[
            pl.BlockSpec((b, d), lambda j: (0, 0)),
            pl.BlockSpec((d, d), lambda j: (layer, j)),
            pl.BlockSpec((1, d), lambda j: (0, j)),
        ],
        out_specs=pl.BlockSpec((b, d), lambda j: (0, j)),
        out_shape=jax.ShapeDtypeStruct((b, n), F32),
        compiler_params=pltpu.CompilerParams(dimension_semantics=("arbitrary",)),
        name="adaln",
    )(c, w_ada_all, b_ada.reshape(1, n))


def _gelu_tanh(x):
    c = np.float32(np.sqrt(2.0 / np.pi))
    return x * (0.5 * (1.0 + jnp.tanh(c * (x + 0.044715 * (x * x * x)))))


def _proj_kernel(x_ref, mod_ref, g_ref, w_ref, q_ref, kv_ref, u_ref, vg_ref):
    shift = mod_ref[0, 0:1, :]
    gain = g_ref[...] * (1.0 + mod_ref[0, 1:2, :])
    n_att = D_ATTN + 2 * D_KV
    dg = u_ref.shape[1]
    lo = lax.broadcasted_iota(jnp.int32, (1, LANES), 1) < HEAD_DIM
    for r in range(0, TM, PROJ_ROWS):
        rows = slice(r, r + PROJ_ROWS)
        h = (_rms(x_ref[rows, :], gain) + shift).astype(BF16)
        z = _gelu_tanh(_dot(h, w_ref[:, n_att:]))
        u_ref[rows, :] = z[:, :dg]
        vg_ref[rows, :] = z[:, dg:]
        p = _dot(h, w_ref[:, :n_att])
        q_ref[rows, :] = (p[:, :D_ATTN] * (HEAD_DIM ** -0.5)).astype(BF16)
        for j in range(2):
            t = p[:, D_ATTN + j * D_KV:D_ATTN + (j + 1) * D_KV]
            t_sw = pltpu.roll(t, HEAD_DIM, axis=1)
            kv_ref[rows, (2 * j) * LANES:(2 * j + 1) * LANES] = jnp.where(lo, t, t_sw).astype(BF16)
            kv_ref[rows, (2 * j + 1) * LANES:(2 * j + 2) * LANES] = jnp.where(lo, t_sw, t).astype(BF16)


def _proj(x2, mod, g_pre, w_in, s):
    t, d = x2.shape
    nw = w_in.shape[1]
    dg = (nw - D_ATTN - 2 * D_KV) // 2
    nsb = s // TM
    return pl.pallas_call(
        _proj_kernel,
        grid=(t // TM,),
        in_specs=[
            pl.BlockSpec((TM, d), lambda i: (i, 0)),
            pl.BlockSpec((1, 6, d), lambda i: (i // nsb, 0, 0)),
            pl.BlockSpec((1, d), lambda i: (0, 0)),
            pl.BlockSpec((d, nw), lambda i: (0, 0)),
        ],
        out_specs=[
            pl.BlockSpec((TM, D_ATTN), lambda i: (i, 0)),
            pl.BlockSpec((TM, D_ATTN), lambda i: (i, 0)),
            pl.BlockSpec((TM, dg), lambda i: (i, 0)),
            pl.BlockSpec((TM, dg), lambda i: (i, 0)),
        ],
        out_shape=[
            jax.ShapeDtypeStruct((t, D_ATTN), BF16),
            jax.ShapeDtypeStruct((t, D_ATTN), BF16),
            jax.ShapeDtypeStruct((t, dg), F32),
            jax.ShapeDtypeStruct((t, dg), F32),
        ],
        compiler_params=pltpu.CompilerParams(
            dimension_semantics=("arbitrary",), vmem_limit_bytes=VMEM_LIMIT),
        name="proj",
    )(x2, mod, g_pre, w_in)


def _attn_bias_table():
    qi = np.arange(BLOCK)[:, None]
    kj = np.arange(3 * BLOCK)[None, :] - BLOCK
    dist = np.abs(qi - kj).astype(np.float32)
    band = dist <= WINDOW
    slopes = np.exp2(-8.0 * np.arange(1, N_HEADS + 1, dtype=np.float32) / N_HEADS).astype(np.float32)
    key_ok = [kj >= 0, np.ones_like(kj, bool), kj < BLOCK]
    tab = np.empty((3, N_HEADS, BLOCK, 3 * BLOCK), np.float32)
    for v in range(3):
        ok = band & key_ok[v]
        for h in range(N_HEADS):
            tab[v, h] = np.where(ok, -slopes[h] * dist, np.float32(MASK_VALUE))
    return tab.reshape(3, N_KV_HEADS, GROUP * BLOCK, 3 * BLOCK)


def _mix_kernel(nblk_seq, sink_ref, x_ref, q_ref, kvp_ref, kvc_ref, kvn_ref, u_ref, vg_ref,
                mod_ref, bias_ref, ws_ref, bs_ref, lng_ref, lnb_ref, ga_ref, gg_ref,
                wout_ref, gpost_ref, gpre2_ref, wr_ref,
                x1_ref, h2_ref, aff_ref, kcat, attn_s, gm_s):
    i = pl.program_id(0)
    nb_tile = TM // BLOCK
    kcat[0:BLOCK, :] = kvp_ref[...]
    kcat[BLOCK:BLOCK + TM, :] = kvc_ref[...]
    kcat[BLOCK + TM:, :] = kvn_ref[...]
    lane = lax.broadcasted_iota(jnp.int32, (1, LANES), 1)
    lo = lane < HEAD_DIM
    zero = jnp.zeros((), BF16)
    head_of_row = lax.broadcasted_iota(jnp.int32, (GROUP * BLOCK, 1), 0) // BLOCK
    sinks = []
    for g in range(N_KV_HEADS):
        col = jnp.zeros((GROUP * BLOCK, 1), F32)
        for j in range(GROUP):
            col = jnp.where(head_of_row == j, sink_ref[g * GROUP + j], col)
        sinks.append(col)

    def block_body(a, carry):
        r0 = pl.multiple_of(a * BLOCK, BLOCK)
        pos = (i * nb_tile + a) % nblk_seq
        var = jnp.where(pos == 0, 0, jnp.where(pos == nblk_seq - 1, 2, 1))
        kv = kcat[pl.ds(r0, 3 * BLOCK), :]
        for g in range(N_KV_HEADS):
            k = kv[:, g * LANES:(g + 1) * LANES]
            v = kv[:, (N_KV_HEADS + g) * LANES:(N_KV_HEADS + g + 1) * LANES]
            pieces = []
            for pi in range(GROUP // 2):
                qp = q_ref[pl.ds(r0, BLOCK), (g * GROUP // 2 + pi) * LANES:(g * GROUP // 2 + pi + 1) * LANES]
                pieces += [jnp.where(lo, qp, zero), jnp.where(lo, zero, qp)]
            s = _dot_nt(jnp.concatenate(pieces, axis=0), k) + bias_ref[var, g]
            m = jnp.maximum(jnp.max(s, axis=-1, keepdims=True), sinks[g])
            p = jnp.exp(s - m)
            den = jnp.sum(p, axis=-1, keepdims=True) + jnp.exp(sinks[g] - m)
            o = _dot(p.astype(BF16), v) / den
            for pi in range(GROUP // 2):
                even = o[(2 * pi) * BLOCK:(2 * pi + 1) * BLOCK, :]
                odd = o[(2 * pi + 1) * BLOCK:(2 * pi + 2) * BLOCK, :]
                col0 = (g * GROUP // 2 + pi) * LANES
                attn_s[pl.ds(r0, BLOCK), col0:col0 + LANES] = jnp.where(lo, even, odd)
        vg = vg_ref[pl.ds(r0, BLOCK), :]
        mu = jnp.mean(vg, axis=-1, keepdims=True)
        xc = vg - mu
        vn = (xc * lax.rsqrt(jnp.mean(xc * xc, axis=-1, keepdims=True) + EPS) * lng_ref[...]
              + lnb_ref[...]).astype(BF16)
        for pi in range(N_GMLP_GROUPS // 2):
            vnp = vn[:, pi * LANES:(pi + 1) * LANES]
            z = jnp.where(lo, _dot(ws_ref[2 * pi], vnp), _dot(ws_ref[2 * pi + 1], vnp))
            sl = slice(pi * LANES, (pi + 1) * LANES)
            gm_s[pl.ds(r0, BLOCK), sl] = u_ref[pl.ds(r0, BLOCK), sl] * (z + bs_ref[:, sl])
        return carry

    lax.fori_loop(0, nb_tile, block_body, 0, unroll=2)

    na = _rms(attn_s[...], ga_ref[...]).astype(BF16)
    ng = _rms(gm_s[...], gg_ref[...]).astype(BF16)
    mo = _dot(na, wout_ref[0:D_ATTN, :]) + _dot(ng, wout_ref[D_ATTN:, :])
    x1 = x_ref[...] + _rms(mo, gpost_ref[...] * mod_ref[0, 2:3, :])
    x1_ref[...] = x1
    h2 = (_rms(x1, gpre2_ref[...] * (1.0 + mod_ref[0, 4:5, :])) + mod_ref[0, 3:4, :]).astype(BF16)
    h2_ref[...] = h2
    lg = _dot_nt(wr_ref[...], h2)
    lg = lg - jnp.max(lg, axis=0, keepdims=True)
    ex = jnp.exp(lg)
    aff_ref[0] = ex / jnp.sum(ex, axis=0, keepdims=True)


def _mix(x2, q, kv, u, vg, mod, sink, bias, ws, bs_full, ln_g, ln_b, g_attn, g_gmlp,
         w_out, g_post, g_pre2, w_r_t, b, s):
    t, d = x2.shape
    dg = u.shape[1]
    e = w_r_t.shape[0]
    nsb = s // TM
    nb_tile = TM // BLOCK
    nblk = t // BLOCK
    row = lambda i: (i, 0)
    const2 = lambda i: (0, 0)
    in_specs = [
        pl.BlockSpec(memory_space=pltpu.SMEM),
        pl.BlockSpec((TM, d), row),
        pl.BlockSpec((TM, D_ATTN), row),
        pl.BlockSpec((BLOCK, 4 * LANES), lambda i: (jnp.maximum(i * nb_tile - 1, 0), 0)),
        pl.BlockSpec((TM, 4 * LANES), row),
        pl.BlockSpec((BLOCK, 4 * LANES), lambda i: (jnp.minimum((i + 1) * nb_tile, nblk - 1), 0)),
        pl.BlockSpec((TM, dg), row),
        pl.BlockSpec((TM, dg), row),
        pl.BlockSpec((1, 6, d), lambda i: (i // nsb, 0, 0)),
        pl.BlockSpec(bias.shape, lambda i: (0, 0, 0, 0)),
        pl.BlockSpec(ws.shape, lambda i: (0, 0, 0)),
        pl.BlockSpec((BLOCK, dg), const2),
        pl.BlockSpec((1, dg), const2),
        pl.BlockSpec((1, dg), const2),
        pl.BlockSpec((1, D_ATTN), const2),
        pl.BlockSpec((1, dg), const2),
        pl.BlockSpec(w_out.shape, const2),
        pl.BlockSpec((1, d), const2),
        pl.BlockSpec((1, d), const2),
        pl.BlockSpec(w_r_t.shape, const2),
    ]
    out_specs = [
        pl.BlockSpec((TM, d), row),
        pl.BlockSpec((TM, d), row),
        pl.BlockSpec((1, e, TM), lambda i: (i // nsb, 0, i % nsb)),
    ]
    out_shape = [
        jax.ShapeDtypeStruct((t, d), F32),
        jax.ShapeDtypeStruct((t, d), BF16),
        jax.ShapeDtypeStruct((b, e, s), F32),
    ]
    return pl.pallas_call(
        functools.partial(_mix_kernel, s // BLOCK),
        grid=(t // TM,),
        in_specs=in_specs,
        out_specs=out_specs,
        out_shape=out_shape,
        scratch_shapes=[
            pltpu.VMEM((TM + 2 * BLOCK, 4 * LANES), BF16),
            pltpu.VMEM((TM, D_ATTN), F32),
            pltpu.VMEM((TM, dg), F32),
        ],
        compiler_params=pltpu.CompilerParams(
            dimension_semantics=("arbitrary",), vmem_limit_bytes=VMEM_LIMIT),
        name="mix",
    )(sink, x2, q, kv, kv, kv, u, vg, mod, bias, ws, bs_full, ln_g, ln_b, g_attn, g_gmlp,
      w_out, g_post, g_pre2, w_r_t)


def _route_kernel(cap, aff_ref, slot_ref, affb_ref, cnt_ref):
    aff = aff_ref[...]
    e, s = aff.shape
    capf = jnp.float32(cap)

    def count_ge(t):
        return jnp.sum(jnp.where(aff >= t, 1.0, 0.0), axis=1, keepdims=True)

    def bit_body(it, cand):
        trial = cand | jnp.left_shift(jnp.int32(1), 30 - it)
        t = lax.bitcast_convert_type(trial, F32)
        return jnp.where(count_ge(t) >= capf, trial, cand)

    cand = lax.fori_loop(0, 31, bit_body, jnp.zeros((e, 1), jnp.int32))
    thr = lax.bitcast_convert_type(cand, F32)
    need = capf - jnp.sum(jnp.where(aff > thr, 1.0, 0.0), axis=1, keepdims=True)

    r = lax.broadcasted_iota(jnp.int32, (LANES, LANES), 0)
    c = lax.broadcasted_iota(jnp.int32, (LANES, LANES), 1)
    tri = jnp.where(r < c, 1.0, 0.0).astype(BF16)
    lane = lax.broadcasted_iota(jnp.int32, (e, LANES), 1)
    carry_eq = jnp.zeros((e, 1), F32)
    carry_sel = jnp.zeros((e, 1), F32)
    cnt = jnp.zeros((e, LANES), F32)
    per_blk = MXU_DIM // LANES
    for j in range(s // LANES):
        a = aff[:, j * LANES:(j + 1) * LANES]
        gt = a > thr
        eq = jnp.where(a == thr, 1.0, 0.0)
        tie_rank = _dot(eq.astype(BF16), tri) + carry_eq
        sel = jnp.where(gt | ((eq > 0.0) & (tie_rank < need)), 1.0, 0.0)
        slot = _dot(sel.astype(BF16), tri) + carry_sel
        blk = (slice(None), j // per_blk, slice((j % per_blk) * LANES, (j % per_blk + 1) * LANES))
        slot_ref[blk] = jnp.where(sel > 0.0, slot, -1.0).astype(jnp.int32)
        affb_ref[blk] = a
        if j % per_blk == 0:
            cnt = jnp.where(lane == j // per_blk, carry_sel, cnt)
        carry_eq = carry_eq + jnp.sum(eq, axis=1, keepdims=True)
        carry_sel = carry_sel + jnp.sum(sel, axis=1, keepdims=True)
    cnt = jnp.where(lane >= s // MXU_DIM, carry_sel, cnt)
    cnt_ref[...] = cnt.astype(jnp.int32)


def _route(aff2, cap):
    r, s = aff2.shape
    nblk = s // MXU_DIM
    return pl.pallas_call(
        functools.partial(_route_kernel, cap),
        grid=(1,),
        in_specs=[pl.BlockSpec((r, s), lambda i: (0, 0))],
        out_specs=[
            pl.BlockSpec((r, nblk, MXU_DIM), lambda i: (0, 0, 0)),
            pl.BlockSpec((r, nblk, MXU_DIM), lambda i: (0, 0, 0)),
            pl.BlockSpec((r, LANES), lambda i: (0, 0)),
        ],
        out_shape=[
            jax.ShapeDtypeStruct((r, nblk, MXU_DIM), jnp.int32),
            jax.ShapeDtypeStruct((r, nblk, MXU_DIM), F32),
            jax.ShapeDtypeStruct((r, LANES), jnp.int32),
        ],
        compiler_params=pltpu.CompilerParams(dimension_semantics=("arbitrary",)),
        name="route",
    )(aff2)


def _segment(cnt_ref, row, pb, cap):
    c0 = jnp.minimum(cnt_ref[row, pb], cap)
    c1 = jnp.minimum(cnt_ref[row, pb + 1], cap)
    a0 = jnp.minimum(jnp.bitwise_and(c0, -SLOT_ALIGN), cap - WIN)
    return c0, c1, pl.multiple_of(a0, SLOT_ALIGN)


def _extra_windows(c1, a0):
    return (jnp.maximum(c1 - (a0 + WIN), 0) + WIN - 1) // WIN


def _gather_kernel(n_exp, cnt_ref, h2_ref, slot_ref, aff_ref, x_ref, gs_ref):
    b, eg = pl.program_id(0), pl.program_id(1)
    ng, cap = x_ref.shape[1], x_ref.shape[2]
    nblk = slot_ref.shape[2]
    sub = lax.broadcasted_iota(jnp.int32, (WIN, MXU_DIM), 0)
    x_ref[...] = jnp.zeros_like(x_ref)
    gs_ref[...] = jnp.zeros_like(gs_ref)

    def merge(k, a0, rows, gate):
        win = (0, k, pl.ds(a0, WIN), slice(None))
        x_ref[win] += rows.astype(BF16)
        gs_ref[win] += gate

    def tokens(pb):
        return pl.ds(pl.multiple_of(pb * MXU_DIM, MXU_DIM), MXU_DIM)

    def first_windows(pb):
        segs, gates, onehots = [], [], []
        overflow = jnp.int32(0)
        for k in range(ng):
            c0, c1, a0 = _segment(cnt_ref, b * n_exp + eg * ng + k, pb, cap)
            match = slot_ref[0, k, pl.ds(pb, 1), :] == (sub + a0)
            onehots.append(jnp.where(match, 1.0, 0.0).astype(BF16))
            gates.append(jnp.sum(jnp.where(match, aff_ref[0, k, pl.ds(pb, 1), :], 0.0),
                                 axis=1, keepdims=True))
            segs.append((c0, c1, a0))
            overflow = overflow + jnp.maximum(c1 - (a0 + WIN), 0)
        rows = _dot(jnp.concatenate(onehots, axis=0), h2_ref[0, tokens(pb), :])
        for k in range(ng):
            merge(k, segs[k][2], rows[k * WIN:(k + 1) * WIN, :], gates[k])
        return overflow

    def further_windows(pb):
        def per_expert(k, carry2):
            _, c1, a0 = _segment(cnt_ref, b * n_exp + eg * ng + k, pb, cap)

            def per_window(w, carry3):
                first = a0 + (w + 1) * WIN
                aw = pl.multiple_of(jnp.minimum(first, cap - WIN), SLOT_ALIGN)
                srow = slot_ref[0, k, pl.ds(pb, 1), :]
                match = (srow == (sub + aw)) & (srow >= first)
                onehot = jnp.where(match, 1.0, 0.0).astype(BF16)
                gate = jnp.sum(jnp.where(match, aff_ref[0, k, pl.ds(pb, 1), :], 0.0),
                               axis=1, keepdims=True)
                merge(k, aw, _dot(onehot, h2_ref[0, tokens(pb), :]), gate)
                return carry3

            return lax.fori_loop(0, _extra_windows(c1, a0), per_window, carry2)

        lax.fori_loop(0, ng, per_expert, 0)

    def pair_body(i, carry):
        overflows = [first_windows(2 * i + j) for j in range(2)]
        for j in range(2):
            pl.when(overflows[j] > 0)(functools.partial(further_windows, 2 * i + j))
        return carry

    lax.fori_loop(0, nblk // 2, pair_body, 0)


def _gather(cnt2, h2, slots4, aff4, cap):
    b, s, d = h2.shape
    e = slots4.shape[1]
    nblk = s // MXU_DIM
    grid_spec = pltpu.PrefetchScalarGridSpec(
        num_scalar_prefetch=1,
        grid=(b, e // EG),
        in_specs=[
            pl.BlockSpec((1, s, d), lambda bi, gi, c: (bi, 0, 0)),
            pl.BlockSpec((1, EG, nblk, MXU_DIM), lambda bi, gi, c: (bi, gi, 0, 0)),
            pl.BlockSpec((1, EG, nblk, MXU_DIM), lambda bi, gi, c: (bi, gi, 0, 0)),
        ],
        out_specs=[
            pl.BlockSpec((1, EG, cap, d), lambda bi, gi, c: (bi, gi, 0, 0)),
            pl.BlockSpec((1, EG, cap, LANES), lambda bi, gi, c: (bi, gi, 0, 0)),
        ],
    )
    return pl.pallas_call(
        functools.partial(_gather_kernel, e),
        grid_spec=grid_spec,
        out_shape=[
            jax.ShapeDtypeStruct((b, e, cap, d), BF16),
            jax.ShapeDtypeStruct((b, e, cap, LANES), F32),
        ],
        compiler_params=pltpu.CompilerParams(
            dimension_semantics=("arbitrary", "arbitrary"), vmem_limit_bytes=VMEM_LIMIT),
        name="gather",
    )(cnt2, h2, slots4, aff4)


def _ffn_kernel(nchunk, x_ref, gs_ref, *refs):
    w_refs, out_ref, acc_s = refs[:-2], refs[-2], refs[-1]
    groups = [w_refs[3 * k:3 * k + 3] for k in range(FFN_CHUNKS)]
    e, f = pl.program_id(0), pl.program_id(1)
    nb, _, cap, d = x_ref.shape
    last = pl.num_programs(1) - 1

    @pl.when((e == 0) & (f == 0))
    def _init():
        acc_s[...] = jnp.zeros_like(acc_s)

    def accumulate(chunks):
        wg = jnp.concatenate([w[0][0].astype(BF16) for w in chunks], axis=1)
        wu = jnp.concatenate([w[1][0].astype(BF16) for w in chunks], axis=1)
        wd = jnp.concatenate([w[2][0].astype(BF16) for w in chunks], axis=0)
        for bi in range(nb):
            x = x_ref[bi, 0]
            g = _dot(x, wg)
            u = _dot(x, wu)
            hid = (g * jax.nn.sigmoid(g) * u).astype(BF16)
            y = _dot(hid, wd)
            rows = slice(bi * cap, (bi + 1) * cap)
            acc_s[rows, :] = jnp.where(f == 0, y, acc_s[rows, :] + y)

    tail = nchunk % FFN_CHUNKS
    if tail == 0:
        accumulate(groups)
    else:
        pl.when(f < last)(lambda: accumulate(groups))
        pl.when(f == last)(lambda: accumulate(groups[:tail]))

    @pl.when(f == last)
    def _emit():
        gate = gs_ref[:, 0, :, 0:1].reshape(nb * cap, 1)
        out_ref[:, 0] = (acc_s[...] * gate).reshape(nb, cap, d).astype(BF16)


def _ffn(xg, gs, w_gate, w_up, w_down, e0):
    b, e, cap, d = xg.shape
    nchunk = w_gate.shape[2] // FC
    weight_specs = []
    for k in range(FFN_CHUNKS):
        chunk = lambda fi, k=k: jnp.minimum(FFN_CHUNKS * fi + k, nchunk - 1)
        weight_specs += [
            pl.BlockSpec((1, d, FC), lambda ei, fi, chunk=chunk: (e0 + ei, 0, chunk(fi))),
            pl.BlockSpec((1, d, FC), lambda ei, fi, chunk=chunk: (e0 + ei, 0, chunk(fi))),
            pl.BlockSpec((1, FC, d), lambda ei, fi, chunk=chunk: (e0 + ei, chunk(fi), 0)),
        ]
    return pl.pallas_call(
        functools.partial(_ffn_kernel, nchunk),
        grid=(e, pl.cdiv(nchunk, FFN_CHUNKS)),
        in_specs=[
            pl.BlockSpec((b, 1, cap, d), lambda ei, fi: (0, ei, 0, 0)),
            pl.BlockSpec((b, 1, cap, LANES), lambda ei, fi: (0, ei, 0, 0)),
        ] + weight_specs,
        out_specs=pl.BlockSpec((b, 1, cap, d), lambda ei, fi: (0, ei, 0, 0)),
        out_shape=jax.ShapeDtypeStruct((b, e, cap, d), BF16),
        scratch_shapes=[pltpu.VMEM((b * cap, d), F32)],
        compiler_params=pltpu.CompilerParams(
            dimension_semantics=("arbitrary", "arbitrary"), vmem_limit_bytes=VMEM_LIMIT),
        name="ffn",
    )(xg, gs, *([w_gate, w_up, w_down] * FFN_CHUNKS))


def _combine_kernel(n_exp, band, cnt_ref, eo_hbm, slot_ref, x1_ref, mod_ref, g_ref, o_ref,
                    band_s, extra_s, stage_s, oh_s, y_s, sem, extra_sem):
    b, q = pl.program_id(0), pl.program_id(1)
    nq = pl.num_programs(1)
    t, total = b * nq + q, pl.num_programs(0) * nq
    cap = eo_hbm.shape[2]
    nblk_q = o_ref.shape[0] // MXU_DIM
    sub = lax.broadcasted_iota(jnp.int32, (WIN, MXU_DIM), 0)
    gain = g_ref[...] * mod_ref[0, 5:6, :]

    def band_start(qs):
        return pl.multiple_of(jnp.clip((qs - 1) * (band // 3), 0, cap - band), SLOT_ALIGN)

    def band_copies(step):
        bs, qs = step // nq, step % nq
        return [pltpu.make_async_copy(eo_hbm.at[bs, e, pl.ds(band_start(qs), band), :],
                                      band_s.at[step % 2, e], sem.at[step % 2])
                for e in range(n_exp)]

    @pl.when(t == 0)
    def _prime():
        for c in band_copies(t):
            c.start()

    @pl.when(t + 1 < total)
    def _prefetch():
        for c in band_copies(t + 1):
            c.start()

    for c in band_copies(t):
        c.wait()

    lo = band_start(q)

    def window(e, pb):
        c0, c1, a0 = _segment(cnt_ref, b * n_exp + e, pb, cap)
        return c0, c1, a0, (a0 >= lo) & (a0 + WIN <= lo + band)

    for j in range(nblk_q):
        pb = q * nblk_q + j
        uncovered = jnp.int32(0)
        for e in range(n_exp):
            c0, c1, a0, in_band = window(e, pb)
            off = pl.multiple_of(jnp.clip(a0 - lo, 0, band - WIN), SLOT_ALIGN)
            stage_s[e * WIN:(e + 1) * WIN, :] = band_s[t % 2, e, pl.ds(off, WIN), :]
            match = (slot_ref[0, e, pl.ds(pb, 1), :] == (sub + a0)) & in_band
            oh_s[e * WIN:(e + 1) * WIN, :] = jnp.where(match, 1.0, 0.0).astype(BF16)
            uncovered = uncovered + jnp.where(in_band, jnp.maximum(c1 - (a0 + WIN), 0), c1 - c0)
        y_s[...] = _dot_tn(oh_s[...], stage_s[...])

        @pl.when(uncovered > 0)
        def _other_windows(j=j, pb=pb):
            def per_expert(e, carry):
                c0, c1, a0, in_band = window(e, pb)
                n_win = jnp.where(c1 > c0, (c1 - a0 + WIN - 1) // WIN, 0)

                def per_window(k, carry2):
                    first = a0 + k * WIN
                    aw = pl.multiple_of(jnp.minimum(first, cap - WIN), SLOT_ALIGN)
                    fetch = pltpu.make_async_copy(eo_hbm.at[b, e, pl.ds(aw, WIN), :], extra_s, extra_sem)
                    fetch.start()
                    srow = slot_ref[0, e, pl.ds(pb, 1), :]
                    match = (srow == (sub + aw)) & (srow >= first)
                    onehot = jnp.where(match, 1.0, 0.0).astype(BF16)
                    fetch.wait()
                    y_s[...] += _dot_tn(onehot, extra_s[...])
                    return carry2

                return lax.fori_loop(jnp.where(in_band, 1, 0), n_win, per_window, carry)

            lax.fori_loop(0, n_exp, per_expert, 0)

        sl = slice(j * MXU_DIM, (j + 1) * MXU_DIM)
        o_ref[sl, :] = x1_ref[sl, :] + _rms(y_s[...], gain)


def _combine(cnt2, eo, slots4, x1, mod, g_post):
    b, e, cap, d = eo.shape
    t = x1.shape[0]
    s = t // b
    nq = s // QP
    band = min(3 * (QP * cap // s), cap)
    assert band % SLOT_ALIGN == 0 and band >= WIN
    grid_spec = pltpu.PrefetchScalarGridSpec(
        num_scalar_prefetch=1,
        grid=(b, nq),
        in_specs=[
            pl.BlockSpec(memory_space=pl.ANY),
            pl.BlockSpec((1, e, s // MXU_DIM, MXU_DIM), lambda bi, qi, c: (bi, 0, 0, 0)),
            pl.BlockSpec((QP, d), lambda bi, qi, c: (bi * nq + qi, 0)),
            pl.BlockSpec((1, 6, d), lambda bi, qi, c: (bi, 0, 0)),
            pl.BlockSpec((1, d), lambda bi, qi, c: (0, 0)),
        ],
        out_specs=pl.BlockSpec((QP, d), lambda bi, qi, c: (bi * nq + qi, 0)),
        scratch_shapes=[
            pltpu.VMEM((2, e, band, d), BF16),
            pltpu.VMEM((WIN, d), BF16),
            pltpu.VMEM((e * WIN, d), BF16),
            pltpu.VMEM((e * WIN, MXU_DIM), BF16),
            pltpu.VMEM((MXU_DIM, d), F32),
            pltpu.SemaphoreType.DMA((2,)),
            pltpu.SemaphoreType.DMA(()),
        ],
    )
    return pl.pallas_call(
        functools.partial(_combine_kernel, e, band),
        grid_spec=grid_spec,
        out_shape=jax.ShapeDtypeStruct((t, d), F32),
        compiler_params=pltpu.CompilerParams(
            dimension_semantics=("arbitrary", "arbitrary"),
            vmem_limit_bytes=VMEM_LIMIT),
        name="combine",
    )(cnt2, eo, slots4, x1, mod, g_post)


def _layer(layer, x, c, w_ada_all, b_ada, norm_pre_mix, norm_post_mix, w_in, sink, sgu_ln_g,
           sgu_ln_b, w_s, b_s, norm_out_attn, norm_out_gmlp, w_out, norm_pre_ffn, norm_post_ffn,
           w_router, w_gate_all, w_up_all, w_down_all):
    b, s, d = x.shape
    t = b * s
    dg = sgu_ln_g.shape[0]
    e = w_router.shape[1]
    cap = CAPACITY_FACTOR * s // e
    assert s % QP == 0 and s % TM == 0 and w_gate_all.shape[2] % FC == 0 and e % EG == 0
    assert cap % SLOT_ALIGN == 0 and cap >= WIN and s // MXU_DIM < LANES and d == D_ATTN + dg
    assert (s // MXU_DIM) % 2 == 0

    mod = _adaln(c, w_ada_all, b_ada, layer).reshape(b, 6, d)

    x2 = x.reshape(t, d)
    q, kv, u, vg = _proj(x2, mod, norm_pre_mix.reshape(1, d), w_in.astype(BF16), s)

    bias = jnp.asarray(_attn_bias_table())
    bs_full = jnp.repeat(b_s.T, dg // N_GMLP_GROUPS, axis=1)
    x1, h2, aff_t = _mix(
        x2, q, kv, u, vg, mod, sink, bias, w_s.astype(BF16), bs_full,
        sgu_ln_g.reshape(1, dg), sgu_ln_b.reshape(1, dg), norm_out_attn.reshape(1, D_ATTN),
        norm_out_gmlp.reshape(1, dg), w_out.astype(BF16), norm_post_mix.reshape(1, d),
        norm_pre_ffn.reshape(1, d), w_router.T.astype(BF16), b, s)

    slots3, aff3, cnt2 = _route(aff_t.reshape(b * e, s), cap)
    nblk = s // MXU_DIM
    slots4 = slots3.reshape(b, e, nblk, MXU_DIM)
    aff4 = aff3.reshape(b, e, nblk, MXU_DIM)
    xg, gs = _gather(cnt2, h2.reshape(b, s, d), slots4, aff4, cap)
    eo = _ffn(xg, gs, w_gate_all, w_up_all, w_down_all, layer * e)
    out = _combine(cnt2, eo, slots4, x1, mod, norm_post_ffn.reshape(1, d))
    return out.reshape(b, s, d)


def kernel(x, c, w_ada, b_ada, norm_pre_mix, norm_post_mix, w_in, sink, sgu_ln_g, sgu_ln_b, w_s, b_s, norm_out_attn, norm_out_gmlp, w_out, norm_pre_ffn, norm_post_ffn, w_router, w_gate, w_up, w_down):
    depth, d = w_ada.shape[0], w_ada.shape[1]
    w_ada_all = w_ada.reshape(depth * d, w_ada.shape[2])
    stack = lambda w: w.reshape((depth * w.shape[1],) + w.shape[2:])
    w_gate_all, w_up_all, w_down_all = stack(w_gate), stack(w_up), stack(w_down)
    for l in range(depth):
        x = _layer(l, x, c, w_ada_all, b_ada[l], norm_pre_mix[l], norm_post_mix[l], w_in[l],
                   sink[l], sgu_ln_g[l], sgu_ln_b[l], w_s[l], b_s[l], norm_out_attn[l],
                   norm_out_gmlp[l], w_out[l], norm_pre_ffn[l], norm_post_ffn[l], w_router[l],
                   w_gate_all, w_up_all, w_down_all)
    return x
```

```python
import functools

import numpy as np
import jax
import jax.numpy as jnp
from jax import lax
from jax.experimental import pallas as pl
from jax.experimental.pallas import tpu as pltpu

N_HEADS = 8
N_KV_HEADS = 2
HEAD_DIM = 64
D_ATTN = N_HEADS * HEAD_DIM
D_KV = N_KV_HEADS * HEAD_DIM
GROUP = N_HEADS // N_KV_HEADS
WINDOW = 128
BLOCK = 128
N_GMLP_GROUPS = 8
N_EXPERTS = 16
CAPACITY_FACTOR = 2
EPS = 1e-6
MASK_VALUE = -1e30

LANES = 128
MXU_DIM = 256

TM = 1024
FC = 256
FFN_CHUNKS = 3
PROJ_ROWS = 256
EG = 4
GATHER_BLOCKS = 8
QP = 1024
WIN = 64
SLOT_ALIGN = 16
VMEM_LIMIT = 56 * 1024 * 1024

BF16 = jnp.bfloat16
F32 = jnp.float32


def _rms(x, g):
    return x * lax.rsqrt(jnp.mean(x * x, axis=-1, keepdims=True) + EPS) * g


def _dot(a, b):
    return jnp.dot(a, b, preferred_element_type=F32)


def _dot_nt(a, b):
    return lax.dot_general(a, b, (((1,), (1,)), ((), ())), preferred_element_type=F32)


def _dot_tn(a, b):
    return lax.dot_general(a, b, (((0,), (0,)), ((), ())), preferred_element_type=F32)


def _adaln_kernel(c_ref, w_ref, b_ref, o_ref):
    c = c_ref[...]
    a = (c * jax.nn.sigmoid(c)).astype(BF16)
    o_ref[...] = _dot(a, w_ref[...].astype(BF16)) + b_ref[...]


def _adaln(c, w_ada_all, b_ada, layer):
    b, d = c.shape
    n = w_ada_all.shape[1]
    return pl.pallas_call(
        _adaln_kernel,
        grid=(n // d,),
        in_specs=[
            pl.BlockSpec((b, d), lambda j: (0, 0)),
            pl.BlockSpec((d, d), lambda j: (layer, j)),
            pl.BlockSpec((1, d), lambda j: (0, j)),
        ],
        out_specs=pl.BlockSpec((b, d), lambda j: (0, j)),
        out_shape=jax.ShapeDtypeStruct((b, n), F32),
        compiler_params=pltpu.CompilerParams(dimension_semantics=("arbitrary",)),
        name="adaln",
    )(c, w_ada_all, b_ada.reshape(1, n))


def _gelu_tanh(x):
    c = np.float32(np.sqrt(2.0 / np.pi))
    return x * (0.5 * (1.0 + jnp.tanh(c * (x + 0.044715 * (x * x * x)))))


def _proj_kernel(x_ref, mod_ref, g_ref, w_ref, q_ref, kv_ref, u_ref, vg_ref):
    shift = mod_ref[0, 0:1, :]
    gain = g_ref[...] * (1.0 + mod_ref[0, 1:2, :])
    n_att = D_ATTN + 2 * D_KV
    dg = u_ref.shape[1]
    lo = lax.broadcasted_iota(jnp.int32, (1, LANES), 1) < HEAD_DIM
    for r in range(0, TM, PROJ_ROWS):
        rows = slice(r, r + PROJ_ROWS)
        h = (_rms(x_ref[rows, :], gain) + shift).astype(BF16)
        z = _gelu_tanh(_dot(h, w_ref[:, n_att:]))
        u_ref[rows, :] = z[:, :dg]
        vg_ref[rows, :] = z[:, dg:]
        p = _dot(h, w_ref[:, :n_att])
        q_ref[rows, :] = (p[:, :D_ATTN] * (HEAD_DIM ** -0.5)).astype(BF16)
        for j in range(2):
            t = p[:, D_ATTN + j * D_KV:D_ATTN + (j + 1) * D_KV]
            t_sw = pltpu.roll(t, HEAD_DIM, axis=1)
            kv_ref[rows, (2 * j) * LANES:(2 * j + 1) * LANES] = jnp.where(lo, t, t_sw).astype(BF16)
            kv_ref[rows, (2 * j + 1) * LANES:(2 * j + 2) * LANES] = jnp.where(lo, t_sw, t).astype(BF16)


def _proj(x2, mod, g_pre, w_in, s):
    t, d = x2.shape
    nw = w_in.shape[1]
    dg = (nw - D_ATTN - 2 * D_KV) // 2
    nsb = s // TM
    return pl.pallas_call(
        _proj_kernel,
        grid=(t // TM,),
        in_specs=[
            pl.BlockSpec((TM, d), lambda i: (i, 0)),
            pl.BlockSpec((1, 6, d), lambda i: (i // nsb, 0, 0)),
            pl.BlockSpec((1, d), lambda i: (0, 0)),
            pl.BlockSpec((d, nw), lambda i: (0, 0)),
        ],
        out_specs=[
            pl.BlockSpec((TM, D_ATTN), lambda i: (i, 0)),
            pl.BlockSpec((TM, D_ATTN), lambda i: (i, 0)),
            pl.BlockSpec((TM, dg), lambda i: (i, 0)),
            pl.BlockSpec((TM, dg), lambda i: (i, 0)),
        ],
        out_shape=[
            jax.ShapeDtypeStruct((t, D_ATTN), BF16),
            jax.ShapeDtypeStruct((t, D_ATTN), BF16),
            jax.ShapeDtypeStruct((t, dg), F32),
            jax.ShapeDtypeStruct((t, dg), F32),
        ],
        compiler_params=pltpu.CompilerParams(
            dimension_semantics=("arbitrary",), vmem_limit_bytes=VMEM_LIMIT),
        name="proj",
    )(x2, mod, g_pre, w_in)


def _attn_bias_table():
    qi = np.arange(BLOCK)[:, None]
    kj = np.arange(3 * BLOCK)[None, :] - BLOCK
    dist = np.abs(qi - kj).astype(np.float32)
    band = dist <= WINDOW
    slopes = np.exp2(-8.0 * np.arange(1, N_HEADS + 1, dtype=np.float32) / N_HEADS).astype(np.float32)
    key_ok = [kj >= 0, np.ones_like(kj, bool), kj < BLOCK]
    tab = np.empty((3, N_HEADS, BLOCK, 3 * BLOCK), np.float32)
    for v in range(3):
        ok = band & key_ok[v]
        for h in range(N_HEADS):
            tab[v, h] = np.where(ok, -slopes[h] * dist, np.float32(MASK_VALUE))
    return tab.reshape(3, N_KV_HEADS, GROUP * BLOCK, 3 * BLOCK)


def _mix_kernel(nblk_seq, sink_ref, x_ref, q_ref, kvp_ref, kvc_ref, kvn_ref, u_ref, vg_ref,
                mod_ref, bias_ref, ws_ref, bs_ref, lng_ref, lnb_ref, ga_ref, gg_ref,
                wout_ref, gpost_ref, gpre2_ref, wr_ref,
                x1_ref, h2_ref, aff_ref, kcat, attn_s, gm_s):
    i = pl.program_id(0)
    nb_tile = TM // BLOCK
    kcat[0:BLOCK, :] = kvp_ref[...]
    kcat[BLOCK:BLOCK + TM, :] = kvc_ref[...]
    kcat[BLOCK + TM:, :] = kvn_ref[...]
    lane = lax.broadcasted_iota(jnp.int32, (1, LANES), 1)
    lo = lane < HEAD_DIM
    zero = jnp.zeros((), BF16)
    head_of_row = lax.broadcasted_iota(jnp.int32, (GROUP * BLOCK, 1), 0) // BLOCK
    sinks = []
    for g in range(N_KV_HEADS):
        col = jnp.zeros((GROUP * BLOCK, 1), F32)
        for j in range(GROUP):
            col = jnp.where(head_of_row == j, sink_ref[g * GROUP + j], col)
        sinks.append(col)

    def block_body(a, carry):
        r0 = pl.multiple_of(a * BLOCK, BLOCK)
        pos = (i * nb_tile + a) % nblk_seq
        var = jnp.where(pos == 0, 0, jnp.where(pos == nblk_seq - 1, 2, 1))
        kv = kcat[pl.ds(r0, 3 * BLOCK), :]
        for g in range(N_KV_HEADS):
            k = kv[:, g * LANES:(g + 1) * LANES]
            v = kv[:, (N_KV_HEADS + g) * LANES:(N_KV_HEADS + g + 1) * LANES]
            pieces = []
            for pi in range(GROUP // 2):
                qp = q_ref[pl.ds(r0, BLOCK), (g * GROUP // 2 + pi) * LANES:(g * GROUP // 2 + pi + 1) * LANES]
                pieces += [jnp.where(lo, qp, zero), jnp.where(lo, zero, qp)]
            s = _dot_nt(jnp.concatenate(pieces, axis=0), k) + bias_ref[var, g]
            m = jnp.maximum(jnp.max(s, axis=-1, keepdims=True), sinks[g])
            p = jnp.exp(s - m)
            den = jnp.sum(p, axis=-1, keepdims=True) + jnp.exp(sinks[g] - m)
            o = _dot(p.astype(BF16), v) / den
            for pi in range(GROUP // 2):
                even = o[(2 * pi) * BLOCK:(2 * pi + 1) * BLOCK, :]
                odd = o[(2 * pi + 1) * BLOCK:(2 * pi + 2) * BLOCK, :]
                col0 = (g * GROUP // 2 + pi) * LANES
                attn_s[pl.ds(r0, BLOCK), col0:col0 + LANES] = jnp.where(lo, even, odd)
        vg = vg_ref[pl.ds(r0, BLOCK), :]
        mu = jnp.mean(vg, axis=-1, keepdims=True)
        xc = vg - mu
        vn = (xc * lax.rsqrt(jnp.mean(xc * xc, axis=-1, keepdims=True) + EPS) * lng_ref[...]
              + lnb_ref[...]).astype(BF16)
        for pi in range(N_GMLP_GROUPS // 2):
            vnp = vn[:, pi * LANES:(pi + 1) * LANES]
            z = jnp.where(lo, _dot(ws_ref[2 * pi], vnp), _dot(ws_ref[2 * pi + 1], vnp))
            sl = slice(pi * LANES, (pi + 1) * LANES)
            gm_s[pl.ds(r0, BLOCK), sl] = u_ref[pl.ds(r0, BLOCK), sl] * (z + bs_ref[:, sl])
        return carry

    lax.fori_loop(0, nb_tile, block_body, 0, unroll=2)

    na = _rms(attn_s[...], ga_ref[...]).astype(BF16)
    ng = _rms(gm_s[...], gg_ref[...]).astype(BF16)
    mo = _dot(na, wout_ref[0:D_ATTN, :]) + _dot(ng, wout_ref[D_ATTN:, :])
    x1 = x_ref[...] + _rms(mo, gpost_ref[...] * mod_ref[0, 2:3, :])
    x1_ref[...] = x1
    h2 = (_rms(x1, gpre2_ref[...] * (1.0 + mod_ref[0, 4:5, :])) + mod_ref[0, 3:4, :]).astype(BF16)
    h2_ref[...] = h2
    lg = _dot_nt(wr_ref[...], h2)
    lg = lg - jnp.max(lg, axis=0, keepdims=True)
    ex = jnp.exp(lg)
    aff_ref[0] = ex / jnp.sum(ex, axis=0, keepdims=True)


def _mix(x2, q, kv, u, vg, mod, sink, bias, ws, bs_full, ln_g, ln_b, g_attn, g_gmlp,
         w_out, g_post, g_pre2, w_r_t, b, s):
    t, d = x2.shape
    dg = u.shape[1]
    e = w_r_t.shape[0]
    nsb = s // TM
    nb_tile = TM // BLOCK
    nblk = t // BLOCK
    row = lambda i: (i, 0)
    const2 = lambda i: (0, 0)
    in_specs = [
        pl.BlockSpec(memory_space=pltpu.SMEM),
        pl.BlockSpec((TM, d), row),
        pl.BlockSpec((TM, D_ATTN), row),
        pl.BlockSpec((BLOCK, 4 * LANES), lambda i: (jnp.maximum(i * nb_tile - 1, 0), 0)),
        pl.BlockSpec((TM, 4 * LANES), row),
        pl.BlockSpec((BLOCK, 4 * LANES), lambda i: (jnp.minimum((i + 1) * nb_tile, nblk - 1), 0)),
        pl.BlockSpec((TM, dg), row),
        pl.BlockSpec((TM, dg), row),
        pl.BlockSpec((1, 6, d), lambda i: (i // nsb, 0, 0)),
        pl.BlockSpec(bias.shape, lambda i: (0, 0, 0, 0)),
        pl.BlockSpec(ws.shape, lambda i: (0, 0, 0)),
        pl.BlockSpec((BLOCK, dg), const2),
        pl.BlockSpec((1, dg), const2),
        pl.BlockSpec((1, dg), const2),
        pl.BlockSpec((1, D_ATTN), const2),
        pl.BlockSpec((1, dg), const2),
        pl.BlockSpec(w_out.shape, const2),
        pl.BlockSpec((1, d), const2),
        pl.BlockSpec((1, d), const2),
        pl.BlockSpec(w_r_t.shape, const2),
    ]
    out_specs = [
        pl.BlockSpec((TM, d), row),
        pl.BlockSpec((TM, d), row),
        pl.BlockSpec((1, e, TM), lambda i: (i // nsb, 0, i % nsb)),
    ]
    out_shape = [
        jax.ShapeDtypeStruct((t, d), F32),
        jax.ShapeDtypeStruct((t, d), BF16),
        jax.ShapeDtypeStruct((b, e, s), F32),
    ]
    return pl.pallas_call(
        functools.partial(_mix_kernel, s // BLOCK),
        grid=(t // TM,),
        in_specs=in_specs,
        out_specs=out_specs,
        out_shape=out_shape,
        scratch_shapes=[
            pltpu.VMEM((TM + 2 * BLOCK, 4 * LANES), BF16),
            pltpu.VMEM((TM, D_ATTN), F32),
            pltpu.VMEM((TM, dg), F32),
        ],
        compiler_params=pltpu.CompilerParams(
            dimension_semantics=("arbitrary",), vmem_limit_bytes=VMEM_LIMIT),
        name="mix",
    )(sink, x2, q, kv, kv, kv, u, vg, mod, bias, ws, bs_full, ln_g, ln_b, g_attn, g_gmlp,
      w_out, g_post, g_pre2, w_r_t)


def _route_kernel(cap, aff_ref, slot_ref, affb_ref, cnt_ref):
    aff = aff_ref[...]
    e, s = aff.shape
    capf = jnp.float32(cap)

    def count_ge(t):
        return jnp.sum(jnp.where(aff >= t, 1.0, 0.0), axis=1, keepdims=True)

    def bit_body(it, cand):
        trial = cand | jnp.left_shift(jnp.int32(1), 30 - it)
        t = lax.bitcast_convert_type(trial, F32)
        return jnp.where(count_ge(t) >= capf, trial, cand)

    cand = lax.fori_loop(0, 31, bit_body, jnp.zeros((e, 1), jnp.int32))
    thr = lax.bitcast_convert_type(cand, F32)
    need = capf - jnp.sum(jnp.where(aff > thr, 1.0, 0.0), axis=1, keepdims=True)

    r = lax.broadcasted_iota(jnp.int32, (LANES, LANES), 0)
    c = lax.broadcasted_iota(jnp.int32, (LANES, LANES), 1)
    tri = jnp.where(r < c, 1.0, 0.0).astype(BF16)
    lane = lax.broadcasted_iota(jnp.int32, (e, LANES), 1)
    carry_eq = jnp.zeros((e, 1), F32)
    carry_sel = jnp.zeros((e, 1), F32)
    cnt = jnp.zeros((e, LANES), F32)
    per_blk = MXU_DIM // LANES
    for j in range(s // LANES):
        a = aff[:, j * LANES:(j + 1) * LANES]
        gt = a > thr
        eq = jnp.where(a == thr, 1.0, 0.0)
        tie_rank = _dot(eq.astype(BF16), tri) + carry_eq
        sel = jnp.where(gt | ((eq > 0.0) & (tie_rank < need)), 1.0, 0.0)
        slot = _dot(sel.astype(BF16), tri) + carry_sel
        blk = (slice(None), j // per_blk, slice((j % per_blk) * LANES, (j % per_blk + 1) * LANES))
        slot_ref[blk] = jnp.where(sel > 0.0, slot, -1.0).astype(jnp.int32)
        affb_ref[blk] = a
        if j % per_blk == 0:
            cnt = jnp.where(lane == j // per_blk, carry_sel, cnt)
        carry_eq = carry_eq + jnp.sum(eq, axis=1, keepdims=True)
        carry_sel = carry_sel + jnp.sum(sel, axis=1, keepdims=True)
    cnt = jnp.where(lane >= s // MXU_DIM, carry_sel, cnt)
    cnt_ref[...] = cnt.astype(jnp.int32)


def _route(aff2, cap):
    r, s = aff2.shape
    nblk = s // MXU_DIM
    return pl.pallas_call(
        functools.partial(_route_kernel, cap),
        grid=(1,),
        in_specs=[pl.BlockSpec((r, s), lambda i: (0, 0))],
        out_specs=[
            pl.BlockSpec((r, nblk, MXU_DIM), lambda i: (0, 0, 0)),
            pl.BlockSpec((r, nblk, MXU_DIM), lambda i: (0, 0, 0)),
            pl.BlockSpec((r, LANES), lambda i: (0, 0)),
        ],
        out_shape=[
            jax.ShapeDtypeStruct((r, nblk, MXU_DIM), jnp.int32),
            jax.ShapeDtypeStruct((r, nblk, MXU_DIM), F32),
            jax.ShapeDtypeStruct((r, LANES), jnp.int32),
        ],
        compiler_params=pltpu.CompilerParams(dimension_semantics=("arbitrary",)),
        name="route",
    )(aff2)


def _segment(cnt_ref, row, pb, cap):
    c0 = jnp.minimum(cnt_ref[row, pb], cap)
    c1 = jnp.minimum(cnt_ref[row, pb + 1], cap)
    a0 = jnp.minimum(jnp.bitwise_and(c0, -SLOT_ALIGN), cap - WIN)
    return c0, c1, pl.multiple_of(a0, SLOT_ALIGN)


def _extra_windows(c1, a0):
    return (jnp.maximum(c1 - (a0 + WIN), 0) + WIN - 1) // WIN


def _gather_kernel(n_exp, cnt_ref, h2_ref, slot_ref, aff_ref, x_ref, gs_ref):
    b, eg = pl.program_id(0), pl.program_id(1)
    ng, cap = x_ref.shape[1], x_ref.shape[2]
    nblk = slot_ref.shape[2]
    sub = lax.broadcasted_iota(jnp.int32, (WIN, MXU_DIM), 0)
    x_ref[...] = jnp.zeros_like(x_ref)
    gs_ref[...] = jnp.zeros_like(gs_ref)

    def merge(k, a0, rows, gate):
        win = (0, k, pl.ds(a0, WIN), slice(None))
        x_ref[win] += rows.astype(BF16)
        gs_ref[win] += gate

    def tokens(pb):
        return pl.ds(pl.multiple_of(pb * MXU_DIM, MXU_DIM), MXU_DIM)

    def first_windows(pb):
        segs, gates, onehots = [], [], []
        overflow = jnp.int32(0)
        for k in range(ng):
            c0, c1, a0 = _segment(cnt_ref, b * n_exp + eg * ng + k, pb, cap)
            match = slot_ref[0, k, pl.ds(pb, 1), :] == (sub + a0)
            onehots.append(jnp.where(match, 1.0, 0.0).astype(BF16))
            gates.append(jnp.sum(jnp.where(match, aff_ref[0, k, pl.ds(pb, 1), :], 0.0),
                                 axis=1, keepdims=True))
            segs.append((c0, c1, a0))
            overflow = overflow + jnp.maximum(c1 - (a0 + WIN), 0)
        rows = _dot(jnp.concatenate(onehots, axis=0), h2_ref[0, tokens(pb), :])
        for k in range(ng):
            merge(k, segs[k][2], rows[k * WIN:(k + 1) * WIN, :], gates[k])
        return overflow

    def further_windows(pb):
        def per_expert(k, carry2):
            _, c1, a0 = _segment(cnt_ref, b * n_exp + eg * ng + k, pb, cap)

            def per_window(w, carry3):
                first = a0 + (w + 1) * WIN
                aw = pl.multiple_of(jnp.minimum(first, cap - WIN), SLOT_ALIGN)
                srow = slot_ref[0, k, pl.ds(pb, 1), :]
                match = (srow == (sub + aw)) & (srow >= first)
                onehot = jnp.where(match, 1.0, 0.0).astype(BF16)
                gate = jnp.sum(jnp.where(match, aff_ref[0, k, pl.ds(pb, 1), :], 0.0),
                               axis=1, keepdims=True)
                merge(k, aw, _dot(onehot, h2_ref[0, tokens(pb), :]), gate)
                return carry3

            return lax.fori_loop(0, _extra_windows(c1, a0), per_window, carry2)

        lax.fori_loop(0, ng, per_expert, 0)

    def trip_body(i, carry):
        overflows = [first_windows(GATHER_BLOCKS * i + j) for j in range(GATHER_BLOCKS)]
        for j in range(GATHER_BLOCKS):
            pl.when(overflows[j] > 0)(functools.partial(further_windows, GATHER_BLOCKS * i + j))
        return carry

    lax.fori_loop(0, nblk // GATHER_BLOCKS, trip_body, 0)


def _gather(cnt2, h2, slots4, aff4, cap):
    b, s, d = h2.shape
    e = slots4.shape[1]
    nblk = s // MXU_DIM
    grid_spec = pltpu.PrefetchScalarGridSpec(
        num_scalar_prefetch=1,
        grid=(b, e // EG),
        in_specs=[
            pl.BlockSpec((1, s, d), lambda bi, gi, c: (bi, 0, 0)),
            pl.BlockSpec((1, EG, nblk, MXU_DIM), lambda bi, gi, c: (bi, gi, 0, 0)),
            pl.BlockSpec((1, EG, nblk, MXU_DIM), lambda bi, gi, c: (bi, gi, 0, 0)),
        ],
        out_specs=[
            pl.BlockSpec((1, EG, cap, d), lambda bi, gi, c: (bi, gi, 0, 0)),
            pl.BlockSpec((1, EG, cap, LANES), lambda bi, gi, c: (bi, gi, 0, 0)),
        ],
    )
    return pl.pallas_call(
        functools.partial(_gather_kernel, e),
        grid_spec=grid_spec,
        out_shape=[
            jax.ShapeDtypeStruct((b, e, cap, d), BF16),
            jax.ShapeDtypeStruct((b, e, cap, LANES), F32),
        ],
        compiler_params=pltpu.CompilerParams(
            dimension_semantics=("arbitrary", "arbitrary"), vmem_limit_bytes=VMEM_LIMIT),
        name="gather",
    )(cnt2, h2, slots4, aff4)


def _ffn_kernel(nchunk, x_ref, gs_ref, *refs):
    w_refs, out_ref, acc_s = refs[:-2], refs[-2], refs[-1]
    groups = [w_refs[3 * k:3 * k + 3] for k in range(FFN_CHUNKS)]
    e, f = pl.program_id(0), pl.program_id(1)
    nb, _, cap, d = x_ref.shape
    last = pl.num_programs(1) - 1

    @pl.when((e == 0) & (f == 0))
    def _init():
        acc_s[...] = jnp.zeros_like(acc_s)

    def accumulate(chunks):
        wg = jnp.concatenate([w[0][0].astype(BF16) for w in chunks], axis=1)
        wu = jnp.concatenate([w[1][0].astype(BF16) for w in chunks], axis=1)
        wd = jnp.concatenate([w[2][0].astype(BF16) for w in chunks], axis=0)
        for bi in range(nb):
            x = x_ref[bi, 0]
            g = _dot(x, wg)
            u = _dot(x, wu)
            hid = (g * jax.nn.sigmoid(g) * u).astype(BF16)
            y = _dot(hid, wd)
            rows = slice(bi * cap, (bi + 1) * cap)
            acc_s[rows, :] = jnp.where(f == 0, y, acc_s[rows, :] + y)

    tail = nchunk % FFN_CHUNKS
    if tail == 0:
        accumulate(groups)
    else:
        pl.when(f < last)(lambda: accumulate(groups))
        pl.when(f == last)(lambda: accumulate(groups[:tail]))

    @pl.when(f == last)
    def _emit():
        gate = gs_ref[:, 0, :, 0:1].reshape(nb * cap, 1)
        out_ref[:, 0] = (acc_s[...] * gate).reshape(nb, cap, d).astype(BF16)


def _ffn(xg, gs, w_gate, w_up, w_down, e0):
    b, e, cap, d = xg.shape
    nchunk = w_gate.shape[2] // FC
    weight_specs = []
    for k in range(FFN_CHUNKS):
        chunk = lambda fi, k=k: jnp.minimum(FFN_CHUNKS * fi + k, nchunk - 1)
        weight_specs += [
            pl.BlockSpec((1, d, FC), lambda ei, fi, chunk=chunk: (e0 + ei, 0, chunk(fi))),
            pl.BlockSpec((1, d, FC), lambda ei, fi, chunk=chunk: (e0 + ei, 0, chunk(fi))),
            pl.BlockSpec((1, FC, d), lambda ei, fi, chunk=chunk: (e0 + ei, chunk(fi), 0)),
        ]
    return pl.pallas_call(
        functools.partial(_ffn_kernel, nchunk),
        grid=(e, pl.cdiv(nchunk, FFN_CHUNKS)),
        in_specs=[
            pl.BlockSpec((b, 1, cap, d), lambda ei, fi: (0, ei, 0, 0)),
            pl.BlockSpec((b, 1, cap, LANES), lambda ei, fi: (0, ei, 0, 0)),
        ] + weight_specs,
        out_specs=pl.BlockSpec((b, 1, cap, d), lambda ei, fi: (0, ei, 0, 0)),
        out_shape=jax.ShapeDtypeStruct((b, e, cap, d), BF16),
        scratch_shapes=[pltpu.VMEM((b * cap, d), F32)],
        compiler_params=pltpu.CompilerParams(
            dimension_semantics=("arbitrary", "arbitrary"), vmem_limit_bytes=VMEM_LIMIT),
        name="ffn",
    )(xg, gs, *([w_gate, w_up, w_down] * FFN_CHUNKS))


def _combine_kernel(n_exp, cnt_ref, eo_ref, slot_ref, x1_ref, mod_ref, g_ref, o_ref,
                    stage_s, oh_s, y_s):
    b, q = pl.program_id(0), pl.program_id(1)
    cap = eo_ref.shape[2]
    nblk_q = o_ref.shape[0] // MXU_DIM
    sub = lax.broadcasted_iota(jnp.int32, (WIN, MXU_DIM), 0)
    gain = g_ref[...] * mod_ref[0, 5:6, :]

    for j in range(nblk_q):
        pb = q * nblk_q + j
        overflow = jnp.int32(0)
        for e in range(n_exp):
            _, c1, a0 = _segment(cnt_ref, b * n_exp + e, pb, cap)
            stage_s[e * WIN:(e + 1) * WIN, :] = eo_ref[0, e, pl.ds(a0, WIN), :]
            match = slot_ref[0, e, pl.ds(pb, 1), :] == (sub + a0)
            oh_s[e * WIN:(e + 1) * WIN, :] = jnp.where(match, 1.0, 0.0).astype(BF16)
            overflow = overflow + jnp.maximum(c1 - (a0 + WIN), 0)
        y_s[...] = _dot_tn(oh_s[...], stage_s[...])

        @pl.when(overflow > 0)
        def _long_segments(j=j, pb=pb):
            def per_expert(e, carry):
                _, c1, a0 = _segment(cnt_ref, b * n_exp + e, pb, cap)

                def per_window(k, carry2):
                    first = a0 + (k + 1) * WIN
                    aw = pl.multiple_of(jnp.minimum(first, cap - WIN), SLOT_ALIGN)
                    srow = slot_ref[0, e, pl.ds(pb, 1), :]
                    match = (srow == (sub + aw)) & (srow >= first)
                    onehot = jnp.where(match, 1.0, 0.0).astype(BF16)
                    y_s[...] += _dot_tn(onehot, eo_ref[0, e, pl.ds(aw, WIN), :])
                    return carry2

                return lax.fori_loop(0, _extra_windows(c1, a0), per_window, carry)

            lax.fori_loop(0, n_exp, per_expert, 0)

        sl = slice(j * MXU_DIM, (j + 1) * MXU_DIM)
        o_ref[sl, :] = x1_ref[sl, :] + _rms(y_s[...], gain)


def _combine(cnt2, eo, slots4, x1, mod, g_post):
    b, e, cap, d = eo.shape
    t = x1.shape[0]
    s = t // b
    nq = s // QP
    nblk_q = QP // MXU_DIM
    grid_spec = pltpu.PrefetchScalarGridSpec(
        num_scalar_prefetch=1,
        grid=(b, nq),
        in_specs=[
            pl.BlockSpec((1, e, cap, d), lambda bi, qi, c: (bi, 0, 0, 0), pipeline_mode=pl.Buffered(1)),
            pl.BlockSpec((1, e, s // MXU_DIM, MXU_DIM), lambda bi, qi, c: (bi, 0, 0, 0)),
            pl.BlockSpec((QP, d), lambda bi, qi, c: (bi * nq + qi, 0)),
            pl.BlockSpec((1, 6, d), lambda bi, qi, c: (bi, 0, 0)),
            pl.BlockSpec((1, d), lambda bi, qi, c: (0, 0)),
        ],
        out_specs=pl.BlockSpec((QP, d), lambda bi, qi, c: (bi * nq + qi, 0)),
        scratch_shapes=[
            pltpu.VMEM((e * WIN, d), BF16),
            pltpu.VMEM((e * WIN, MXU_DIM), BF16),
            pltpu.VMEM((MXU_DIM, d), F32),
        ],
    )
    return pl.pallas_call(
        functools.partial(_combine_kernel, e),
        grid_spec=grid_spec,
        out_shape=jax.ShapeDtypeStruct((t, d), F32),
        compiler_params=pltpu.CompilerParams(
            dimension_semantics=("arbitrary", "arbitrary"),
            vmem_limit_bytes=VMEM_LIMIT),
        name="combine",
    )(cnt2, eo, slots4, x1, mod, g_post)


def _layer(layer, x, c, w_ada_all, b_ada, norm_pre_mix, norm_post_mix, w_in, sink, sgu_ln_g,
           sgu_ln_b, w_s, b_s, norm_out_attn, norm_out_gmlp, w_out, norm_pre_ffn, norm_post_ffn,
           w_router, w_gate_all, w_up_all, w_down_all):
    b, s, d = x.shape
    t = b * s
    dg = sgu_ln_g.shape[0]
    e = w_router.shape[1]
    cap = CAPACITY_FACTOR * s // e
    assert s % QP == 0 and s % TM == 0 and w_gate_all.shape[2] % FC == 0 and e % EG == 0
    assert cap % SLOT_ALIGN == 0 and cap >= WIN and s // MXU_DIM < LANES and d == D_ATTN + dg
    assert (s // MXU_DIM) % GATHER_BLOCKS == 0

    mod = _adaln(c, w_ada_all, b_ada, layer).reshape(b, 6, d)

    x2 = x.reshape(t, d)
    q, kv, u, vg = _proj(x2, mod, norm_pre_mix.reshape(1, d), w_in.astype(BF16), s)

    bias = jnp.asarray(_attn_bias_table())
    bs_full = jnp.repeat(b_s.T, dg // N_GMLP_GROUPS, axis=1)
    x1, h2, aff_t = _mix(
        x2, q, kv, u, vg, mod, sink, bias, w_s.astype(BF16), bs_full,
        sgu_ln_g.reshape(1, dg), sgu_ln_b.reshape(1, dg), norm_out_attn.reshape(1, D_ATTN),
        norm_out_gmlp.reshape(1, dg), w_out.astype(BF16), norm_post_mix.reshape(1, d),
        norm_pre_ffn.reshape(1, d), w_router.T.astype(BF16), b, s)

    slots3, aff3, cnt2 = _route(aff_t.reshape(b * e, s), cap)
    nblk = s // MXU_DIM
    slots4 = slots3.reshape(b, e, nblk, MXU_DIM)
    aff4 = aff3.reshape(b, e, nblk, MXU_DIM)
    xg, gs = _gather(cnt2, h2.reshape(b, s, d), slots4, aff4, cap)
    eo = _ffn(xg, gs, w_gate_all, w_up_all, w_down_all, layer * e)
    out = _combine(cnt2, eo, slots4, x1, mod, norm_post_ffn.reshape(1, d))
    return out.reshape(b, s, d)


def kernel(x, c, w_ada, b_ada, norm_pre_mix, norm_post_mix, w_in, sink, sgu_ln_g, sgu_ln_b, w_s, b_s, norm_out_attn, norm_out_gmlp, w_out, norm_pre_ffn, norm_post_ffn, w_router, w_gate, w_up, w_down):
    depth, d = w_ada.shape[0], w_ada.shape[1]
    w_ada_all = w_ada.reshape(depth * d, w_ada.shape[2])
    stack = lambda w: w.reshape((depth * w.shape[1],) + w.shape[2:])
    w_gate_all, w_up_all, w_down_all = stack(w_gate), stack(w_up), stack(w_down)
    for l in range(depth):
        x = _layer(l, x, c, w_ada_all, b_ada[l], norm_pre_mix[l], norm_post_mix[l], w_in[l],
                   sink[l], sgu_ln_g[l], sgu_ln_b[l], w_s[l], b_s[l], norm_out_attn[l],
                   norm_out_gmlp[l], w_out[l], norm_pre_ffn[l], norm_post_ffn[l], w_router[l],
                   w_gate_all, w_up_all, w_down_all)
    return x
```

```python
import functools

import numpy as np
import jax
import jax.numpy as jnp
from jax import lax
from jax.experimental import pallas as pl
from jax.experimental.pallas import tpu as pltpu

N_HEADS = 8
N_KV_HEADS = 2
HEAD_DIM = 64
D_ATTN = N_HEADS * HEAD_DIM
D_KV = N_KV_HEADS * HEAD_DIM
GROUP = N_HEADS // N_KV_HEADS
WINDOW = 128
BLOCK = 128
N_GMLP_GROUPS = 8
CAPACITY_FACTOR = 2
EPS = 1e-6
MASK_VALUE = -1e30

LANES = 128
MXU_DIM = 256

TM = 1024
FC = 256
FFN_CHUNKS = 3
PROJ_ROWS = 256
EG = 4
GATHER_BLOCKS = 8
QP = 1024
WIN = 64
SLOT_ALIGN = 16
VMEM_LIMIT = 56 * 1024 * 1024

BF16 = jnp.bfloat16
F32 = jnp.float32


def _rms(x, g):
    return x * lax.rsqrt(jnp.mean(x * x, axis=-1, keepdims=True) + EPS) * g


def _dot(a, b):
    return jnp.dot(a, b, preferred_element_type=F32)


def _dot_nt(a, b):
    return lax.dot_general(a, b, (((1,), (1,)), ((), ())), preferred_element_type=F32)


def _dot_tn(a, b):
    return lax.dot_general(a, b, (((0,), (0,)), ((), ())), preferred_element_type=F32)


def _adaln_kernel(c_ref, w_ref, b_ref, o_ref):
    c = c_ref[...]
    a = (c * jax.nn.sigmoid(c)).astype(BF16)
    o_ref[...] = _dot(a, w_ref[...].astype(BF16)) + b_ref[...]


def _adaln(c, w_ada_all, b_ada, layer):
    b, d = c.shape
    n = w_ada_all.shape[1]
    return pl.pallas_call(
        _adaln_kernel,
        grid=(n // d,),
        in_specs=[
            pl.BlockSpec((b, d), lambda j: (0, 0)),
            pl.BlockSpec((d, d), lambda j: (layer, j)),
            pl.BlockSpec((1, d), lambda j: (0, j)),
        ],
        out_specs=pl.BlockSpec((b, d), lambda j: (0, j)),
        out_shape=jax.ShapeDtypeStruct((b, n), F32),
        compiler_params=pltpu.CompilerParams(dimension_semantics=("arbitrary",)),
        name="adaln",
    )(c, w_ada_all, b_ada.reshape(1, n))


def _gelu_tanh(x):
    c = np.float32(np.sqrt(2.0 / np.pi))
    return x * (0.5 * (1.0 + jnp.tanh(c * (x + 0.044715 * (x * x * x)))))


def _proj_kernel(x_ref, mod_ref, g_ref, w_ref, q_ref, kv_ref, u_ref, vg_ref):
    shift = mod_ref[0, 0:1, :]
    gain = g_ref[...] * (1.0 + mod_ref[0, 1:2, :])
    n_att = D_ATTN + 2 * D_KV
    dg = u_ref.shape[1]
    lo = lax.broadcasted_iota(jnp.int32, (1, LANES), 1) < HEAD_DIM
    for r in range(0, TM, PROJ_ROWS):
        rows = slice(r, r + PROJ_ROWS)
        h = (_rms(x_ref[rows, :], gain) + shift).astype(BF16)
        z = _gelu_tanh(_dot(h, w_ref[:, n_att:]))
        u_ref[rows, :] = z[:, :dg]
        vg_ref[rows, :] = z[:, dg:]
        p = _dot(h, w_ref[:, :n_att])
        q_ref[rows, :] = (p[:, :D_ATTN] * (HEAD_DIM ** -0.5)).astype(BF16)
        for j in range(2):
            t = p[:, D_ATTN + j * D_KV:D_ATTN + (j + 1) * D_KV]
            t_sw = pltpu.roll(t, HEAD_DIM, axis=1)
            kv_ref[rows, (2 * j) * LANES:(2 * j + 1) * LANES] = jnp.where(lo, t, t_sw).astype(BF16)
            kv_ref[rows, (2 * j + 1) * LANES:(2 * j + 2) * LANES] = jnp.where(lo, t_sw, t).astype(BF16)


def _proj(x2, mod, g_pre, w_in, s):
    t, d = x2.shape
    nw = w_in.shape[1]
    dg = (nw - D_ATTN - 2 * D_KV) // 2
    nsb = s // TM
    return pl.pallas_call(
        _proj_kernel,
        grid=(t // TM,),
        in_specs=[
            pl.BlockSpec((TM, d), lambda i: (i, 0)),
            pl.BlockSpec((1, 6, d), lambda i: (i // nsb, 0, 0)),
            pl.BlockSpec((1, d), lambda i: (0, 0)),
            pl.BlockSpec((d, nw), lambda i: (0, 0)),
        ],
        out_specs=[
            pl.BlockSpec((TM, D_ATTN), lambda i: (i, 0)),
            pl.BlockSpec((TM, D_ATTN), lambda i: (i, 0)),
            pl.BlockSpec((TM, dg), lambda i: (i, 0)),
            pl.BlockSpec((TM, dg), lambda i: (i, 0)),
        ],
        out_shape=[
            jax.ShapeDtypeStruct((t, D_ATTN), BF16),
            jax.ShapeDtypeStruct((t, D_ATTN), BF16),
            jax.ShapeDtypeStruct((t, dg), F32),
            jax.ShapeDtypeStruct((t, dg), F32),
        ],
        compiler_params=pltpu.CompilerParams(
            dimension_semantics=("arbitrary",), vmem_limit_bytes=VMEM_LIMIT),
        name="proj",
    )(x2, mod, g_pre, w_in)


def _attn_bias_table():
    qi = np.arange(BLOCK)[:, None]
    kj = np.arange(3 * BLOCK)[None, :] - BLOCK
    dist = np.abs(qi - kj).astype(np.float32)
    band = dist <= WINDOW
    slopes = np.exp2(-8.0 * np.arange(1, N_HEADS + 1, dtype=np.float32) / N_HEADS).astype(np.float32)
    key_ok = [kj >= 0, np.ones_like(kj, bool), kj < BLOCK]
    tab = np.empty((3, N_HEADS, BLOCK, 3 * BLOCK), np.float32)
    for v in range(3):
        ok = band & key_ok[v]
        for h in range(N_HEADS):
            tab[v, h] = np.where(ok, -slopes[h] * dist, np.float32(MASK_VALUE))
    return tab.reshape(3, N_KV_HEADS, GROUP * BLOCK, 3 * BLOCK)


def _mix_kernel(nblk_seq, sink_ref, x_ref, q_ref, kvp_ref, kvc_ref, kvn_ref, u_ref, vg_ref,
                mod_ref, bias_ref, ws_ref, bs_ref, lng_ref, lnb_ref, ga_ref, gg_ref,
                wout_ref, gpost_ref, gpre2_ref, wr_ref,
                x1_ref, h2_ref, aff_ref, kcat, attn_s, gm_s):
    i = pl.program_id(0)
    nb_tile = TM // BLOCK
    kcat[0:BLOCK, :] = kvp_ref[...]
    kcat[BLOCK:BLOCK + TM, :] = kvc_ref[...]
    kcat[BLOCK + TM:, :] = kvn_ref[...]
    lane = lax.broadcasted_iota(jnp.int32, (1, LANES), 1)
    lo = lane < HEAD_DIM
    zero = jnp.zeros((), BF16)
    head_of_row = lax.broadcasted_iota(jnp.int32, (GROUP * BLOCK, 1), 0) // BLOCK
    sinks = []
    for g in range(N_KV_HEADS):
        col = jnp.zeros((GROUP * BLOCK, 1), F32)
        for j in range(GROUP):
            col = jnp.where(head_of_row == j, sink_ref[g * GROUP + j], col)
        sinks.append(col)

    def block_body(a, carry):
        r0 = pl.multiple_of(a * BLOCK, BLOCK)
        pos = (i * nb_tile + a) % nblk_seq
        var = jnp.where(pos == 0, 0, jnp.where(pos == nblk_seq - 1, 2, 1))
        kv = kcat[pl.ds(r0, 3 * BLOCK), :]
        for g in range(N_KV_HEADS):
            k = kv[:, g * LANES:(g + 1) * LANES]
            v = kv[:, (N_KV_HEADS + g) * LANES:(N_KV_HEADS + g + 1) * LANES]
            pieces = []
            for pi in range(GROUP // 2):
                qp = q_ref[pl.ds(r0, BLOCK), (g * GROUP // 2 + pi) * LANES:(g * GROUP // 2 + pi + 1) * LANES]
                pieces += [jnp.where(lo, qp, zero), jnp.where(lo, zero, qp)]
            s = _dot_nt(jnp.concatenate(pieces, axis=0), k) + bias_ref[var, g]
            m = jnp.maximum(jnp.max(s, axis=-1, keepdims=True), sinks[g])
            p = jnp.exp(s - m)
            den = jnp.sum(p, axis=-1, keepdims=True) + jnp.exp(sinks[g] - m)
            o = _dot(p.astype(BF16), v) / den
            for pi in range(GROUP // 2):
                even = o[(2 * pi) * BLOCK:(2 * pi + 1) * BLOCK, :]
                odd = o[(2 * pi + 1) * BLOCK:(2 * pi + 2) * BLOCK, :]
                col0 = (g * GROUP // 2 + pi) * LANES
                attn_s[pl.ds(r0, BLOCK), col0:col0 + LANES] = jnp.where(lo, even, odd)
        vg = vg_ref[pl.ds(r0, BLOCK), :]
        mu = jnp.mean(vg, axis=-1, keepdims=True)
        xc = vg - mu
        vn = (xc * lax.rsqrt(jnp.mean(xc * xc, axis=-1, keepdims=True) + EPS) * lng_ref[...]
              + lnb_ref[...]).astype(BF16)
        for pi in range(N_GMLP_GROUPS // 2):
            vnp = vn[:, pi * LANES:(pi + 1) * LANES]
            both = _dot(ws_ref[pi], vnp)
            z = jnp.where(lo, both[:BLOCK], both[BLOCK:])
            sl = slice(pi * LANES, (pi + 1) * LANES)
            gm_s[pl.ds(r0, BLOCK), sl] = u_ref[pl.ds(r0, BLOCK), sl] * (z + bs_ref[:, sl])
        return carry

    lax.fori_loop(0, nb_tile, block_body, 0, unroll=2)

    na = _rms(attn_s[...], ga_ref[...]).astype(BF16)
    ng = _rms(gm_s[...], gg_ref[...]).astype(BF16)
    mo = _dot(na, wout_ref[0:D_ATTN, :]) + _dot(ng, wout_ref[D_ATTN:, :])
    x1 = x_ref[...] + _rms(mo, gpost_ref[...] * mod_ref[0, 2:3, :])
    x1_ref[...] = x1
    h2 = (_rms(x1, gpre2_ref[...] * (1.0 + mod_ref[0, 4:5, :])) + mod_ref[0, 3:4, :]).astype(BF16)
    h2_ref[...] = h2
    lg = _dot_nt(wr_ref[...], h2)
    lg = lg - jnp.max(lg, axis=0, keepdims=True)
    ex = jnp.exp(lg)
    aff_ref[0] = ex / jnp.sum(ex, axis=0, keepdims=True)


def _mix(x2, q, kv, u, vg, mod, sink, bias, ws, bs_full, ln_g, ln_b, g_attn, g_gmlp,
         w_out, g_post, g_pre2, w_r_t, b, s):
    t, d = x2.shape
    dg = u.shape[1]
    e = w_r_t.shape[0]
    nsb = s // TM
    nb_tile = TM // BLOCK
    nblk = t // BLOCK
    row = lambda i: (i, 0)
    const2 = lambda i: (0, 0)
    in_specs = [
        pl.BlockSpec(memory_space=pltpu.SMEM),
        pl.BlockSpec((TM, d), row),
        pl.BlockSpec((TM, D_ATTN), row),
        pl.BlockSpec((BLOCK, 4 * LANES), lambda i: (jnp.maximum(i * nb_tile - 1, 0), 0)),
        pl.BlockSpec((TM, 4 * LANES), row),
        pl.BlockSpec((BLOCK, 4 * LANES), lambda i: (jnp.minimum((i + 1) * nb_tile, nblk - 1), 0)),
        pl.BlockSpec((TM, dg), row),
        pl.BlockSpec((TM, dg), row),
        pl.BlockSpec((1, 6, d), lambda i: (i // nsb, 0, 0)),
        pl.BlockSpec(bias.shape, lambda i: (0, 0, 0, 0)),
        pl.BlockSpec(ws.shape, lambda i: (0, 0, 0)),
        pl.BlockSpec((BLOCK, dg), const2),
        pl.BlockSpec((1, dg), const2),
        pl.BlockSpec((1, dg), const2),
        pl.BlockSpec((1, D_ATTN), const2),
        pl.BlockSpec((1, dg), const2),
        pl.BlockSpec(w_out.shape, const2),
        pl.BlockSpec((1, d), const2),
        pl.BlockSpec((1, d), const2),
        pl.BlockSpec(w_r_t.shape, const2),
    ]
    out_specs = [
        pl.BlockSpec((TM, d), row),
        pl.BlockSpec((TM, d), row),
        pl.BlockSpec((1, e, TM), lambda i: (i // nsb, 0, i % nsb)),
    ]
    out_shape = [
        jax.ShapeDtypeStruct((t, d), F32),
        jax.ShapeDtypeStruct((t, d), BF16),
        jax.ShapeDtypeStruct((b, e, s), F32),
    ]
    return pl.pallas_call(
        functools.partial(_mix_kernel, s // BLOCK),
        grid=(t // TM,),
        in_specs=in_specs,
        out_specs=out_specs,
        out_shape=out_shape,
        scratch_shapes=[
            pltpu.VMEM((TM + 2 * BLOCK, 4 * LANES), BF16),
            pltpu.VMEM((TM, D_ATTN), F32),
            pltpu.VMEM((TM, dg), F32),
        ],
        compiler_params=pltpu.CompilerParams(
            dimension_semantics=("arbitrary",), vmem_limit_bytes=VMEM_LIMIT),
        name="mix",
    )(sink, x2, q, kv, kv, kv, u, vg, mod, bias, ws, bs_full, ln_g, ln_b, g_attn, g_gmlp,
      w_out, g_post, g_pre2, w_r_t)


def _route_kernel(cap, aff_ref, slot_ref, affb_ref, cnt_ref):
    aff = aff_ref[...]
    e, s = aff.shape
    capf = jnp.float32(cap)

    def count_ge(t):
        return jnp.sum(jnp.where(aff >= t, 1.0, 0.0), axis=1, keepdims=True)

    def bit_body(it, cand):
        trial = cand | jnp.left_shift(jnp.int32(1), 30 - it)
        t = lax.bitcast_convert_type(trial, F32)
        return jnp.where(count_ge(t) >= capf, trial, cand)

    cand = lax.fori_loop(0, 31, bit_body, jnp.zeros((e, 1), jnp.int32))
    thr = lax.bitcast_convert_type(cand, F32)
    need = capf - jnp.sum(jnp.where(aff > thr, 1.0, 0.0), axis=1, keepdims=True)

    r = lax.broadcasted_iota(jnp.int32, (LANES, LANES), 0)
    c = lax.broadcasted_iota(jnp.int32, (LANES, LANES), 1)
    tri = jnp.where(r < c, 1.0, 0.0).astype(BF16)
    lane = lax.broadcasted_iota(jnp.int32, (e, LANES), 1)
    carry_eq = jnp.zeros((e, 1), F32)
    carry_sel = jnp.zeros((e, 1), F32)
    cnt = jnp.zeros((e, LANES), F32)
    per_blk = MXU_DIM // LANES
    for j in range(s // LANES):
        a = aff[:, j * LANES:(j + 1) * LANES]
        gt = a > thr
        eq = jnp.where(a == thr, 1.0, 0.0)
        tie_rank = _dot(eq.astype(BF16), tri) + carry_eq
        sel = jnp.where(gt | ((eq > 0.0) & (tie_rank < need)), 1.0, 0.0)
        slot = _dot(sel.astype(BF16), tri) + carry_sel
        blk = (slice(None), j // per_blk, slice((j % per_blk) * LANES, (j % per_blk + 1) * LANES))
        slot_ref[blk] = jnp.where(sel > 0.0, slot, -1.0).astype(jnp.int32)
        affb_ref[blk] = a
        if j % per_blk == 0:
            cnt = jnp.where(lane == j // per_blk, carry_sel, cnt)
        carry_eq = carry_eq + jnp.sum(eq, axis=1, keepdims=True)
        carry_sel = carry_sel + jnp.sum(sel, axis=1, keepdims=True)
    cnt = jnp.where(lane >= s // MXU_DIM, carry_sel, cnt)
    cnt_ref[...] = cnt.astype(jnp.int32)


def _route(aff2, cap):
    r, s = aff2.shape
    nblk = s // MXU_DIM
    return pl.pallas_call(
        functools.partial(_route_kernel, cap),
        grid=(1,),
        in_specs=[pl.BlockSpec((r, s), lambda i: (0, 0))],
        out_specs=[
            pl.BlockSpec((r, nblk, MXU_DIM), lambda i: (0, 0, 0)),
            pl.BlockSpec((r, nblk, MXU_DIM), lambda i: (0, 0, 0)),
            pl.BlockSpec((r, LANES), lambda i: (0, 0)),
        ],
        out_shape=[
            jax.ShapeDtypeStruct((r, nblk, MXU_DIM), jnp.int32),
            jax.ShapeDtypeStruct((r, nblk, MXU_DIM), F32),
            jax.ShapeDtypeStruct((r, LANES), jnp.int32),
        ],
        compiler_params=pltpu.CompilerParams(dimension_semantics=("arbitrary",)),
        name="route",
    )(aff2)


def _segment(cnt_ref, row, pb, cap):
    c0 = jnp.minimum(cnt_ref[row, pb], cap)
    c1 = jnp.minimum(cnt_ref[row, pb + 1], cap)
    a0 = jnp.minimum(jnp.bitwise_and(c0, -SLOT_ALIGN), cap - WIN)
    return c0, c1, pl.multiple_of(a0, SLOT_ALIGN)


def _extra_windows(c1, a0):
    return (jnp.maximum(c1 - (a0 + WIN), 0) + WIN - 1) // WIN


def _gather_kernel(n_exp, cnt_ref, h2_ref, slot_ref, aff_ref, x_ref, gs_ref):
    b, eg = pl.program_id(0), pl.program_id(1)
    ng, cap = x_ref.shape[1], x_ref.shape[2]
    nblk = slot_ref.shape[2]
    sub = lax.broadcasted_iota(jnp.int32, (WIN, MXU_DIM), 0)
    x_ref[...] = jnp.zeros_like(x_ref)
    gs_ref[...] = jnp.zeros_like(gs_ref)

    def merge(k, a0, rows, gate):
        win = (0, k, pl.ds(a0, WIN), slice(None))
        x_ref[win] += rows.astype(BF16)
        gs_ref[win] += gate

    def tokens(pb):
        return pl.ds(pl.multiple_of(pb * MXU_DIM, MXU_DIM), MXU_DIM)

    def first_windows(pb):
        segs, gates, onehots = [], [], []
        overflow = jnp.int32(0)
        for k in range(ng):
            c0, c1, a0 = _segment(cnt_ref, b * n_exp + eg * ng + k, pb, cap)
            match = slot_ref[0, k, pl.ds(pb, 1), :] == (sub + a0)
            onehots.append(jnp.where(match, 1.0, 0.0).astype(BF16))
            gates.append(jnp.sum(jnp.where(match, aff_ref[0, k, pl.ds(pb, 1), :], 0.0),
                                 axis=1, keepdims=True))
            segs.append((c0, c1, a0))
            overflow = overflow + jnp.maximum(c1 - (a0 + WIN), 0)
        rows = _dot(jnp.concatenate(onehots, axis=0), h2_ref[0, tokens(pb), :])
        for k in range(ng):
            merge(k, segs[k][2], rows[k * WIN:(k + 1) * WIN, :], gates[k])
        return overflow

    def further_windows(pb):
        def per_expert(k, carry2):
            _, c1, a0 = _segment(cnt_ref, b * n_exp + eg * ng + k, pb, cap)

            def per_window(w, carry3):
                first = a0 + (w + 1) * WIN
                aw = pl.multiple_of(jnp.minimum(first, cap - WIN), SLOT_ALIGN)
                srow = slot_ref[0, k, pl.ds(pb, 1), :]
                match = (srow == (sub + aw)) & (srow >= first)
                onehot = jnp.where(match, 1.0, 0.0).astype(BF16)
                gate = jnp.sum(jnp.where(match, aff_ref[0, k, pl.ds(pb, 1), :], 0.0),
                               axis=1, keepdims=True)
                merge(k, aw, _dot(onehot, h2_ref[0, tokens(pb), :]), gate)
                return carry3

            return lax.fori_loop(0, _extra_windows(c1, a0), per_window, carry2)

        lax.fori_loop(0, ng, per_expert, 0)

    def trip_body(i, carry):
        overflows = [first_windows(GATHER_BLOCKS * i + j) for j in range(GATHER_BLOCKS)]
        for j in range(GATHER_BLOCKS):
            pl.when(overflows[j] > 0)(functools.partial(further_windows, GATHER_BLOCKS * i + j))
        return carry

    lax.fori_loop(0, nblk // GATHER_BLOCKS, trip_body, 0)


def _gather(cnt2, h2, slots4, aff4, cap):
    b, s, d = h2.shape
    e = slots4.shape[1]
    nblk = s // MXU_DIM
    grid_spec = pltpu.PrefetchScalarGridSpec(
        num_scalar_prefetch=1,
        grid=(b, e // EG),
        in_specs=[
            pl.BlockSpec((1, s, d), lambda bi, gi, c: (bi, 0, 0)),
            pl.BlockSpec((1, EG, nblk, MXU_DIM), lambda bi, gi, c: (bi, gi, 0, 0)),
            pl.BlockSpec((1, EG, nblk, MXU_DIM), lambda bi, gi, c: (bi, gi, 0, 0)),
        ],
        out_specs=[
            pl.BlockSpec((1, EG, cap, d), lambda bi, gi, c: (bi, gi, 0, 0)),
            pl.BlockSpec((1, EG, cap, LANES), lambda bi, gi, c: (bi, gi, 0, 0)),
        ],
    )
    return pl.pallas_call(
        functools.partial(_gather_kernel, e),
        grid_spec=grid_spec,
        out_shape=[
            jax.ShapeDtypeStruct((b, e, cap, d), BF16),
            jax.ShapeDtypeStruct((b, e, cap, LANES), F32),
        ],
        compiler_params=pltpu.CompilerParams(
            dimension_semantics=("arbitrary", "arbitrary"), vmem_limit_bytes=VMEM_LIMIT),
        name="gather",
    )(cnt2, h2, slots4, aff4)


def _ffn_kernel(nchunk, x_ref, gs_ref, *refs):
    w_refs, out_ref, acc_s = refs[:-2], refs[-2], refs[-1]
    groups = [w_refs[3 * k:3 * k + 3] for k in range(FFN_CHUNKS)]
    e, f = pl.program_id(0), pl.program_id(1)
    nb, _, cap, d = x_ref.shape
    last = pl.num_programs(1) - 1

    @pl.when((e == 0) & (f == 0))
    def _init():
        acc_s[...] = jnp.zeros_like(acc_s)

    def accumulate(chunks):
        wg = jnp.concatenate([w[0][0].astype(BF16) for w in chunks], axis=1)
        wu = jnp.concatenate([w[1][0].astype(BF16) for w in chunks], axis=1)
        wd = jnp.concatenate([w[2][0].astype(BF16) for w in chunks], axis=0)
        for bi in range(nb):
            x = x_ref[bi, 0]
            g = _dot(x, wg)
            u = _dot(x, wu)
            hid = (g * jax.nn.sigmoid(g) * u).astype(BF16)
            y = _dot(hid, wd)
            rows = slice(bi * cap, (bi + 1) * cap)
            acc_s[rows, :] = jnp.where(f == 0, y, acc_s[rows, :] + y)

    tail = nchunk % FFN_CHUNKS
    if tail == 0:
        accumulate(groups)
    else:
        pl.when(f < last)(lambda: accumulate(groups))
        pl.when(f == last)(lambda: accumulate(groups[:tail]))

    @pl.when(f == last)
    def _emit():
        gate = gs_ref[:, 0, :, 0:1].reshape(nb * cap, 1)
        out_ref[:, 0] = (acc_s[...] * gate).reshape(nb, cap, d).astype(BF16)


def _ffn(xg, gs, w_gate, w_up, w_down, e0):
    b, e, cap, d = xg.shape
    nchunk = w_gate.shape[2] // FC
    weight_specs = []
    for k in range(FFN_CHUNKS):
        chunk = lambda fi, k=k: jnp.minimum(FFN_CHUNKS * fi + k, nchunk - 1)
        weight_specs += [
            pl.BlockSpec((1, d, FC), lambda ei, fi, chunk=chunk: (e0 + ei, 0, chunk(fi))),
            pl.BlockSpec((1, d, FC), lambda ei, fi, chunk=chunk: (e0 + ei, 0, chunk(fi))),
            pl.BlockSpec((1, FC, d), lambda ei, fi, chunk=chunk: (e0 + ei, chunk(fi), 0)),
        ]
    return pl.pallas_call(
        functools.partial(_ffn_kernel, nchunk),
        grid=(e, pl.cdiv(nchunk, FFN_CHUNKS)),
        in_specs=[
            pl.BlockSpec((b, 1, cap, d), lambda ei, fi: (0, ei, 0, 0)),
            pl.BlockSpec((b, 1, cap, LANES), lambda ei, fi: (0, ei, 0, 0)),
        ] + weight_specs,
        out_specs=pl.BlockSpec((b, 1, cap, d), lambda ei, fi: (0, ei, 0, 0)),
        out_shape=jax.ShapeDtypeStruct((b, e, cap, d), BF16),
        scratch_shapes=[pltpu.VMEM((b * cap, d), F32)],
        compiler_params=pltpu.CompilerParams(
            dimension_semantics=("arbitrary", "arbitrary"), vmem_limit_bytes=VMEM_LIMIT),
        name="ffn",
    )(xg, gs, *([w_gate, w_up, w_down] * FFN_CHUNKS))


def _combine_kernel(n_exp, cnt_ref, eo_ref, slot_ref, x1_ref, mod_ref, g_ref, o_ref,
                    stage_s, oh_s, y_s):
    b, q = pl.program_id(0), pl.program_id(1)
    cap = eo_ref.shape[2]
    nblk_q = o_ref.shape[0] // MXU_DIM
    sub = lax.broadcasted_iota(jnp.int32, (WIN, MXU_DIM), 0)
    gain = g_ref[...] * mod_ref[0, 5:6, :]

    for j in range(nblk_q):
        pb = q * nblk_q + j
        overflow = jnp.int32(0)
        for e in range(n_exp):
            _, c1, a0 = _segment(cnt_ref, b * n_exp + e, pb, cap)
            stage_s[e * WIN:(e + 1) * WIN, :] = eo_ref[0, e, pl.ds(a0, WIN), :]
            match = slot_ref[0, e, pl.ds(pb, 1), :] == (sub + a0)
            oh_s[e * WIN:(e + 1) * WIN, :] = jnp.where(match, 1.0, 0.0).astype(BF16)
            overflow = overflow + jnp.maximum(c1 - (a0 + WIN), 0)
        y_s[...] = _dot_tn(oh_s[...], stage_s[...])

        @pl.when(overflow > 0)
        def _long_segments(j=j, pb=pb):
            def per_expert(e, carry):
                _, c1, a0 = _segment(cnt_ref, b * n_exp + e, pb, cap)

                def per_window(k, carry2):
                    first = a0 + (k + 1) * WIN
                    aw = pl.multiple_of(jnp.minimum(first, cap - WIN), SLOT_ALIGN)
                    srow = slot_ref[0, e, pl.ds(pb, 1), :]
                    match = (srow == (sub + aw)) & (srow >= first)
                    onehot = jnp.where(match, 1.0, 0.0).astype(BF16)
                    y_s[...] += _dot_tn(onehot, eo_ref[0, e, pl.ds(aw, WIN), :])
                    return carry2

                return lax.fori_loop(0, _extra_windows(c1, a0), per_window, carry)

            lax.fori_loop(0, n_exp, per_expert, 0)

        sl = slice(j * MXU_DIM, (j + 1) * MXU_DIM)
        o_ref[sl, :] = x1_ref[sl, :] + _rms(y_s[...], gain)


def _combine(cnt2, eo, slots4, x1, mod, g_post):
    b, e, cap, d = eo.shape
    t = x1.shape[0]
    s = t // b
    nq = s // QP
    nblk_q = QP // MXU_DIM
    grid_spec = pltpu.PrefetchScalarGridSpec(
        num_scalar_prefetch=1,
        grid=(b, nq),
        in_specs=[
            pl.BlockSpec((1, e, cap, d), lambda bi, qi, c: (bi, 0, 0, 0), pipeline_mode=pl.Buffered(1)),
            pl.BlockSpec((1, e, s // MXU_DIM, MXU_DIM), lambda bi, qi, c: (bi, 0, 0, 0)),
            pl.BlockSpec((QP, d), lambda bi, qi, c: (bi * nq + qi, 0)),
            pl.BlockSpec((1, 6, d), lambda bi, qi, c: (bi, 0, 0)),
            pl.BlockSpec((1, d), lambda bi, qi, c: (0, 0)),
        ],
        out_specs=pl.BlockSpec((QP, d), lambda bi, qi, c: (bi * nq + qi, 0)),
        scratch_shapes=[
            pltpu.VMEM((e * WIN, d), BF16),
            pltpu.VMEM((e * WIN, MXU_DIM), BF16),
            pltpu.VMEM((MXU_DIM, d), F32),
        ],
    )
    return pl.pallas_call(
        functools.partial(_combine_kernel, e),
        grid_spec=grid_spec,
        out_shape=jax.ShapeDtypeStruct((t, d), F32),
        compiler_params=pltpu.CompilerParams(
            dimension_semantics=("arbitrary", "arbitrary"),
            vmem_limit_bytes=VMEM_LIMIT),
        name="combine",
    )(cnt2, eo, slots4, x1, mod, g_post)


def _layer(layer, x, c, w_ada_all, b_ada, norm_pre_mix, norm_post_mix, w_in, sink, sgu_ln_g,
           sgu_ln_b, w_s, b_s, norm_out_attn, norm_out_gmlp, w_out, norm_pre_ffn, norm_post_ffn,
           w_router, w_gate_all, w_up_all, w_down_all):
    b, s, d = x.shape
    t = b * s
    dg = sgu_ln_g.shape[0]
    e = w_router.shape[1]
    cap = CAPACITY_FACTOR * s // e
    assert s % QP == 0 and s % TM == 0 and w_gate_all.shape[2] % FC == 0 and e % EG == 0
    assert cap % SLOT_ALIGN == 0 and cap >= WIN and s // MXU_DIM < LANES and d == D_ATTN + dg
    assert (s // MXU_DIM) % GATHER_BLOCKS == 0

    mod = _adaln(c, w_ada_all, b_ada, layer).reshape(b, 6, d)

    x2 = x.reshape(t, d)
    q, kv, u, vg = _proj(x2, mod, norm_pre_mix.reshape(1, d), w_in.astype(BF16), s)

    bias = jnp.asarray(_attn_bias_table())
    bs_full = jnp.repeat(b_s.T, dg // N_GMLP_GROUPS, axis=1)
    x1, h2, aff_t = _mix(
        x2, q, kv, u, vg, mod, sink, bias,
        w_s.astype(BF16).reshape(N_GMLP_GROUPS // 2, 2 * BLOCK, BLOCK), bs_full,
        sgu_ln_g.reshape(1, dg), sgu_ln_b.reshape(1, dg), norm_out_attn.reshape(1, D_ATTN),
        norm_out_gmlp.reshape(1, dg), w_out.astype(BF16), norm_post_mix.reshape(1, d),
        norm_pre_ffn.reshape(1, d), w_router.T.astype(BF16), b, s)

    slots3, aff3, cnt2 = _route(aff_t.reshape(b * e, s), cap)
    nblk = s // MXU_DIM
    slots4 = slots3.reshape(b, e, nblk, MXU_DIM)
    aff4 = aff3.reshape(b, e, nblk, MXU_DIM)
    xg, gs = _gather(cnt2, h2.reshape(b, s, d), slots4, aff4, cap)
    eo = _ffn(xg, gs, w_gate_all, w_up_all, w_down_all, layer * e)
    out = _combine(cnt2, eo, slots4, x1, mod, norm_post_ffn.reshape(1, d))
    return out.reshape(b, s, d)


def kernel(x, c, w_ada, b_ada, norm_pre_mix, norm_post_mix, w_in, sink, sgu_ln_g, sgu_ln_b, w_s, b_s, norm_out_attn, norm_out_gmlp, w_out, norm_pre_ffn, norm_post_ffn, w_router, w_gate, w_up, w_down):
    depth, d = w_ada.shape[0], w_ada.shape[1]
    w_ada_all = w_ada.reshape(depth * d, w_ada.shape[2])
    stack = lambda w: w.reshape((depth * w.shape[1],) + w.shape[2:])
    w_gate_all, w_up_all, w_down_all = stack(w_gate), stack(w_up), stack(w_down)
    for l in range(depth):
        x = _layer(l, x, c, w_ada_all, b_ada[l], norm_pre_mix[l], norm_post_mix[l], w_in[l],
                   sink[l], sgu_ln_g[l], sgu_ln_b[l], w_s[l], b_s[l], norm_out_attn[l],
                   norm_out_gmlp[l], w_out[l], norm_pre_ffn[l], norm_post_ffn[l], w_router[l],
                   w_gate_all, w_up_all, w_down_all)
    return x
```

```python
import functools

import numpy as np
import jax
import jax.numpy as jnp
from jax import lax
from jax.experimental import pallas as pl
from jax.experimental.pallas import tpu as pltpu

N_HEADS = 8
N_KV_HEADS = 2
HEAD_DIM = 64
D_ATTN = N_HEADS * HEAD_DIM
D_KV = N_KV_HEADS * HEAD_DIM
GROUP = N_HEADS // N_KV_HEADS
WINDOW = 128
BLOCK = 128
N_GMLP_GROUPS = 8
N_MOD = 6
CAPACITY_FACTOR = 2
EPS = 1e-6
MASK_VALUE = -1e30

LANES = 128
MXU_DIM = 256

TM = 1024
FC = 256
FFN_CHUNKS = 3
PROJ_ROWS = 256
EG = 4
GATHER_BLOCKS = 8
QP = 1024
WIN = 64
SLOT_ALIGN = 16
VMEM_LIMIT = 56 * 1024 * 1024

BF16 = jnp.bfloat16
F32 = jnp.float32


def _rms(x, g):
    return x * lax.rsqrt(jnp.mean(x * x, axis=-1, keepdims=True) + EPS) * g


def _dot(a, b):
    return jnp.dot(a, b, preferred_element_type=F32)


def _dot_nt(a, b):
    return lax.dot_general(a, b, (((1,), (1,)), ((), ())), preferred_element_type=F32)


def _dot_tn(a, b):
    return lax.dot_general(a, b, (((0,), (0,)), ((), ())), preferred_element_type=F32)


def _gelu_tanh(x):
    c = np.float32(np.sqrt(2.0 / np.pi))
    return x * (0.5 * (1.0 + jnp.tanh(c * (x + 0.044715 * (x * x * x)))))


def _proj_kernel(nsb, x_ref, c_ref, wa12_ref, ba12_ref, wa_ref, ba_ref, g_ref, w_ref,
                 q_ref, kv_ref, u_ref, vg_ref, mod12_ref, mod_ref):
    i = pl.program_id(0)
    d = x_ref.shape[1]

    def adaln(w, bias):
        c = c_ref[...]
        return _dot((c * jax.nn.sigmoid(c)).astype(BF16), w.astype(BF16)) + bias

    @pl.when(i == 0)
    def _():
        mod12_ref[...] = adaln(wa12_ref[...], ba12_ref[...])

    @pl.when(i < N_MOD - 2)
    def _():
        mod_ref[...] = adaln(wa_ref[...], ba_ref[...])

    mod12 = mod12_ref[pl.ds(i // nsb, 1), :]
    shift = mod12[:, :d]
    gain = g_ref[...] * (1.0 + mod12[:, d:])
    n_att = D_ATTN + 2 * D_KV
    dg = u_ref.shape[1]
    lo = lax.broadcasted_iota(jnp.int32, (1, LANES), 1) < HEAD_DIM
    for r in range(0, TM, PROJ_ROWS):
        rows = slice(r, r + PROJ_ROWS)
        h = (_rms(x_ref[rows, :], gain) + shift).astype(BF16)
        z = _gelu_tanh(_dot(h, w_ref[:, n_att:]))
        u_ref[rows, :] = z[:, :dg]
        vg_ref[rows, :] = z[:, dg:]
        p = _dot(h, w_ref[:, :n_att])
        q_ref[rows, :] = (p[:, :D_ATTN] * (HEAD_DIM ** -0.5)).astype(BF16)
        for j in range(2):
            t = p[:, D_ATTN + j * D_KV:D_ATTN + (j + 1) * D_KV]
            t_sw = pltpu.roll(t, HEAD_DIM, axis=1)
            kv_ref[rows, (2 * j) * LANES:(2 * j + 1) * LANES] = jnp.where(lo, t, t_sw).astype(BF16)
            kv_ref[rows, (2 * j + 1) * LANES:(2 * j + 2) * LANES] = jnp.where(lo, t_sw, t).astype(BF16)


def _proj(x2, c, w_ada_all, b_ada, layer, g_pre, w_in, s):
    t, d = x2.shape
    b = c.shape[0]
    nw = w_in.shape[1]
    dg = (nw - D_ATTN - 2 * D_KV) // 2
    nsb = s // TM
    assert t // TM >= N_MOD - 2
    later = lambda i: 2 + jnp.minimum(i, N_MOD - 3)
    return pl.pallas_call(
        functools.partial(_proj_kernel, nsb),
        grid=(t // TM,),
        in_specs=[
            pl.BlockSpec((TM, d), lambda i: (i, 0)),
            pl.BlockSpec((b, d), lambda i: (0, 0)),
            pl.BlockSpec((d, 2 * d), lambda i: (layer, 0)),
            pl.BlockSpec((1, 2 * d), lambda i: (0, 0)),
            pl.BlockSpec((d, d), lambda i: (layer, later(i))),
            pl.BlockSpec((1, d), lambda i: (0, later(i))),
            pl.BlockSpec((1, d), lambda i: (0, 0)),
            pl.BlockSpec((d, nw), lambda i: (0, 0)),
        ],
        out_specs=[
            pl.BlockSpec((TM, D_ATTN), lambda i: (i, 0)),
            pl.BlockSpec((TM, D_ATTN), lambda i: (i, 0)),
            pl.BlockSpec((TM, dg), lambda i: (i, 0)),
            pl.BlockSpec((TM, dg), lambda i: (i, 0)),
            pl.BlockSpec((b, 2 * d), lambda i: (0, 0)),
            pl.BlockSpec((b, d), lambda i: (0, later(i) - 2)),
        ],
        out_shape=[
            jax.ShapeDtypeStruct((t, D_ATTN), BF16),
            jax.ShapeDtypeStruct((t, D_ATTN), BF16),
            jax.ShapeDtypeStruct((t, dg), F32),
            jax.ShapeDtypeStruct((t, dg), F32),
            jax.ShapeDtypeStruct((b, 2 * d), F32),
            jax.ShapeDtypeStruct((b, (N_MOD - 2) * d), F32),
        ],
        compiler_params=pltpu.CompilerParams(
            dimension_semantics=("arbitrary",), vmem_limit_bytes=VMEM_LIMIT),
        name="proj",
    )(x2, c, w_ada_all, b_ada, w_ada_all, b_ada, g_pre, w_in)


def _attn_bias_table():
    qi = np.arange(BLOCK)[:, None]
    kj = np.arange(3 * BLOCK)[None, :] - BLOCK
    dist = np.abs(qi - kj).astype(np.float32)
    band = dist <= WINDOW
    slopes = np.exp2(-8.0 * np.arange(1, N_HEADS + 1, dtype=np.float32) / N_HEADS).astype(np.float32)
    key_ok = [kj >= 0, np.ones_like(kj, bool), kj < BLOCK]
    tab = np.empty((3, N_HEADS, BLOCK, 3 * BLOCK), np.float32)
    for v in range(3):
        ok = band & key_ok[v]
        for h in range(N_HEADS):
            tab[v, h] = np.where(ok, -slopes[h] * dist, np.float32(MASK_VALUE))
    return tab.reshape(3, N_KV_HEADS, GROUP * BLOCK, 3 * BLOCK)


def _mix_kernel(nblk_seq, sink_ref, x_ref, q_ref, kvp_ref, kvc_ref, kvn_ref, u_ref, vg_ref,
                mod_ref, bias_ref, ws_ref, bs_ref, lng_ref, lnb_ref, ga_ref, gg_ref,
                wout_ref, gpost_ref, gpre2_ref, wr_ref,
                x1_ref, h2_ref, aff_ref, kcat, attn_s, gm_s):
    i = pl.program_id(0)
    nb_tile = TM // BLOCK
    kcat[0:BLOCK, :] = kvp_ref[...]
    kcat[BLOCK:BLOCK + TM, :] = kvc_ref[...]
    kcat[BLOCK + TM:, :] = kvn_ref[...]
    lane = lax.broadcasted_iota(jnp.int32, (1, LANES), 1)
    lo = lane < HEAD_DIM
    zero = jnp.zeros((), BF16)
    head_of_row = lax.broadcasted_iota(jnp.int32, (GROUP * BLOCK, 1), 0) // BLOCK
    sinks = []
    for g in range(N_KV_HEADS):
        col = jnp.zeros((GROUP * BLOCK, 1), F32)
        for j in range(GROUP):
            col = jnp.where(head_of_row == j, sink_ref[g * GROUP + j], col)
        sinks.append(col)

    def block_body(a, carry):
        r0 = pl.multiple_of(a * BLOCK, BLOCK)
        pos = (i * nb_tile + a) % nblk_seq
        var = jnp.where(pos == 0, 0, jnp.where(pos == nblk_seq - 1, 2, 1))
        kv = kcat[pl.ds(r0, 3 * BLOCK), :]
        for g in range(N_KV_HEADS):
            k = kv[:, g * LANES:(g + 1) * LANES]
            v = kv[:, (N_KV_HEADS + g) * LANES:(N_KV_HEADS + g + 1) * LANES]
            pieces = []
            for pi in range(GROUP // 2):
                qp = q_ref[pl.ds(r0, BLOCK), (g * GROUP // 2 + pi) * LANES:(g * GROUP // 2 + pi + 1) * LANES]
                pieces += [jnp.where(lo, qp, zero), jnp.where(lo, zero, qp)]
            s = _dot_nt(jnp.concatenate(pieces, axis=0), k) + bias_ref[var, g]
            m = jnp.maximum(jnp.max(s, axis=-1, keepdims=True), sinks[g])
            p = jnp.exp(s - m)
            den = jnp.sum(p, axis=-1, keepdims=True) + jnp.exp(sinks[g] - m)
            o = _dot(p.astype(BF16), v) / den
            for pi in range(GROUP // 2):
                even = o[(2 * pi) * BLOCK:(2 * pi + 1) * BLOCK, :]
                odd = o[(2 * pi + 1) * BLOCK:(2 * pi + 2) * BLOCK, :]
                col0 = (g * GROUP // 2 + pi) * LANES
                attn_s[pl.ds(r0, BLOCK), col0:col0 + LANES] = jnp.where(lo, even, odd)
        vg = vg_ref[pl.ds(r0, BLOCK), :]
        mu = jnp.mean(vg, axis=-1, keepdims=True)
        xc = vg - mu
        vn = (xc * lax.rsqrt(jnp.mean(xc * xc, axis=-1, keepdims=True) + EPS) * lng_ref[...]
              + lnb_ref[...]).astype(BF16)
        for pi in range(N_GMLP_GROUPS // 2):
            vnp = vn[:, pi * LANES:(pi + 1) * LANES]
            both = _dot(ws_ref[pi], vnp)
            z = jnp.where(lo, both[:BLOCK], both[BLOCK:])
            sl = slice(pi * LANES, (pi + 1) * LANES)
            gm_s[pl.ds(r0, BLOCK), sl] = u_ref[pl.ds(r0, BLOCK), sl] * (z + bs_ref[:, sl])
        return carry

    lax.fori_loop(0, nb_tile, block_body, 0, unroll=2)

    na = _rms(attn_s[...], ga_ref[...]).astype(BF16)
    ng = _rms(gm_s[...], gg_ref[...]).astype(BF16)
    mo = _dot(na, wout_ref[0:D_ATTN, :]) + _dot(ng, wout_ref[D_ATTN:, :])
    x1 = x_ref[...] + _rms(mo, gpost_ref[...] * mod_ref[0, 2:3, :])
    x1_ref[...] = x1
    h2 = (_rms(x1, gpre2_ref[...] * (1.0 + mod_ref[0, 4:5, :])) + mod_ref[0, 3:4, :]).astype(BF16)
    h2_ref[...] = h2
    lg = _dot_nt(wr_ref[...], h2)
    lg = lg - jnp.max(lg, axis=0, keepdims=True)
    ex = jnp.exp(lg)
    aff_ref[0] = ex / jnp.sum(ex, axis=0, keepdims=True)


def _mix(x2, q, kv, u, vg, mod, sink, bias, ws, bs_full, ln_g, ln_b, g_attn, g_gmlp,
         w_out, g_post, g_pre2, w_r_t, b, s):
    t, d = x2.shape
    dg = u.shape[1]
    e = w_r_t.shape[0]
    nsb = s // TM
    nb_tile = TM // BLOCK
    nblk = t // BLOCK
    row = lambda i: (i, 0)
    const2 = lambda i: (0, 0)
    in_specs = [
        pl.BlockSpec(memory_space=pltpu.SMEM),
        pl.BlockSpec((TM, d), row),
        pl.BlockSpec((TM, D_ATTN), row),
        pl.BlockSpec((BLOCK, 4 * LANES), lambda i: (jnp.maximum(i * nb_tile - 1, 0), 0)),
        pl.BlockSpec((TM, 4 * LANES), row),
        pl.BlockSpec((BLOCK, 4 * LANES), lambda i: (jnp.minimum((i + 1) * nb_tile, nblk - 1), 0)),
        pl.BlockSpec((TM, dg), row),
        pl.BlockSpec((TM, dg), row),
        pl.BlockSpec((1, 6, d), lambda i: (i // nsb, 0, 0)),
        pl.BlockSpec(bias.shape, lambda i: (0, 0, 0, 0)),
        pl.BlockSpec(ws.shape, lambda i: (0, 0, 0)),
        pl.BlockSpec((BLOCK, dg), const2),
        pl.BlockSpec((1, dg), const2),
        pl.BlockSpec((1, dg), const2),
        pl.BlockSpec((1, D_ATTN), const2),
        pl.BlockSpec((1, dg), const2),
        pl.BlockSpec(w_out.shape, const2),
        pl.BlockSpec((1, d), const2),
        pl.BlockSpec((1, d), const2),
        pl.BlockSpec(w_r_t.shape, const2),
    ]
    out_specs = [
        pl.BlockSpec((TM, d), row),
        pl.BlockSpec((TM, d), row),
        pl.BlockSpec((1, e, TM), lambda i: (i // nsb, 0, i % nsb)),
    ]
    out_shape = [
        jax.ShapeDtypeStruct((t, d), F32),
        jax.ShapeDtypeStruct((t, d), BF16),
        jax.ShapeDtypeStruct((b, e, s), F32),
    ]
    return pl.pallas_call(
        functools.partial(_mix_kernel, s // BLOCK),
        grid=(t // TM,),
        in_specs=in_specs,
        out_specs=out_specs,
        out_shape=out_shape,
        scratch_shapes=[
            pltpu.VMEM((TM + 2 * BLOCK, 4 * LANES), BF16),
            pltpu.VMEM((TM, D_ATTN), F32),
            pltpu.VMEM((TM, dg), F32),
        ],
        compiler_params=pltpu.CompilerParams(
            dimension_semantics=("arbitrary",), vmem_limit_bytes=VMEM_LIMIT),
        name="mix",
    )(sink, x2, q, kv, kv, kv, u, vg, mod, bias, ws, bs_full, ln_g, ln_b, g_attn, g_gmlp,
      w_out, g_post, g_pre2, w_r_t)


def _route_kernel(cap, aff_ref, slot_ref, affb_ref, cnt_ref):
    aff = aff_ref[...]
    e, s = aff.shape
    capf = jnp.float32(cap)

    def count_ge(t):
        return jnp.sum(jnp.where(aff >= t, 1.0, 0.0), axis=1, keepdims=True)

    def bit_body(it, cand):
        trial = cand | jnp.left_shift(jnp.int32(1), 30 - it)
        t = lax.bitcast_convert_type(trial, F32)
        return jnp.where(count_ge(t) >= capf, trial, cand)

    cand = lax.fori_loop(0, 31, bit_body, jnp.zeros((e, 1), jnp.int32))
    thr = lax.bitcast_convert_type(cand, F32)
    need = capf - jnp.sum(jnp.where(aff > thr, 1.0, 0.0), axis=1, keepdims=True)

    r = lax.broadcasted_iota(jnp.int32, (LANES, LANES), 0)
    c = lax.broadcasted_iota(jnp.int32, (LANES, LANES), 1)
    tri = jnp.where(r < c, 1.0, 0.0).astype(BF16)
    lane = lax.broadcasted_iota(jnp.int32, (e, LANES), 1)
    carry_eq = jnp.zeros((e, 1), F32)
    carry_sel = jnp.zeros((e, 1), F32)
    cnt = jnp.zeros((e, LANES), F32)
    per_blk = MXU_DIM // LANES
    for j in range(s // LANES):
        a = aff[:, j * LANES:(j + 1) * LANES]
        gt = a > thr
        eq = jnp.where(a == thr, 1.0, 0.0)
        tie_rank = _dot(eq.astype(BF16), tri) + carry_eq
        sel = jnp.where(gt | ((eq > 0.0) & (tie_rank < need)), 1.0, 0.0)
        slot = _dot(sel.astype(BF16), tri) + carry_sel
        blk = (slice(None), j // per_blk, slice((j % per_blk) * LANES, (j % per_blk + 1) * LANES))
        slot_ref[blk] = jnp.where(sel > 0.0, slot, -1.0).astype(jnp.int32)
        affb_ref[blk] = a
        if j % per_blk == 0:
            cnt = jnp.where(lane == j // per_blk, carry_sel, cnt)
        carry_eq = carry_eq + jnp.sum(eq, axis=1, keepdims=True)
        carry_sel = carry_sel + jnp.sum(sel, axis=1, keepdims=True)
    cnt = jnp.where(lane >= s // MXU_DIM, carry_sel, cnt)
    cnt_ref[...] = cnt.astype(jnp.int32)


def _route(aff2, cap):
    r, s = aff2.shape
    nblk = s // MXU_DIM
    return pl.pallas_call(
        functools.partial(_route_kernel, cap),
        grid=(1,),
        in_specs=[pl.BlockSpec((r, s), lambda i: (0, 0))],
        out_specs=[
            pl.BlockSpec((r, nblk, MXU_DIM), lambda i: (0, 0, 0)),
            pl.BlockSpec((r, nblk, MXU_DIM), lambda i: (0, 0, 0)),
            pl.BlockSpec((r, LANES), lambda i: (0, 0)),
        ],
        out_shape=[
            jax.ShapeDtypeStruct((r, nblk, MXU_DIM), jnp.int32),
            jax.ShapeDtypeStruct((r, nblk, MXU_DIM), F32),
            jax.ShapeDtypeStruct((r, LANES), jnp.int32),
        ],
        compiler_params=pltpu.CompilerParams(dimension_semantics=("arbitrary",)),
        name="route",
    )(aff2)


def _segment(cnt_ref, row, pb, cap):
    c0 = jnp.minimum(cnt_ref[row, pb], cap)
    c1 = jnp.minimum(cnt_ref[row, pb + 1], cap)
    a0 = jnp.minimum(jnp.bitwise_and(c0, -SLOT_ALIGN), cap - WIN)
    return c0, c1, pl.multiple_of(a0, SLOT_ALIGN)


def _extra_windows(c1, a0):
    return (jnp.maximum(c1 - (a0 + WIN), 0) + WIN - 1) // WIN


def _gather_kernel(n_exp, cnt_ref, h2_ref, slot_ref, aff_ref, x_ref, gs_ref):
    b, eg = pl.program_id(0), pl.program_id(1)
    ng, cap = x_ref.shape[1], x_ref.shape[2]
    nblk = slot_ref.shape[2]
    sub = lax.broadcasted_iota(jnp.int32, (WIN, MXU_DIM), 0)
    x_ref[...] = jnp.zeros_like(x_ref)
    gs_ref[...] = jnp.zeros_like(gs_ref)

    def merge(k, a0, rows, gate):
        win = (0, k, pl.ds(a0, WIN), slice(None))
        x_ref[win] += rows.astype(BF16)
        gs_ref[win] += gate

    def tokens(pb):
        return pl.ds(pl.multiple_of(pb * MXU_DIM, MXU_DIM), MXU_DIM)

    def first_windows(pb):
        segs, gates, onehots = [], [], []
        overflow = jnp.int32(0)
        for k in range(ng):
            c0, c1, a0 = _segment(cnt_ref, b * n_exp + eg * ng + k, pb, cap)
            match = slot_ref[0, k, pl.ds(pb, 1), :] == (sub + a0)
            onehots.append(jnp.where(match, 1.0, 0.0).astype(BF16))
            gates.append(jnp.sum(jnp.where(match, aff_ref[0, k, pl.ds(pb, 1), :], 0.0),
                                 axis=1, keepdims=True))
            segs.append((c0, c1, a0))
            overflow = overflow + jnp.maximum(c1 - (a0 + WIN), 0)
        rows = _dot(jnp.concatenate(onehots, axis=0), h2_ref[0, tokens(pb), :])
        for k in range(ng):
            merge(k, segs[k][2], rows[k * WIN:(k + 1) * WIN, :], gates[k])
        return overflow

    def further_windows(pb):
        def per_expert(k, carry2):
            _, c1, a0 = _segment(cnt_ref, b * n_exp + eg * ng + k, pb, cap)

            def per_window(w, carry3):
                first = a0 + (w + 1) * WIN
                aw = pl.multiple_of(jnp.minimum(first, cap - WIN), SLOT_ALIGN)
                srow = slot_ref[0, k, pl.ds(pb, 1), :]
                match = (srow == (sub + aw)) & (srow >= first)
                onehot = jnp.where(match, 1.0, 0.0).astype(BF16)
                gate = jnp.sum(jnp.where(match, aff_ref[0, k, pl.ds(pb, 1), :], 0.0),
                               axis=1, keepdims=True)
                merge(k, aw, _dot(onehot, h2_ref[0, tokens(pb), :]), gate)
                return carry3

            return lax.fori_loop(0, _extra_windows(c1, a0), per_window, carry2)

        lax.fori_loop(0, ng, per_expert, 0)

    def trip_body(i, carry):
        overflows = [first_windows(GATHER_BLOCKS * i + j) for j in range(GATHER_BLOCKS)]
        for j in range(GATHER_BLOCKS):
            pl.when(overflows[j] > 0)(functools.partial(further_windows, GATHER_BLOCKS * i + j))
        return carry

    lax.fori_loop(0, nblk // GATHER_BLOCKS, trip_body, 0)


def _gather(cnt2, h2, slots4, aff4, cap):
    b, s, d = h2.shape
    e = slots4.shape[1]
    nblk = s // MXU_DIM
    grid_spec = pltpu.PrefetchScalarGridSpec(
        num_scalar_prefetch=1,
        grid=(b, e // EG),
        in_specs=[
            pl.BlockSpec((1, s, d), lambda bi, gi, c: (bi, 0, 0)),
            pl.BlockSpec((1, EG, nblk, MXU_DIM), lambda bi, gi, c: (bi, gi, 0, 0)),
            pl.BlockSpec((1, EG, nblk, MXU_DIM), lambda bi, gi, c: (bi, gi, 0, 0)),
        ],
        out_specs=[
            pl.BlockSpec((1, EG, cap, d), lambda bi, gi, c: (bi, gi, 0, 0)),
            pl.BlockSpec((1, EG, cap, LANES), lambda bi, gi, c: (bi, gi, 0, 0)),
        ],
    )
    return pl.pallas_call(
        functools.partial(_gather_kernel, e),
        grid_spec=grid_spec,
        out_shape=[
            jax.ShapeDtypeStruct((b, e, cap, d), BF16),
            jax.ShapeDtypeStruct((b, e, cap, LANES), F32),
        ],
        compiler_params=pltpu.CompilerParams(
            dimension_semantics=("arbitrary", "arbitrary"), vmem_limit_bytes=VMEM_LIMIT),
        name="gather",
    )(cnt2, h2, slots4, aff4)


def _ffn_kernel(nchunk, x_ref, gs_ref, *refs):
    w_refs, out_ref, acc_s = refs[:-2], refs[-2], refs[-1]
    groups = [w_refs[3 * k:3 * k + 3] for k in range(FFN_CHUNKS)]
    e, f = pl.program_id(0), pl.program_id(1)
    nb, _, cap, d = x_ref.shape
    last = pl.num_programs(1) - 1

    @pl.when((e == 0) & (f == 0))
    def _init():
        acc_s[...] = jnp.zeros_like(acc_s)

    def accumulate(chunks):
        wg = jnp.concatenate([w[0][0].astype(BF16) for w in chunks], axis=1)
        wu = jnp.concatenate([w[1][0].astype(BF16) for w in chunks], axis=1)
        wd = jnp.concatenate([w[2][0].astype(BF16) for w in chunks], axis=0)
        for bi in range(nb):
            x = x_ref[bi, 0]
            g = _dot(x, wg)
            u = _dot(x, wu)
            hid = (g * jax.nn.sigmoid(g) * u).astype(BF16)
            y = _dot(hid, wd)
            rows = slice(bi * cap, (bi + 1) * cap)
            acc_s[rows, :] = jnp.where(f == 0, y, acc_s[rows, :] + y)

    tail = nchunk % FFN_CHUNKS
    if tail == 0:
        accumulate(groups)
    else:
        pl.when(f < last)(lambda: accumulate(groups))
        pl.when(f == last)(lambda: accumulate(groups[:tail]))

    @pl.when(f == last)
    def _emit():
        gate = gs_ref[:, 0, :, 0:1].reshape(nb * cap, 1)
        out_ref[:, 0] = (acc_s[...] * gate).reshape(nb, cap, d).astype(BF16)


def _ffn(xg, gs, w_gate, w_up, w_down, e0):
    b, e, cap, d = xg.shape
    nchunk = w_gate.shape[2] // FC
    weight_specs = []
    for k in range(FFN_CHUNKS):
        chunk = lambda fi, k=k: jnp.minimum(FFN_CHUNKS * fi + k, nchunk - 1)
        weight_specs += [
            pl.BlockSpec((1, d, FC), lambda ei, fi, chunk=chunk: (e0 + ei, 0, chunk(fi))),
            pl.BlockSpec((1, d, FC), lambda ei, fi, chunk=chunk: (e0 + ei, 0, chunk(fi))),
            pl.BlockSpec((1, FC, d), lambda ei, fi, chunk=chunk: (e0 + ei, chunk(fi), 0)),
        ]
    return pl.pallas_call(
        functools.partial(_ffn_kernel, nchunk),
        grid=(e, pl.cdiv(nchunk, FFN_CHUNKS)),
        in_specs=[
            pl.BlockSpec((b, 1, cap, d), lambda ei, fi: (0, ei, 0, 0)),
            pl.BlockSpec((b, 1, cap, LANES), lambda ei, fi: (0, ei, 0, 0)),
        ] + weight_specs,
        out_specs=pl.BlockSpec((b, 1, cap, d), lambda ei, fi: (0, ei, 0, 0)),
        out_shape=jax.ShapeDtypeStruct((b, e, cap, d), BF16),
        scratch_shapes=[pltpu.VMEM((b * cap, d), F32)],
        compiler_params=pltpu.CompilerParams(
            dimension_semantics=("arbitrary", "arbitrary"), vmem_limit_bytes=VMEM_LIMIT),
        name="ffn",
    )(xg, gs, *([w_gate, w_up, w_down] * FFN_CHUNKS))


def _combine_kernel(n_exp, cnt_ref, eo_ref, slot_ref, x1_ref, mod_ref, g_ref, o_ref,
                    stage_s, oh_s, y_s):
    b, q = pl.program_id(0), pl.program_id(1)
    cap = eo_ref.shape[2]
    nblk_q = o_ref.shape[0] // MXU_DIM
    sub = lax.broadcasted_iota(jnp.int32, (WIN, MXU_DIM), 0)
    gain = g_ref[...] * mod_ref[0, 5:6, :]

    for j in range(nblk_q):
        pb = q * nblk_q + j
        overflow = jnp.int32(0)
        for e in range(n_exp):
            _, c1, a0 = _segment(cnt_ref, b * n_exp + e, pb, cap)
            stage_s[e * WIN:(e + 1) * WIN, :] = eo_ref[0, e, pl.ds(a0, WIN), :]
            match = slot_ref[0, e, pl.ds(pb, 1), :] == (sub + a0)
            oh_s[e * WIN:(e + 1) * WIN, :] = jnp.where(match, 1.0, 0.0).astype(BF16)
            overflow = overflow + jnp.maximum(c1 - (a0 + WIN), 0)
        y_s[...] = _dot_tn(oh_s[...], stage_s[...])

        @pl.when(overflow > 0)
        def _long_segments(j=j, pb=pb):
            def per_expert(e, carry):
                _, c1, a0 = _segment(cnt_ref, b * n_exp + e, pb, cap)

                def per_window(k, carry2):
                    first = a0 + (k + 1) * WIN
                    aw = pl.multiple_of(jnp.minimum(first, cap - WIN), SLOT_ALIGN)
                    srow = slot_ref[0, e, pl.ds(pb, 1), :]
                    match = (srow == (sub + aw)) & (srow >= first)
                    onehot = jnp.where(match, 1.0, 0.0).astype(BF16)
                    y_s[...] += _dot_tn(onehot, eo_ref[0, e, pl.ds(aw, WIN), :])
                    return carry2

                return lax.fori_loop(0, _extra_windows(c1, a0), per_window, carry)

            lax.fori_loop(0, n_exp, per_expert, 0)

        sl = slice(j * MXU_DIM, (j + 1) * MXU_DIM)
        o_ref[sl, :] = x1_ref[sl, :] + _rms(y_s[...], gain)


def _combine(cnt2, eo, slots4, x1, mod, g_post):
    b, e, cap, d = eo.shape
    t = x1.shape[0]
    s = t // b
    nq = s // QP
    nblk_q = QP // MXU_DIM
    grid_spec = pltpu.PrefetchScalarGridSpec(
        num_scalar_prefetch=1,
        grid=(b, nq),
        in_specs=[
            pl.BlockSpec((1, e, cap, d), lambda bi, qi, c: (bi, 0, 0, 0), pipeline_mode=pl.Buffered(1)),
            pl.BlockSpec((1, e, s // MXU_DIM, MXU_DIM), lambda bi, qi, c: (bi, 0, 0, 0)),
            pl.BlockSpec((QP, d), lambda bi, qi, c: (bi * nq + qi, 0)),
            pl.BlockSpec((1, 6, d), lambda bi, qi, c: (bi, 0, 0)),
            pl.BlockSpec((1, d), lambda bi, qi, c: (0, 0)),
        ],
        out_specs=pl.BlockSpec((QP, d), lambda bi, qi, c: (bi * nq + qi, 0)),
        scratch_shapes=[
            pltpu.VMEM((e * WIN, d), BF16),
            pltpu.VMEM((e * WIN, MXU_DIM), BF16),
            pltpu.VMEM((MXU_DIM, d), F32),
        ],
    )
    return pl.pallas_call(
        functools.partial(_combine_kernel, e),
        grid_spec=grid_spec,
        out_shape=jax.ShapeDtypeStruct((t, d), F32),
        compiler_params=pltpu.CompilerParams(
            dimension_semantics=("arbitrary", "arbitrary"),
            vmem_limit_bytes=VMEM_LIMIT),
        name="combine",
    )(cnt2, eo, slots4, x1, mod, g_post)


def _layer(layer, x, c, w_ada_all, b_ada, norm_pre_mix, norm_post_mix, w_in, sink, sgu_ln_g,
           sgu_ln_b, w_s, b_s, norm_out_attn, norm_out_gmlp, w_out, norm_pre_ffn, norm_post_ffn,
           w_router, w_gate_all, w_up_all, w_down_all):
    b, s, d = x.shape
    t = b * s
    dg = sgu_ln_g.shape[0]
    e = w_router.shape[1]
    cap = CAPACITY_FACTOR * s // e
    assert s % QP == 0 and s % TM == 0 and w_gate_all.shape[2] % FC == 0 and e % EG == 0
    assert cap % SLOT_ALIGN == 0 and cap >= WIN and s // MXU_DIM < LANES and d == D_ATTN + dg
    assert (s // MXU_DIM) % GATHER_BLOCKS == 0

    x2 = x.reshape(t, d)
    q, kv, u, vg, mod12, mod_rest = _proj(
        x2, c, w_ada_all, b_ada.reshape(1, N_MOD * d), layer, norm_pre_mix.reshape(1, d),
        w_in.astype(BF16), s)
    mod = jnp.concatenate([mod12, mod_rest], axis=1).reshape(b, N_MOD, d)

    bias = jnp.asarray(_attn_bias_table())
    bs_full = jnp.repeat(b_s.T, dg // N_GMLP_GROUPS, axis=1)
    x1, h2, aff_t = _mix(
        x2, q, kv, u, vg, mod, sink, bias,
        w_s.astype(BF16).reshape(N_GMLP_GROUPS // 2, 2 * BLOCK, BLOCK), bs_full,
        sgu_ln_g.reshape(1, dg), sgu_ln_b.reshape(1, dg), norm_out_attn.reshape(1, D_ATTN),
        norm_out_gmlp.reshape(1, dg), w_out.astype(BF16), norm_post_mix.reshape(1, d),
        norm_pre_ffn.reshape(1, d), w_router.T.astype(BF16), b, s)

    slots3, aff3, cnt2 = _route(aff_t.reshape(b * e, s), cap)
    nblk = s // MXU_DIM
    slots4 = slots3.reshape(b, e, nblk, MXU_DIM)
    aff4 = aff3.reshape(b, e, nblk, MXU_DIM)
    xg, gs = _gather(cnt2, h2.reshape(b, s, d), slots4, aff4, cap)
    eo = _ffn(xg, gs, w_gate_all, w_up_all, w_down_all, layer * e)
    out = _combine(cnt2, eo, slots4, x1, mod, norm_post_ffn.reshape(1, d))
    return out.reshape(b, s, d)


def kernel(x, c, w_ada, b_ada, norm_pre_mix, norm_post_mix, w_in, sink, sgu_ln_g, sgu_ln_b, w_s, b_s, norm_out_attn, norm_out_gmlp, w_out, norm_pre_ffn, norm_post_ffn, w_router, w_gate, w_up, w_down):
    depth, d = w_ada.shape[0], w_ada.shape[1]
    w_ada_all = w_ada.reshape(depth * d, w_ada.shape[2])
    stack = lambda w: w.reshape((depth * w.shape[1],) + w.shape[2:])
    w_gate_all, w_up_all, w_down_all = stack(w_gate), stack(w_up), stack(w_down)
    for l in range(depth):
        x = _layer(l, x, c, w_ada_all, b_ada[l], norm_pre_mix[l], norm_post_mix[l], w_in[l],
                   sink[l], sgu_ln_g[l], sgu_ln_b[l], w_s[l], b_s[l], norm_out_attn[l],
                   norm_out_gmlp[l], w_out[l], norm_pre_ffn[l], norm_post_ffn[l], w_router[l],
                   w_gate_all, w_up_all, w_down_all)
    return x
```

```python
import functools

import numpy as np
import jax
import jax.numpy as jnp
from jax import lax
from jax.experimental import pallas as pl
from jax.experimental.pallas import tpu as pltpu

N_HEADS = 8
N_KV_HEADS = 2
HEAD_DIM = 64
D_ATTN = N_HEADS * HEAD_DIM
D_KV = N_KV_HEADS * HEAD_DIM
GROUP = N_HEADS // N_KV_HEADS
WINDOW = 128
BLOCK = 128
N_GMLP_GROUPS = 8
CAPACITY_FACTOR = 2
EPS = 1e-6
MASK_VALUE = -1e30

LANES = 128
MXU_DIM = 256

TM = 1024
FC = 256
FFN_CHUNKS = 4
FFN_ROWS = 256
PROJ_ROWS = 256
EG = 4
GATHER_BLOCKS = 8
QP = 1024
WIN = 64
SLOT_ALIGN = 16
VMEM_LIMIT = 56 * 1024 * 1024

BF16 = jnp.bfloat16
F32 = jnp.float32


def _rms(x, g):
    return x * lax.rsqrt(jnp.mean(x * x, axis=-1, keepdims=True) + EPS) * g


def _dot(a, b):
    return jnp.dot(a, b, preferred_element_type=F32)


def _dot_nt(a, b):
    return lax.dot_general(a, b, (((1,), (1,)), ((), ())), preferred_element_type=F32)


def _dot_tn(a, b):
    return lax.dot_general(a, b, (((0,), (0,)), ((), ())), preferred_element_type=F32)


def _adaln_kernel(c_ref, w_ref, b_ref, o_ref):
    c = c_ref[...]
    a = (c * jax.nn.sigmoid(c)).astype(BF16)
    o_ref[...] = _dot(a, w_ref[...].astype(BF16)) + b_ref[...]


def _adaln(c, w_ada_all, b_ada, layer):
    b, d = c.shape
    n = w_ada_all.shape[1]
    return pl.pallas_call(
        _adaln_kernel,
        grid=(n // d,),
        in_specs=[
            pl.BlockSpec((b, d), lambda j: (0, 0)),
            pl.BlockSpec((d, d), lambda j: (layer, j)),
            pl.BlockSpec((1, d), lambda j: (0, j)),
        ],
        out_specs=pl.BlockSpec((b, d), lambda j: (0, j)),
        out_shape=jax.ShapeDtypeStruct((b, n), F32),
        compiler_params=pltpu.CompilerParams(dimension_semantics=("arbitrary",)),
        name="adaln",
    )(c, w_ada_all, b_ada.reshape(1, n))


def _gelu_tanh(x):
    c = np.float32(np.sqrt(2.0 / np.pi))
    return x * (0.5 * (1.0 + jnp.tanh(c * (x + 0.044715 * (x * x * x)))))


def _proj_kernel(x_ref, mod_ref, g_ref, w_ref, q_ref, kv_ref, u_ref, vg_ref):
    shift = mod_ref[0, 0:1, :]
    gain = g_ref[...] * (1.0 + mod_ref[0, 1:2, :])
    n_att = D_ATTN + 2 * D_KV
    dg = u_ref.shape[1]
    lo = lax.broadcasted_iota(jnp.int32, (1, LANES), 1) < HEAD_DIM
    for r in range(0, TM, PROJ_ROWS):
        rows = slice(r, r + PROJ_ROWS)
        h = (_rms(x_ref[rows, :], gain) + shift).astype(BF16)
        z = _gelu_tanh(_dot(h, w_ref[:, n_att:]))
        u_ref[rows, :] = z[:, :dg]
        vg_ref[rows, :] = z[:, dg:]
        p = _dot(h, w_ref[:, :n_att])
        q_ref[rows, :] = (p[:, :D_ATTN] * (HEAD_DIM ** -0.5)).astype(BF16)
        for j in range(2):
            t = p[:, D_ATTN + j * D_KV:D_ATTN + (j + 1) * D_KV]
            t_sw = pltpu.roll(t, HEAD_DIM, axis=1)
            kv_ref[rows, (2 * j) * LANES:(2 * j + 1) * LANES] = jnp.where(lo, t, t_sw).astype(BF16)
            kv_ref[rows, (2 * j + 1) * LANES:(2 * j + 2) * LANES] = jnp.where(lo, t_sw, t).astype(BF16)


def _proj(x2, mod, g_pre, w_in, s):
    t, d = x2.shape
    nw = w_in.shape[1]
    dg = (nw - D_ATTN - 2 * D_KV) // 2
    nsb = s // TM
    return pl.pallas_call(
        _proj_kernel,
        grid=(t // TM,),
        in_specs=[
            pl.BlockSpec((TM, d), lambda i: (i, 0)),
            pl.BlockSpec((1, 6, d), lambda i: (i // nsb, 0, 0)),
            pl.BlockSpec((1, d), lambda i: (0, 0)),
            pl.BlockSpec((d, nw), lambda i: (0, 0)),
        ],
        out_specs=[
            pl.BlockSpec((TM, D_ATTN), lambda i: (i, 0)),
            pl.BlockSpec((TM, D_ATTN), lambda i: (i, 0)),
            pl.BlockSpec((TM, dg), lambda i: (i, 0)),
            pl.BlockSpec((TM, dg), lambda i: (i, 0)),
        ],
        out_shape=[
            jax.ShapeDtypeStruct((t, D_ATTN), BF16),
            jax.ShapeDtypeStruct((t, D_ATTN), BF16),
            jax.ShapeDtypeStruct((t, dg), F32),
            jax.ShapeDtypeStruct((t, dg), F32),
        ],
        compiler_params=pltpu.CompilerParams(
            dimension_semantics=("arbitrary",), vmem_limit_bytes=VMEM_LIMIT),
        name="proj",
    )(x2, mod, g_pre, w_in)


def _attn_bias_table():
    qi = np.arange(BLOCK)[:, None]
    kj = np.arange(3 * BLOCK)[None, :] - BLOCK
    dist = np.abs(qi - kj).astype(np.float32)
    band = dist <= WINDOW
    slopes = np.exp2(-8.0 * np.arange(1, N_HEADS + 1, dtype=np.float32) / N_HEADS).astype(np.float32)
    key_ok = [kj >= 0, np.ones_like(kj, bool), kj < BLOCK]
    tab = np.empty((3, N_HEADS, BLOCK, 3 * BLOCK), np.float32)
    for v in range(3):
        ok = band & key_ok[v]
        for h in range(N_HEADS):
            tab[v, h] = np.where(ok, -slopes[h] * dist, np.float32(MASK_VALUE))
    return tab.reshape(3, N_KV_HEADS, GROUP * BLOCK, 3 * BLOCK)


def _mix_kernel(nblk_seq, sink_ref, x_ref, q_ref, kvp_ref, kvc_ref, kvn_ref, u_ref, vg_ref,
                mod_ref, bias_ref, ws_ref, bs_ref, lng_ref, lnb_ref, ga_ref, gg_ref,
                wout_ref, gpost_ref, gpre2_ref, wr_ref,
                x1_ref, h2_ref, aff_ref, kcat, attn_s, gm_s):
    i = pl.program_id(0)
    nb_tile = TM // BLOCK
    kcat[0:BLOCK, :] = kvp_ref[...]
    kcat[BLOCK:BLOCK + TM, :] = kvc_ref[...]
    kcat[BLOCK + TM:, :] = kvn_ref[...]
    lane = lax.broadcasted_iota(jnp.int32, (1, LANES), 1)
    lo = lane < HEAD_DIM
    zero = jnp.zeros((), BF16)
    head_of_row = lax.broadcasted_iota(jnp.int32, (GROUP * BLOCK, 1), 0) // BLOCK
    sinks = []
    for g in range(N_KV_HEADS):
        col = jnp.zeros((GROUP * BLOCK, 1), F32)
        for j in range(GROUP):
            col = jnp.where(head_of_row == j, sink_ref[g * GROUP + j], col)
        sinks.append(col)

    def block_body(a, carry):
        r0 = pl.multiple_of(a * BLOCK, BLOCK)
        pos = (i * nb_tile + a) % nblk_seq
        var = jnp.where(pos == 0, 0, jnp.where(pos == nblk_seq - 1, 2, 1))
        kv = kcat[pl.ds(r0, 3 * BLOCK), :]
        for g in range(N_KV_HEADS):
            k = kv[:, g * LANES:(g + 1) * LANES]
            v = kv[:, (N_KV_HEADS + g) * LANES:(N_KV_HEADS + g + 1) * LANES]
            pieces = []
            for pi in range(GROUP // 2):
                qp = q_ref[pl.ds(r0, BLOCK), (g * GROUP // 2 + pi) * LANES:(g * GROUP // 2 + pi + 1) * LANES]
                pieces += [jnp.where(lo, qp, zero), jnp.where(lo, zero, qp)]
            s = _dot_nt(jnp.concatenate(pieces, axis=0), k) + bias_ref[var, g]
            m = jnp.maximum(jnp.max(s, axis=-1, keepdims=True), sinks[g])
            p = jnp.exp(s - m)
            den = jnp.sum(p, axis=-1, keepdims=True) + jnp.exp(sinks[g] - m)
            o = _dot(p.astype(BF16), v) / den
            for pi in range(GROUP // 2):
                even = o[(2 * pi) * BLOCK:(2 * pi + 1) * BLOCK, :]
                odd = o[(2 * pi + 1) * BLOCK:(2 * pi + 2) * BLOCK, :]
                col0 = (g * GROUP // 2 + pi) * LANES
                attn_s[pl.ds(r0, BLOCK), col0:col0 + LANES] = jnp.where(lo, even, odd)
        vg = vg_ref[pl.ds(r0, BLOCK), :]
        mu = jnp.mean(vg, axis=-1, keepdims=True)
        xc = vg - mu
        vn = (xc * lax.rsqrt(jnp.mean(xc * xc, axis=-1, keepdims=True) + EPS) * lng_ref[...]
              + lnb_ref[...]).astype(BF16)
        for pi in range(N_GMLP_GROUPS // 2):
            vnp = vn[:, pi * LANES:(pi + 1) * LANES]
            both = _dot(ws_ref[pi], vnp)
            z = jnp.where(lo, both[:BLOCK], both[BLOCK:])
            sl = slice(pi * LANES, (pi + 1) * LANES)
            gm_s[pl.ds(r0, BLOCK), sl] = u_ref[pl.ds(r0, BLOCK), sl] * (z + bs_ref[:, sl])
        return carry

    lax.fori_loop(0, nb_tile, block_body, 0, unroll=2)

    na = _rms(attn_s[...], ga_ref[...]).astype(BF16)
    ng = _rms(gm_s[...], gg_ref[...]).astype(BF16)
    mo = _dot(na, wout_ref[0:D_ATTN, :]) + _dot(ng, wout_ref[D_ATTN:, :])
    x1 = x_ref[...] + _rms(mo, gpost_ref[...] * mod_ref[0, 2:3, :])
    x1_ref[...] = x1
    h2 = (_rms(x1, gpre2_ref[...] * (1.0 + mod_ref[0, 4:5, :])) + mod_ref[0, 3:4, :]).astype(BF16)
    h2_ref[...] = h2
    lg = _dot_nt(wr_ref[...], h2)
    lg = lg - jnp.max(lg, axis=0, keepdims=True)
    ex = jnp.exp(lg)
    aff_ref[0] = ex / jnp.sum(ex, axis=0, keepdims=True)


def _mix(x2, q, kv, u, vg, mod, sink, bias, ws, bs_full, ln_g, ln_b, g_attn, g_gmlp,
         w_out, g_post, g_pre2, w_r_t, b, s):
    t, d = x2.shape
    dg = u.shape[1]
    e = w_r_t.shape[0]
    nsb = s // TM
    nb_tile = TM // BLOCK
    nblk = t // BLOCK
    row = lambda i: (i, 0)
    const2 = lambda i: (0, 0)
    in_specs = [
        pl.BlockSpec(memory_space=pltpu.SMEM),
        pl.BlockSpec((TM, d), row),
        pl.BlockSpec((TM, D_ATTN), row),
        pl.BlockSpec((BLOCK, 4 * LANES), lambda i: (jnp.maximum(i * nb_tile - 1, 0), 0)),
        pl.BlockSpec((TM, 4 * LANES), row),
        pl.BlockSpec((BLOCK, 4 * LANES), lambda i: (jnp.minimum((i + 1) * nb_tile, nblk - 1), 0)),
        pl.BlockSpec((TM, dg), row),
        pl.BlockSpec((TM, dg), row),
        pl.BlockSpec((1, 6, d), lambda i: (i // nsb, 0, 0)),
        pl.BlockSpec(bias.shape, lambda i: (0, 0, 0, 0)),
        pl.BlockSpec(ws.shape, lambda i: (0, 0, 0)),
        pl.BlockSpec((BLOCK, dg), const2),
        pl.BlockSpec((1, dg), const2),
        pl.BlockSpec((1, dg), const2),
        pl.BlockSpec((1, D_ATTN), const2),
        pl.BlockSpec((1, dg), const2),
        pl.BlockSpec(w_out.shape, const2),
        pl.BlockSpec((1, d), const2),
        pl.BlockSpec((1, d), const2),
        pl.BlockSpec(w_r_t.shape, const2),
    ]
    out_specs = [
        pl.BlockSpec((TM, d), row),
        pl.BlockSpec((TM, d), row),
        pl.BlockSpec((1, e, TM), lambda i: (i // nsb, 0, i % nsb)),
    ]
    out_shape = [
        jax.ShapeDtypeStruct((t, d), F32),
        jax.ShapeDtypeStruct((t, d), BF16),
        jax.ShapeDtypeStruct((b, e, s), F32),
    ]
    return pl.pallas_call(
        functools.partial(_mix_kernel, s // BLOCK),
        grid=(t // TM,),
        in_specs=in_specs,
        out_specs=out_specs,
        out_shape=out_shape,
        scratch_shapes=[
            pltpu.VMEM((TM + 2 * BLOCK, 4 * LANES), BF16),
            pltpu.VMEM((TM, D_ATTN), F32),
            pltpu.VMEM((TM, dg), F32),
        ],
        compiler_params=pltpu.CompilerParams(
            dimension_semantics=("arbitrary",), vmem_limit_bytes=VMEM_LIMIT),
        name="mix",
    )(sink, x2, q, kv, kv, kv, u, vg, mod, bias, ws, bs_full, ln_g, ln_b, g_attn, g_gmlp,
      w_out, g_post, g_pre2, w_r_t)


def _route_kernel(cap, aff_ref, slot_ref, affb_ref, cnt_ref):
    aff = aff_ref[...]
    e, s = aff.shape
    capf = jnp.float32(cap)

    def count_ge(t):
        return jnp.sum(jnp.where(aff >= t, 1.0, 0.0), axis=1, keepdims=True)

    def bit_body(it, cand):
        trial = cand | jnp.left_shift(jnp.int32(1), 30 - it)
        t = lax.bitcast_convert_type(trial, F32)
        return jnp.where(count_ge(t) >= capf, trial, cand)

    cand = lax.fori_loop(0, 31, bit_body, jnp.zeros((e, 1), jnp.int32))
    thr = lax.bitcast_convert_type(cand, F32)
    need = capf - jnp.sum(jnp.where(aff > thr, 1.0, 0.0), axis=1, keepdims=True)

    r = lax.broadcasted_iota(jnp.int32, (LANES, LANES), 0)
    c = lax.broadcasted_iota(jnp.int32, (LANES, LANES), 1)
    tri = jnp.where(r < c, 1.0, 0.0).astype(BF16)
    lane = lax.broadcasted_iota(jnp.int32, (e, LANES), 1)
    carry_eq = jnp.zeros((e, 1), F32)
    carry_sel = jnp.zeros((e, 1), F32)
    cnt = jnp.zeros((e, LANES), F32)
    per_blk = MXU_DIM // LANES
    for j in range(s // LANES):
        a = aff[:, j * LANES:(j + 1) * LANES]
        gt = a > thr
        eq = jnp.where(a == thr, 1.0, 0.0)
        tie_rank = _dot(eq.astype(BF16), tri) + carry_eq
        sel = jnp.where(gt | ((eq > 0.0) & (tie_rank < need)), 1.0, 0.0)
        slot = _dot(sel.astype(BF16), tri) + carry_sel
        blk = (slice(None), j // per_blk, slice((j % per_blk) * LANES, (j % per_blk + 1) * LANES))
        slot_ref[blk] = jnp.where(sel > 0.0, slot, -1.0).astype(jnp.int32)
        affb_ref[blk] = a
        if j % per_blk == 0:
            cnt = jnp.where(lane == j // per_blk, carry_sel, cnt)
        carry_eq = carry_eq + jnp.sum(eq, axis=1, keepdims=True)
        carry_sel = carry_sel + jnp.sum(sel, axis=1, keepdims=True)
    cnt = jnp.where(lane >= s // MXU_DIM, carry_sel, cnt)
    cnt_ref[...] = cnt.astype(jnp.int32)


def _route(aff2, cap):
    r, s = aff2.shape
    nblk = s // MXU_DIM
    return pl.pallas_call(
        functools.partial(_route_kernel, cap),
        grid=(1,),
        in_specs=[pl.BlockSpec((r, s), lambda i: (0, 0))],
        out_specs=[
            pl.BlockSpec((r, nblk, MXU_DIM), lambda i: (0, 0, 0)),
            pl.BlockSpec((r, nblk, MXU_DIM), lambda i: (0, 0, 0)),
            pl.BlockSpec((r, LANES), lambda i: (0, 0)),
        ],
        out_shape=[
            jax.ShapeDtypeStruct((r, nblk, MXU_DIM), jnp.int32),
            jax.ShapeDtypeStruct((r, nblk, MXU_DIM), F32),
            jax.ShapeDtypeStruct((r, LANES), jnp.int32),
        ],
        compiler_params=pltpu.CompilerParams(dimension_semantics=("arbitrary",)),
        name="route",
    )(aff2)


def _segment(cnt_ref, row, pb, cap):
    c0 = jnp.minimum(cnt_ref[row, pb], cap)
    c1 = jnp.minimum(cnt_ref[row, pb + 1], cap)
    a0 = jnp.minimum(jnp.bitwise_and(c0, -SLOT_ALIGN), cap - WIN)
    return c0, c1, pl.multiple_of(a0, SLOT_ALIGN)


def _extra_windows(c1, a0):
    return (jnp.maximum(c1 - (a0 + WIN), 0) + WIN - 1) // WIN


def _gather_kernel(n_exp, cnt_ref, h2_ref, slot_ref, aff_ref, x_ref, gs_ref):
    b, eg = pl.program_id(0), pl.program_id(1)
    ng, cap = x_ref.shape[1], x_ref.shape[2]
    nblk = slot_ref.shape[2]
    sub = lax.broadcasted_iota(jnp.int32, (WIN, MXU_DIM), 0)
    x_ref[...] = jnp.zeros_like(x_ref)
    gs_ref[...] = jnp.zeros_like(gs_ref)

    def merge(k, a0, rows, gate):
        win = (0, k, pl.ds(a0, WIN), slice(None))
        x_ref[win] += rows.astype(BF16)
        gs_ref[win] += gate

    def tokens(pb):
        return pl.ds(pl.multiple_of(pb * MXU_DIM, MXU_DIM), MXU_DIM)

    def first_windows(pb):
        segs, gates, onehots = [], [], []
        overflow = jnp.int32(0)
        for k in range(ng):
            c0, c1, a0 = _segment(cnt_ref, b * n_exp + eg * ng + k, pb, cap)
            match = slot_ref[0, k, pl.ds(pb, 1), :] == (sub + a0)
            onehots.append(jnp.where(match, 1.0, 0.0).astype(BF16))
            gates.append(jnp.sum(jnp.where(match, aff_ref[0, k, pl.ds(pb, 1), :], 0.0),
                                 axis=1, keepdims=True))
            segs.append((c0, c1, a0))
            overflow = overflow + jnp.maximum(c1 - (a0 + WIN), 0)
        rows = _dot(jnp.concatenate(onehots, axis=0), h2_ref[0, tokens(pb), :])
        for k in range(ng):
            merge(k, segs[k][2], rows[k * WIN:(k + 1) * WIN, :], gates[k])
        return overflow

    def further_windows(pb):
        def per_expert(k, carry2):
            _, c1, a0 = _segment(cnt_ref, b * n_exp + eg * ng + k, pb, cap)

            def per_window(w, carry3):
                first = a0 + (w + 1) * WIN
                aw = pl.multiple_of(jnp.minimum(first, cap - WIN), SLOT_ALIGN)
                srow = slot_ref[0, k, pl.ds(pb, 1), :]
                match = (srow == (sub + aw)) & (srow >= first)
                onehot = jnp.where(match, 1.0, 0.0).astype(BF16)
                gate = jnp.sum(jnp.where(match, aff_ref[0, k, pl.ds(pb, 1), :], 0.0),
                               axis=1, keepdims=True)
                merge(k, aw, _dot(onehot, h2_ref[0, tokens(pb), :]), gate)
                return carry3

            return lax.fori_loop(0, _extra_windows(c1, a0), per_window, carry2)

        lax.fori_loop(0, ng, per_expert, 0)

    def trip_body(i, carry):
        overflows = [first_windows(GATHER_BLOCKS * i + j) for j in range(GATHER_BLOCKS)]
        for j in range(GATHER_BLOCKS):
            pl.when(overflows[j] > 0)(functools.partial(further_windows, GATHER_BLOCKS * i + j))
        return carry

    lax.fori_loop(0, nblk // GATHER_BLOCKS, trip_body, 0)


def _gather(cnt2, h2, slots4, aff4, cap):
    b, s, d = h2.shape
    e = slots4.shape[1]
    nblk = s // MXU_DIM
    grid_spec = pltpu.PrefetchScalarGridSpec(
        num_scalar_prefetch=1,
        grid=(b, e // EG),
        in_specs=[
            pl.BlockSpec((1, s, d), lambda bi, gi, c: (bi, 0, 0)),
            pl.BlockSpec((1, EG, nblk, MXU_DIM), lambda bi, gi, c: (bi, gi, 0, 0)),
            pl.BlockSpec((1, EG, nblk, MXU_DIM), lambda bi, gi, c: (bi, gi, 0, 0)),
        ],
        out_specs=[
            pl.BlockSpec((1, EG, cap, d), lambda bi, gi, c: (bi, gi, 0, 0)),
            pl.BlockSpec((1, EG, cap, LANES), lambda bi, gi, c: (bi, gi, 0, 0)),
        ],
    )
    return pl.pallas_call(
        functools.partial(_gather_kernel, e),
        grid_spec=grid_spec,
        out_shape=[
            jax.ShapeDtypeStruct((b, e, cap, d), BF16),
            jax.ShapeDtypeStruct((b, e, cap, LANES), F32),
        ],
        compiler_params=pltpu.CompilerParams(
            dimension_semantics=("arbitrary", "arbitrary"), vmem_limit_bytes=VMEM_LIMIT),
        name="gather",
    )(cnt2, h2, slots4, aff4)


def _ffn_kernel(nchunk, x_ref, gs_ref, *refs):
    w_refs, out_hbm, acc_s, out_s, out_sem = refs[:-4], refs[-4], refs[-3], refs[-2], refs[-1]
    groups = [w_refs[3 * k:3 * k + 3] for k in range(FFN_CHUNKS)]
    e, f = pl.program_id(0), pl.program_id(1)
    nb, _, cap, d = x_ref.shape
    last = pl.num_programs(1) - 1

    def writeback(expert):
        return pltpu.make_async_copy(out_s, out_hbm.at[:, expert], out_sem)

    @pl.when((e == 0) & (f == 0))
    def _init():
        acc_s[...] = jnp.zeros_like(acc_s)

    def accumulate(chunks):
        wg = jnp.concatenate([w[0][0].astype(BF16) for w in chunks], axis=1)
        wu = jnp.concatenate([w[1][0].astype(BF16) for w in chunks], axis=1)
        wd = jnp.concatenate([w[2][0].astype(BF16) for w in chunks], axis=0)
        for piece in range(nb * cap // FFN_ROWS):
            bi, r0 = divmod(piece * FFN_ROWS, cap)
            x = x_ref[bi, 0, r0:r0 + FFN_ROWS, :]
            g = _dot(x, wg)
            u = _dot(x, wu)
            hid = (g * jax.nn.sigmoid(g) * u).astype(BF16)
            y = _dot(hid, wd)
            rows = slice(piece * FFN_ROWS, (piece + 1) * FFN_ROWS)
            acc_s[rows, :] = jnp.where(f == 0, y, acc_s[rows, :] + y)

    tail = nchunk % FFN_CHUNKS
    if tail == 0:
        accumulate(groups)
    else:
        pl.when(f < last)(lambda: accumulate(groups))
        pl.when(f == last)(lambda: accumulate(groups[:tail]))

    @pl.when((f == last) & (e > 0))
    def _previous_done():
        writeback(e - 1).wait()

    @pl.when(f == last)
    def _emit():
        gate = gs_ref[:, 0, :, 0:1].reshape(nb * cap, 1)
        out_s[...] = (acc_s[...] * gate).reshape(nb, cap, d).astype(BF16)
        writeback(e).start()

    @pl.when((f == last) & (e == pl.num_programs(0) - 1))
    def _drain():
        writeback(e).wait()


def _ffn(xg, gs, w_gate, w_up, w_down, e0):
    b, e, cap, d = xg.shape
    nchunk = w_gate.shape[2] // FC
    weight_specs = []
    for k in range(FFN_CHUNKS):
        chunk = lambda fi, k=k: jnp.minimum(FFN_CHUNKS * fi + k, nchunk - 1)
        weight_specs += [
            pl.BlockSpec((1, d, FC), lambda ei, fi, chunk=chunk: (e0 + ei, 0, chunk(fi))),
            pl.BlockSpec((1, d, FC), lambda ei, fi, chunk=chunk: (e0 + ei, 0, chunk(fi))),
            pl.BlockSpec((1, FC, d), lambda ei, fi, chunk=chunk: (e0 + ei, chunk(fi), 0)),
        ]
    return pl.pallas_call(
        functools.partial(_ffn_kernel, nchunk),
        grid=(e, pl.cdiv(nchunk, FFN_CHUNKS)),
        in_specs=[
            pl.BlockSpec((b, 1, cap, d), lambda ei, fi: (0, ei, 0, 0)),
            pl.BlockSpec((b, 1, cap, LANES), lambda ei, fi: (0, ei, 0, 0)),
        ] + weight_specs,
        out_specs=pl.BlockSpec(memory_space=pl.ANY),
        out_shape=jax.ShapeDtypeStruct((b, e, cap, d), BF16),
        scratch_shapes=[
            pltpu.VMEM((b * cap, d), F32),
            pltpu.VMEM((b, cap, d), BF16),
            pltpu.SemaphoreType.DMA(()),
        ],
        compiler_params=pltpu.CompilerParams(
            dimension_semantics=("arbitrary", "arbitrary"), vmem_limit_bytes=VMEM_LIMIT),
        name="ffn",
    )(xg, gs, *([w_gate, w_up, w_down] * FFN_CHUNKS))


def _combine_kernel(n_exp, cnt_ref, eo_ref, slot_ref, x1_ref, mod_ref, g_ref, o_ref,
                    stage_s, oh_s, y_s):
    b, q = pl.program_id(0), pl.program_id(1)
    cap = eo_ref.shape[2]
    nblk_q = o_ref.shape[0] // MXU_DIM
    sub = lax.broadcasted_iota(jnp.int32, (WIN, MXU_DIM), 0)
    gain = g_ref[...] * mod_ref[0, 5:6, :]

    for j in range(nblk_q):
        pb = q * nblk_q + j
        overflow = jnp.int32(0)
        for e in range(n_exp):
            _, c1, a0 = _segment(cnt_ref, b * n_exp + e, pb, cap)
            stage_s[e * WIN:(e + 1) * WIN, :] = eo_ref[0, e, pl.ds(a0, WIN), :]
            match = slot_ref[0, e, pl.ds(pb, 1), :] == (sub + a0)
            oh_s[e * WIN:(e + 1) * WIN, :] = jnp.where(match, 1.0, 0.0).astype(BF16)
            overflow = overflow + jnp.maximum(c1 - (a0 + WIN), 0)
        y_s[...] = _dot_tn(oh_s[...], stage_s[...])

        @pl.when(overflow > 0)
        def _long_segments(j=j, pb=pb):
            def per_expert(e, carry):
                _, c1, a0 = _segment(cnt_ref, b * n_exp + e, pb, cap)

                def per_window(k, carry2):
                    first = a0 + (k + 1) * WIN
                    aw = pl.multiple_of(jnp.minimum(first, cap - WIN), SLOT_ALIGN)
                    srow = slot_ref[0, e, pl.ds(pb, 1), :]
                    match = (srow == (sub + aw)) & (srow >= first)
                    onehot = jnp.where(match, 1.0, 0.0).astype(BF16)
                    y_s[...] += _dot_tn(onehot, eo_ref[0, e, pl.ds(aw, WIN), :])
                    return carry2

                return lax.fori_loop(0, _extra_windows(c1, a0), per_window, carry)

            lax.fori_loop(0, n_exp, per_expert, 0)

        sl = slice(j * MXU_DIM, (j + 1) * MXU_DIM)
        o_ref[sl, :] = x1_ref[sl, :] + _rms(y_s[...], gain)


def _combine(cnt2, eo, slots4, x1, mod, g_post):
    b, e, cap, d = eo.shape
    t = x1.shape[0]
    s = t // b
    nq = s // QP
    nblk_q = QP // MXU_DIM
    grid_spec = pltpu.PrefetchScalarGridSpec(
        num_scalar_prefetch=1,
        grid=(b, nq),
        in_specs=[
            pl.BlockSpec((1, e, cap, d), lambda bi, qi, c: (bi, 0, 0, 0), pipeline_mode=pl.Buffered(1)),
            pl.BlockSpec((1, e, s // MXU_DIM, MXU_DIM), lambda bi, qi, c: (bi, 0, 0, 0)),
            pl.BlockSpec((QP, d), lambda bi, qi, c: (bi * nq + qi, 0)),
            pl.BlockSpec((1, 6, d), lambda bi, qi, c: (bi, 0, 0)),
            pl.BlockSpec((1, d), lambda bi, qi, c: (0, 0)),
        ],
        out_specs=pl.BlockSpec((QP, d), lambda bi, qi, c: (bi * nq + qi, 0)),
        scratch_shapes=[
            pltpu.VMEM((e * WIN, d), BF16),
            pltpu.VMEM((e * WIN, MXU_DIM), BF16),
            pltpu.VMEM((MXU_DIM, d), F32),
        ],
    )
    return pl.pallas_call(
        functools.partial(_combine_kernel, e),
        grid_spec=grid_spec,
        out_shape=jax.ShapeDtypeStruct((t, d), F32),
        compiler_params=pltpu.CompilerParams(
            dimension_semantics=("arbitrary", "arbitrary"),
            vmem_limit_bytes=VMEM_LIMIT),
        name="combine",
    )(cnt2, eo, slots4, x1, mod, g_post)


def _layer(layer, x, c, w_ada_all, b_ada, norm_pre_mix, norm_post_mix, w_in, sink, sgu_ln_g,
           sgu_ln_b, w_s, b_s, norm_out_attn, norm_out_gmlp, w_out, norm_pre_ffn, norm_post_ffn,
           w_router, w_gate_all, w_up_all, w_down_all):
    b, s, d = x.shape
    t = b * s
    dg = sgu_ln_g.shape[0]
    e = w_router.shape[1]
    cap = CAPACITY_FACTOR * s // e
    assert s % QP == 0 and s % TM == 0 and w_gate_all.shape[2] % FC == 0 and e % EG == 0
    assert cap % SLOT_ALIGN == 0 and cap >= WIN and s // MXU_DIM < LANES and d == D_ATTN + dg
    assert (s // MXU_DIM) % GATHER_BLOCKS == 0

    mod = _adaln(c, w_ada_all, b_ada, layer).reshape(b, 6, d)

    x2 = x.reshape(t, d)
    q, kv, u, vg = _proj(x2, mod, norm_pre_mix.reshape(1, d), w_in.astype(BF16), s)

    bias = jnp.asarray(_attn_bias_table())
    bs_full = jnp.repeat(b_s.T, dg // N_GMLP_GROUPS, axis=1)
    x1, h2, aff_t = _mix(
        x2, q, kv, u, vg, mod, sink, bias,
        w_s.astype(BF16).reshape(N_GMLP_GROUPS // 2, 2 * BLOCK, BLOCK), bs_full,
        sgu_ln_g.reshape(1, dg), sgu_ln_b.reshape(1, dg), norm_out_attn.reshape(1, D_ATTN),
        norm_out_gmlp.reshape(1, dg), w_out.astype(BF16), norm_post_mix.reshape(1, d),
        norm_pre_ffn.reshape(1, d), w_router.T.astype(BF16), b, s)

    slots3, aff3, cnt2 = _route(aff_t.reshape(b * e, s), cap)
    nblk = s // MXU_DIM
    slots4 = slots3.reshape(b, e, nblk, MXU_DIM)
    aff4 = aff3.reshape(b, e, nblk, MXU_DIM)
    xg, gs = _gather(cnt2, h2.reshape(b, s, d), slots4, aff4, cap)
    eo = _ffn(xg, gs, w_gate_all, w_up_all, w_down_all, layer * e)
    out = _combine(cnt2, eo, slots4, x1, mod, norm_post_ffn.reshape(1, d))
    return out.reshape(b, s, d)


def kernel(x, c, w_ada, b_ada, norm_pre_mix, norm_post_mix, w_in, sink, sgu_ln_g, sgu_ln_b, w_s, b_s, norm_out_attn, norm_out_gmlp, w_out, norm_pre_ffn, norm_post_ffn, w_router, w_gate, w_up, w_down):
    depth, d = w_ada.shape[0], w_ada.shape[1]
    w_ada_all = w_ada.reshape(depth * d, w_ada.shape[2])
    stack = lambda w: w.reshape((depth * w.shape[1],) + w.shape[2:])
    w_gate_all, w_up_all, w_down_all = stack(w_gate), stack(w_up), stack(w_down)
    for l in range(depth):
        x = _layer(l, x, c, w_ada_all, b_ada[l], norm_pre_mix[l], norm_post_mix[l], w_in[l],
                   sink[l], sgu_ln_g[l], sgu_ln_b[l], w_s[l], b_s[l], norm_out_attn[l],
                   norm_out_gmlp[l], w_out[l], norm_pre_ffn[l], norm_post_ffn[l], w_router[l],
                   w_gate_all, w_up_all, w_down_all)
    return x
```

```python
import functools

import numpy as np
import jax
import jax.numpy as jnp
from jax import lax
from jax.experimental import pallas as pl
from jax.experimental.pallas import tpu as pltpu

N_HEADS = 8
N_KV_HEADS = 2
HEAD_DIM = 64
D_ATTN = N_HEADS * HEAD_DIM
D_KV = N_KV_HEADS * HEAD_DIM
GROUP = N_HEADS // N_KV_HEADS
WINDOW = 128
BLOCK = 128
N_GMLP_GROUPS = 8
CAPACITY_FACTOR = 2
EPS = 1e-6
MASK_VALUE = -1e30

LANES = 128
MXU_DIM = 256

TM = 1024
FC = 256
FFN_CHUNKS = 4
FFN_ROWS = 512
PROJ_ROWS = 256
EG = 4
GATHER_BLOCKS = 8
QP = 1024
WIN = 64
SLOT_ALIGN = 16
VMEM_LIMIT = 56 * 1024 * 1024

BF16 = jnp.bfloat16
F32 = jnp.float32


def _rms(x, g):
    return x * lax.rsqrt(jnp.mean(x * x, axis=-1, keepdims=True) + EPS) * g


def _dot(a, b):
    return jnp.dot(a, b, preferred_element_type=F32)


def _dot_nt(a, b):
    return lax.dot_general(a, b, (((1,), (1,)), ((), ())), preferred_element_type=F32)


def _dot_tn(a, b):
    return lax.dot_general(a, b, (((0,), (0,)), ((), ())), preferred_element_type=F32)


def _adaln_kernel(c_ref, w_ref, b_ref, o_ref):
    c = c_ref[...]
    a = (c * jax.nn.sigmoid(c)).astype(BF16)
    o_ref[...] = _dot(a, w_ref[...].astype(BF16)) + b_ref[...]


def _adaln(c, w_ada_all, b_ada, layer):
    b, d = c.shape
    n = w_ada_all.shape[1]
    return pl.pallas_call(
        _adaln_kernel,
        grid=(n // d,),
        in_specs=[
            pl.BlockSpec((b, d), lambda j: (0, 0)),
            pl.BlockSpec((d, d), lambda j: (layer, j)),
            pl.BlockSpec((1, d), lambda j: (0, j)),
        ],
        out_specs=pl.BlockSpec((b, d), lambda j: (0, j)),
        out_shape=jax.ShapeDtypeStruct((b, n), F32),
        compiler_params=pltpu.CompilerParams(dimension_semantics=("arbitrary",)),
        name="adaln",
    )(c, w_ada_all, b_ada.reshape(1, n))


def _gelu_tanh(x):
    c = np.float32(np.sqrt(2.0 / np.pi))
    return x * (0.5 * (1.0 + jnp.tanh(c * (x + 0.044715 * (x * x * x)))))


def _proj_kernel(x_ref, mod_ref, g_ref, w_ref, q_ref, kv_ref, u_ref, vg_ref):
    shift = mod_ref[0, 0:1, :]
    gain = g_ref[...] * (1.0 + mod_ref[0, 1:2, :])
    n_att = D_ATTN + 2 * D_KV
    dg = u_ref.shape[1]
    lo = lax.broadcasted_iota(jnp.int32, (1, LANES), 1) < HEAD_DIM
    for r in range(0, TM, PROJ_ROWS):
        rows = slice(r, r + PROJ_ROWS)
        h = (_rms(x_ref[rows, :], gain) + shift).astype(BF16)
        z = _gelu_tanh(_dot(h, w_ref[:, n_att:]))
        u_ref[rows, :] = z[:, :dg]
        vg_ref[rows, :] = z[:, dg:]
        p = _dot(h, w_ref[:, :n_att])
        q_ref[rows, :] = (p[:, :D_ATTN] * (HEAD_DIM ** -0.5)).astype(BF16)
        for j in range(2):
            t = p[:, D_ATTN + j * D_KV:D_ATTN + (j + 1) * D_KV]
            t_sw = pltpu.roll(t, HEAD_DIM, axis=1)
            kv_ref[rows, (2 * j) * LANES:(2 * j + 1) * LANES] = jnp.where(lo, t, t_sw).astype(BF16)
            kv_ref[rows, (2 * j + 1) * LANES:(2 * j + 2) * LANES] = jnp.where(lo, t_sw, t).astype(BF16)


def _proj(x2, mod, g_pre, w_in, s):
    t, d = x2.shape
    nw = w_in.shape[1]
    dg = (nw - D_ATTN - 2 * D_KV) // 2
    nsb = s // TM
    return pl.pallas_call(
        _proj_kernel,
        grid=(t // TM,),
        in_specs=[
            pl.BlockSpec((TM, d), lambda i: (i, 0)),
            pl.BlockSpec((1, 6, d), lambda i: (i // nsb, 0, 0)),
            pl.BlockSpec((1, d), lambda i: (0, 0)),
            pl.BlockSpec((d, nw), lambda i: (0, 0)),
        ],
        out_specs=[
            pl.BlockSpec((TM, D_ATTN), lambda i: (i, 0)),
            pl.BlockSpec((TM, D_ATTN), lambda i: (i, 0)),
            pl.BlockSpec((TM, dg), lambda i: (i, 0)),
            pl.BlockSpec((TM, dg), lambda i: (i, 0)),
        ],
        out_shape=[
            jax.ShapeDtypeStruct((t, D_ATTN), BF16),
            jax.ShapeDtypeStruct((t, D_ATTN), BF16),
            jax.ShapeDtypeStruct((t, dg), F32),
            jax.ShapeDtypeStruct((t, dg), F32),
        ],
        compiler_params=pltpu.CompilerParams(
            dimension_semantics=("arbitrary",), vmem_limit_bytes=VMEM_LIMIT),
        name="proj",
    )(x2, mod, g_pre, w_in)


def _attn_bias_table():
    qi = np.arange(BLOCK)[:, None]
    kj = np.arange(3 * BLOCK)[None, :] - BLOCK
    dist = np.abs(qi - kj).astype(np.float32)
    band = dist <= WINDOW
    slopes = np.exp2(-8.0 * np.arange(1, N_HEADS + 1, dtype=np.float32) / N_HEADS).astype(np.float32)
    key_ok = [kj >= 0, np.ones_like(kj, bool), kj < BLOCK]
    tab = np.empty((3, N_HEADS, BLOCK, 3 * BLOCK), np.float32)
    for v in range(3):
        ok = band & key_ok[v]
        for h in range(N_HEADS):
            tab[v, h] = np.where(ok, -slopes[h] * dist, np.float32(MASK_VALUE))
    return tab.reshape(3, N_KV_HEADS, GROUP * BLOCK, 3 * BLOCK)


def _mix_kernel(nblk_seq, sink_ref, x_ref, q_ref, kvp_ref, kvc_ref, kvn_ref, u_ref, vg_ref,
                mod_ref, bias_ref, ws_ref, bs_ref, lng_ref, lnb_ref, ga_ref, gg_ref,
                wout_ref, gpost_ref, gpre2_ref, wr_ref,
                x1_ref, h2_ref, aff_ref, kcat, attn_s, gm_s):
    i = pl.program_id(0)
    nb_tile = TM // BLOCK
    kcat[0:BLOCK, :] = kvp_ref[...]
    kcat[BLOCK:BLOCK + TM, :] = kvc_ref[...]
    kcat[BLOCK + TM:, :] = kvn_ref[...]
    lane = lax.broadcasted_iota(jnp.int32, (1, LANES), 1)
    lo = lane < HEAD_DIM
    zero = jnp.zeros((), BF16)
    head_of_row = lax.broadcasted_iota(jnp.int32, (GROUP * BLOCK, 1), 0) // BLOCK
    sinks = []
    for g in range(N_KV_HEADS):
        col = jnp.zeros((GROUP * BLOCK, 1), F32)
        for j in range(GROUP):
            col = jnp.where(head_of_row == j, sink_ref[g * GROUP + j], col)
        sinks.append(col)

    def block_body(a, carry):
        r0 = pl.multiple_of(a * BLOCK, BLOCK)
        pos = (i * nb_tile + a) % nblk_seq
        var = jnp.where(pos == 0, 0, jnp.where(pos == nblk_seq - 1, 2, 1))
        kv = kcat[pl.ds(r0, 3 * BLOCK), :]
        for g in range(N_KV_HEADS):
            k = kv[:, g * LANES:(g + 1) * LANES]
            v = kv[:, (N_KV_HEADS + g) * LANES:(N_KV_HEADS + g + 1) * LANES]
            pieces = []
            for pi in range(GROUP // 2):
                qp = q_ref[pl.ds(r0, BLOCK), (g * GROUP // 2 + pi) * LANES:(g * GROUP // 2 + pi + 1) * LANES]
                pieces += [jnp.where(lo, qp, zero), jnp.where(lo, zero, qp)]
            s = _dot_nt(jnp.concatenate(pieces, axis=0), k) + bias_ref[var, g]
            m = jnp.maximum(jnp.max(s, axis=-1, keepdims=True), sinks[g])
            p = jnp.exp(s - m)
            den = jnp.sum(p, axis=-1, keepdims=True) + jnp.exp(sinks[g] - m)
            o = _dot(p.astype(BF16), v) / den
            for pi in range(GROUP // 2):
                even = o[(2 * pi) * BLOCK:(2 * pi + 1) * BLOCK, :]
                odd = o[(2 * pi + 1) * BLOCK:(2 * pi + 2) * BLOCK, :]
                col0 = (g * GROUP // 2 + pi) * LANES
                attn_s[pl.ds(r0, BLOCK), col0:col0 + LANES] = jnp.where(lo, even, odd)
        vg = vg_ref[pl.ds(r0, BLOCK), :]
        mu = jnp.mean(vg, axis=-1, keepdims=True)
        xc = vg - mu
        vn = (xc * lax.rsqrt(jnp.mean(xc * xc, axis=-1, keepdims=True) + EPS) * lng_ref[...]
              + lnb_ref[...]).astype(BF16)
        for pi in range(N_GMLP_GROUPS // 2):
            vnp = vn[:, pi * LANES:(pi + 1) * LANES]
            both = _dot(ws_ref[pi], vnp)
            z = jnp.where(lo, both[:BLOCK], both[BLOCK:])
            sl = slice(pi * LANES, (pi + 1) * LANES)
            gm_s[pl.ds(r0, BLOCK), sl] = u_ref[pl.ds(r0, BLOCK), sl] * (z + bs_ref[:, sl])
        return carry

    lax.fori_loop(0, nb_tile, block_body, 0, unroll=2)

    na = _rms(attn_s[...], ga_ref[...]).astype(BF16)
    ng = _rms(gm_s[...], gg_ref[...]).astype(BF16)
    mo = _dot(na, wout_ref[0:D_ATTN, :]) + _dot(ng, wout_ref[D_ATTN:, :])
    x1 = x_ref[...] + _rms(mo, gpost_ref[...] * mod_ref[0, 2:3, :])
    x1_ref[...] = x1
    h2 = (_rms(x1, gpre2_ref[...] * (1.0 + mod_ref[0, 4:5, :])) + mod_ref[0, 3:4, :]).astype(BF16)
    h2_ref[...] = h2
    lg = _dot_nt(wr_ref[...], h2)
    lg = lg - jnp.max(lg, axis=0, keepdims=True)
    ex = jnp.exp(lg)
    aff_ref[0] = ex / jnp.sum(ex, axis=0, keepdims=True)


def _mix(x2, q, kv, u, vg, mod, sink, bias, ws, bs_full, ln_g, ln_b, g_attn, g_gmlp,
         w_out, g_post, g_pre2, w_r_t, b, s):
    t, d = x2.shape
    dg = u.shape[1]
    e = w_r_t.shape[0]
    nsb = s // TM
    nb_tile = TM // BLOCK
    nblk = t // BLOCK
    row = lambda i: (i, 0)
    const2 = lambda i: (0, 0)
    in_specs = [
        pl.BlockSpec(memory_space=pltpu.SMEM),
        pl.BlockSpec((TM, d), row),
        pl.BlockSpec((TM, D_ATTN), row),
        pl.BlockSpec((BLOCK, 4 * LANES), lambda i: (jnp.maximum(i * nb_tile - 1, 0), 0)),
        pl.BlockSpec((TM, 4 * LANES), row),
        pl.BlockSpec((BLOCK, 4 * LANES), lambda i: (jnp.minimum((i + 1) * nb_tile, nblk - 1), 0)),
        pl.BlockSpec((TM, dg), row),
        pl.BlockSpec((TM, dg), row),
        pl.BlockSpec((1, 6, d), lambda i: (i // nsb, 0, 0)),
        pl.BlockSpec(bias.shape, lambda i: (0, 0, 0, 0)),
        pl.BlockSpec(ws.shape, lambda i: (0, 0, 0)),
        pl.BlockSpec((BLOCK, dg), const2),
        pl.BlockSpec((1, dg), const2),
        pl.BlockSpec((1, dg), const2),
        pl.BlockSpec((1, D_ATTN), const2),
        pl.BlockSpec((1, dg), const2),
        pl.BlockSpec(w_out.shape, const2),
        pl.BlockSpec((1, d), const2),
        pl.BlockSpec((1, d), const2),
        pl.BlockSpec(w_r_t.shape, const2),
    ]
    out_specs = [
        pl.BlockSpec((TM, d), row),
        pl.BlockSpec((TM, d), row),
        pl.BlockSpec((1, e, TM), lambda i: (i // nsb, 0, i % nsb)),
    ]
    out_shape = [
        jax.ShapeDtypeStruct((t, d), F32),
        jax.ShapeDtypeStruct((t, d), BF16),
        jax.ShapeDtypeStruct((b, e, s), F32),
    ]
    return pl.pallas_call(
        functools.partial(_mix_kernel, s // BLOCK),
        grid=(t // TM,),
        in_specs=in_specs,
        out_specs=out_specs,
        out_shape=out_shape,
        scratch_shapes=[
            pltpu.VMEM((TM + 2 * BLOCK, 4 * LANES), BF16),
            pltpu.VMEM((TM, D_ATTN), F32),
            pltpu.VMEM((TM, dg), F32),
        ],
        compiler_params=pltpu.CompilerParams(
            dimension_semantics=("arbitrary",), vmem_limit_bytes=VMEM_LIMIT),
        name="mix",
    )(sink, x2, q, kv, kv, kv, u, vg, mod, bias, ws, bs_full, ln_g, ln_b, g_attn, g_gmlp,
      w_out, g_post, g_pre2, w_r_t)


def _route_kernel(cap, aff_ref, slot_ref, affb_ref, cnt_ref):
    aff = aff_ref[...]
    e, s = aff.shape
    capf = jnp.float32(cap)

    def count_ge(t):
        return jnp.sum(jnp.where(aff >= t, 1.0, 0.0), axis=1, keepdims=True)

    def bit_body(it, cand):
        trial = cand | jnp.left_shift(jnp.int32(1), 30 - it)
        t = lax.bitcast_convert_type(trial, F32)
        return jnp.where(count_ge(t) >= capf, trial, cand)

    cand = lax.fori_loop(0, 31, bit_body, jnp.zeros((e, 1), jnp.int32))
    thr = lax.bitcast_convert_type(cand, F32)
    need = capf - jnp.sum(jnp.where(aff > thr, 1.0, 0.0), axis=1, keepdims=True)

    r = lax.broadcasted_iota(jnp.int32, (LANES, LANES), 0)
    c = lax.broadcasted_iota(jnp.int32, (LANES, LANES), 1)
    tri = jnp.where(r < c, 1.0, 0.0).astype(BF16)
    lane = lax.broadcasted_iota(jnp.int32, (e, LANES), 1)
    carry_eq = jnp.zeros((e, 1), F32)
    carry_sel = jnp.zeros((e, 1), F32)
    cnt = jnp.zeros((e, LANES), F32)
    per_blk = MXU_DIM // LANES
    for j in range(s // LANES):
        a = aff[:, j * LANES:(j + 1) * LANES]
        gt = a > thr
        eq = jnp.where(a == thr, 1.0, 0.0)
        tie_rank = _dot(eq.astype(BF16), tri) + carry_eq
        sel = jnp.where(gt | ((eq > 0.0) & (tie_rank < need)), 1.0, 0.0)
        slot = _dot(sel.astype(BF16), tri) + carry_sel
        blk = (slice(None), j // per_blk, slice((j % per_blk) * LANES, (j % per_blk + 1) * LANES))
        slot_ref[blk] = jnp.where(sel > 0.0, slot, -1.0).astype(jnp.int32)
        affb_ref[blk] = a
        if j % per_blk == 0:
            cnt = jnp.where(lane == j // per_blk, carry_sel, cnt)
        carry_eq = carry_eq + jnp.sum(eq, axis=1, keepdims=True)
        carry_sel = carry_sel + jnp.sum(sel, axis=1, keepdims=True)
    cnt = jnp.where(lane >= s // MXU_DIM, carry_sel, cnt)
    cnt_ref[...] = cnt.astype(jnp.int32)


def _route(aff2, cap):
    r, s = aff2.shape
    nblk = s // MXU_DIM
    return pl.pallas_call(
        functools.partial(_route_kernel, cap),
        grid=(1,),
        in_specs=[pl.BlockSpec((r, s), lambda i: (0, 0))],
        out_specs=[
            pl.BlockSpec((r, nblk, MXU_DIM), lambda i: (0, 0, 0)),
            pl.BlockSpec((r, nblk, MXU_DIM), lambda i: (0, 0, 0)),
            pl.BlockSpec((r, LANES), lambda i: (0, 0)),
        ],
        out_shape=[
            jax.ShapeDtypeStruct((r, nblk, MXU_DIM), jnp.int32),
            jax.ShapeDtypeStruct((r, nblk, MXU_DIM), F32),
            jax.ShapeDtypeStruct((r, LANES), jnp.int32),
        ],
        compiler_params=pltpu.CompilerParams(dimension_semantics=("arbitrary",)),
        name="route",
    )(aff2)


def _segment(cnt_ref, row, pb, cap):
    c0 = jnp.minimum(cnt_ref[row, pb], cap)
    c1 = jnp.minimum(cnt_ref[row, pb + 1], cap)
    a0 = jnp.minimum(jnp.bitwise_and(c0, -SLOT_ALIGN), cap - WIN)
    return c0, c1, pl.multiple_of(a0, SLOT_ALIGN)


def _extra_windows(c1, a0):
    return (jnp.maximum(c1 - (a0 + WIN), 0) + WIN - 1) // WIN


def _gather_kernel(n_exp, cnt_ref, h2_ref, slot_ref, aff_ref, x_ref, gs_ref):
    b, eg = pl.program_id(0), pl.program_id(1)
    ng, cap = x_ref.shape[1], x_ref.shape[2]
    nblk = slot_ref.shape[2]
    sub = lax.broadcasted_iota(jnp.int32, (WIN, MXU_DIM), 0)
    x_ref[...] = jnp.zeros_like(x_ref)
    gs_ref[...] = jnp.zeros_like(gs_ref)

    def merge(k, a0, rows, gate):
        win = (0, k, pl.ds(a0, WIN), slice(None))
        x_ref[win] += rows.astype(BF16)
        gs_ref[win] += gate

    def tokens(pb):
        return pl.ds(pl.multiple_of(pb * MXU_DIM, MXU_DIM), MXU_DIM)

    def first_windows(pb):
        segs, gates, onehots = [], [], []
        overflow = jnp.int32(0)
        for k in range(ng):
            c0, c1, a0 = _segment(cnt_ref, b * n_exp + eg * ng + k, pb, cap)
            match = slot_ref[0, k, pl.ds(pb, 1), :] == (sub + a0)
            onehots.append(jnp.where(match, 1.0, 0.0).astype(BF16))
            gates.append(jnp.sum(jnp.where(match, aff_ref[0, k, pl.ds(pb, 1), :], 0.0),
                                 axis=1, keepdims=True))
            segs.append((c0, c1, a0))
            overflow = overflow + jnp.maximum(c1 - (a0 + WIN), 0)
        rows = _dot(jnp.concatenate(onehots, axis=0), h2_ref[0, tokens(pb), :])
        for k in range(ng):
            merge(k, segs[k][2], rows[k * WIN:(k + 1) * WIN, :], gates[k])
        return overflow

    def further_windows(pb):
        def per_expert(k, carry2):
            _, c1, a0 = _segment(cnt_ref, b * n_exp + eg * ng + k, pb, cap)

            def per_window(w, carry3):
                first = a0 + (w + 1) * WIN
                aw = pl.multiple_of(jnp.minimum(first, cap - WIN), SLOT_ALIGN)
                srow = slot_ref[0, k, pl.ds(pb, 1), :]
                match = (srow == (sub + aw)) & (srow >= first)
                onehot = jnp.where(match, 1.0, 0.0).astype(BF16)
                gate = jnp.sum(jnp.where(match, aff_ref[0, k, pl.ds(pb, 1), :], 0.0),
                               axis=1, keepdims=True)
                merge(k, aw, _dot(onehot, h2_ref[0, tokens(pb), :]), gate)
                return carry3

            return lax.fori_loop(0, _extra_windows(c1, a0), per_window, carry2)

        lax.fori_loop(0, ng, per_expert, 0)

    def trip_body(i, carry):
        overflows = [first_windows(GATHER_BLOCKS * i + j) for j in range(GATHER_BLOCKS)]
        for j in range(GATHER_BLOCKS):
            pl.when(overflows[j] > 0)(functools.partial(further_windows, GATHER_BLOCKS * i + j))
        return carry

    lax.fori_loop(0, nblk // GATHER_BLOCKS, trip_body, 0)


def _gather(cnt2, h2, slots4, aff4, cap):
    b, s, d = h2.shape
    e = slots4.shape[1]
    nblk = s // MXU_DIM
    grid_spec = pltpu.PrefetchScalarGridSpec(
        num_scalar_prefetch=1,
        grid=(b, e // EG),
        in_specs=[
            pl.BlockSpec((1, s, d), lambda bi, gi, c: (bi, 0, 0)),
            pl.BlockSpec((1, EG, nblk, MXU_DIM), lambda bi, gi, c: (bi, gi, 0, 0)),
            pl.BlockSpec((1, EG, nblk, MXU_DIM), lambda bi, gi, c: (bi, gi, 0, 0)),
        ],
        out_specs=[
            pl.BlockSpec((1, EG, cap, d), lambda bi, gi, c: (bi, gi, 0, 0)),
            pl.BlockSpec((1, EG, cap, LANES), lambda bi, gi, c: (bi, gi, 0, 0)),
        ],
    )
    return pl.pallas_call(
        functools.partial(_gather_kernel, e),
        grid_spec=grid_spec,
        out_shape=[
            jax.ShapeDtypeStruct((b, e, cap, d), BF16),
            jax.ShapeDtypeStruct((b, e, cap, LANES), F32),
        ],
        compiler_params=pltpu.CompilerParams(
            dimension_semantics=("arbitrary", "arbitrary"), vmem_limit_bytes=VMEM_LIMIT),
        name="gather",
    )(cnt2, h2, slots4, aff4)


def _ffn_kernel(nchunk, x_ref, gs_ref, *refs):
    w_refs, out_hbm, acc_s, out_s, out_sem = refs[:-4], refs[-4], refs[-3], refs[-2], refs[-1]
    groups = [w_refs[3 * k:3 * k + 3] for k in range(FFN_CHUNKS)]
    e, f = pl.program_id(0), pl.program_id(1)
    nb, _, cap, d = x_ref.shape
    last = pl.num_programs(1) - 1

    def writeback(expert):
        return pltpu.make_async_copy(out_s, out_hbm.at[:, expert], out_sem)

    @pl.when((e == 0) & (f == 0))
    def _init():
        acc_s[...] = jnp.zeros_like(acc_s)

    def accumulate(chunks):
        wg = jnp.concatenate([w[0][0].astype(BF16) for w in chunks], axis=1)
        wu = jnp.concatenate([w[1][0].astype(BF16) for w in chunks], axis=1)
        wd = jnp.concatenate([w[2][0].astype(BF16) for w in chunks], axis=0)
        for piece in range(nb * cap // FFN_ROWS):
            bi, r0 = divmod(piece * FFN_ROWS, cap)
            x = x_ref[bi, 0, r0:r0 + FFN_ROWS, :]
            g = _dot(x, wg)
            u = _dot(x, wu)
            hid = (g * jax.nn.sigmoid(g) * u).astype(BF16)
            y = _dot(hid, wd)
            rows = slice(piece * FFN_ROWS, (piece + 1) * FFN_ROWS)
            acc_s[rows, :] = jnp.where(f == 0, y, acc_s[rows, :] + y)

    tail = nchunk % FFN_CHUNKS
    if tail == 0:
        accumulate(groups)
    else:
        pl.when(f < last)(lambda: accumulate(groups))
        pl.when(f == last)(lambda: accumulate(groups[:tail]))

    @pl.when((f == last) & (e > 0))
    def _previous_done():
        writeback(e - 1).wait()

    @pl.when(f == last)
    def _emit():
        gate = gs_ref[:, 0, :, 0:1].reshape(nb * cap, 1)
        out_s[...] = (acc_s[...] * gate).reshape(nb, cap, d).astype(BF16)
        writeback(e).start()

    @pl.when((f == last) & (e == pl.num_programs(0) - 1))
    def _drain():
        writeback(e).wait()


def _ffn(xg, gs, w_gate, w_up, w_down, e0):
    b, e, cap, d = xg.shape
    nchunk = w_gate.shape[2] // FC
    weight_specs = []
    for k in range(FFN_CHUNKS):
        chunk = lambda fi, k=k: jnp.minimum(FFN_CHUNKS * fi + k, nchunk - 1)
        weight_specs += [
            pl.BlockSpec((1, d, FC), lambda ei, fi, chunk=chunk: (e0 + ei, 0, chunk(fi))),
            pl.BlockSpec((1, d, FC), lambda ei, fi, chunk=chunk: (e0 + ei, 0, chunk(fi))),
            pl.BlockSpec((1, FC, d), lambda ei, fi, chunk=chunk: (e0 + ei, chunk(fi), 0)),
        ]
    return pl.pallas_call(
        functools.partial(_ffn_kernel, nchunk),
        grid=(e, pl.cdiv(nchunk, FFN_CHUNKS)),
        in_specs=[
            pl.BlockSpec((b, 1, cap, d), lambda ei, fi: (0, ei, 0, 0)),
            pl.BlockSpec((b, 1, cap, LANES), lambda ei, fi: (0, ei, 0, 0)),
        ] + weight_specs,
        out_specs=pl.BlockSpec(memory_space=pl.ANY),
        out_shape=jax.ShapeDtypeStruct((b, e, cap, d), BF16),
        scratch_shapes=[
            pltpu.VMEM((b * cap, d), F32),
            pltpu.VMEM((b, cap, d), BF16),
            pltpu.SemaphoreType.DMA(()),
        ],
        compiler_params=pltpu.CompilerParams(
            dimension_semantics=("arbitrary", "arbitrary"), vmem_limit_bytes=VMEM_LIMIT),
        name="ffn",
    )(xg, gs, *([w_gate, w_up, w_down] * FFN_CHUNKS))


def _combine_kernel(n_exp, cnt_ref, eo_ref, slot_ref, x1_ref, mod_ref, g_ref, o_ref,
                    stage_s, oh_s, y_s):
    b, q = pl.program_id(0), pl.program_id(1)
    cap = eo_ref.shape[2]
    nblk_q = o_ref.shape[0] // MXU_DIM
    sub = lax.broadcasted_iota(jnp.int32, (WIN, MXU_DIM), 0)
    gain = g_ref[...] * mod_ref[0, 5:6, :]

    for j in range(nblk_q):
        pb = q * nblk_q + j
        overflow = jnp.int32(0)
        for e in range(n_exp):
            _, c1, a0 = _segment(cnt_ref, b * n_exp + e, pb, cap)
            stage_s[e * WIN:(e + 1) * WIN, :] = eo_ref[0, e, pl.ds(a0, WIN), :]
            match = slot_ref[0, e, pl.ds(pb, 1), :] == (sub + a0)
            oh_s[e * WIN:(e + 1) * WIN, :] = jnp.where(match, 1.0, 0.0).astype(BF16)
            overflow = overflow + jnp.maximum(c1 - (a0 + WIN), 0)
        y_s[...] = _dot_tn(oh_s[...], stage_s[...])

        @pl.when(overflow > 0)
        def _long_segments(j=j, pb=pb):
            def per_expert(e, carry):
                _, c1, a0 = _segment(cnt_ref, b * n_exp + e, pb, cap)

                def per_window(k, carry2):
                    first = a0 + (k + 1) * WIN
                    aw = pl.multiple_of(jnp.minimum(first, cap - WIN), SLOT_ALIGN)
                    srow = slot_ref[0, e, pl.ds(pb, 1), :]
                    match = (srow == (sub + aw)) & (srow >= first)
                    onehot = jnp.where(match, 1.0, 0.0).astype(BF16)
                    y_s[...] += _dot_tn(onehot, eo_ref[0, e, pl.ds(aw, WIN), :])
                    return carry2

                return lax.fori_loop(0, _extra_windows(c1, a0), per_window, carry)

            lax.fori_loop(0, n_exp, per_expert, 0)

        sl = slice(j * MXU_DIM, (j + 1) * MXU_DIM)
        o_ref[sl, :] = x1_ref[sl, :] + _rms(y_s[...], gain)


def _combine(cnt2, eo, slots4, x1, mod, g_post):
    b, e, cap, d = eo.shape
    t = x1.shape[0]
    s = t // b
    nq = s // QP
    nblk_q = QP // MXU_DIM
    grid_spec = pltpu.PrefetchScalarGridSpec(
        num_scalar_prefetch=1,
        grid=(b, nq),
        in_specs=[
            pl.BlockSpec((1, e, cap, d), lambda bi, qi, c: (bi, 0, 0, 0), pipeline_mode=pl.Buffered(1)),
            pl.BlockSpec((1, e, s // MXU_DIM, MXU_DIM), lambda bi, qi, c: (bi, 0, 0, 0)),
            pl.BlockSpec((QP, d), lambda bi, qi, c: (bi * nq + qi, 0)),
            pl.BlockSpec((1, 6, d), lambda bi, qi, c: (bi, 0, 0)),
            pl.BlockSpec((1, d), lambda bi, qi, c: (0, 0)),
        ],
        out_specs=pl.BlockSpec((QP, d), lambda bi, qi, c: (bi * nq + qi, 0)),
        scratch_shapes=[
            pltpu.VMEM((e * WIN, d), BF16),
            pltpu.VMEM((e * WIN, MXU_DIM), BF16),
            pltpu.VMEM((MXU_DIM, d), F32),
        ],
    )
    return pl.pallas_call(
        functools.partial(_combine_kernel, e),
        grid_spec=grid_spec,
        out_shape=jax.ShapeDtypeStruct((t, d), F32),
        compiler_params=pltpu.CompilerParams(
            dimension_semantics=("arbitrary", "arbitrary"),
            vmem_limit_bytes=VMEM_LIMIT),
        name="combine",
    )(cnt2, eo, slots4, x1, mod, g_post)


def _layer(layer, x, c, w_ada_all, b_ada, norm_pre_mix, norm_post_mix, w_in, sink, sgu_ln_g,
           sgu_ln_b, w_s, b_s, norm_out_attn, norm_out_gmlp, w_out, norm_pre_ffn, norm_post_ffn,
           w_router, w_gate_all, w_up_all, w_down_all):
    b, s, d = x.shape
    t = b * s
    dg = sgu_ln_g.shape[0]
    e = w_router.shape[1]
    cap = CAPACITY_FACTOR * s // e
    assert s % QP == 0 and s % TM == 0 and w_gate_all.shape[2] % FC == 0 and e % EG == 0
    assert cap % SLOT_ALIGN == 0 and cap >= WIN and s // MXU_DIM < LANES and d == D_ATTN + dg
    assert (s // MXU_DIM) % GATHER_BLOCKS == 0

    mod = _adaln(c, w_ada_all, b_ada, layer).reshape(b, 6, d)

    x2 = x.reshape(t, d)
    q, kv, u, vg = _proj(x2, mod, norm_pre_mix.reshape(1, d), w_in.astype(BF16), s)

    bias = jnp.asarray(_attn_bias_table())
    bs_full = jnp.repeat(b_s.T, dg // N_GMLP_GROUPS, axis=1)
    x1, h2, aff_t = _mix(
        x2, q, kv, u, vg, mod, sink, bias,
        w_s.astype(BF16).reshape(N_GMLP_GROUPS // 2, 2 * BLOCK, BLOCK), bs_full,
        sgu_ln_g.reshape(1, dg), sgu_ln_b.reshape(1, dg), norm_out_attn.reshape(1, D_ATTN),
        norm_out_gmlp.reshape(1, dg), w_out.astype(BF16), norm_post_mix.reshape(1, d),
        norm_pre_ffn.reshape(1, d), w_router.T.astype(BF16), b, s)

    slots3, aff3, cnt2 = _route(aff_t.reshape(b * e, s), cap)
    nblk = s // MXU_DIM
    slots4 = slots3.reshape(b, e, nblk, MXU_DIM)
    aff4 = aff3.reshape(b, e, nblk, MXU_DIM)
    xg, gs = _gather(cnt2, h2.reshape(b, s, d), slots4, aff4, cap)
    eo = _ffn(xg, gs, w_gate_all, w_up_all, w_down_all, layer * e)
    out = _combine(cnt2, eo, slots4, x1, mod, norm_post_ffn.reshape(1, d))
    return out.reshape(b, s, d)


def kernel(x, c, w_ada, b_ada, norm_pre_mix, norm_post_mix, w_in, sink, sgu_ln_g, sgu_ln_b, w_s, b_s, norm_out_attn, norm_out_gmlp, w_out, norm_pre_ffn, norm_post_ffn, w_router, w_gate, w_up, w_down):
    depth, d = w_ada.shape[0], w_ada.shape[1]
    w_ada_all = w_ada.reshape(depth * d, w_ada.shape[2])
    stack = lambda w: w.reshape((depth * w.shape[1],) + w.shape[2:])
    w_gate_all, w_up_all, w_down_all = stack(w_gate), stack(w_up), stack(w_down)
    for l in range(depth):
        x = _layer(l, x, c, w_ada_all, b_ada[l], norm_pre_mix[l], norm_post_mix[l], w_in[l],
                   sink[l], sgu_ln_g[l], sgu_ln_b[l], w_s[l], b_s[l], norm_out_attn[l],
                   norm_out_gmlp[l], w_out[l], norm_pre_ffn[l], norm_post_ffn[l], w_router[l],
                   w_gate_all, w_up_all, w_down_all)
    return x
```

```python
import functools

import numpy as np
import jax
import jax.numpy as jnp
from jax import lax
from jax.experimental import pallas as pl
from jax.experimental.pallas import tpu as pltpu

N_HEADS = 8
N_KV_HEADS = 2
HEAD_DIM = 64
D_ATTN = N_HEADS * HEAD_DIM
D_KV = N_KV_HEADS * HEAD_DIM
GROUP = N_HEADS // N_KV_HEADS
WINDOW = 128
BLOCK = 128
N_GMLP_GROUPS = 8
CAPACITY_FACTOR = 2
EPS = 1e-6
MASK_VALUE = -1e30

LANES = 128
MXU_DIM = 256

TM = 1024
FC = 256
FFN_CHUNKS = 3
PROJ_ROWS = 256
EG = 4
GATHER_BLOCKS = 8
QP = 1024
WIN = 64
SLOT_ALIGN = 16
VMEM_LIMIT = 56 * 1024 * 1024

BF16 = jnp.bfloat16
F32 = jnp.float32


def _rms(x, g):
    return x * lax.rsqrt(jnp.mean(x * x, axis=-1, keepdims=True) + EPS) * g


def _dot(a, b):
    return jnp.dot(a, b, preferred_element_type=F32)


def _dot_nt(a, b):
    return lax.dot_general(a, b, (((1,), (1,)), ((), ())), preferred_element_type=F32)


def _dot_tn(a, b):
    return lax.dot_general(a, b, (((0,), (0,)), ((), ())), preferred_element_type=F32)


def _adaln_kernel(c_ref, w_ref, b_ref, o_ref):
    c = c_ref[...]
    a = (c * jax.nn.sigmoid(c)).astype(BF16)
    o_ref[...] = _dot(a, w_ref[...].astype(BF16)) + b_ref[...]


def _adaln(c, w_ada_all, b_ada, layer):
    b, d = c.shape
    n = w_ada_all.shape[1]
    return pl.pallas_call(
        _adaln_kernel,
        grid=(n // d,),
        in_specs=[
            pl.BlockSpec((b, d), lambda j: (0, 0)),
            pl.BlockSpec((d, d), lambda j: (layer, j)),
            pl.BlockSpec((1, d), lambda j: (0, j)),
        ],
        out_specs=pl.BlockSpec((b, d), lambda j: (0, j)),
        out_shape=jax.ShapeDtypeStruct((b, n), F32),
        compiler_params=pltpu.CompilerParams(dimension_semantics=("arbitrary",)),
        name="adaln",
    )(c, w_ada_all, b_ada.reshape(1, n))


def _gelu_tanh(x):
    c = np.float32(np.sqrt(2.0 / np.pi))
    return x * (0.5 * (1.0 + jnp.tanh(c * (x + 0.044715 * (x * x * x)))))


def _proj_kernel(x_ref, mod_ref, g_ref, w_ref, q_ref, kv_ref, u_ref, vg_ref):
    shift = mod_ref[0, 0:1, :]
    gain = g_ref[...] * (1.0 + mod_ref[0, 1:2, :])
    n_att = D_ATTN + 2 * D_KV
    dg = u_ref.shape[1]
    lo = lax.broadcasted_iota(jnp.int32, (1, LANES), 1) < HEAD_DIM
    for r in range(0, TM, PROJ_ROWS):
        rows = slice(r, r + PROJ_ROWS)
        h = (_rms(x_ref[rows, :], gain) + shift).astype(BF16)
        z = _gelu_tanh(_dot(h, w_ref[:, n_att:]))
        u_ref[rows, :] = z[:, :dg]
        vg_ref[rows, :] = z[:, dg:]
        p = _dot(h, w_ref[:, :n_att])
        q_ref[rows, :] = (p[:, :D_ATTN] * (HEAD_DIM ** -0.5)).astype(BF16)
        for j in range(2):
            t = p[:, D_ATTN + j * D_KV:D_ATTN + (j + 1) * D_KV]
            t_sw = pltpu.roll(t, HEAD_DIM, axis=1)
            kv_ref[rows, (2 * j) * LANES:(2 * j + 1) * LANES] = jnp.where(lo, t, t_sw).astype(BF16)
            kv_ref[rows, (2 * j + 1) * LANES:(2 * j + 2) * LANES] = jnp.where(lo, t_sw, t).astype(BF16)


def _proj(x2, mod, g_pre, w_in, s):
    t, d = x2.shape
    nw = w_in.shape[1]
    dg = (nw - D_ATTN - 2 * D_KV) // 2
    nsb = s // TM
    return pl.pallas_call(
        _proj_kernel,
        grid=(t // TM,),
        in_specs=[
            pl.BlockSpec((TM, d), lambda i: (i, 0)),
            pl.BlockSpec((1, 6, d), lambda i: (i // nsb, 0, 0)),
            pl.BlockSpec((1, d), lambda i: (0, 0)),
            pl.BlockSpec((d, nw), lambda i: (0, 0)),
        ],
        out_specs=[
            pl.BlockSpec((TM, D_ATTN), lambda i: (i, 0)),
            pl.BlockSpec((TM, D_ATTN), lambda i: (i, 0)),
            pl.BlockSpec((TM, dg), lambda i: (i, 0)),
            pl.BlockSpec((TM, dg), lambda i: (i, 0)),
        ],
        out_shape=[
            jax.ShapeDtypeStruct((t, D_ATTN), BF16),
            jax.ShapeDtypeStruct((t, D_ATTN), BF16),
            jax.ShapeDtypeStruct((t, dg), F32),
            jax.ShapeDtypeStruct((t, dg), F32),
        ],
        compiler_params=pltpu.CompilerParams(
            dimension_semantics=("arbitrary",), vmem_limit_bytes=VMEM_LIMIT),
        name="proj",
    )(x2, mod, g_pre, w_in)


def _attn_bias_table():
    qi = np.arange(BLOCK)[:, None]
    kj = np.arange(3 * BLOCK)[None, :] - BLOCK
    dist = np.abs(qi - kj).astype(np.float32)
    band = dist <= WINDOW
    slopes = np.exp2(-8.0 * np.arange(1, N_HEADS + 1, dtype=np.float32) / N_HEADS).astype(np.float32)
    key_ok = [kj >= 0, np.ones_like(kj, bool), kj < BLOCK]
    tab = np.empty((3, N_HEADS, BLOCK, 3 * BLOCK), np.float32)
    for v in range(3):
        ok = band & key_ok[v]
        for h in range(N_HEADS):
            tab[v, h] = np.where(ok, -slopes[h] * dist, np.float32(MASK_VALUE))
    return tab.reshape(3, N_KV_HEADS, GROUP * BLOCK, 3 * BLOCK)


def _mix_kernel(nblk_seq, sink_ref, x_ref, q_ref, kvp_ref, kvc_ref, kvn_ref, u_ref, vg_ref,
                mod_ref, bias_ref, ws_ref, bs_ref, lng_ref, lnb_ref, ga_ref, gg_ref,
                wout_ref, gpost_ref, gpre2_ref, wr_ref,
                x1_ref, h2_ref, aff_ref, kcat, attn_s, gm_s):
    i = pl.program_id(0)
    nb_tile = TM // BLOCK
    kcat[0:BLOCK, :] = kvp_ref[...]
    kcat[BLOCK:BLOCK + TM, :] = kvc_ref[...]
    kcat[BLOCK + TM:, :] = kvn_ref[...]
    lane = lax.broadcasted_iota(jnp.int32, (1, LANES), 1)
    lo = lane < HEAD_DIM
    zero = jnp.zeros((), BF16)
    head_of_row = lax.broadcasted_iota(jnp.int32, (GROUP * BLOCK, 1), 0) // BLOCK
    sinks = []
    for g in range(N_KV_HEADS):
        col = jnp.zeros((GROUP * BLOCK, 1), F32)
        for j in range(GROUP):
            col = jnp.where(head_of_row == j, sink_ref[g * GROUP + j], col)
        sinks.append(col)

    def block_body(a, carry):
        r0 = pl.multiple_of(a * BLOCK, BLOCK)
        pos = (i * nb_tile + a) % nblk_seq
        var = jnp.where(pos == 0, 0, jnp.where(pos == nblk_seq - 1, 2, 1))
        kv = kcat[pl.ds(r0, 3 * BLOCK), :]
        for g in range(N_KV_HEADS):
            k = kv[:, g * LANES:(g + 1) * LANES]
            v = kv[:, (N_KV_HEADS + g) * LANES:(N_KV_HEADS + g + 1) * LANES]
            pieces = []
            for pi in range(GROUP // 2):
                qp = q_ref[pl.ds(r0, BLOCK), (g * GROUP // 2 + pi) * LANES:(g * GROUP // 2 + pi + 1) * LANES]
                pieces += [jnp.where(lo, qp, zero), jnp.where(lo, zero, qp)]
            s = _dot_nt(jnp.concatenate(pieces, axis=0), k) + bias_ref[var, g]
            m = jnp.maximum(jnp.max(s, axis=-1, keepdims=True), sinks[g])
            p = jnp.exp(s - m)
            den = jnp.sum(p, axis=-1, keepdims=True) + jnp.exp(sinks[g] - m)
            o = _dot(p.astype(BF16), v) / den
            for pi in range(GROUP // 2):
                even = o[(2 * pi) * BLOCK:(2 * pi + 1) * BLOCK, :]
                odd = o[(2 * pi + 1) * BLOCK:(2 * pi + 2) * BLOCK, :]
                col0 = (g * GROUP // 2 + pi) * LANES
                attn_s[pl.ds(r0, BLOCK), col0:col0 + LANES] = jnp.where(lo, even, odd)
        vg = vg_ref[pl.ds(r0, BLOCK), :]
        mu = jnp.mean(vg, axis=-1, keepdims=True)
        xc = vg - mu
        vn = (xc * lax.rsqrt(jnp.mean(xc * xc, axis=-1, keepdims=True) + EPS) * lng_ref[...]
              + lnb_ref[...]).astype(BF16)
        for pi in range(N_GMLP_GROUPS // 2):
            vnp = vn[:, pi * LANES:(pi + 1) * LANES]
            both = _dot(ws_ref[pi], vnp)
            z = jnp.where(lo, both[:BLOCK], both[BLOCK:])
            sl = slice(pi * LANES, (pi + 1) * LANES)
            gm_s[pl.ds(r0, BLOCK), sl] = u_ref[pl.ds(r0, BLOCK), sl] * (z + bs_ref[:, sl])
        return carry

    lax.fori_loop(0, nb_tile, block_body, 0, unroll=2)

    na = _rms(attn_s[...], ga_ref[...]).astype(BF16)
    ng = _rms(gm_s[...], gg_ref[...]).astype(BF16)
    mo = _dot(na, wout_ref[0:D_ATTN, :]) + _dot(ng, wout_ref[D_ATTN:, :])
    x1 = x_ref[...] + _rms(mo, gpost_ref[...] * mod_ref[0, 2:3, :])
    x1_ref[...] = x1
    h2 = (_rms(x1, gpre2_ref[...] * (1.0 + mod_ref[0, 4:5, :])) + mod_ref[0, 3:4, :]).astype(BF16)
    h2_ref[...] = h2
    lg = _dot_nt(wr_ref[...], h2)
    lg = lg - jnp.max(lg, axis=0, keepdims=True)
    ex = jnp.exp(lg)
    aff_ref[0] = ex / jnp.sum(ex, axis=0, keepdims=True)


def _mix(x2, q, kv, u, vg, mod, sink, bias, ws, bs_full, ln_g, ln_b, g_attn, g_gmlp,
         w_out, g_post, g_pre2, w_r_t, b, s):
    t, d = x2.shape
    dg = u.shape[1]
    e = w_r_t.shape[0]
    nsb = s // TM
    nb_tile = TM // BLOCK
    nblk = t // BLOCK
    row = lambda i: (i, 0)
    const2 = lambda i: (0, 0)
    in_specs = [
        pl.BlockSpec(memory_space=pltpu.SMEM),
        pl.BlockSpec((TM, d), row),
        pl.BlockSpec((TM, D_ATTN), row),
        pl.BlockSpec((BLOCK, 4 * LANES), lambda i: (jnp.maximum(i * nb_tile - 1, 0), 0)),
        pl.BlockSpec((TM, 4 * LANES), row),
        pl.BlockSpec((BLOCK, 4 * LANES), lambda i: (jnp.minimum((i + 1) * nb_tile, nblk - 1), 0)),
        pl.BlockSpec((TM, dg), row),
        pl.BlockSpec((TM, dg), row),
        pl.BlockSpec((1, 6, d), lambda i: (i // nsb, 0, 0)),
        pl.BlockSpec(bias.shape, lambda i: (0, 0, 0, 0)),
        pl.BlockSpec(ws.shape, lambda i: (0, 0, 0)),
        pl.BlockSpec((BLOCK, dg), const2),
        pl.BlockSpec((1, dg), const2),
        pl.BlockSpec((1, dg), const2),
        pl.BlockSpec((1, D_ATTN), const2),
        pl.BlockSpec((1, dg), const2),
        pl.BlockSpec(w_out.shape, const2),
        pl.BlockSpec((1, d), const2),
        pl.BlockSpec((1, d), const2),
        pl.BlockSpec(w_r_t.shape, const2),
    ]
    out_specs = [
        pl.BlockSpec((TM, d), row),
        pl.BlockSpec((TM, d), row),
        pl.BlockSpec((1, e, TM), lambda i: (i // nsb, 0, i % nsb)),
    ]
    out_shape = [
        jax.ShapeDtypeStruct((t, d), F32),
        jax.ShapeDtypeStruct((t, d), BF16),
        jax.ShapeDtypeStruct((b, e, s), F32),
    ]
    return pl.pallas_call(
        functools.partial(_mix_kernel, s // BLOCK),
        grid=(t // TM,),
        in_specs=in_specs,
        out_specs=out_specs,
        out_shape=out_shape,
        scratch_shapes=[
            pltpu.VMEM((TM + 2 * BLOCK, 4 * LANES), BF16),
            pltpu.VMEM((TM, D_ATTN), F32),
            pltpu.VMEM((TM, dg), F32),
        ],
        compiler_params=pltpu.CompilerParams(
            dimension_semantics=("arbitrary",), vmem_limit_bytes=VMEM_LIMIT),
        name="mix",
    )(sink, x2, q, kv, kv, kv, u, vg, mod, bias, ws, bs_full, ln_g, ln_b, g_attn, g_gmlp,
      w_out, g_post, g_pre2, w_r_t)


def _route_kernel(cap, aff_ref, slot_ref, affb_ref, cnt_ref):
    aff = aff_ref[...]
    e, s = aff.shape
    capf = jnp.float32(cap)

    def count_ge(t):
        return jnp.sum(jnp.where(aff >= t, 1.0, 0.0), axis=1, keepdims=True)

    def bit_body(it, cand):
        trial = cand | jnp.left_shift(jnp.int32(1), 30 - it)
        t = lax.bitcast_convert_type(trial, F32)
        return jnp.where(count_ge(t) >= capf, trial, cand)

    cand = lax.fori_loop(0, 31, bit_body, jnp.zeros((e, 1), jnp.int32))
    thr = lax.bitcast_convert_type(cand, F32)
    need = capf - jnp.sum(jnp.where(aff > thr, 1.0, 0.0), axis=1, keepdims=True)

    r = lax.broadcasted_iota(jnp.int32, (LANES, LANES), 0)
    c = lax.broadcasted_iota(jnp.int32, (LANES, LANES), 1)
    tri = jnp.where(r < c, 1.0, 0.0).astype(BF16)
    lane = lax.broadcasted_iota(jnp.int32, (e, LANES), 1)
    carry_eq = jnp.zeros((e, 1), F32)
    carry_sel = jnp.zeros((e, 1), F32)
    cnt = jnp.zeros((e, LANES), F32)
    per_blk = MXU_DIM // LANES
    for j in range(s // LANES):
        a = aff[:, j * LANES:(j + 1) * LANES]
        gt = a > thr
        eq = jnp.where(a == thr, 1.0, 0.0)
        tie_rank = _dot(eq.astype(BF16), tri) + carry_eq
        sel = jnp.where(gt | ((eq > 0.0) & (tie_rank < need)), 1.0, 0.0)
        slot = _dot(sel.astype(BF16), tri) + carry_sel
        blk = (slice(None), j // per_blk, slice((j % per_blk) * LANES, (j % per_blk + 1) * LANES))
        slot_ref[blk] = jnp.where(sel > 0.0, slot, -1.0).astype(jnp.int32)
        affb_ref[blk] = a
        if j % per_blk == 0:
            cnt = jnp.where(lane == j // per_blk, carry_sel, cnt)
        carry_eq = carry_eq + jnp.sum(eq, axis=1, keepdims=True)
        carry_sel = carry_sel + jnp.sum(sel, axis=1, keepdims=True)
    cnt = jnp.where(lane >= s // MXU_DIM, carry_sel, cnt)
    cnt_ref[...] = cnt.astype(jnp.int32)


def _route(aff2, cap):
    r, s = aff2.shape
    nblk = s // MXU_DIM
    return pl.pallas_call(
        functools.partial(_route_kernel, cap),
        grid=(1,),
        in_specs=[pl.BlockSpec((r, s), lambda i: (0, 0))],
        out_specs=[
            pl.BlockSpec((r, nblk, MXU_DIM), lambda i: (0, 0, 0)),
            pl.BlockSpec((r, nblk, MXU_DIM), lambda i: (0, 0, 0)),
            pl.BlockSpec((r, LANES), lambda i: (0, 0)),
        ],
        out_shape=[
            jax.ShapeDtypeStruct((r, nblk, MXU_DIM), jnp.int32),
            jax.ShapeDtypeStruct((r, nblk, MXU_DIM), F32),
            jax.ShapeDtypeStruct((r, LANES), jnp.int32),
        ],
        compiler_params=pltpu.CompilerParams(dimension_semantics=("arbitrary",)),
        name="route",
    )(aff2)


def _segment(cnt_ref, row, pb, cap):
    c0 = jnp.minimum(cnt_ref[row, pb], cap)
    c1 = jnp.minimum(cnt_ref[row, pb + 1], cap)
    a0 = jnp.minimum(jnp.bitwise_and(c0, -SLOT_ALIGN), cap - WIN)
    return c0, c1, pl.multiple_of(a0, SLOT_ALIGN)


def _extra_windows(c1, a0):
    return (jnp.maximum(c1 - (a0 + WIN), 0) + WIN - 1) // WIN


def _gather_kernel(n_exp, cnt_ref, h2_ref, slot_ref, aff_ref, x_ref, gs_ref):
    b, eg = pl.program_id(0), pl.program_id(1)
    ng, cap = x_ref.shape[1], x_ref.shape[2]
    nblk = slot_ref.shape[2]
    sub = lax.broadcasted_iota(jnp.int32, (WIN, MXU_DIM), 0)
    x_ref[...] = jnp.zeros_like(x_ref)
    gs_ref[...] = jnp.zeros_like(gs_ref)

    def merge(k, a0, rows, gate):
        win = (0, k, pl.ds(a0, WIN), slice(None))
        x_ref[win] += rows.astype(BF16)
        gs_ref[win] += gate

    def tokens(pb):
        return pl.ds(pl.multiple_of(pb * MXU_DIM, MXU_DIM), MXU_DIM)

    def first_windows(pb):
        segs, gates, onehots = [], [], []
        overflow = jnp.int32(0)
        for k in range(ng):
            c0, c1, a0 = _segment(cnt_ref, b * n_exp + eg * ng + k, pb, cap)
            match = slot_ref[0, k, pl.ds(pb, 1), :] == (sub + a0)
            onehots.append(jnp.where(match, 1.0, 0.0).astype(BF16))
            gates.append(jnp.sum(jnp.where(match, aff_ref[0, k, pl.ds(pb, 1), :], 0.0),
                                 axis=1, keepdims=True))
            segs.append((c0, c1, a0))
            overflow = overflow + jnp.maximum(c1 - (a0 + WIN), 0)
        rows = _dot(jnp.concatenate(onehots, axis=0), h2_ref[0, tokens(pb), :])
        for k in range(ng):
            merge(k, segs[k][2], rows[k * WIN:(k + 1) * WIN, :], gates[k])
        return overflow

    def further_windows(pb):
        def per_expert(k, carry2):
            _, c1, a0 = _segment(cnt_ref, b * n_exp + eg * ng + k, pb, cap)

            def per_window(w, carry3):
                first = a0 + (w + 1) * WIN
                aw = pl.multiple_of(jnp.minimum(first, cap - WIN), SLOT_ALIGN)
                srow = slot_ref[0, k, pl.ds(pb, 1), :]
                match = (srow == (sub + aw)) & (srow >= first)
                onehot = jnp.where(match, 1.0, 0.0).astype(BF16)
                gate = jnp.sum(jnp.where(match, aff_ref[0, k, pl.ds(pb, 1), :], 0.0),
                               axis=1, keepdims=True)
                merge(k, aw, _dot(onehot, h2_ref[0, tokens(pb), :]), gate)
                return carry3

            return lax.fori_loop(0, _extra_windows(c1, a0), per_window, carry2)

        lax.fori_loop(0, ng, per_expert, 0)

    def trip_body(i, carry):
        overflows = [first_windows(GATHER_BLOCKS * i + j) for j in range(GATHER_BLOCKS)]
        for j in range(GATHER_BLOCKS):
            pl.when(overflows[j] > 0)(functools.partial(further_windows, GATHER_BLOCKS * i + j))
        return carry

    lax.fori_loop(0, nblk // GATHER_BLOCKS, trip_body, 0)


def _gather(cnt2, h2, slots4, aff4, cap):
    b, s, d = h2.shape
    e = slots4.shape[1]
    nblk = s // MXU_DIM
    grid_spec = pltpu.PrefetchScalarGridSpec(
        num_scalar_prefetch=1,
        grid=(b, e // EG),
        in_specs=[
            pl.BlockSpec((1, s, d), lambda bi, gi, c: (bi, 0, 0)),
            pl.BlockSpec((1, EG, nblk, MXU_DIM), lambda bi, gi, c: (bi, gi, 0, 0)),
            pl.BlockSpec((1, EG, nblk, MXU_DIM), lambda bi, gi, c: (bi, gi, 0, 0)),
        ],
        out_specs=[
            pl.BlockSpec((1, EG, cap, d), lambda bi, gi, c: (bi, gi, 0, 0)),
            pl.BlockSpec((1, EG, cap, LANES), lambda bi, gi, c: (bi, gi, 0, 0)),
        ],
    )
    return pl.pallas_call(
        functools.partial(_gather_kernel, e),
        grid_spec=grid_spec,
        out_shape=[
            jax.ShapeDtypeStruct((b, e, cap, d), BF16),
            jax.ShapeDtypeStruct((b, e, cap, LANES), F32),
        ],
        compiler_params=pltpu.CompilerParams(
            dimension_semantics=("arbitrary", "arbitrary"), vmem_limit_bytes=VMEM_LIMIT),
        name="gather",
    )(cnt2, h2, slots4, aff4)


def _ffn_kernel(nchunk, x_ref, gs_ref, *refs):
    w_refs, out_ref, acc_s = refs[:-2], refs[-2], refs[-1]
    groups = [w_refs[3 * k:3 * k + 3] for k in range(FFN_CHUNKS)]
    e, f = pl.program_id(0), pl.program_id(1)
    nb, _, cap, d = x_ref.shape
    last = pl.num_programs(1) - 1

    @pl.when((e == 0) & (f == 0))
    def _init():
        acc_s[...] = jnp.zeros_like(acc_s)

    def accumulate(chunks):
        wg = jnp.concatenate([w[0][0].astype(BF16) for w in chunks], axis=1)
        wu = jnp.concatenate([w[1][0].astype(BF16) for w in chunks], axis=1)
        wd = jnp.concatenate([w[2][0].astype(BF16) for w in chunks], axis=0)
        for bi in range(nb):
            x = x_ref[bi, 0]
            g = _dot(x, wg)
            u = _dot(x, wu)
            hid = (g * jax.nn.sigmoid(g) * u).astype(BF16)
            y = _dot(hid, wd)
            rows = slice(bi * cap, (bi + 1) * cap)
            acc_s[rows, :] = jnp.where(f == 0, y, acc_s[rows, :] + y)

    tail = nchunk % FFN_CHUNKS
    if tail == 0:
        accumulate(groups)
    else:
        pl.when(f < last)(lambda: accumulate(groups))
        pl.when(f == last)(lambda: accumulate(groups[:tail]))

    @pl.when(f == last)
    def _emit():
        gate = gs_ref[:, 0, :, 0:1].reshape(nb * cap, 1)
        out_ref[:, 0] = (acc_s[...] * gate).reshape(nb, cap, d).astype(BF16)


def _ffn(xg, gs, w_gate, w_up, w_down, e0):
    b, e, cap, d = xg.shape
    nchunk = w_gate.shape[2] // FC
    weight_specs = []
    for k in range(FFN_CHUNKS):
        chunk = lambda fi, k=k: jnp.minimum(FFN_CHUNKS * fi + k, nchunk - 1)
        weight_specs += [
            pl.BlockSpec((1, d, FC), lambda ei, fi, chunk=chunk: (e0 + ei, 0, chunk(fi))),
            pl.BlockSpec((1, d, FC), lambda ei, fi, chunk=chunk: (e0 + ei, 0, chunk(fi))),
            pl.BlockSpec((1, FC, d), lambda ei, fi, chunk=chunk: (e0 + ei, chunk(fi), 0)),
        ]
    ahead = pl.Buffered(2, use_lookahead=True)
    in_specs = [
        pl.BlockSpec((b, 1, cap, d), lambda ei, fi: (0, ei, 0, 0), pipeline_mode=ahead),
        pl.BlockSpec((b, 1, cap, LANES), lambda ei, fi: (0, ei, 0, 0), pipeline_mode=ahead),
    ] + weight_specs
    out_spec = pl.BlockSpec((b, 1, cap, d), lambda ei, fi: (0, ei, 0, 0))
    steps = pltpu.emit_pipeline(
        functools.partial(_ffn_kernel, nchunk),
        grid=(e, pl.cdiv(nchunk, FFN_CHUNKS)), in_specs=in_specs, out_specs=out_spec)

    def stream(*refs):
        steps(*refs[:-1], scratches=(refs[-1],))

    return pl.pallas_call(
        stream,
        in_specs=[pl.BlockSpec(memory_space=pl.ANY)] * len(in_specs),
        out_specs=pl.BlockSpec(memory_space=pl.ANY),
        out_shape=jax.ShapeDtypeStruct((b, e, cap, d), BF16),
        scratch_shapes=[pltpu.VMEM((b * cap, d), F32)],
        compiler_params=pltpu.CompilerParams(vmem_limit_bytes=VMEM_LIMIT),
        name="ffn",
    )(xg, gs, *([w_gate, w_up, w_down] * FFN_CHUNKS))


def _combine_kernel(n_exp, cnt_ref, eo_ref, slot_ref, x1_ref, mod_ref, g_ref, o_ref,
                    stage_s, oh_s, y_s):
    b, q = pl.program_id(0), pl.program_id(1)
    cap = eo_ref.shape[2]
    nblk_q = o_ref.shape[0] // MXU_DIM
    sub = lax.broadcasted_iota(jnp.int32, (WIN, MXU_DIM), 0)
    gain = g_ref[...] * mod_ref[0, 5:6, :]

    for j in range(nblk_q):
        pb = q * nblk_q + j
        overflow = jnp.int32(0)
        for e in range(n_exp):
            _, c1, a0 = _segment(cnt_ref, b * n_exp + e, pb, cap)
            stage_s[e * WIN:(e + 1) * WIN, :] = eo_ref[0, e, pl.ds(a0, WIN), :]
            match = slot_ref[0, e, pl.ds(pb, 1), :] == (sub + a0)
            oh_s[e * WIN:(e + 1) * WIN, :] = jnp.where(match, 1.0, 0.0).astype(BF16)
            overflow = overflow + jnp.maximum(c1 - (a0 + WIN), 0)
        y_s[...] = _dot_tn(oh_s[...], stage_s[...])

        @pl.when(overflow > 0)
        def _long_segments(j=j, pb=pb):
            def per_expert(e, carry):
                _, c1, a0 = _segment(cnt_ref, b * n_exp + e, pb, cap)

                def per_window(k, carry2):
                    first = a0 + (k + 1) * WIN
                    aw = pl.multiple_of(jnp.minimum(first, cap - WIN), SLOT_ALIGN)
                    srow = slot_ref[0, e, pl.ds(pb, 1), :]
                    match = (srow == (sub + aw)) & (srow >= first)
                    onehot = jnp.where(match, 1.0, 0.0).astype(BF16)
                    y_s[...] += _dot_tn(onehot, eo_ref[0, e, pl.ds(aw, WIN), :])
                    return carry2

                return lax.fori_loop(0, _extra_windows(c1, a0), per_window, carry)

            lax.fori_loop(0, n_exp, per_expert, 0)

        sl = slice(j * MXU_DIM, (j + 1) * MXU_DIM)
        o_ref[sl, :] = x1_ref[sl, :] + _rms(y_s[...], gain)


def _combine(cnt2, eo, slots4, x1, mod, g_post):
    b, e, cap, d = eo.shape
    t = x1.shape[0]
    s = t // b
    nq = s // QP
    nblk_q = QP // MXU_DIM
    grid_spec = pltpu.PrefetchScalarGridSpec(
        num_scalar_prefetch=1,
        grid=(b, nq),
        in_specs=[
            pl.BlockSpec((1, e, cap, d), lambda bi, qi, c: (bi, 0, 0, 0), pipeline_mode=pl.Buffered(1)),
            pl.BlockSpec((1, e, s // MXU_DIM, MXU_DIM), lambda bi, qi, c: (bi, 0, 0, 0)),
            pl.BlockSpec((QP, d), lambda bi, qi, c: (bi * nq + qi, 0)),
            pl.BlockSpec((1, 6, d), lambda bi, qi, c: (bi, 0, 0)),
            pl.BlockSpec((1, d), lambda bi, qi, c: (0, 0)),
        ],
        out_specs=pl.BlockSpec((QP, d), lambda bi, qi, c: (bi * nq + qi, 0)),
        scratch_shapes=[
            pltpu.VMEM((e * WIN, d), BF16),
            pltpu.VMEM((e * WIN, MXU_DIM), BF16),
            pltpu.VMEM((MXU_DIM, d), F32),
        ],
    )
    return pl.pallas_call(
        functools.partial(_combine_kernel, e),
        grid_spec=grid_spec,
        out_shape=jax.ShapeDtypeStruct((t, d), F32),
        compiler_params=pltpu.CompilerParams(
            dimension_semantics=("arbitrary", "arbitrary"),
            vmem_limit_bytes=VMEM_LIMIT),
        name="combine",
    )(cnt2, eo, slots4, x1, mod, g_post)


def _layer(layer, x, c, w_ada_all, b_ada, norm_pre_mix, norm_post_mix, w_in, sink, sgu_ln_g,
           sgu_ln_b, w_s, b_s, norm_out_attn, norm_out_gmlp, w_out, norm_pre_ffn, norm_post_ffn,
           w_router, w_gate_all, w_up_all, w_down_all):
    b, s, d = x.shape
    t = b * s
    dg = sgu_ln_g.shape[0]
    e = w_router.shape[1]
    cap = CAPACITY_FACTOR * s // e
    assert s % QP == 0 and s % TM == 0 and w_gate_all.shape[2] % FC == 0 and e % EG == 0
    assert cap % SLOT_ALIGN == 0 and cap >= WIN and s // MXU_DIM < LANES and d == D_ATTN + dg
    assert (s // MXU_DIM) % GATHER_BLOCKS == 0

    mod = _adaln(c, w_ada_all, b_ada, layer).reshape(b, 6, d)

    x2 = x.reshape(t, d)
    q, kv, u, vg = _proj(x2, mod, norm_pre_mix.reshape(1, d), w_in.astype(BF16), s)

    bias = jnp.asarray(_attn_bias_table())
    bs_full = jnp.repeat(b_s.T, dg // N_GMLP_GROUPS, axis=1)
    x1, h2, aff_t = _mix(
        x2, q, kv, u, vg, mod, sink, bias,
        w_s.astype(BF16).reshape(N_GMLP_GROUPS // 2, 2 * BLOCK, BLOCK), bs_full,
        sgu_ln_g.reshape(1, dg), sgu_ln_b.reshape(1, dg), norm_out_attn.reshape(1, D_ATTN),
        norm_out_gmlp.reshape(1, dg), w_out.astype(BF16), norm_post_mix.reshape(1, d),
        norm_pre_ffn.reshape(1, d), w_router.T.astype(BF16), b, s)

    slots3, aff3, cnt2 = _route(aff_t.reshape(b * e, s), cap)
    nblk = s // MXU_DIM
    slots4 = slots3.reshape(b, e, nblk, MXU_DIM)
    aff4 = aff3.reshape(b, e, nblk, MXU_DIM)
    xg, gs = _gather(cnt2, h2.reshape(b, s, d), slots4, aff4, cap)
    eo = _ffn(xg, gs, w_gate_all, w_up_all, w_down_all, layer * e)
    out = _combine(cnt2, eo, slots4, x1, mod, norm_post_ffn.reshape(1, d))
    return out.reshape(b, s, d)


def kernel(x, c, w_ada, b_ada, norm_pre_mix, norm_post_mix, w_in, sink, sgu_ln_g, sgu_ln_b, w_s, b_s, norm_out_attn, norm_out_gmlp, w_out, norm_pre_ffn, norm_post_ffn, w_router, w_gate, w_up, w_down):
    depth, d = w_ada.shape[0], w_ada.shape[1]
    w_ada_all = w_ada.reshape(depth * d, w_ada.shape[2])
    stack = lambda w: w.reshape((depth * w.shape[1],) + w.shape[2:])
    w_gate_all, w_up_all, w_down_all = stack(w_gate), stack(w_up), stack(w_down)
    for l in range(depth):
        x = _layer(l, x, c, w_ada_all, b_ada[l], norm_pre_mix[l], norm_post_mix[l], w_in[l],
                   sink[l], sgu_ln_g[l], sgu_ln_b[l], w_s[l], b_s[l], norm_out_attn[l],
                   norm_out_gmlp[l], w_out[l], norm_pre_ffn[l], norm_post_ffn[l], w_router[l],
                   w_gate_all, w_up_all, w_down_all)
    return x
```

```python
import functools

import numpy as np
import jax
import jax.numpy as jnp
from jax import lax
from jax.experimental import pallas as pl
from jax.experimental.pallas import tpu as pltpu

N_HEADS = 8
N_KV_HEADS = 2
HEAD_DIM = 64
D_ATTN = N_HEADS * HEAD_DIM
D_KV = N_KV_HEADS * HEAD_DIM
GROUP = N_HEADS // N_KV_HEADS
WINDOW = 128
BLOCK = 128
N_GMLP_GROUPS = 8
CAPACITY_FACTOR = 2
EPS = 1e-6
MASK_VALUE = -1e30

LANES = 128
MXU_DIM = 256

TM = 1024
FC = 256
FFN_CHUNKS = 4
PROJ_ROWS = 256
EG = 4
GATHER_BLOCKS = 8
QP = 1024
WIN = 64
SLOT_ALIGN = 16
VMEM_LIMIT = 56 * 1024 * 1024
FFN_VMEM_LIMIT = 60 * 1024 * 1024

BF16 = jnp.bfloat16
F32 = jnp.float32


def _rms(x, g):
    return x * lax.rsqrt(jnp.mean(x * x, axis=-1, keepdims=True) + EPS) * g


def _dot(a, b):
    return jnp.dot(a, b, preferred_element_type=F32)


def _dot_nt(a, b):
    return lax.dot_general(a, b, (((1,), (1,)), ((), ())), preferred_element_type=F32)


def _dot_tn(a, b):
    return lax.dot_general(a, b, (((0,), (0,)), ((), ())), preferred_element_type=F32)


def _adaln_kernel(c_ref, w_ref, b_ref, o_ref):
    c = c_ref[...]
    a = (c * jax.nn.sigmoid(c)).astype(BF16)
    o_ref[...] = _dot(a, w_ref[...].astype(BF16)) + b_ref[...]


def _adaln(c, w_ada_all, b_ada, layer):
    b, d = c.shape
    n = w_ada_all.shape[1]
    return pl.pallas_call(
        _adaln_kernel,
        grid=(n // d,),
        in_specs=[
            pl.BlockSpec((b, d), lambda j: (0, 0)),
            pl.BlockSpec((d, d), lambda j: (layer, j)),
            pl.BlockSpec((1, d), lambda j: (0, j)),
        ],
        out_specs=pl.BlockSpec((b, d), lambda j: (0, j)),
        out_shape=jax.ShapeDtypeStruct((b, n), F32),
        compiler_params=pltpu.CompilerParams(dimension_semantics=("arbitrary",)),
        name="adaln",
    )(c, w_ada_all, b_ada.reshape(1, n))


def _gelu_tanh(x):
    c = np.float32(np.sqrt(2.0 / np.pi))
    return x * (0.5 * (1.0 + jnp.tanh(c * (x + 0.044715 * (x * x * x)))))


def _proj_kernel(x_ref, mod_ref, g_ref, w_ref, q_ref, kv_ref, u_ref, vg_ref):
    shift = mod_ref[0, 0:1, :]
    gain = g_ref[...] * (1.0 + mod_ref[0, 1:2, :])
    n_att = D_ATTN + 2 * D_KV
    dg = u_ref.shape[1]
    lo = lax.broadcasted_iota(jnp.int32, (1, LANES), 1) < HEAD_DIM
    for r in range(0, TM, PROJ_ROWS):
        rows = slice(r, r + PROJ_ROWS)
        h = (_rms(x_ref[rows, :], gain) + shift).astype(BF16)
        z = _gelu_tanh(_dot(h, w_ref[:, n_att:]))
        u_ref[rows, :] = z[:, :dg]
        vg_ref[rows, :] = z[:, dg:]
        p = _dot(h, w_ref[:, :n_att])
        q_ref[rows, :] = (p[:, :D_ATTN] * (HEAD_DIM ** -0.5)).astype(BF16)
        for j in range(2):
            t = p[:, D_ATTN + j * D_KV:D_ATTN + (j + 1) * D_KV]
            t_sw = pltpu.roll(t, HEAD_DIM, axis=1)
            kv_ref[rows, (2 * j) * LANES:(2 * j + 1) * LANES] = jnp.where(lo, t, t_sw).astype(BF16)
            kv_ref[rows, (2 * j + 1) * LANES:(2 * j + 2) * LANES] = jnp.where(lo, t_sw, t).astype(BF16)


def _proj(x2, mod, g_pre, w_in, s):
    t, d = x2.shape
    nw = w_in.shape[1]
    dg = (nw - D_ATTN - 2 * D_KV) // 2
    nsb = s // TM
    return pl.pallas_call(
        _proj_kernel,
        grid=(t // TM,),
        in_specs=[
            pl.BlockSpec((TM, d), lambda i: (i, 0)),
            pl.BlockSpec((1, 6, d), lambda i: (i // nsb, 0, 0)),
            pl.BlockSpec((1, d), lambda i: (0, 0)),
            pl.BlockSpec((d, nw), lambda i: (0, 0)),
        ],
        out_specs=[
            pl.BlockSpec((TM, D_ATTN), lambda i: (i, 0)),
            pl.BlockSpec((TM, D_ATTN), lambda i: (i, 0)),
            pl.BlockSpec((TM, dg), lambda i: (i, 0)),
            pl.BlockSpec((TM, dg), lambda i: (i, 0)),
        ],
        out_shape=[
            jax.ShapeDtypeStruct((t, D_ATTN), BF16),
            jax.ShapeDtypeStruct((t, D_ATTN), BF16),
            jax.ShapeDtypeStruct((t, dg), F32),
            jax.ShapeDtypeStruct((t, dg), F32),
        ],
        compiler_params=pltpu.CompilerParams(
            dimension_semantics=("arbitrary",), vmem_limit_bytes=VMEM_LIMIT),
        name="proj",
    )(x2, mod, g_pre, w_in)


def _attn_bias_table():
    qi = np.arange(BLOCK)[:, None]
    kj = np.arange(3 * BLOCK)[None, :] - BLOCK
    dist = np.abs(qi - kj).astype(np.float32)
    band = dist <= WINDOW
    slopes = np.exp2(-8.0 * np.arange(1, N_HEADS + 1, dtype=np.float32) / N_HEADS).astype(np.float32)
    key_ok = [kj >= 0, np.ones_like(kj, bool), kj < BLOCK]
    tab = np.empty((3, N_HEADS, BLOCK, 3 * BLOCK), np.float32)
    for v in range(3):
        ok = band & key_ok[v]
        for h in range(N_HEADS):
            tab[v, h] = np.where(ok, -slopes[h] * dist, np.float32(MASK_VALUE))
    return tab.reshape(3, N_KV_HEADS, GROUP * BLOCK, 3 * BLOCK)


def _mix_kernel(nblk_seq, sink_ref, x_ref, q_ref, kvp_ref, kvc_ref, kvn_ref, u_ref, vg_ref,
                mod_ref, bias_ref, ws_ref, bs_ref, lng_ref, lnb_ref, ga_ref, gg_ref,
                wout_ref, gpost_ref, gpre2_ref, wr_ref,
                x1_ref, h2_ref, aff_ref, kcat, attn_s, gm_s):
    i = pl.program_id(0)
    nb_tile = TM // BLOCK
    kcat[0:BLOCK, :] = kvp_ref[...]
    kcat[BLOCK:BLOCK + TM, :] = kvc_ref[...]
    kcat[BLOCK + TM:, :] = kvn_ref[...]
    lane = lax.broadcasted_iota(jnp.int32, (1, LANES), 1)
    lo = lane < HEAD_DIM
    zero = jnp.zeros((), BF16)
    head_of_row = lax.broadcasted_iota(jnp.int32, (GROUP * BLOCK, 1), 0) // BLOCK
    sinks = []
    for g in range(N_KV_HEADS):
        col = jnp.zeros((GROUP * BLOCK, 1), F32)
        for j in range(GROUP):
            col = jnp.where(head_of_row == j, sink_ref[g * GROUP + j], col)
        sinks.append(col)

    def block_body(a, carry):
        r0 = pl.multiple_of(a * BLOCK, BLOCK)
        pos = (i * nb_tile + a) % nblk_seq
        var = jnp.where(pos == 0, 0, jnp.where(pos == nblk_seq - 1, 2, 1))
        kv = kcat[pl.ds(r0, 3 * BLOCK), :]
        for g in range(N_KV_HEADS):
            k = kv[:, g * LANES:(g + 1) * LANES]
            v = kv[:, (N_KV_HEADS + g) * LANES:(N_KV_HEADS + g + 1) * LANES]
            pieces = []
            for pi in range(GROUP // 2):
                qp = q_ref[pl.ds(r0, BLOCK), (g * GROUP // 2 + pi) * LANES:(g * GROUP // 2 + pi + 1) * LANES]
                pieces += [jnp.where(lo, qp, zero), jnp.where(lo, zero, qp)]
            s = _dot_nt(jnp.concatenate(pieces, axis=0), k) + bias_ref[var, g]
            m = jnp.maximum(jnp.max(s, axis=-1, keepdims=True), sinks[g])
            p = jnp.exp(s - m)
            den = jnp.sum(p, axis=-1, keepdims=True) + jnp.exp(sinks[g] - m)
            o = _dot(p.astype(BF16), v) / den
            for pi in range(GROUP // 2):
                even = o[(2 * pi) * BLOCK:(2 * pi + 1) * BLOCK, :]
                odd = o[(2 * pi + 1) * BLOCK:(2 * pi + 2) * BLOCK, :]
                col0 = (g * GROUP // 2 + pi) * LANES
                attn_s[pl.ds(r0, BLOCK), col0:col0 + LANES] = jnp.where(lo, even, odd)
        vg = vg_ref[pl.ds(r0, BLOCK), :]
        mu = jnp.mean(vg, axis=-1, keepdims=True)
        xc = vg - mu
        vn = (xc * lax.rsqrt(jnp.mean(xc * xc, axis=-1, keepdims=True) + EPS) * lng_ref[...]
              + lnb_ref[...]).astype(BF16)
        for pi in range(N_GMLP_GROUPS // 2):
            vnp = vn[:, pi * LANES:(pi + 1) * LANES]
            both = _dot(ws_ref[pi], vnp)
            z = jnp.where(lo, both[:BLOCK], both[BLOCK:])
            sl = slice(pi * LANES, (pi + 1) * LANES)
            gm_s[pl.ds(r0, BLOCK), sl] = u_ref[pl.ds(r0, BLOCK), sl] * (z + bs_ref[:, sl])
        return carry

    lax.fori_loop(0, nb_tile, block_body, 0, unroll=2)

    na = _rms(attn_s[...], ga_ref[...]).astype(BF16)
    ng = _rms(gm_s[...], gg_ref[...]).astype(BF16)
    mo = _dot(na, wout_ref[0:D_ATTN, :]) + _dot(ng, wout_ref[D_ATTN:, :])
    x1 = x_ref[...] + _rms(mo, gpost_ref[...] * mod_ref[0, 2:3, :])
    x1_ref[...] = x1
    h2 = (_rms(x1, gpre2_ref[...] * (1.0 + mod_ref[0, 4:5, :])) + mod_ref[0, 3:4, :]).astype(BF16)
    h2_ref[...] = h2
    lg = _dot_nt(wr_ref[...], h2)
    lg = lg - jnp.max(lg, axis=0, keepdims=True)
    ex = jnp.exp(lg)
    aff_ref[0] = ex / jnp.sum(ex, axis=0, keepdims=True)


def _mix(x2, q, kv, u, vg, mod, sink, bias, ws, bs_full, ln_g, ln_b, g_attn, g_gmlp,
         w_out, g_post, g_pre2, w_r_t, b, s):
    t, d = x2.shape
    dg = u.shape[1]
    e = w_r_t.shape[0]
    nsb = s // TM
    nb_tile = TM // BLOCK
    nblk = t // BLOCK
    row = lambda i: (i, 0)
    const2 = lambda i: (0, 0)
    in_specs = [
        pl.BlockSpec(memory_space=pltpu.SMEM),
        pl.BlockSpec((TM, d), row),
        pl.BlockSpec((TM, D_ATTN), row),
        pl.BlockSpec((BLOCK, 4 * LANES), lambda i: (jnp.maximum(i * nb_tile - 1, 0), 0)),
        pl.BlockSpec((TM, 4 * LANES), row),
        pl.BlockSpec((BLOCK, 4 * LANES), lambda i: (jnp.minimum((i + 1) * nb_tile, nblk - 1), 0)),
        pl.BlockSpec((TM, dg), row),
        pl.BlockSpec((TM, dg), row),
        pl.BlockSpec((1, 6, d), lambda i: (i // nsb, 0, 0)),
        pl.BlockSpec(bias.shape, lambda i: (0, 0, 0, 0)),
        pl.BlockSpec(ws.shape, lambda i: (0, 0, 0)),
        pl.BlockSpec((BLOCK, dg), const2),
        pl.BlockSpec((1, dg), const2),
        pl.BlockSpec((1, dg), const2),
        pl.BlockSpec((1, D_ATTN), const2),
        pl.BlockSpec((1, dg), const2),
        pl.BlockSpec(w_out.shape, const2),
        pl.BlockSpec((1, d), const2),
        pl.BlockSpec((1, d), const2),
        pl.BlockSpec(w_r_t.shape, const2),
    ]
    out_specs = [
        pl.BlockSpec((TM, d), row),
        pl.BlockSpec((TM, d), row),
        pl.BlockSpec((1, e, TM), lambda i: (i // nsb, 0, i % nsb)),
    ]
    out_shape = [
        jax.ShapeDtypeStruct((t, d), F32),
        jax.ShapeDtypeStruct((t, d), BF16),
        jax.ShapeDtypeStruct((b, e, s), F32),
    ]
    return pl.pallas_call(
        functools.partial(_mix_kernel, s // BLOCK),
        grid=(t // TM,),
        in_specs=in_specs,
        out_specs=out_specs,
        out_shape=out_shape,
        scratch_shapes=[
            pltpu.VMEM((TM + 2 * BLOCK, 4 * LANES), BF16),
            pltpu.VMEM((TM, D_ATTN), F32),
            pltpu.VMEM((TM, dg), F32),
        ],
        compiler_params=pltpu.CompilerParams(
            dimension_semantics=("arbitrary",), vmem_limit_bytes=VMEM_LIMIT),
        name="mix",
    )(sink, x2, q, kv, kv, kv, u, vg, mod, bias, ws, bs_full, ln_g, ln_b, g_attn, g_gmlp,
      w_out, g_post, g_pre2, w_r_t)


def _route_kernel(cap, aff_ref, slot_ref, affb_ref, cnt_ref):
    aff = aff_ref[...]
    e, s = aff.shape
    capf = jnp.float32(cap)

    def count_ge(t):
        return jnp.sum(jnp.where(aff >= t, 1.0, 0.0), axis=1, keepdims=True)

    def bit_body(it, cand):
        trial = cand | jnp.left_shift(jnp.int32(1), 30 - it)
        t = lax.bitcast_convert_type(trial, F32)
        return jnp.where(count_ge(t) >= capf, trial, cand)

    cand = lax.fori_loop(0, 31, bit_body, jnp.zeros((e, 1), jnp.int32))
    thr = lax.bitcast_convert_type(cand, F32)
    need = capf - jnp.sum(jnp.where(aff > thr, 1.0, 0.0), axis=1, keepdims=True)

    r = lax.broadcasted_iota(jnp.int32, (LANES, LANES), 0)
    c = lax.broadcasted_iota(jnp.int32, (LANES, LANES), 1)
    tri = jnp.where(r < c, 1.0, 0.0).astype(BF16)
    lane = lax.broadcasted_iota(jnp.int32, (e, LANES), 1)
    carry_eq = jnp.zeros((e, 1), F32)
    carry_sel = jnp.zeros((e, 1), F32)
    cnt = jnp.zeros((e, LANES), F32)
    per_blk = MXU_DIM // LANES
    for j in range(s // LANES):
        a = aff[:, j * LANES:(j + 1) * LANES]
        gt = a > thr
        eq = jnp.where(a == thr, 1.0, 0.0)
        tie_rank = _dot(eq.astype(BF16), tri) + carry_eq
        sel = jnp.where(gt | ((eq > 0.0) & (tie_rank < need)), 1.0, 0.0)
        slot = _dot(sel.astype(BF16), tri) + carry_sel
        blk = (slice(None), j // per_blk, slice((j % per_blk) * LANES, (j % per_blk + 1) * LANES))
        slot_ref[blk] = jnp.where(sel > 0.0, slot, -1.0).astype(jnp.int32)
        affb_ref[blk] = a
        if j % per_blk == 0:
            cnt = jnp.where(lane == j // per_blk, carry_sel, cnt)
        carry_eq = carry_eq + jnp.sum(eq, axis=1, keepdims=True)
        carry_sel = carry_sel + jnp.sum(sel, axis=1, keepdims=True)
    cnt = jnp.where(lane >= s // MXU_DIM, carry_sel, cnt)
    cnt_ref[...] = cnt.astype(jnp.int32)


def _route(aff2, cap):
    r, s = aff2.shape
    nblk = s // MXU_DIM
    return pl.pallas_call(
        functools.partial(_route_kernel, cap),
        grid=(1,),
        in_specs=[pl.BlockSpec((r, s), lambda i: (0, 0))],
        out_specs=[
            pl.BlockSpec((r, nblk, MXU_DIM), lambda i: (0, 0, 0)),
            pl.BlockSpec((r, nblk, MXU_DIM), lambda i: (0, 0, 0)),
            pl.BlockSpec((r, LANES), lambda i: (0, 0)),
        ],
        out_shape=[
            jax.ShapeDtypeStruct((r, nblk, MXU_DIM), jnp.int32),
            jax.ShapeDtypeStruct((r, nblk, MXU_DIM), F32),
            jax.ShapeDtypeStruct((r, LANES), jnp.int32),
        ],
        compiler_params=pltpu.CompilerParams(dimension_semantics=("arbitrary",)),
        name="route",
    )(aff2)


def _segment(cnt_ref, row, pb, cap):
    c0 = jnp.minimum(cnt_ref[row, pb], cap)
    c1 = jnp.minimum(cnt_ref[row, pb + 1], cap)
    a0 = jnp.minimum(jnp.bitwise_and(c0, -SLOT_ALIGN), cap - WIN)
    return c0, c1, pl.multiple_of(a0, SLOT_ALIGN)


def _extra_windows(c1, a0):
    return (jnp.maximum(c1 - (a0 + WIN), 0) + WIN - 1) // WIN


def _gather_kernel(n_exp, cnt_ref, h2_ref, slot_ref, aff_ref, x_ref, gs_ref):
    b, eg = pl.program_id(0), pl.program_id(1)
    ng, cap = x_ref.shape[1], x_ref.shape[2]
    nblk = slot_ref.shape[2]
    sub = lax.broadcasted_iota(jnp.int32, (WIN, MXU_DIM), 0)
    x_ref[...] = jnp.zeros_like(x_ref)
    gs_ref[...] = jnp.zeros_like(gs_ref)

    def merge(k, a0, rows, gate):
        win = (0, k, pl.ds(a0, WIN), slice(None))
        x_ref[win] += rows.astype(BF16)
        gs_ref[win] += gate

    def tokens(pb):
        return pl.ds(pl.multiple_of(pb * MXU_DIM, MXU_DIM), MXU_DIM)

    def first_windows(pb):
        segs, gates, onehots = [], [], []
        overflow = jnp.int32(0)
        for k in range(ng):
            c0, c1, a0 = _segment(cnt_ref, b * n_exp + eg * ng + k, pb, cap)
            match = slot_ref[0, k, pl.ds(pb, 1), :] == (sub + a0)
            onehots.append(jnp.where(match, 1.0, 0.0).astype(BF16))
            gates.append(jnp.sum(jnp.where(match, aff_ref[0, k, pl.ds(pb, 1), :], 0.0),
                                 axis=1, keepdims=True))
            segs.append((c0, c1, a0))
            overflow = overflow + jnp.maximum(c1 - (a0 + WIN), 0)
        rows = _dot(jnp.concatenate(onehots, axis=0), h2_ref[0, tokens(pb), :])
        for k in range(ng):
            merge(k, segs[k][2], rows[k * WIN:(k + 1) * WIN, :], gates[k])
        return overflow

    def further_windows(pb):
        def per_expert(k, carry2):
            _, c1, a0 = _segment(cnt_ref, b * n_exp + eg * ng + k, pb, cap)

            def per_window(w, carry3):
                first = a0 + (w + 1) * WIN
                aw = pl.multiple_of(jnp.minimum(first, cap - WIN), SLOT_ALIGN)
                srow = slot_ref[0, k, pl.ds(pb, 1), :]
                match = (srow == (sub + aw)) & (srow >= first)
                onehot = jnp.where(match, 1.0, 0.0).astype(BF16)
                gate = jnp.sum(jnp.where(match, aff_ref[0, k, pl.ds(pb, 1), :], 0.0),
                               axis=1, keepdims=True)
                merge(k, aw, _dot(onehot, h2_ref[0, tokens(pb), :]), gate)
                return carry3

            return lax.fori_loop(0, _extra_windows(c1, a0), per_window, carry2)

        lax.fori_loop(0, ng, per_expert, 0)

    def trip_body(i, carry):
        overflows = [first_windows(GATHER_BLOCKS * i + j) for j in range(GATHER_BLOCKS)]
        for j in range(GATHER_BLOCKS):
            pl.when(overflows[j] > 0)(functools.partial(further_windows, GATHER_BLOCKS * i + j))
        return carry

    lax.fori_loop(0, nblk // GATHER_BLOCKS, trip_body, 0)


def _gather(cnt2, h2, slots4, aff4, cap):
    b, s, d = h2.shape
    e = slots4.shape[1]
    nblk = s // MXU_DIM
    grid_spec = pltpu.PrefetchScalarGridSpec(
        num_scalar_prefetch=1,
        grid=(b, e // EG),
        in_specs=[
            pl.BlockSpec((1, s, d), lambda bi, gi, c: (bi, 0, 0)),
            pl.BlockSpec((1, EG, nblk, MXU_DIM), lambda bi, gi, c: (bi, gi, 0, 0)),
            pl.BlockSpec((1, EG, nblk, MXU_DIM), lambda bi, gi, c: (bi, gi, 0, 0)),
        ],
        out_specs=[
            pl.BlockSpec((1, EG, cap, d), lambda bi, gi, c: (bi, gi, 0, 0)),
            pl.BlockSpec((1, EG, cap, LANES), lambda bi, gi, c: (bi, gi, 0, 0)),
        ],
    )
    return pl.pallas_call(
        functools.partial(_gather_kernel, e),
        grid_spec=grid_spec,
        out_shape=[
            jax.ShapeDtypeStruct((b, e, cap, d), BF16),
            jax.ShapeDtypeStruct((b, e, cap, LANES), F32),
        ],
        compiler_params=pltpu.CompilerParams(
            dimension_semantics=("arbitrary", "arbitrary"), vmem_limit_bytes=VMEM_LIMIT),
        name="gather",
    )(cnt2, h2, slots4, aff4)


def _ffn_kernel(nchunk, x_ref, gs_ref, *refs):
    w_refs, out_ref, acc_s = refs[:-2], refs[-2], refs[-1]
    groups = [w_refs[3 * k:3 * k + 3] for k in range(FFN_CHUNKS)]
    e, f = pl.program_id(0), pl.program_id(1)
    nb, _, cap, d = x_ref.shape
    last = pl.num_programs(1) - 1

    @pl.when((e == 0) & (f == 0))
    def _init():
        acc_s[...] = jnp.zeros_like(acc_s)

    def accumulate(chunks):
        wg = jnp.concatenate([w[0][0].astype(BF16) for w in chunks], axis=1)
        wu = jnp.concatenate([w[1][0].astype(BF16) for w in chunks], axis=1)
        wd = jnp.concatenate([w[2][0].astype(BF16) for w in chunks], axis=0)
        for bi in range(nb):
            x = x_ref[bi, 0]
            g = _dot(x, wg)
            u = _dot(x, wu)
            hid = (g * jax.nn.sigmoid(g) * u).astype(BF16)
            y = _dot(hid, wd)
            rows = slice(bi * cap, (bi + 1) * cap)
            acc_s[rows, :] = jnp.where(f == 0, y, acc_s[rows, :] + y)

    tail = nchunk % FFN_CHUNKS
    if tail == 0:
        accumulate(groups)
    else:
        pl.when(f < last)(lambda: accumulate(groups))
        pl.when(f == last)(lambda: accumulate(groups[:tail]))

    @pl.when(f == last)
    def _emit():
        gate = gs_ref[:, 0, :, 0:1].reshape(nb * cap, 1)
        out_ref[:, 0] = (acc_s[...] * gate).reshape(nb, cap, d).astype(BF16)


def _ffn(xg, gs, w_gate, w_up, w_down, e0):
    b, e, cap, d = xg.shape
    nchunk = w_gate.shape[2] // FC
    weight_specs = []
    for k in range(FFN_CHUNKS):
        chunk = lambda fi, k=k: jnp.minimum(FFN_CHUNKS * fi + k, nchunk - 1)
        weight_specs += [
            pl.BlockSpec((1, d, FC), lambda ei, fi, chunk=chunk: (e0 + ei, 0, chunk(fi))),
            pl.BlockSpec((1, d, FC), lambda ei, fi, chunk=chunk: (e0 + ei, 0, chunk(fi))),
            pl.BlockSpec((1, FC, d), lambda ei, fi, chunk=chunk: (e0 + ei, chunk(fi), 0)),
        ]
    ahead = pl.Buffered(2, use_lookahead=True)
    in_specs = [
        pl.BlockSpec((b, 1, cap, d), lambda ei, fi: (0, ei, 0, 0), pipeline_mode=ahead),
        pl.BlockSpec((b, 1, cap, LANES), lambda ei, fi: (0, ei, 0, 0), pipeline_mode=ahead),
    ] + weight_specs
    out_spec = pl.BlockSpec((b, 1, cap, d), lambda ei, fi: (0, ei, 0, 0))
    steps = pltpu.emit_pipeline(
        functools.partial(_ffn_kernel, nchunk),
        grid=(e, pl.cdiv(nchunk, FFN_CHUNKS)), in_specs=in_specs, out_specs=out_spec)

    def stream(*refs):
        steps(*refs[:-1], scratches=(refs[-1],))

    return pl.pallas_call(
        stream,
        in_specs=[pl.BlockSpec(memory_space=pl.ANY)] * len(in_specs),
        out_specs=pl.BlockSpec(memory_space=pl.ANY),
        out_shape=jax.ShapeDtypeStruct((b, e, cap, d), BF16),
        scratch_shapes=[pltpu.VMEM((b * cap, d), F32)],
        compiler_params=pltpu.CompilerParams(vmem_limit_bytes=FFN_VMEM_LIMIT),
        name="ffn",
    )(xg, gs, *([w_gate, w_up, w_down] * FFN_CHUNKS))


def _combine_kernel(n_exp, cnt_ref, eo_ref, slot_ref, x1_ref, mod_ref, g_ref, o_ref,
                    stage_s, oh_s, y_s):
    b, q = pl.program_id(0), pl.program_id(1)
    cap = eo_ref.shape[2]
    nblk_q = o_ref.shape[0] // MXU_DIM
    sub = lax.broadcasted_iota(jnp.int32, (WIN, MXU_DIM), 0)
    gain = g_ref[...] * mod_ref[0, 5:6, :]

    for j in range(nblk_q):
        pb = q * nblk_q + j
        overflow = jnp.int32(0)
        for e in range(n_exp):
            _, c1, a0 = _segment(cnt_ref, b * n_exp + e, pb, cap)
            stage_s[e * WIN:(e + 1) * WIN, :] = eo_ref[0, e, pl.ds(a0, WIN), :]
            match = slot_ref[0, e, pl.ds(pb, 1), :] == (sub + a0)
            oh_s[e * WIN:(e + 1) * WIN, :] = jnp.where(match, 1.0, 0.0).astype(BF16)
            overflow = overflow + jnp.maximum(c1 - (a0 + WIN), 0)
        y_s[...] = _dot_tn(oh_s[...], stage_s[...])

        @pl.when(overflow > 0)
        def _long_segments(j=j, pb=pb):
            def per_expert(e, carry):
                _, c1, a0 = _segment(cnt_ref, b * n_exp + e, pb, cap)

                def per_window(k, carry2):
                    first = a0 + (k + 1) * WIN
                    aw = pl.multiple_of(jnp.minimum(first, cap - WIN), SLOT_ALIGN)
                    srow = slot_ref[0, e, pl.ds(pb, 1), :]
                    match = (srow == (sub + aw)) & (srow >= first)
                    onehot = jnp.where(match, 1.0, 0.0).astype(BF16)
                    y_s[...] += _dot_tn(onehot, eo_ref[0, e, pl.ds(aw, WIN), :])
                    return carry2

                return lax.fori_loop(0, _extra_windows(c1, a0), per_window, carry)

            lax.fori_loop(0, n_exp, per_expert, 0)

        sl = slice(j * MXU_DIM, (j + 1) * MXU_DIM)
        o_ref[sl, :] = x1_ref[sl, :] + _rms(y_s[...], gain)


def _combine(cnt2, eo, slots4, x1, mod, g_post):
    b, e, cap, d = eo.shape
    t = x1.shape[0]
    s = t // b
    nq = s // QP
    nblk_q = QP // MXU_DIM
    grid_spec = pltpu.PrefetchScalarGridSpec(
        num_scalar_prefetch=1,
        grid=(b, nq),
        in_specs=[
            pl.BlockSpec((1, e, cap, d), lambda bi, qi, c: (bi, 0, 0, 0), pipeline_mode=pl.Buffered(1)),
            pl.BlockSpec((1, e, s // MXU_DIM, MXU_DIM), lambda bi, qi, c: (bi, 0, 0, 0)),
            pl.BlockSpec((QP, d), lambda bi, qi, c: (bi * nq + qi, 0)),
            pl.BlockSpec((1, 6, d), lambda bi, qi, c: (bi, 0, 0)),
            pl.BlockSpec((1, d), lambda bi, qi, c: (0, 0)),
        ],
        out_specs=pl.BlockSpec((QP, d), lambda bi, qi, c: (bi * nq + qi, 0)),
        scratch_shapes=[
            pltpu.VMEM((e * WIN, d), BF16),
            pltpu.VMEM((e * WIN, MXU_DIM), BF16),
            pltpu.VMEM((MXU_DIM, d), F32),
        ],
    )
    return pl.pallas_call(
        functools.partial(_combine_kernel, e),
        grid_spec=grid_spec,
        out_shape=jax.ShapeDtypeStruct((t, d), F32),
        compiler_params=pltpu.CompilerParams(
            dimension_semantics=("arbitrary", "arbitrary"),
            vmem_limit_bytes=VMEM_LIMIT),
        name="combine",
    )(cnt2, eo, slots4, x1, mod, g_post)


def _layer(layer, x, c, w_ada_all, b_ada, norm_pre_mix, norm_post_mix, w_in, sink, sgu_ln_g,
           sgu_ln_b, w_s, b_s, norm_out_attn, norm_out_gmlp, w_out, norm_pre_ffn, norm_post_ffn,
           w_router, w_gate_all, w_up_all, w_down_all):
    b, s, d = x.shape
    t = b * s
    dg = sgu_ln_g.shape[0]
    e = w_router.shape[1]
    cap = CAPACITY_FACTOR * s // e
    assert s % QP == 0 and s % TM == 0 and w_gate_all.shape[2] % FC == 0 and e % EG == 0
    assert cap % SLOT_ALIGN == 0 and cap >= WIN and s // MXU_DIM < LANES and d == D_ATTN + dg
    assert (s // MXU_DIM) % GATHER_BLOCKS == 0

    mod = _adaln(c, w_ada_all, b_ada, layer).reshape(b, 6, d)

    x2 = x.reshape(t, d)
    q, kv, u, vg = _proj(x2, mod, norm_pre_mix.reshape(1, d), w_in.astype(BF16), s)

    bias = jnp.asarray(_attn_bias_table())
    bs_full = jnp.repeat(b_s.T, dg // N_GMLP_GROUPS, axis=1)
    x1, h2, aff_t = _mix(
        x2, q, kv, u, vg, mod, sink, bias,
        w_s.astype(BF16).reshape(N_GMLP_GROUPS // 2, 2 * BLOCK, BLOCK), bs_full,
        sgu_ln_g.reshape(1, dg), sgu_ln_b.reshape(1, dg), norm_out_attn.reshape(1, D_ATTN),
        norm_out_gmlp.reshape(1, dg), w_out.astype(BF16), norm_post_mix.reshape(1, d),
        norm_pre_ffn.reshape(1, d), w_router.T.astype(BF16), b, s)

    slots3, aff3, cnt2 = _route(aff_t.reshape(b * e, s), cap)
    nblk = s // MXU_DIM
    slots4 = slots3.reshape(b, e, nblk, MXU_DIM)
    aff4 = aff3.reshape(b, e, nblk, MXU_DIM)
    xg, gs = _gather(cnt2, h2.reshape(b, s, d), slots4, aff4, cap)
    eo = _ffn(xg, gs, w_gate_all, w_up_all, w_down_all, layer * e)
    out = _combine(cnt2, eo, slots4, x1, mod, norm_post_ffn.reshape(1, d))
    return out.reshape(b, s, d)


def kernel(x, c, w_ada, b_ada, norm_pre_mix, norm_post_mix, w_in, sink, sgu_ln_g, sgu_ln_b, w_s, b_s, norm_out_attn, norm_out_gmlp, w_out, norm_pre_ffn, norm_post_ffn, w_router, w_gate, w_up, w_down):
    depth, d = w_ada.shape[0], w_ada.shape[1]
    w_ada_all = w_ada.reshape(depth * d, w_ada.shape[2])
    stack = lambda w: w.reshape((depth * w.shape[1],) + w.shape[2:])
    w_gate_all, w_up_all, w_down_all = stack(w_gate), stack(w_up), stack(w_down)
    for l in range(depth):
        x = _layer(l, x, c, w_ada_all, b_ada[l], norm_pre_mix[l], norm_post_mix[l], w_in[l],
                   sink[l], sgu_ln_g[l], sgu_ln_b[l], w_s[l], b_s[l], norm_out_attn[l],
                   norm_out_gmlp[l], w_out[l], norm_pre_ffn[l], norm_post_ffn[l], w_router[l],
                   w_gate_all, w_up_all, w_down_all)
    return x
```

```python
import functools

import numpy as np
import jax
import jax.numpy as jnp
from jax import lax
from jax.experimental import pallas as pl
from jax.experimental.pallas import tpu as pltpu

N_HEADS = 8
N_KV_HEADS = 2
HEAD_DIM = 64
D_ATTN = N_HEADS * HEAD_DIM
D_KV = N_KV_HEADS * HEAD_DIM
GROUP = N_HEADS // N_KV_HEADS
WINDOW = 128
BLOCK = 128
N_GMLP_GROUPS = 8
CAPACITY_FACTOR = 2
EPS = 1e-6
MASK_VALUE = -1e30

LANES = 128
MXU_DIM = 256

TM = 1024
FC = 256
FFN_CHUNKS = 4
PROJ_ROWS = 256
EG = 4
GATHER_BLOCKS = 8
QP = 1024
WIN = 64
SLOT_ALIGN = 16
VMEM_LIMIT = 56 * 1024 * 1024
FFN_VMEM_LIMIT = 59 * 1024 * 1024

BF16 = jnp.bfloat16
F32 = jnp.float32


def _rms(x, g):
    return x * lax.rsqrt(jnp.mean(x * x, axis=-1, keepdims=True) + EPS) * g


def _dot(a, b):
    return jnp.dot(a, b, preferred_element_type=F32)


def _dot_nt(a, b):
    return lax.dot_general(a, b, (((1,), (1,)), ((), ())), preferred_element_type=F32)


def _dot_tn(a, b):
    return lax.dot_general(a, b, (((0,), (0,)), ((), ())), preferred_element_type=F32)


def _adaln_kernel(c_ref, w_ref, b_ref, o_ref):
    c = c_ref[...]
    a = (c * jax.nn.sigmoid(c)).astype(BF16)
    o_ref[...] = _dot(a, w_ref[...].astype(BF16)) + b_ref[...]


def _adaln(c, w_ada_all, b_ada, layer):
    b, d = c.shape
    n = w_ada_all.shape[1]
    return pl.pallas_call(
        _adaln_kernel,
        grid=(n // d,),
        in_specs=[
            pl.BlockSpec((b, d), lambda j: (0, 0)),
            pl.BlockSpec((d, d), lambda j: (layer, j)),
            pl.BlockSpec((1, d), lambda j: (0, j)),
        ],
        out_specs=pl.BlockSpec((b, d), lambda j: (0, j)),
        out_shape=jax.ShapeDtypeStruct((b, n), F32),
        compiler_params=pltpu.CompilerParams(dimension_semantics=("arbitrary",)),
        name="adaln",
    )(c, w_ada_all, b_ada.reshape(1, n))


def _gelu_tanh(x):
    c = np.float32(np.sqrt(2.0 / np.pi))
    return x * (0.5 * (1.0 + jnp.tanh(c * (x + 0.044715 * (x * x * x)))))


def _proj_kernel(x_ref, mod_ref, g_ref, w_ref, q_ref, kv_ref, u_ref, vg_ref):
    shift = mod_ref[0, 0:1, :]
    gain = g_ref[...] * (1.0 + mod_ref[0, 1:2, :])
    n_att = D_ATTN + 2 * D_KV
    dg = u_ref.shape[1]
    lo = lax.broadcasted_iota(jnp.int32, (1, LANES), 1) < HEAD_DIM
    for r in range(0, TM, PROJ_ROWS):
        rows = slice(r, r + PROJ_ROWS)
        h = (_rms(x_ref[rows, :], gain) + shift).astype(BF16)
        z = _gelu_tanh(_dot(h, w_ref[:, n_att:]))
        u_ref[rows, :] = z[:, :dg]
        vg_ref[rows, :] = z[:, dg:]
        p = _dot(h, w_ref[:, :n_att])
        q_ref[rows, :] = (p[:, :D_ATTN] * (HEAD_DIM ** -0.5)).astype(BF16)
        for j in range(2):
            t = p[:, D_ATTN + j * D_KV:D_ATTN + (j + 1) * D_KV]
            t_sw = pltpu.roll(t, HEAD_DIM, axis=1)
            kv_ref[rows, (2 * j) * LANES:(2 * j + 1) * LANES] = jnp.where(lo, t, t_sw).astype(BF16)
            kv_ref[rows, (2 * j + 1) * LANES:(2 * j + 2) * LANES] = jnp.where(lo, t_sw, t).astype(BF16)


def _proj(x2, mod, g_pre, w_in, s):
    t, d = x2.shape
    nw = w_in.shape[1]
    dg = (nw - D_ATTN - 2 * D_KV) // 2
    nsb = s // TM
    return pl.pallas_call(
        _proj_kernel,
        grid=(t // TM,),
        in_specs=[
            pl.BlockSpec((TM, d), lambda i: (i, 0)),
            pl.BlockSpec((1, 6, d), lambda i: (i // nsb, 0, 0)),
            pl.BlockSpec((1, d), lambda i: (0, 0)),
            pl.BlockSpec((d, nw), lambda i: (0, 0)),
        ],
        out_specs=[
            pl.BlockSpec((TM, D_ATTN), lambda i: (i, 0)),
            pl.BlockSpec((TM, D_ATTN), lambda i: (i, 0)),
            pl.BlockSpec((TM, dg), lambda i: (i, 0)),
            pl.BlockSpec((TM, dg), lambda i: (i, 0)),
        ],
        out_shape=[
            jax.ShapeDtypeStruct((t, D_ATTN), BF16),
            jax.ShapeDtypeStruct((t, D_ATTN), BF16),
            jax.ShapeDtypeStruct((t, dg), F32),
            jax.ShapeDtypeStruct((t, dg), F32),
        ],
        compiler_params=pltpu.CompilerParams(
            dimension_semantics=("arbitrary",), vmem_limit_bytes=VMEM_LIMIT),
        name="proj",
    )(x2, mod, g_pre, w_in)


def _attn_bias_table():
    qi = np.arange(BLOCK)[:, None]
    kj = np.arange(3 * BLOCK)[None, :] - BLOCK
    dist = np.abs(qi - kj).astype(np.float32)
    band = dist <= WINDOW
    slopes = np.exp2(-8.0 * np.arange(1, N_HEADS + 1, dtype=np.float32) / N_HEADS).astype(np.float32)
    key_ok = [kj >= 0, np.ones_like(kj, bool), kj < BLOCK]
    tab = np.empty((3, N_HEADS, BLOCK, 3 * BLOCK), np.float32)
    for v in range(3):
        ok = band & key_ok[v]
        for h in range(N_HEADS):
            tab[v, h] = np.where(ok, -slopes[h] * dist, np.float32(MASK_VALUE))
    return tab.reshape(3, N_KV_HEADS, GROUP * BLOCK, 3 * BLOCK)


def _mix_kernel(nblk_seq, sink_ref, x_ref, q_ref, kvp_ref, kvc_ref, kvn_ref, u_ref, vg_ref,
                mod_ref, bias_ref, ws_ref, bs_ref, lng_ref, lnb_ref, ga_ref, gg_ref,
                wout_ref, gpost_ref, gpre2_ref, wr_ref,
                x1_ref, h2_ref, aff_ref, kcat, attn_s, gm_s):
    i = pl.program_id(0)
    nb_tile = TM // BLOCK
    kcat[0:BLOCK, :] = kvp_ref[...]
    kcat[BLOCK:BLOCK + TM, :] = kvc_ref[...]
    kcat[BLOCK + TM:, :] = kvn_ref[...]
    lane = lax.broadcasted_iota(jnp.int32, (1, LANES), 1)
    lo = lane < HEAD_DIM
    zero = jnp.zeros((), BF16)
    head_of_row = lax.broadcasted_iota(jnp.int32, (GROUP * BLOCK, 1), 0) // BLOCK
    sinks = []
    for g in range(N_KV_HEADS):
        col = jnp.zeros((GROUP * BLOCK, 1), F32)
        for j in range(GROUP):
            col = jnp.where(head_of_row == j, sink_ref[g * GROUP + j], col)
        sinks.append(col)

    def block_body(a, carry):
        r0 = pl.multiple_of(a * BLOCK, BLOCK)
        pos = (i * nb_tile + a) % nblk_seq
        var = jnp.where(pos == 0, 0, jnp.where(pos == nblk_seq - 1, 2, 1))
        kv = kcat[pl.ds(r0, 3 * BLOCK), :]
        for g in range(N_KV_HEADS):
            k = kv[:, g * LANES:(g + 1) * LANES]
            v = kv[:, (N_KV_HEADS + g) * LANES:(N_KV_HEADS + g + 1) * LANES]
            pieces = []
            for pi in range(GROUP // 2):
                qp = q_ref[pl.ds(r0, BLOCK), (g * GROUP // 2 + pi) * LANES:(g * GROUP // 2 + pi + 1) * LANES]
                pieces += [jnp.where(lo, qp, zero), jnp.where(lo, zero, qp)]
            s = _dot_nt(jnp.concatenate(pieces, axis=0), k) + bias_ref[var, g]
            m = jnp.maximum(jnp.max(s, axis=-1, keepdims=True), sinks[g])
            p = jnp.exp(s - m)
            den = jnp.sum(p, axis=-1, keepdims=True) + jnp.exp(sinks[g] - m)
            o = _dot(p.astype(BF16), v) / den
            for pi in range(GROUP // 2):
                even = o[(2 * pi) * BLOCK:(2 * pi + 1) * BLOCK, :]
                odd = o[(2 * pi + 1) * BLOCK:(2 * pi + 2) * BLOCK, :]
                col0 = (g * GROUP // 2 + pi) * LANES
                attn_s[pl.ds(r0, BLOCK), col0:col0 + LANES] = jnp.where(lo, even, odd)
        vg = vg_ref[pl.ds(r0, BLOCK), :]
        mu = jnp.mean(vg, axis=-1, keepdims=True)
        xc = vg - mu
        vn = (xc * lax.rsqrt(jnp.mean(xc * xc, axis=-1, keepdims=True) + EPS) * lng_ref[...]
              + lnb_ref[...]).astype(BF16)
        for pi in range(N_GMLP_GROUPS // 2):
            vnp = vn[:, pi * LANES:(pi + 1) * LANES]
            both = _dot(ws_ref[pi], vnp)
            z = jnp.where(lo, both[:BLOCK], both[BLOCK:])
            sl = slice(pi * LANES, (pi + 1) * LANES)
            gm_s[pl.ds(r0, BLOCK), sl] = u_ref[pl.ds(r0, BLOCK), sl] * (z + bs_ref[:, sl])
        return carry

    lax.fori_loop(0, nb_tile, block_body, 0, unroll=2)

    na = _rms(attn_s[...], ga_ref[...]).astype(BF16)
    ng = _rms(gm_s[...], gg_ref[...]).astype(BF16)
    mo = _dot(na, wout_ref[0:D_ATTN, :]) + _dot(ng, wout_ref[D_ATTN:, :])
    x1 = x_ref[...] + _rms(mo, gpost_ref[...] * mod_ref[0, 2:3, :])
    x1_ref[...] = x1
    h2 = (_rms(x1, gpre2_ref[...] * (1.0 + mod_ref[0, 4:5, :])) + mod_ref[0, 3:4, :]).astype(BF16)
    h2_ref[...] = h2
    lg = _dot_nt(wr_ref[...], h2)
    lg = lg - jnp.max(lg, axis=0, keepdims=True)
    ex = jnp.exp(lg)
    aff_ref[0] = ex / jnp.sum(ex, axis=0, keepdims=True)


def _mix(x2, q, kv, u, vg, mod, sink, bias, ws, bs_full, ln_g, ln_b, g_attn, g_gmlp,
         w_out, g_post, g_pre2, w_r_t, b, s):
    t, d = x2.shape
    dg = u.shape[1]
    e = w_r_t.shape[0]
    nsb = s // TM
    nb_tile = TM // BLOCK
    nblk = t // BLOCK
    row = lambda i: (i, 0)
    const2 = lambda i: (0, 0)
    in_specs = [
        pl.BlockSpec(memory_space=pltpu.SMEM),
        pl.BlockSpec((TM, d), row),
        pl.BlockSpec((TM, D_ATTN), row),
        pl.BlockSpec((BLOCK, 4 * LANES), lambda i: (jnp.maximum(i * nb_tile - 1, 0), 0)),
        pl.BlockSpec((TM, 4 * LANES), row),
        pl.BlockSpec((BLOCK, 4 * LANES), lambda i: (jnp.minimum((i + 1) * nb_tile, nblk - 1), 0)),
        pl.BlockSpec((TM, dg), row),
        pl.BlockSpec((TM, dg), row),
        pl.BlockSpec((1, 6, d), lambda i: (i // nsb, 0, 0)),
        pl.BlockSpec(bias.shape, lambda i: (0, 0, 0, 0)),
        pl.BlockSpec(ws.shape, lambda i: (0, 0, 0)),
        pl.BlockSpec((BLOCK, dg), const2),
        pl.BlockSpec((1, dg), const2),
        pl.BlockSpec((1, dg), const2),
        pl.BlockSpec((1, D_ATTN), const2),
        pl.BlockSpec((1, dg), const2),
        pl.BlockSpec(w_out.shape, const2),
        pl.BlockSpec((1, d), const2),
        pl.BlockSpec((1, d), const2),
        pl.BlockSpec(w_r_t.shape, const2),
    ]
    out_specs = [
        pl.BlockSpec((TM, d), row),
        pl.BlockSpec((TM, d), row),
        pl.BlockSpec((1, e, TM), lambda i: (i // nsb, 0, i % nsb)),
    ]
    out_shape = [
        jax.ShapeDtypeStruct((t, d), F32),
        jax.ShapeDtypeStruct((t, d), BF16),
        jax.ShapeDtypeStruct((b, e, s), F32),
    ]
    return pl.pallas_call(
        functools.partial(_mix_kernel, s // BLOCK),
        grid=(t // TM,),
        in_specs=in_specs,
        out_specs=out_specs,
        out_shape=out_shape,
        scratch_shapes=[
            pltpu.VMEM((TM + 2 * BLOCK, 4 * LANES), BF16),
            pltpu.VMEM((TM, D_ATTN), F32),
            pltpu.VMEM((TM, dg), F32),
        ],
        compiler_params=pltpu.CompilerParams(
            dimension_semantics=("arbitrary",), vmem_limit_bytes=VMEM_LIMIT),
        name="mix",
    )(sink, x2, q, kv, kv, kv, u, vg, mod, bias, ws, bs_full, ln_g, ln_b, g_attn, g_gmlp,
      w_out, g_post, g_pre2, w_r_t)


def _route_kernel(cap, aff_ref, slot_ref, affb_ref, cnt_ref):
    aff = aff_ref[...]
    e, s = aff.shape
    capf = jnp.float32(cap)

    def count_ge(t):
        return jnp.sum(jnp.where(aff >= t, 1.0, 0.0), axis=1, keepdims=True)

    def bit_body(it, cand):
        trial = cand | jnp.left_shift(jnp.int32(1), 30 - it)
        t = lax.bitcast_convert_type(trial, F32)
        return jnp.where(count_ge(t) >= capf, trial, cand)

    cand = lax.fori_loop(0, 31, bit_body, jnp.zeros((e, 1), jnp.int32))
    thr = lax.bitcast_convert_type(cand, F32)
    need = capf - jnp.sum(jnp.where(aff > thr, 1.0, 0.0), axis=1, keepdims=True)

    r = lax.broadcasted_iota(jnp.int32, (LANES, LANES), 0)
    c = lax.broadcasted_iota(jnp.int32, (LANES, LANES), 1)
    tri = jnp.where(r < c, 1.0, 0.0).astype(BF16)
    lane = lax.broadcasted_iota(jnp.int32, (e, LANES), 1)
    carry_eq = jnp.zeros((e, 1), F32)
    carry_sel = jnp.zeros((e, 1), F32)
    cnt = jnp.zeros((e, LANES), F32)
    per_blk = MXU_DIM // LANES
    for j in range(s // LANES):
        a = aff[:, j * LANES:(j + 1) * LANES]
        gt = a > thr
        eq = jnp.where(a == thr, 1.0, 0.0)
        tie_rank = _dot(eq.astype(BF16), tri) + carry_eq
        sel = jnp.where(gt | ((eq > 0.0) & (tie_rank < need)), 1.0, 0.0)
        slot = _dot(sel.astype(BF16), tri) + carry_sel
        blk = (slice(None), j // per_blk, slice((j % per_blk) * LANES, (j % per_blk + 1) * LANES))
        slot_ref[blk] = jnp.where(sel > 0.0, slot, -1.0).astype(jnp.int32)
        affb_ref[blk] = a
        if j % per_blk == 0:
            cnt = jnp.where(lane == j // per_blk, carry_sel, cnt)
        carry_eq = carry_eq + jnp.sum(eq, axis=1, keepdims=True)
        carry_sel = carry_sel + jnp.sum(sel, axis=1, keepdims=True)
    cnt = jnp.where(lane >= s // MXU_DIM, carry_sel, cnt)
    cnt_ref[...] = cnt.astype(jnp.int32)


def _route(aff2, cap):
    r, s = aff2.shape
    nblk = s // MXU_DIM
    return pl.pallas_call(
        functools.partial(_route_kernel, cap),
        grid=(1,),
        in_specs=[pl.BlockSpec((r, s), lambda i: (0, 0))],
        out_specs=[
            pl.BlockSpec((r, nblk, MXU_DIM), lambda i: (0, 0, 0)),
            pl.BlockSpec((r, nblk, MXU_DIM), lambda i: (0, 0, 0)),
            pl.BlockSpec((r, LANES), lambda i: (0, 0)),
        ],
        out_shape=[
            jax.ShapeDtypeStruct((r, nblk, MXU_DIM), jnp.int32),
            jax.ShapeDtypeStruct((r, nblk, MXU_DIM), F32),
            jax.ShapeDtypeStruct((r, LANES), jnp.int32),
        ],
        compiler_params=pltpu.CompilerParams(dimension_semantics=("arbitrary",)),
        name="route",
    )(aff2)


def _segment(cnt_ref, row, pb, cap):
    c0 = jnp.minimum(cnt_ref[row, pb], cap)
    c1 = jnp.minimum(cnt_ref[row, pb + 1], cap)
    a0 = jnp.minimum(jnp.bitwise_and(c0, -SLOT_ALIGN), cap - WIN)
    return c0, c1, pl.multiple_of(a0, SLOT_ALIGN)


def _extra_windows(c1, a0):
    return (jnp.maximum(c1 - (a0 + WIN), 0) + WIN - 1) // WIN


def _gather_kernel(n_exp, cnt_ref, h2_ref, slot_ref, aff_ref, x_ref, gs_ref):
    b, eg = pl.program_id(0), pl.program_id(1)
    ng, cap = x_ref.shape[1], x_ref.shape[2]
    nblk = slot_ref.shape[2]
    sub = lax.broadcasted_iota(jnp.int32, (WIN, MXU_DIM), 0)
    x_ref[...] = jnp.zeros_like(x_ref)
    gs_ref[...] = jnp.zeros_like(gs_ref)

    def merge(k, a0, rows, gate):
        win = (0, k, pl.ds(a0, WIN), slice(None))
        x_ref[win] += rows.astype(BF16)
        gs_ref[win] += gate

    def tokens(pb):
        return pl.ds(pl.multiple_of(pb * MXU_DIM, MXU_DIM), MXU_DIM)

    def first_windows(pb):
        segs, gates, onehots = [], [], []
        overflow = jnp.int32(0)
        for k in range(ng):
            c0, c1, a0 = _segment(cnt_ref, b * n_exp + eg * ng + k, pb, cap)
            match = slot_ref[0, k, pl.ds(pb, 1), :] == (sub + a0)
            onehots.append(jnp.where(match, 1.0, 0.0).astype(BF16))
            gates.append(jnp.sum(jnp.where(match, aff_ref[0, k, pl.ds(pb, 1), :], 0.0),
                                 axis=1, keepdims=True))
            segs.append((c0, c1, a0))
            overflow = overflow + jnp.maximum(c1 - (a0 + WIN), 0)
        rows = _dot(jnp.concatenate(onehots, axis=0), h2_ref[0, tokens(pb), :])
        for k in range(ng):
            merge(k, segs[k][2], rows[k * WIN:(k + 1) * WIN, :], gates[k])
        return overflow

    def further_windows(pb):
        def per_expert(k, carry2):
            _, c1, a0 = _segment(cnt_ref, b * n_exp + eg * ng + k, pb, cap)

            def per_window(w, carry3):
                first = a0 + (w + 1) * WIN
                aw = pl.multiple_of(jnp.minimum(first, cap - WIN), SLOT_ALIGN)
                srow = slot_ref[0, k, pl.ds(pb, 1), :]
                match = (srow == (sub + aw)) & (srow >= first)
                onehot = jnp.where(match, 1.0, 0.0).astype(BF16)
                gate = jnp.sum(jnp.where(match, aff_ref[0, k, pl.ds(pb, 1), :], 0.0),
                               axis=1, keepdims=True)
                merge(k, aw, _dot(onehot, h2_ref[0, tokens(pb), :]), gate)
                return carry3

            return lax.fori_loop(0, _extra_windows(c1, a0), per_window, carry2)

        lax.fori_loop(0, ng, per_expert, 0)

    def trip_body(i, carry):
        overflows = [first_windows(GATHER_BLOCKS * i + j) for j in range(GATHER_BLOCKS)]
        for j in range(GATHER_BLOCKS):
            pl.when(overflows[j] > 0)(functools.partial(further_windows, GATHER_BLOCKS * i + j))
        return carry

    lax.fori_loop(0, nblk // GATHER_BLOCKS, trip_body, 0)


def _gather(cnt2, h2, slots4, aff4, cap):
    b, s, d = h2.shape
    e = slots4.shape[1]
    nblk = s // MXU_DIM
    grid_spec = pltpu.PrefetchScalarGridSpec(
        num_scalar_prefetch=1,
        grid=(b, e // EG),
        in_specs=[
            pl.BlockSpec((1, s, d), lambda bi, gi, c: (bi, 0, 0)),
            pl.BlockSpec((1, EG, nblk, MXU_DIM), lambda bi, gi, c: (bi, gi, 0, 0)),
            pl.BlockSpec((1, EG, nblk, MXU_DIM), lambda bi, gi, c: (bi, gi, 0, 0)),
        ],
        out_specs=[
            pl.BlockSpec((1, EG, cap, d), lambda bi, gi, c: (bi, gi, 0, 0)),
            pl.BlockSpec((1, EG, cap, LANES), lambda bi, gi, c: (bi, gi, 0, 0)),
        ],
    )
    return pl.pallas_call(
        functools.partial(_gather_kernel, e),
        grid_spec=grid_spec,
        out_shape=[
            jax.ShapeDtypeStruct((b, e, cap, d), BF16),
            jax.ShapeDtypeStruct((b, e, cap, LANES), F32),
        ],
        compiler_params=pltpu.CompilerParams(
            dimension_semantics=("arbitrary", "arbitrary"), vmem_limit_bytes=VMEM_LIMIT),
        name="gather",
    )(cnt2, h2, slots4, aff4)


def _ffn_kernel(nchunk, x_ref, gs_ref, *refs):
    w_refs, out_ref, acc_s = refs[:-2], refs[-2], refs[-1]
    groups = [w_refs[3 * k:3 * k + 3] for k in range(FFN_CHUNKS)]
    e, f = pl.program_id(0), pl.program_id(1)
    nb, _, cap, d = x_ref.shape
    last = pl.num_programs(1) - 1

    @pl.when((e == 0) & (f == 0))
    def _init():
        acc_s[...] = jnp.zeros_like(acc_s)

    def accumulate(chunks):
        wg = jnp.concatenate([w[0][0].astype(BF16) for w in chunks], axis=1)
        wu = jnp.concatenate([w[1][0].astype(BF16) for w in chunks], axis=1)
        wd = jnp.concatenate([w[2][0].astype(BF16) for w in chunks], axis=0)
        for bi in range(nb):
            x = x_ref[bi, 0]
            g = _dot(x, wg)
            u = _dot(x, wu)
            hid = (g * jax.nn.sigmoid(g) * u).astype(BF16)
            y = _dot(hid, wd)
            rows = slice(bi * cap, (bi + 1) * cap)
            acc_s[rows, :] = jnp.where(f == 0, y, acc_s[rows, :] + y)

    tail = nchunk % FFN_CHUNKS
    if tail == 0:
        accumulate(groups)
    else:
        pl.when(f < last)(lambda: accumulate(groups))
        pl.when(f == last)(lambda: accumulate(groups[:tail]))

    @pl.when(f == last)
    def _emit():
        gate = gs_ref[:, 0, :, 0:1].reshape(nb * cap, 1)
        out_ref[:, 0] = (acc_s[...] * gate).reshape(nb, cap, d).astype(BF16)


def _ffn(xg, gs, w_gate, w_up, w_down, e0):
    b, e, cap, d = xg.shape
    nchunk = w_gate.shape[2] // FC
    weight_specs = []
    for k in range(FFN_CHUNKS):
        chunk = lambda fi, k=k: jnp.minimum(FFN_CHUNKS * fi + k, nchunk - 1)
        weight_specs += [
            pl.BlockSpec((1, d, FC), lambda ei, fi, chunk=chunk: (e0 + ei, 0, chunk(fi))),
            pl.BlockSpec((1, d, FC), lambda ei, fi, chunk=chunk: (e0 + ei, 0, chunk(fi))),
            pl.BlockSpec((1, FC, d), lambda ei, fi, chunk=chunk: (e0 + ei, chunk(fi), 0)),
        ]
    ahead = pl.Buffered(2, use_lookahead=True)
    in_specs = [
        pl.BlockSpec((b, 1, cap, d), lambda ei, fi: (0, ei, 0, 0), pipeline_mode=ahead),
        pl.BlockSpec((b, 1, cap, LANES), lambda ei, fi: (0, ei, 0, 0), pipeline_mode=ahead),
    ] + weight_specs
    out_spec = pl.BlockSpec((b, 1, cap, d), lambda ei, fi: (0, ei, 0, 0))
    steps = pltpu.emit_pipeline(
        functools.partial(_ffn_kernel, nchunk),
        grid=(e, pl.cdiv(nchunk, FFN_CHUNKS)), in_specs=in_specs, out_specs=out_spec)

    def stream(*refs):
        steps(*refs[:-1], scratches=(refs[-1],))

    return pl.pallas_call(
        stream,
        in_specs=[pl.BlockSpec(memory_space=pl.ANY)] * len(in_specs),
        out_specs=pl.BlockSpec(memory_space=pl.ANY),
        out_shape=jax.ShapeDtypeStruct((b, e, cap, d), BF16),
        scratch_shapes=[pltpu.VMEM((b * cap, d), F32)],
        compiler_params=pltpu.CompilerParams(vmem_limit_bytes=FFN_VMEM_LIMIT),
        name="ffn",
    )(xg, gs, *([w_gate, w_up, w_down] * FFN_CHUNKS))


def _combine_kernel(n_exp, cnt_ref, eo_ref, slot_ref, x1_ref, mod_ref, g_ref, o_ref,
                    stage_s, oh_s, y_s):
    b, q = pl.program_id(0), pl.program_id(1)
    cap = eo_ref.shape[2]
    nblk_q = o_ref.shape[0] // MXU_DIM
    sub = lax.broadcasted_iota(jnp.int32, (WIN, MXU_DIM), 0)
    gain = g_ref[...] * mod_ref[0, 5:6, :]

    for j in range(nblk_q):
        pb = q * nblk_q + j
        overflow = jnp.int32(0)
        for e in range(n_exp):
            _, c1, a0 = _segment(cnt_ref, b * n_exp + e, pb, cap)
            stage_s[e * WIN:(e + 1) * WIN, :] = eo_ref[0, e, pl.ds(a0, WIN), :]
            match = slot_ref[0, e, pl.ds(pb, 1), :] == (sub + a0)
            oh_s[e * WIN:(e + 1) * WIN, :] = jnp.where(match, 1.0, 0.0).astype(BF16)
            overflow = overflow + jnp.maximum(c1 - (a0 + WIN), 0)
        y_s[...] = _dot_tn(oh_s[...], stage_s[...])

        @pl.when(overflow > 0)
        def _long_segments(j=j, pb=pb):
            def per_expert(e, carry):
                _, c1, a0 = _segment(cnt_ref, b * n_exp + e, pb, cap)

                def per_window(k, carry2):
                    first = a0 + (k + 1) * WIN
                    aw = pl.multiple_of(jnp.minimum(first, cap - WIN), SLOT_ALIGN)
                    srow = slot_ref[0, e, pl.ds(pb, 1), :]
                    match = (srow == (sub + aw)) & (srow >= first)
                    onehot = jnp.where(match, 1.0, 0.0).astype(BF16)
                    y_s[...] += _dot_tn(onehot, eo_ref[0, e, pl.ds(aw, WIN), :])
                    return carry2

                return lax.fori_loop(0, _extra_windows(c1, a0), per_window, carry)

            lax.fori_loop(0, n_exp, per_expert, 0)

        sl = slice(j * MXU_DIM, (j + 1) * MXU_DIM)
        o_ref[sl, :] = x1_ref[sl, :] + _rms(y_s[...], gain)


def _combine(cnt2, eo, slots4, x1, mod, g_post):
    b, e, cap, d = eo.shape
    t = x1.shape[0]
    s = t // b
    nq = s // QP
    nblk_q = QP // MXU_DIM
    grid_spec = pltpu.PrefetchScalarGridSpec(
        num_scalar_prefetch=1,
        grid=(b, nq),
        in_specs=[
            pl.BlockSpec((1, e, cap, d), lambda bi, qi, c: (bi, 0, 0, 0), pipeline_mode=pl.Buffered(1)),
            pl.BlockSpec((1, e, s // MXU_DIM, MXU_DIM), lambda bi, qi, c: (bi, 0, 0, 0)),
            pl.BlockSpec((QP, d), lambda bi, qi, c: (bi * nq + qi, 0)),
            pl.BlockSpec((1, 6, d), lambda bi, qi, c: (bi, 0, 0)),
            pl.BlockSpec((1, d), lambda bi, qi, c: (0, 0)),
        ],
        out_specs=pl.BlockSpec((QP, d), lambda bi, qi, c: (bi * nq + qi, 0)),
        scratch_shapes=[
            pltpu.VMEM((e * WIN, d), BF16),
            pltpu.VMEM((e * WIN, MXU_DIM), BF16),
            pltpu.VMEM((MXU_DIM, d), F32),
        ],
    )
    return pl.pallas_call(
        functools.partial(_combine_kernel, e),
        grid_spec=grid_spec,
        out_shape=jax.ShapeDtypeStruct((t, d), F32),
        compiler_params=pltpu.CompilerParams(
            dimension_semantics=("arbitrary", "arbitrary"),
            vmem_limit_bytes=VMEM_LIMIT),
        name="combine",
    )(cnt2, eo, slots4, x1, mod, g_post)


def _layer(layer, x, c, w_ada_all, b_ada, norm_pre_mix, norm_post_mix, w_in, sink, sgu_ln_g,
           sgu_ln_b, w_s, b_s, norm_out_attn, norm_out_gmlp, w_out, norm_pre_ffn, norm_post_ffn,
           w_router, w_gate_all, w_up_all, w_down_all):
    b, s, d = x.shape
    t = b * s
    dg = sgu_ln_g.shape[0]
    e = w_router.shape[1]
    cap = CAPACITY_FACTOR * s // e
    assert s % QP == 0 and s % TM == 0 and w_gate_all.shape[2] % FC == 0 and e % EG == 0
    assert cap % SLOT_ALIGN == 0 and cap >= WIN and s // MXU_DIM < LANES and d == D_ATTN + dg
    assert (s // MXU_DIM) % GATHER_BLOCKS == 0

    mod = _adaln(c, w_ada_all, b_ada, layer).reshape(b, 6, d)

    x2 = x.reshape(t, d)
    q, kv, u, vg = _proj(x2, mod, norm_pre_mix.reshape(1, d), w_in.astype(BF16), s)

    bias = jnp.asarray(_attn_bias_table())
    bs_full = jnp.repeat(b_s.T, dg // N_GMLP_GROUPS, axis=1)
    x1, h2, aff_t = _mix(
        x2, q, kv, u, vg, mod, sink, bias,
        w_s.astype(BF16).reshape(N_GMLP_GROUPS // 2, 2 * BLOCK, BLOCK), bs_full,
        sgu_ln_g.reshape(1, dg), sgu_ln_b.reshape(1, dg), norm_out_attn.reshape(1, D_ATTN),
        norm_out_gmlp.reshape(1, dg), w_out.astype(BF16), norm_post_mix.reshape(1, d),
        norm_pre_ffn.reshape(1, d), w_router.T.astype(BF16), b, s)

    slots3, aff3, cnt2 = _route(aff_t.reshape(b * e, s), cap)
    nblk = s // MXU_DIM
    slots4 = slots3.reshape(b, e, nblk, MXU_DIM)
    aff4 = aff3.reshape(b, e, nblk, MXU_DIM)
    xg, gs = _gather(cnt2, h2.reshape(b, s, d), slots4, aff4, cap)
    eo = _ffn(xg, gs, w_gate_all, w_up_all, w_down_all, layer * e)
    out = _combine(cnt2, eo, slots4, x1, mod, norm_post_ffn.reshape(1, d))
    return out.reshape(b, s, d)


def kernel(x, c, w_ada, b_ada, norm_pre_mix, norm_post_mix, w_in, sink, sgu_ln_g, sgu_ln_b, w_s, b_s, norm_out_attn, norm_out_gmlp, w_out, norm_pre_ffn, norm_post_ffn, w_router, w_gate, w_up, w_down):
    depth, d = w_ada.shape[0], w_ada.shape[1]
    w_ada_all = w_ada.reshape(depth * d, w_ada.shape[2])
    stack = lambda w: w.reshape((depth * w.shape[1],) + w.shape[2:])
    w_gate_all, w_up_all, w_down_all = stack(w_gate), stack(w_up), stack(w_down)
    for l in range(depth):
        x = _layer(l, x, c, w_ada_all, b_ada[l], norm_pre_mix[l], norm_post_mix[l], w_in[l],
                   sink[l], sgu_ln_g[l], sgu_ln_b[l], w_s[l], b_s[l], norm_out_attn[l],
                   norm_out_gmlp[l], w_out[l], norm_pre_ffn[l], norm_post_ffn[l], w_router[l],
                   w_gate_all, w_up_all, w_down_all)
    return x
```
